```python
import math
import jax
import jax.numpy as jnp
from jax import lax
import numpy as np

D_MODEL = 1024
BATCH = 8
SEQ = 8192
DEPTH = 2

CHUNK = 64
N_MIXERS = 2

SSM_EXPAND = 2
D_INNER = SSM_EXPAND * D_MODEL
SSM_HEAD_DIM = 64
SSM_HEADS = D_INNER // SSM_HEAD_DIM
SSM_GROUPS = 8
SSM_HEADS_PER_GROUP = SSM_HEADS // SSM_GROUPS
D_STATE = 128
CONV_WIDTH = 4
CONV_DIM = D_INNER + 2 * SSM_GROUPS * D_STATE
D_IN_PROJ = D_INNER + CONV_DIM + SSM_HEADS

SB_HEAD_DIM = 64
SB_HEADS = D_MODEL // SB_HEAD_DIM
Q_BLOCK = 128

D_FF = -(-8 * D_MODEL // (3 * 256)) * 256

NORM_EPS = 1e-6
GATED_NORM_EPS = 1e-5

kernel_name = "ssd_stickbreaking_interleaved_hybrid"


def rms_norm(x, w, eps=NORM_EPS):
    xf = x.astype(jnp.float32)
    return xf * lax.rsqrt(jnp.mean(xf * xf, axis=-1, keepdims=True) + eps) * w.astype(jnp.float32)


def causal_dwconv(u, w, b):
    k = w.shape[0]
    out = lax.conv_general_dilated(
        u, w[:, None, :], window_strides=(1,), padding=[(k - 1, 0)],
        dimension_numbers=("NWC", "WIO", "NWC"), feature_group_count=u.shape[-1])
    return out + b


def ssd_chunked_scan(xdt, a, b_mat, c_mat):
    bsz, seq = xdt.shape[0], xdt.shape[1]
    nc = seq // CHUNK

    def to_chunks(t):
        return jnp.moveaxis(t.reshape(bsz, nc, CHUNK, *t.shape[2:]), 1, 0)

    causal = jnp.tril(jnp.ones((CHUNK, CHUNK), dtype=bool))

    def step(state, inp):
        xc, ac, bc, cc = inp
        acs = jnp.cumsum(ac, axis=1)
        seg = acs[:, :, None] - acs[:, None, :]
        lmat = jnp.exp(jnp.where(causal[None, :, :, None, None], seg, -jnp.inf))
        cb = jnp.einsum("blgn,bsgn->blsg", cc, bc)
        y_intra = jnp.einsum("blsg,blsgr,bsgrp->blgrp", cb, lmat, xc)
        y_inter = jnp.einsum("blgn,bgrpn->blgrp", cc, state) * jnp.exp(acs)[..., None]
        decay_end = jnp.exp(acs[:, -1:] - acs)
        new_state = (state * jnp.exp(acs[:, -1])[..., None, None]
                     + jnp.einsum("blgn,blgr,blgrp->bgrpn", bc, decay_end, xc))
        return new_state, y_intra + y_inter

    state0 = jnp.zeros((bsz, SSM_GROUPS, SSM_HEADS_PER_GROUP, SSM_HEAD_DIM, D_STATE), jnp.float32)
    _, ys = lax.scan(step, state0, (to_chunks(xdt), to_chunks(a), to_chunks(b_mat), to_chunks(c_mat)))
    return jnp.moveaxis(ys, 0, 1).reshape(xdt.shape)


def mamba2_mixer(h, w_in, conv_w, conv_b, dt_bias, a_log, d_skip, norm_w, w_out):
    bsz, seq, _ = h.shape
    zxbcdt = h @ w_in
    z = zxbcdt[..., :D_INNER]
    xbc = zxbcdt[..., D_INNER:D_INNER + CONV_DIM]
    dt_raw = zxbcdt[..., D_INNER + CONV_DIM:]
    xbc = jax.nn.silu(causal_dwconv(xbc, conv_w, conv_b))
    xs = xbc[..., :D_INNER].astype(jnp.float32).reshape(
        bsz, seq, SSM_GROUPS, SSM_HEADS_PER_GROUP, SSM_HEAD_DIM)
    b_mat = xbc[..., D_INNER:D_INNER + SSM_GROUPS * D_STATE].astype(jnp.float32).reshape(
        bsz, seq, SSM_GROUPS, D_STATE)
    c_mat = xbc[..., D_INNER + SSM_GROUPS * D_STATE:].astype(jnp.float32).reshape(
        bsz, seq, SSM_GROUPS, D_STATE)
    dt = jax.nn.softplus(dt_raw.astype(jnp.float32) + dt_bias.astype(jnp.float32))
    dt = dt.reshape(bsz, seq, SSM_GROUPS, SSM_HEADS_PER_GROUP)
    a_cont = -jnp.exp(a_log.astype(jnp.float32)).reshape(SSM_GROUPS, SSM_HEADS_PER_GROUP)
    y = ssd_chunked_scan(xs * dt[..., None], dt * a_cont, b_mat, c_mat)
    y = y + d_skip.astype(jnp.float32).reshape(SSM_GROUPS, SSM_HEADS_PER_GROUP)[..., None] * xs
    y = y.reshape(bsz, seq, SSM_GROUPS, D_INNER // SSM_GROUPS)
    g = jax.nn.silu(z.astype(jnp.float32)).reshape(y.shape)
    y = rms_norm(y * g, norm_w.reshape(SSM_GROUPS, D_INNER // SSM_GROUPS), GATED_NORM_EPS)
    return y.reshape(bsz, seq, D_INNER).astype(h.dtype) @ w_out


def stick_breaking_block(q_blk, k_pre, v_pre, q_start):
    nq, nk = q_blk.shape[2], k_pre.shape[2]
    z = jnp.einsum("bhqd,bhkd->bhqk", q_blk, k_pre) * (SB_HEAD_DIM ** -0.5)
    q_pos = q_start + jnp.arange(nq)
    k_pos = jnp.arange(nk)
    mask = k_pos[None, :] < q_pos[:, None]
    log_beta = jax.nn.log_sigmoid(z)
    log_keep = jnp.where(mask, jax.nn.log_sigmoid(-z), 0.0)
    log_between = lax.cumsum(log_keep, axis=3, reverse=True) - log_keep
    weights = jnp.where(mask, jnp.exp(log_beta + log_between), 0.0)
    return jnp.einsum("bhqk,bhkd->bhqd", weights, v_pre)


def stick_breaking_mixer(h, w_qkv, q_gain, k_gain, w_o):
    bsz, seq, _ = h.shape
    qkv = (h @ w_qkv).reshape(bsz, seq, 3, SB_HEADS, SB_HEAD_DIM)
    q = jnp.transpose(rms_norm(qkv[:, :, 0], q_gain), (0, 2, 1, 3))
    k = jnp.transpose(rms_norm(qkv[:, :, 1], k_gain), (0, 2, 1, 3))
    v = jnp.transpose(qkv[:, :, 2].astype(jnp.float32), (0, 2, 1, 3))
    outs = []
    for start in range(0, seq, Q_BLOCK):
        end = start + Q_BLOCK
        outs.append(stick_breaking_block(q[:, :, start:end], k[:, :, :end], v[:, :, :end], start))
    o = jnp.concatenate(outs, axis=2)
    o = jnp.transpose(o, (0, 2, 1, 3)).reshape(bsz, seq, D_MODEL).astype(h.dtype)
    return o @ w_o


def swiglu_ffn(h, w_in, w_out):
    gu = h @ w_in
    gate, up = gu[..., :D_FF], gu[..., D_FF:]
    return (jax.nn.silu(gate) * up) @ w_out


def _fwd_setup_inputs(seed: int = 0) -> dict:
    key = jax.random.key(seed)
    ks = jax.random.split(key, 20)
    n_ssm = (DEPTH + 1) // 2
    n_sb = DEPTH // 2
    f32 = jnp.float32

    def normal(k, shape, scale):
        return jax.random.normal(k, shape, f32) * scale

    x = jax.random.normal(ks[0], (BATCH, SEQ, D_MODEL), f32)
    norm_mix = 1.0 + normal(ks[1], (DEPTH, D_MODEL), 0.05)
    norm_ffn = 1.0 + normal(ks[2], (DEPTH, D_MODEL), 0.05)

    ssm_w_in = normal(ks[3], (n_ssm, D_MODEL, D_IN_PROJ), D_MODEL ** -0.5)
    ssm_conv_w = normal(ks[4], (n_ssm, CONV_WIDTH, CONV_DIM), CONV_WIDTH ** -0.5)
    ssm_conv_b = normal(ks[5], (n_ssm, CONV_DIM), 0.01)
    u = jax.random.uniform(ks[6], (n_ssm, SSM_HEADS), f32)
    dt0 = jnp.exp(u * (math.log(0.1) - math.log(0.001)) + math.log(0.001))
    ssm_dt_bias = dt0 + jnp.log(-jnp.expm1(-dt0))
    ssm_a_log = jnp.log(jax.random.uniform(ks[7], (n_ssm, SSM_HEADS), f32, 1.0, 16.0))
    ssm_d = 1.0 + normal(ks[8], (n_ssm, SSM_HEADS), 0.1)
    ssm_norm_w = 1.0 + normal(ks[9], (n_ssm, D_INNER), 0.05)
    ssm_w_out = normal(ks[10], (n_ssm, D_INNER, D_MODEL), D_INNER ** -0.5)

    sb_w_qkv = normal(ks[11], (n_sb, D_MODEL, 3 * D_MODEL), D_MODEL ** -0.5)
    sb_q_gain = 1.0 + normal(ks[12], (n_sb, SB_HEAD_DIM), 0.05)
    sb_k_gain = 1.0 + normal(ks[13], (n_sb, SB_HEAD_DIM), 0.05)
    sb_w_o = normal(ks[14], (n_sb, D_MODEL, D_MODEL), D_MODEL ** -0.5)

    ffn_w_in = normal(ks[15], (DEPTH, D_MODEL, 2 * D_FF), D_MODEL ** -0.5)
    ffn_w_out = normal(ks[16], (DEPTH, D_FF, D_MODEL), D_FF ** -0.5)

    return {
        "x": x, "norm_mix": norm_mix, "norm_ffn": norm_ffn,
        "ssm_w_in": ssm_w_in, "ssm_conv_w": ssm_conv_w, "ssm_conv_b": ssm_conv_b,
        "ssm_dt_bias": ssm_dt_bias, "ssm_a_log": ssm_a_log, "ssm_d": ssm_d,
        "ssm_norm_w": ssm_norm_w, "ssm_w_out": ssm_w_out,
        "sb_w_qkv": sb_w_qkv, "sb_q_gain": sb_q_gain, "sb_k_gain": sb_k_gain, "sb_w_o": sb_w_o,
        "ffn_w_in": ffn_w_in, "ffn_w_out": ffn_w_out,
    }


def _fwd_reference(x, norm_mix, norm_ffn, ssm_w_in, ssm_conv_w, ssm_conv_b, ssm_dt_bias,
              ssm_a_log, ssm_d, ssm_norm_w, ssm_w_out, sb_w_qkv, sb_q_gain, sb_k_gain,
              sb_w_o, ffn_w_in, ffn_w_out):
    for i in range(DEPTH):
        h = rms_norm(x, norm_mix[i]).astype(x.dtype)
        j = i // N_MIXERS
        if i % N_MIXERS == 0:
            mix = mamba2_mixer(h, ssm_w_in[j], ssm_conv_w[j], ssm_conv_b[j], ssm_dt_bias[j],
                               ssm_a_log[j], ssm_d[j], ssm_norm_w[j], ssm_w_out[j])
        else:
            mix = stick_breaking_mixer(h, sb_w_qkv[j], sb_q_gain[j], sb_k_gain[j], sb_w_o[j])
        x = x + mix.astype(x.dtype)
        h = rms_norm(x, norm_ffn[i]).astype(x.dtype)
        x = x + swiglu_ffn(h, ffn_w_in[i], ffn_w_out[i]).astype(x.dtype)
    return x


import jax as _jax
import jax.numpy as _jnp

TWIN_FORMAT = 'train_step'
FWD_PARAMS = ['x', 'norm_mix', 'norm_ffn', 'ssm_w_in', 'ssm_conv_w', 'ssm_conv_b', 'ssm_dt_bias', 'ssm_a_log', 'ssm_d', 'ssm_norm_w', 'ssm_w_out', 'sb_w_qkv', 'sb_q_gain', 'sb_k_gain', 'sb_w_o', 'ffn_w_in', 'ffn_w_out']
TWIN_WEIGHTS = ['norm_mix', 'norm_ffn', 'ssm_w_in', 'ssm_conv_w', 'ssm_conv_b', 'ssm_dt_bias', 'ssm_a_log', 'ssm_d', 'ssm_norm_w', 'ssm_w_out', 'sb_w_qkv', 'sb_q_gain', 'sb_k_gain', 'sb_w_o', 'ffn_w_in', 'ffn_w_out']
TWIN_DIFF_INPUT = 'x'
TWIN_INPUTS = ['x', 'norm_mix', 'norm_ffn', 'ssm_w_in', 'ssm_conv_w', 'ssm_conv_b', 'ssm_dt_bias', 'ssm_a_log', 'ssm_d', 'ssm_norm_w', 'ssm_w_out', 'sb_w_qkv', 'sb_q_gain', 'sb_k_gain', 'sb_w_o', 'ffn_w_in', 'ffn_w_out', 'loss_target', 'm_norm_mix', 'm_norm_ffn', 'm_ssm_w_in', 'm_ssm_conv_w', 'm_ssm_conv_b', 'm_ssm_dt_bias', 'm_ssm_a_log', 'm_ssm_d', 'm_ssm_norm_w', 'm_ssm_w_out', 'm_sb_w_qkv', 'm_sb_q_gain', 'm_sb_k_gain', 'm_sb_w_o', 'm_ffn_w_in', 'm_ffn_w_out', 'v_norm_mix', 'v_norm_ffn', 'v_ssm_w_in', 'v_ssm_conv_w', 'v_ssm_conv_b', 'v_ssm_dt_bias', 'v_ssm_a_log', 'v_ssm_d', 'v_ssm_norm_w', 'v_ssm_w_out', 'v_sb_w_qkv', 'v_sb_q_gain', 'v_sb_k_gain', 'v_sb_w_o', 'v_ffn_w_in', 'v_ffn_w_out']
TWIN_OUTPUTS = ['loss', 'grad_x', 'grad_norm_mix', 'grad_norm_ffn', 'grad_ssm_w_in', 'grad_ssm_conv_w', 'grad_ssm_conv_b', 'grad_ssm_dt_bias', 'grad_ssm_a_log', 'grad_ssm_d', 'grad_ssm_norm_w', 'grad_ssm_w_out', 'grad_sb_w_qkv', 'grad_sb_q_gain', 'grad_sb_k_gain', 'grad_sb_w_o', 'grad_ffn_w_in', 'grad_ffn_w_out', 'delta_norm_mix', 'delta_norm_ffn', 'delta_ssm_w_in', 'delta_ssm_conv_w', 'delta_ssm_conv_b', 'delta_ssm_dt_bias', 'delta_ssm_a_log', 'delta_ssm_d', 'delta_ssm_norm_w', 'delta_ssm_w_out', 'delta_sb_w_qkv', 'delta_sb_q_gain', 'delta_sb_k_gain', 'delta_sb_w_o', 'delta_ffn_w_in', 'delta_ffn_w_out', 'new_m_norm_mix', 'new_m_norm_ffn', 'new_m_ssm_w_in', 'new_m_ssm_conv_w', 'new_m_ssm_conv_b', 'new_m_ssm_dt_bias', 'new_m_ssm_a_log', 'new_m_ssm_d', 'new_m_ssm_norm_w', 'new_m_ssm_w_out', 'new_m_sb_w_qkv', 'new_m_sb_q_gain', 'new_m_sb_k_gain', 'new_m_sb_w_o', 'new_m_ffn_w_in', 'new_m_ffn_w_out', 'new_v_norm_mix', 'new_v_norm_ffn', 'new_v_ssm_w_in', 'new_v_ssm_conv_w', 'new_v_ssm_conv_b', 'new_v_ssm_dt_bias', 'new_v_ssm_a_log', 'new_v_ssm_d', 'new_v_ssm_norm_w', 'new_v_ssm_w_out', 'new_v_sb_w_qkv', 'new_v_sb_q_gain', 'new_v_sb_k_gain', 'new_v_sb_w_o', 'new_v_ffn_w_in', 'new_v_ffn_w_out']
TWIN_LEAF_KINDS = {'loss': 'loss', 'grad_x': 'grad_x', 'grad_norm_mix': 'grad_w', 'grad_norm_ffn': 'grad_w', 'grad_ssm_w_in': 'grad_w', 'grad_ssm_conv_w': 'grad_w', 'grad_ssm_conv_b': 'grad_w', 'grad_ssm_dt_bias': 'grad_w', 'grad_ssm_a_log': 'grad_w', 'grad_ssm_d': 'grad_w', 'grad_ssm_norm_w': 'grad_w', 'grad_ssm_w_out': 'grad_w', 'grad_sb_w_qkv': 'grad_w', 'grad_sb_q_gain': 'grad_w', 'grad_sb_k_gain': 'grad_w', 'grad_sb_w_o': 'grad_w', 'grad_ffn_w_in': 'grad_w', 'grad_ffn_w_out': 'grad_w', 'delta_norm_mix': 'delta_w', 'delta_norm_ffn': 'delta_w', 'delta_ssm_w_in': 'delta_w', 'delta_ssm_conv_w': 'delta_w', 'delta_ssm_conv_b': 'delta_w', 'delta_ssm_dt_bias': 'delta_w', 'delta_ssm_a_log': 'delta_w', 'delta_ssm_d': 'delta_w', 'delta_ssm_norm_w': 'delta_w', 'delta_ssm_w_out': 'delta_w', 'delta_sb_w_qkv': 'delta_w', 'delta_sb_q_gain': 'delta_w', 'delta_sb_k_gain': 'delta_w', 'delta_sb_w_o': 'delta_w', 'delta_ffn_w_in': 'delta_w', 'delta_ffn_w_out': 'delta_w', 'new_m_norm_mix': 'new_m', 'new_m_norm_ffn': 'new_m', 'new_m_ssm_w_in': 'new_m', 'new_m_ssm_conv_w': 'new_m', 'new_m_ssm_conv_b': 'new_m', 'new_m_ssm_dt_bias': 'new_m', 'new_m_ssm_a_log': 'new_m', 'new_m_ssm_d': 'new_m', 'new_m_ssm_norm_w': 'new_m', 'new_m_ssm_w_out': 'new_m', 'new_m_sb_w_qkv': 'new_m', 'new_m_sb_q_gain': 'new_m', 'new_m_sb_k_gain': 'new_m', 'new_m_sb_w_o': 'new_m', 'new_m_ffn_w_in': 'new_m', 'new_m_ffn_w_out': 'new_m', 'new_v_norm_mix': 'new_v', 'new_v_norm_ffn': 'new_v', 'new_v_ssm_w_in': 'new_v', 'new_v_ssm_conv_w': 'new_v', 'new_v_ssm_conv_b': 'new_v', 'new_v_ssm_dt_bias': 'new_v', 'new_v_ssm_a_log': 'new_v', 'new_v_ssm_d': 'new_v', 'new_v_ssm_norm_w': 'new_v', 'new_v_ssm_w_out': 'new_v', 'new_v_sb_w_qkv': 'new_v', 'new_v_sb_q_gain': 'new_v', 'new_v_sb_k_gain': 'new_v', 'new_v_sb_w_o': 'new_v', 'new_v_ffn_w_in': 'new_v', 'new_v_ffn_w_out': 'new_v'}


def _forward(args):
    return _fwd_reference(*[args[k] for k in FWD_PARAMS])


def _output_shape():
    out = _jax.eval_shape(lambda: _forward(_fwd_setup_inputs(0)))
    return out.shape, out.dtype

N_MICROBATCH = 1
ADAM_LR = 0.001
ADAM_B1 = 0.9
ADAM_B2 = 0.999
ADAM_EPS = 1e-08
ADAM_WD = 0.01
ADAM_STEP = 10
PER_EXAMPLE_BATCH_AXIS = {'x': 0, 'loss_target': 0}
SHARED_INPUTS = []
_WEIGHT_DTYPES = {'norm_mix': _jnp.float32, 'norm_ffn': _jnp.float32, 'ssm_w_in': _jnp.float32, 'ssm_conv_w': _jnp.float32, 'ssm_conv_b': _jnp.float32, 'ssm_dt_bias': _jnp.float32, 'ssm_a_log': _jnp.float32, 'ssm_d': _jnp.float32, 'ssm_norm_w': _jnp.float32, 'ssm_w_out': _jnp.float32, 'sb_w_qkv': _jnp.float32, 'sb_q_gain': _jnp.float32, 'sb_k_gain': _jnp.float32, 'sb_w_o': _jnp.float32, 'ffn_w_in': _jnp.float32, 'ffn_w_out': _jnp.float32}
MOMENT_SCALE = {'norm_mix': 1.855393e+01, 'norm_ffn': 4.990493e+01, 'ssm_w_in': 5.490629e-01, 'ssm_conv_w': 1.306654e+00, 'ssm_conv_b': 4.973920e+00, 'ssm_dt_bias': 1.709383e+00, 'ssm_a_log': 9.387243e+00, 'ssm_d': 1.087651e+01, 'ssm_norm_w': 4.477205e+01, 'ssm_w_out': 3.624958e+00, 'sb_w_qkv': 1.056788e+00, 'sb_q_gain': 6.362616e+01, 'sb_k_gain': 6.303500e+01, 'sb_w_o': 1.574905e+00, 'ffn_w_in': 5.233597e-01, 'ffn_w_out': 6.768857e-01}


def _to_microbatches(a, axis):
    t = _jnp.moveaxis(a, axis, 0)
    t = t.reshape((N_MICROBATCH, t.shape[0] // N_MICROBATCH) + t.shape[1:])
    return _jnp.moveaxis(t, 1, axis + 1)


def setup_inputs(seed: int = 0) -> dict:
    inp = _fwd_setup_inputs(seed)
    key = _jax.random.fold_in(_jax.random.key(seed), 7919)
    shape, _ = _output_shape()
    out = dict(inp)
    out["loss_target"] = _jax.random.normal(_jax.random.fold_in(key, 0), shape, _jnp.float32)
    for i, name in enumerate(TWIN_WEIGHTS):
        w = inp[name].astype(_jnp.float32)
        if MOMENT_SCALE is None:
            s = _jnp.sqrt(_jnp.mean(_jnp.square(w)) + 1e-30)
        else:
            s = MOMENT_SCALE[name]
        km, kv = _jax.random.split(_jax.random.fold_in(key, i + 1))
        out[name] = w
        out["m_" + name] = s * _jax.random.normal(km, w.shape, _jnp.float32)
        out["v_" + name] = (s * s) * _jax.random.uniform(kv, w.shape, _jnp.float32, 0.5, 1.5)
    if N_MICROBATCH > 1:
        for name, axis in PER_EXAMPLE_BATCH_AXIS.items():
            out[name] = _to_microbatches(out[name], axis)
    return {'x': out['x'], 'norm_mix': out['norm_mix'], 'norm_ffn': out['norm_ffn'], 'ssm_w_in': out['ssm_w_in'], 'ssm_conv_w': out['ssm_conv_w'], 'ssm_conv_b': out['ssm_conv_b'], 'ssm_dt_bias': out['ssm_dt_bias'], 'ssm_a_log': out['ssm_a_log'], 'ssm_d': out['ssm_d'], 'ssm_norm_w': out['ssm_norm_w'], 'ssm_w_out': out['ssm_w_out'], 'sb_w_qkv': out['sb_w_qkv'], 'sb_q_gain': out['sb_q_gain'], 'sb_k_gain': out['sb_k_gain'], 'sb_w_o': out['sb_w_o'], 'ffn_w_in': out['ffn_w_in'], 'ffn_w_out': out['ffn_w_out'], 'loss_target': out['loss_target'], 'm_norm_mix': out['m_norm_mix'], 'm_norm_ffn': out['m_norm_ffn'], 'm_ssm_w_in': out['m_ssm_w_in'], 'm_ssm_conv_w': out['m_ssm_conv_w'], 'm_ssm_conv_b': out['m_ssm_conv_b'], 'm_ssm_dt_bias': out['m_ssm_dt_bias'], 'm_ssm_a_log': out['m_ssm_a_log'], 'm_ssm_d': out['m_ssm_d'], 'm_ssm_norm_w': out['m_ssm_norm_w'], 'm_ssm_w_out': out['m_ssm_w_out'], 'm_sb_w_qkv': out['m_sb_w_qkv'], 'm_sb_q_gain': out['m_sb_q_gain'], 'm_sb_k_gain': out['m_sb_k_gain'], 'm_sb_w_o': out['m_sb_w_o'], 'm_ffn_w_in': out['m_ffn_w_in'], 'm_ffn_w_out': out['m_ffn_w_out'], 'v_norm_mix': out['v_norm_mix'], 'v_norm_ffn': out['v_norm_ffn'], 'v_ssm_w_in': out['v_ssm_w_in'], 'v_ssm_conv_w': out['v_ssm_conv_w'], 'v_ssm_conv_b': out['v_ssm_conv_b'], 'v_ssm_dt_bias': out['v_ssm_dt_bias'], 'v_ssm_a_log': out['v_ssm_a_log'], 'v_ssm_d': out['v_ssm_d'], 'v_ssm_norm_w': out['v_ssm_norm_w'], 'v_ssm_w_out': out['v_ssm_w_out'], 'v_sb_w_qkv': out['v_sb_w_qkv'], 'v_sb_q_gain': out['v_sb_q_gain'], 'v_sb_k_gain': out['v_sb_k_gain'], 'v_sb_w_o': out['v_sb_w_o'], 'v_ffn_w_in': out['v_ffn_w_in'], 'v_ffn_w_out': out['v_ffn_w_out']}


def _loss(weights, diff, rest, loss_target):
    with _jax.named_scope("forward"):
        args = {**rest, TWIN_DIFF_INPUT: diff, **{k: w.astype(_WEIGHT_DTYPES[k]) for k, w in weights.items()}}
        y = _forward(args)
    with _jax.named_scope("loss_head"):
        err = _jnp.square(y.astype(_jnp.float32) - loss_target)
        return 0.5 * _jnp.sum(_jnp.mean(err, axis=-1)) if err.ndim else 0.5 * err


def _adamw(w, g, m, v):
    m = ADAM_B1 * m + (1.0 - ADAM_B1) * g
    v = ADAM_B2 * v + (1.0 - ADAM_B2) * _jnp.square(g)
    m_hat = m / (1.0 - ADAM_B1 ** ADAM_STEP)
    v_hat = v / (1.0 - ADAM_B2 ** ADAM_STEP)
    delta = -ADAM_LR * (m_hat / (_jnp.sqrt(v_hat) + ADAM_EPS) + ADAM_WD * w)
    return delta, m, v


def reference(x, norm_mix, norm_ffn, ssm_w_in, ssm_conv_w, ssm_conv_b, ssm_dt_bias, ssm_a_log, ssm_d, ssm_norm_w, ssm_w_out, sb_w_qkv, sb_q_gain, sb_k_gain, sb_w_o, ffn_w_in, ffn_w_out, loss_target, m_norm_mix, m_norm_ffn, m_ssm_w_in, m_ssm_conv_w, m_ssm_conv_b, m_ssm_dt_bias, m_ssm_a_log, m_ssm_d, m_ssm_norm_w, m_ssm_w_out, m_sb_w_qkv, m_sb_q_gain, m_sb_k_gain, m_sb_w_o, m_ffn_w_in, m_ffn_w_out, v_norm_mix, v_norm_ffn, v_ssm_w_in, v_ssm_conv_w, v_ssm_conv_b, v_ssm_dt_bias, v_ssm_a_log, v_ssm_d, v_ssm_norm_w, v_ssm_w_out, v_sb_w_qkv, v_sb_q_gain, v_sb_k_gain, v_sb_w_o, v_ffn_w_in, v_ffn_w_out):
    given = dict(x=x, norm_mix=norm_mix, norm_ffn=norm_ffn, ssm_w_in=ssm_w_in, ssm_conv_w=ssm_conv_w, ssm_conv_b=ssm_conv_b, ssm_dt_bias=ssm_dt_bias, ssm_a_log=ssm_a_log, ssm_d=ssm_d, ssm_norm_w=ssm_norm_w, ssm_w_out=ssm_w_out, sb_w_qkv=sb_w_qkv, sb_q_gain=sb_q_gain, sb_k_gain=sb_k_gain, sb_w_o=sb_w_o, ffn_w_in=ffn_w_in, ffn_w_out=ffn_w_out, loss_target=loss_target, m_norm_mix=m_norm_mix, m_norm_ffn=m_norm_ffn, m_ssm_w_in=m_ssm_w_in, m_ssm_conv_w=m_ssm_conv_w, m_ssm_conv_b=m_ssm_conv_b, m_ssm_dt_bias=m_ssm_dt_bias, m_ssm_a_log=m_ssm_a_log, m_ssm_d=m_ssm_d, m_ssm_norm_w=m_ssm_norm_w, m_ssm_w_out=m_ssm_w_out, m_sb_w_qkv=m_sb_w_qkv, m_sb_q_gain=m_sb_q_gain, m_sb_k_gain=m_sb_k_gain, m_sb_w_o=m_sb_w_o, m_ffn_w_in=m_ffn_w_in, m_ffn_w_out=m_ffn_w_out, v_norm_mix=v_norm_mix, v_norm_ffn=v_norm_ffn, v_ssm_w_in=v_ssm_w_in, v_ssm_conv_w=v_ssm_conv_w, v_ssm_conv_b=v_ssm_conv_b, v_ssm_dt_bias=v_ssm_dt_bias, v_ssm_a_log=v_ssm_a_log, v_ssm_d=v_ssm_d, v_ssm_norm_w=v_ssm_norm_w, v_ssm_w_out=v_ssm_w_out, v_sb_w_qkv=v_sb_w_qkv, v_sb_q_gain=v_sb_q_gain, v_sb_k_gain=v_sb_k_gain, v_sb_w_o=v_sb_w_o, v_ffn_w_in=v_ffn_w_in, v_ffn_w_out=v_ffn_w_out)
    weights = {n: given[n] for n in TWIN_WEIGHTS}
    shared = {n: given[n] for n in SHARED_INPUTS}
    per_example = {n: given[n] for n in ['x']}
    grad_fn = _jax.value_and_grad(_loss, argnums=(0, 1))

    def one_microbatch(ex, loss_target):
        ex = dict(ex)
        diff = ex.pop(TWIN_DIFF_INPUT)
        return grad_fn(weights, diff, {**shared, **ex}, loss_target)

    if N_MICROBATCH == 1:
        loss, (grad_w, grad_x) = one_microbatch(per_example, given["loss_target"])
    else:
        def body(carry, xs):
            loss_sum, grad_sum = carry
            l_k, (gw_k, gx_k) = one_microbatch(xs[0], xs[1])
            with _jax.named_scope("update"):
                return (loss_sum + l_k, _jax.tree.map(_jnp.add, grad_sum, gw_k)), gx_k

        init = (_jnp.zeros((), _jnp.float32), _jax.tree.map(_jnp.zeros_like, weights))
        (loss, grad_w), grad_x = _jax.lax.scan(body, init, (per_example, given["loss_target"]))
    with _jax.named_scope("update"):
        delta_w, new_m, new_v = {}, {}, {}
        for n in TWIN_WEIGHTS:
            delta_w[n], new_m[n], new_v[n] = _adamw(weights[n], grad_w[n], given["m_" + n], given["v_" + n])
    return (loss, grad_x, *[grad_w[n] for n in TWIN_WEIGHTS], *[delta_w[n] for n in TWIN_WEIGHTS],
            *[new_m[n] for n in TWIN_WEIGHTS], *[new_v[n] for n in TWIN_WEIGHTS])
```

```python
import functools
import math

import jax
import jax.numpy as jnp
from jax import lax
from jax.experimental import pallas as pl
from jax.experimental.pallas import tpu as pltpu

F32 = jnp.float32
BF16 = jnp.bfloat16

VMEM_LIMIT_BYTES = 48 * 1024 * 1024
LANE = 128
SUBLANE = 8

D_MODEL = 1024
CHUNK = 64
D_INNER = 2048
SSM_HEADS = 32
SSM_GROUPS = 8
GROUP_W = D_INNER // SSM_GROUPS
HEAD_P = 64
D_STATE = 128
CONV_W = 4
CONV_DIM = D_INNER + 2 * SSM_GROUPS * D_STATE
D_IN_PROJ = D_INNER + CONV_DIM + SSM_HEADS
DT_PAD = LANE
D_IN_PAD = D_INNER + CONV_DIM + DT_PAD
SB_HEADS = 16
SB_D = 64
D_FF = 2816
NORM_EPS = 1e-6
GATED_NORM_EPS = 1e-5

ADAM_LR = 0.001
ADAM_B1 = 0.9
ADAM_B2 = 0.999
ADAM_EPS = 1e-08
ADAM_WD = 0.01
ADAM_STEP = 10

MESH_T = pl.DeviceIdType.MESH


def _cparams(sem=None):
    return pltpu.CompilerParams(dimension_semantics=sem, vmem_limit_bytes=VMEM_LIMIT_BYTES)


def _tile(n, target, align):
    best = None
    for t in range(align, min(n, target) + 1, align):
        if n % t == 0:
            best = t
    return best or n


def _mm(a, b, mode, out_dtype, name):
    if mode == "nn":
        (M, K), N = a.shape, b.shape[1]
    elif mode == "nt":
        (M, K), N = a.shape, b.shape[0]
    else:
        (K, M), N = a.shape, b.shape[1]
    tm = _tile(M, 512, LANE)
    tn = _tile(N, 1536, LANE)
    tk = _tile(K, 1024, LANE)
    nk = K // tk
    if mode == "nn":
        a_spec = pl.BlockSpec((tm, tk), lambda i, j, k: (i, k))
        b_spec = pl.BlockSpec((tk, tn), lambda i, j, k: (k, j))
        dims = (((1,), (0,)), ((), ()))
    elif mode == "nt":
        a_spec = pl.BlockSpec((tm, tk), lambda i, j, k: (i, k))
        b_spec = pl.BlockSpec((tn, tk), lambda i, j, k: (j, k))
        dims = (((1,), (1,)), ((), ()))
    else:
        a_spec = pl.BlockSpec((tk, tm), lambda i, j, k: (k, i))
        b_spec = pl.BlockSpec((tk, tn), lambda i, j, k: (k, j))
        dims = (((0,), (0,)), ((), ()))

    def body(a_ref, b_ref, o_ref, acc_ref):
        k = pl.program_id(2)

        @pl.when(k == 0)
        def _():
            acc_ref[...] = jnp.zeros_like(acc_ref)

        acc_ref[...] += lax.dot_general(a_ref[...], b_ref[...], dims, preferred_element_type=F32)

        @pl.when(k == nk - 1)
        def _():
            o_ref[...] = acc_ref[...].astype(o_ref.dtype)

    return pl.pallas_call(
        body,
        name=name,
        grid=(M // tm, N // tn, nk),
        in_specs=[a_spec, b_spec],
        out_specs=pl.BlockSpec((tm, tn), lambda i, j, k: (i, j)),
        out_shape=jax.ShapeDtypeStruct((M, N), out_dtype),
        scratch_shapes=[pltpu.VMEM((tm, tn), F32)],
        compiler_params=_cparams(("parallel", "parallel", "arbitrary")),
    )(a, b)


def _rows(fn, name, tile, row_ins, const_ins, row_outs, acc_outs=(), ncol=1):
    L = row_ins[0][0].shape[0]
    nrow = L // tile
    n_ri, n_ci, n_ro, n_ao = len(row_ins), len(const_ins), len(row_outs), len(acc_outs)

    def body(*refs):
        i = pl.program_id(1)
        j = pl.program_id(0)
        ins = [r[...] for r in refs[: n_ri + n_ci]]
        outs = fn(*ins, j=j)
        o_refs = refs[n_ri + n_ci:]
        for r, v in zip(o_refs[:n_ro], outs[:n_ro]):
            r[...] = v.astype(r.dtype)

        @pl.when(i == 0)
        def _():
            for r in o_refs[n_ro:]:
                r[...] = jnp.zeros_like(r)

        for r, v in zip(o_refs[n_ro:], outs[n_ro:]):
            r[...] += v

    def rspec(bc, cf):
        return pl.BlockSpec((tile, bc), lambda j, i: (i, cf(j)))

    def cspec(r, bc, cf):
        return pl.BlockSpec((r, bc), lambda j, i: (0, cf(j)))

    in_specs = [rspec(bc, cf) for (_, bc, cf) in row_ins]
    in_specs += [cspec(a.shape[0], bc, cf) for (a, bc, cf) in const_ins]
    out_specs = [rspec(bc, cf) for (_, _, bc, cf) in row_outs]
    out_specs += [cspec(1, bc, cf) for (_, bc, cf) in acc_outs]
    out_shape = [jax.ShapeDtypeStruct((L, c), dt) for (c, dt, _, _) in row_outs]
    out_shape += [jax.ShapeDtypeStruct((1, c), F32) for (c, _, _) in acc_outs]
    res = pl.pallas_call(
        body,
        name=name,
        grid=(ncol, nrow),
        in_specs=in_specs,
        out_specs=out_specs,
        out_shape=out_shape,
        compiler_params=_cparams(("arbitrary", "arbitrary")),
    )(*[a for (a, _, _) in row_ins], *[a for (a, _, _) in const_ins])
    return res


def _zero(j):
    return 0


def _whole(a):
    return (a, a.shape[1], _zero)


def _rms(x, w, eps):
    return x * lax.rsqrt(jnp.mean(x * x, axis=-1, keepdims=True) + eps) * w


ROW_TILE = 256


def _norm_fwd(x, r, w, name):
    C = x.shape[1]
    if r is None:
        def fn(x_, w_, j):
            return (_rms(x_, w_, NORM_EPS),)
        (h,) = _rows(fn, name, ROW_TILE, [_whole(x)], [_whole(w)], [(C, BF16, C, _zero)])
        return x, h

    def fn(x_, r_, w_, j):
        x1 = x_ + r_
        return x1, _rms(x1, w_, NORM_EPS)

    x1, h = _rows(fn, name, ROW_TILE, [_whole(x), _whole(r)], [_whole(w)],
                  [(C, F32, C, _zero), (C, BF16, C, _zero)])
    return x1, h


def _norm_bwd(x, w, dh, dres, name):
    C = x.shape[1]

    def fn(x_, dh_, dres_, w_, j):
        _, vjp = jax.vjp(lambda a, b: _rms(a, b, NORM_EPS), x_, w_)
        dx, dw = vjp(dh_.astype(F32))
        return dres_ + dx, dw

    return _rows(fn, name, ROW_TILE, [_whole(x), _whole(dh), _whole(dres)], [_whole(w)],
                 [(C, F32, C, _zero)], [(C, C, _zero)])


def _silu(x):
    return x * jax.nn.sigmoid(x)


FF_BLK = 256
FF_NB = D_FF // FF_BLK


def _swiglu_fwd(gu, name):
    def fn(g, u, j):
        return (_silu(g) * u,)
    (act,) = _rows(fn, name, ROW_TILE,
                   [(gu, FF_BLK, lambda j: j), (gu, FF_BLK, lambda j: j + FF_NB)], [],
                   [(D_FF, BF16, FF_BLK, lambda j: j)], ncol=FF_NB)
    return act


def _swiglu_bwd(gu, dact, name):
    def fn(g, u, da, j):
        _, vjp = jax.vjp(lambda a, b: _silu(a) * b, g, u)
        dg, du = vjp(da.astype(F32))
        return (jnp.where(j < FF_NB, dg, du),)
    (dgu,) = _rows(fn, name, ROW_TILE,
                   [(gu, FF_BLK, lambda j: j % FF_NB), (gu, FF_BLK, lambda j: j % FF_NB + FF_NB),
                    (dact, FF_BLK, lambda j: j % FF_NB)], [],
                   [(2 * D_FF, BF16, FF_BLK, lambda j: j)], ncol=2 * FF_NB)
    return dgu


def _gated_norm_fwd(y, zx, w, name):
    def fn(y_, z_, w_, j):
        return (_rms(y_ * _silu(z_), w_, GATED_NORM_EPS),)
    (yn,) = _rows(fn, name, ROW_TILE,
                  [(y, GROUP_W, lambda j: j), (zx, GROUP_W, lambda j: j)], [(w, GROUP_W, lambda j: j)],
                  [(D_INNER, BF16, GROUP_W, lambda j: j)], ncol=SSM_GROUPS)
    return yn


def _gated_norm_bwd(y, zx, w, dyn, name):
    def fn(y_, z_, dyn_, w_, j):
        _, vjp = jax.vjp(lambda a, b, c: _rms(a * _silu(b), c, GATED_NORM_EPS), y_, z_, w_)
        return vjp(dyn_.astype(F32))
    cj = lambda j: j
    return _rows(fn, name, ROW_TILE,
                 [(y, GROUP_W, cj), (zx, GROUP_W, cj), (dyn, GROUP_W, cj)], [(w, GROUP_W, cj)],
                 [(D_INNER, F32, GROUP_W, cj), (D_INNER, BF16, GROUP_W, cj)], [(D_INNER, GROUP_W, cj)],
                 ncol=SSM_GROUPS)


def _loss_bwd(x3, f, target, name):
    C = x3.shape[1]

    def fn(x_, f_, t_, j):
        err = (x_ + f_) - t_
        return err * (1.0 / C), jnp.sum(err * err, axis=0, keepdims=True)

    return _rows(fn, name, ROW_TILE, [_whole(x3), _whole(f), _whole(target)], [],
                 [(C, F32, C, _zero)], [(C, C, _zero)])


def _qk_norm_fwd(q, gain, scale, name):
    def fn(q_, g_, j):
        return (_rms(q_, g_, NORM_EPS) * scale,)
    (qn,) = _rows(fn, name, 2048, [_whole(q)], [_whole(gain)], [(SB_D, BF16, SB_D, _zero)])
    return qn


def _qk_norm_bwd(q, gain, dqn, scale, name):
    def fn(q_, dqn_, g_, j):
        _, vjp = jax.vjp(lambda a, b: _rms(a, b, NORM_EPS) * scale, q_, g_)
        return vjp(dqn_)
    return _rows(fn, name, 2048, [_whole(q), _whole(dqn)], [_whole(gain)],
                 [(SB_D, F32, SB_D, _zero)], [(SB_D, SB_D, _zero)])


CONV_TILE = 256
CONV_BLK = 512
XBC_COL0 = D_INNER // CONV_BLK


def _shift_down(cur, prev8, k):
    if k == 0:
        return cur
    rolled = pltpu.roll(cur, k, 0)
    head_prev = pltpu.roll(prev8, k, 0)
    rid = lax.broadcasted_iota(jnp.int32, (SUBLANE, cur.shape[1]), 0)
    head = jnp.where(rid < k, head_prev, rolled[:SUBLANE])
    if cur.shape[0] == SUBLANE:
        return head
    return jnp.concatenate([head, rolled[SUBLANE:]], axis=0)


def _shift_up(cur, next8, k):
    if k == 0:
        return cur
    T = cur.shape[0]
    rolled = pltpu.roll(cur, T - k, 0)
    tail_next = pltpu.roll(next8, SUBLANE - k, 0)
    rid = lax.broadcasted_iota(jnp.int32, (SUBLANE, cur.shape[1]), 0)
    tail = jnp.where(rid >= SUBLANE - k, tail_next, rolled[T - SUBLANE:])
    return jnp.concatenate([rolled[: T - SUBLANE], tail], axis=0)


def _conv_pre(cur, prev8, w, b):
    pre = b
    for i in range(CONV_W):
        pre = pre + w[i:i + 1, :] * _shift_down(cur, prev8, CONV_W - 1 - i)
    return pre


def _conv_fwd(zx, w8, b, name):
    L = zx.shape[0]
    nrow = L // CONV_TILE
    r8 = CONV_TILE // SUBLANE

    def body(cur_ref, prev_ref, w_ref, b_ref, o_ref):
        i = pl.program_id(1)
        prev8 = jnp.where(i > 0, prev_ref[...], 0.0)
        pre = _conv_pre(cur_ref[...], prev8, w_ref[...], b_ref[...])
        o_ref[...] = _silu(pre)

    return pl.pallas_call(
        body,
        name=name,
        grid=(CONV_DIM // CONV_BLK, nrow),
        in_specs=[
            pl.BlockSpec((CONV_TILE, CONV_BLK), lambda j, i: (i, j + XBC_COL0)),
            pl.BlockSpec((SUBLANE, CONV_BLK), lambda j, i: (jnp.maximum(i * r8 - 1, 0), j + XBC_COL0)),
            pl.BlockSpec((SUBLANE, CONV_BLK), lambda j, i: (0, j)),
            pl.BlockSpec((1, CONV_BLK), lambda j, i: (0, j)),
        ],
        out_specs=pl.BlockSpec((CONV_TILE, CONV_BLK), lambda j, i: (i, j)),
        out_shape=jax.ShapeDtypeStruct((L, CONV_DIM), F32),
        compiler_params=_cparams(("arbitrary", "arbitrary")),
    )(zx, zx, w8, b)


def _conv_bwd(zx, w8, b, dact, name):
    L = zx.shape[0]
    nrow = L // CONV_TILE
    r8 = CONV_TILE // SUBLANE
    last8 = L // SUBLANE - 1

    def body(cur_ref, prev_ref, next_ref, da_ref, dan_ref, w_ref, b_ref, du_ref, dw_ref, db_ref):
        i = pl.program_id(1)
        w = w_ref[...]
        b_ = b_ref[...]
        cur = cur_ref[...]
        prev8 = jnp.where(i > 0, prev_ref[...], 0.0)
        pre = _conv_pre(cur, prev8, w, b_)
        _, vjp = jax.vjp(_silu, pre)
        (dpre,) = vjp(da_ref[...])
        nxt = next_ref[...]
        pre_n = _conv_pre(nxt, cur[CONV_TILE - SUBLANE:], w, b_)
        _, vjp_n = jax.vjp(_silu, pre_n)
        (dpre_n,) = vjp_n(dan_ref[...])
        dpre_n = jnp.where(i < nrow - 1, dpre_n, 0.0)
        du = jnp.zeros_like(cur)
        dws = []
        for k in range(CONV_W):
            wk = w[CONV_W - 1 - k:CONV_W - k, :]
            du = du + wk * _shift_up(dpre, dpre_n, k)
            dws.append(jnp.sum(dpre * _shift_down(cur, prev8, k), axis=0, keepdims=True))
        du_ref[...] = du.astype(du_ref.dtype)
        dw_tile = jnp.concatenate([dws[3], dws[2], dws[1], dws[0]] + [jnp.zeros_like(dws[0])] * 4, axis=0)

        @pl.when(i == 0)
        def _():
            dw_ref[...] = jnp.zeros_like(dw_ref)
            db_ref[...] = jnp.zeros_like(db_ref)

        dw_ref[...] += dw_tile
        db_ref[...] += jnp.sum(dpre, axis=0, keepdims=True)

    return pl.pallas_call(
        body,
        name=name,
        grid=(CONV_DIM // CONV_BLK, nrow),
        in_specs=[
            pl.BlockSpec((CONV_TILE, CONV_BLK), lambda j, i: (i, j + XBC_COL0)),
            pl.BlockSpec((SUBLANE, CONV_BLK), lambda j, i: (jnp.maximum(i * r8 - 1, 0), j + XBC_COL0)),
            pl.BlockSpec((SUBLANE, CONV_BLK), lambda j, i: (jnp.minimum((i + 1) * r8, last8), j + XBC_COL0)),
            pl.BlockSpec((CONV_TILE, CONV_BLK), lambda j, i: (i, j)),
            pl.BlockSpec((SUBLANE, CONV_BLK), lambda j, i: (jnp.minimum((i + 1) * r8, last8), j)),
            pl.BlockSpec((SUBLANE, CONV_BLK), lambda j, i: (0, j)),
            pl.BlockSpec((1, CONV_BLK), lambda j, i: (0, j)),
        ],
        out_specs=[
            pl.BlockSpec((CONV_TILE, CONV_BLK), lambda j, i: (i, j)),
            pl.BlockSpec((SUBLANE, CONV_BLK), lambda j, i: (0, j)),
            pl.BlockSpec((1, CONV_BLK), lambda j, i: (0, j)),
        ],
        out_shape=[
            jax.ShapeDtypeStruct((L, CONV_DIM), BF16),
            jax.ShapeDtypeStruct((SUBLANE, CONV_DIM), F32),
            jax.ShapeDtypeStruct((1, CONV_DIM), F32),
        ],
        compiler_params=_cparams(("arbitrary", "arbitrary")),
    )(zx, zx, zx, dact, dact, w8, b)


HI = lax.Precision.HIGHEST
XS_COL0 = 0
B_COL0 = D_INNER // D_STATE
C_COL0 = B_COL0 + SSM_GROUPS
DT_COL = (D_INNER + CONV_DIM) // DT_PAD


def _dot(a, b, dims=(((1,), (0,)), ((), ())), precision=None):
    return lax.dot_general(a, b, dims, precision=precision, preferred_element_type=F32)


_DOT_DIMS = {
    "nn": (((1,), (0,)), ((), ())),
    "nt": (((1,), (1,)), ((), ())),
    "tn": (((0,), (0,)), ((), ())),
}


@functools.partial(jax.custom_vjp, nondiff_argnums=(2,))
def _bdot(a, b, mode):
    return _dot(a.astype(BF16), b.astype(BF16), _DOT_DIMS[mode])


def _bdot_fwd(a, b, mode):
    return _bdot(a, b, mode), (a, b)


def _bdot_bwd(mode, res, g):
    a, b = res
    if mode == "nn":
        return _bdot(g, b, "nt"), _bdot(a, g, "tn")
    if mode == "nt":
        return _bdot(g, b, "nn"), _bdot(g, a, "tn")
    return _bdot(b, g, "nt"), _bdot(a, g, "nn")


_bdot.defvjp(_bdot_fwd, _bdot_bwd)


def _softplus(x):
    return jnp.maximum(x, 0.0) + jnp.log(1.0 + jnp.exp(-jnp.abs(x)))


def _ssd_consts():
    l_ = lax.broadcasted_iota(jnp.int32, (CHUNK, GROUP_W), 0)
    c_ = lax.broadcasted_iota(jnp.int32, (CHUNK, GROUP_W), 1)
    s_ = c_ % CHUNK
    causal = s_ <= l_
    eye_t = (s_ == l_).astype(F32)
    r0 = lax.broadcasted_iota(jnp.int32, (CHUNK, CHUNK), 0)
    c0 = lax.broadcasted_iota(jnp.int32, (CHUNK, CHUNK), 1)
    tril = (c0 <= r0).astype(F32)
    rb = lax.broadcasted_iota(jnp.int32, (GROUP_W, GROUP_W), 0) // HEAD_P
    cb = lax.broadcasted_iota(jnp.int32, (GROUP_W, GROUP_W), 1) // HEAD_P
    blockdiag = rb == cb
    return causal, eye_t, tril, blockdiag


def _ssd_expand(g):
    h = lax.broadcasted_iota(jnp.int32, (DT_PAD, GROUP_W), 0)
    c = lax.broadcasted_iota(jnp.int32, (DT_PAD, GROUP_W), 1)
    return (h == g * (GROUP_W // HEAD_P) + c // HEAD_P).astype(F32)


def _ssd_chunk(S, xs, dt_raw, dt_bias, alog_e, d_e, Bm, Cm, E):
    causal, eye_t, tril, blockdiag = _ssd_consts()
    dt = _softplus(dt_raw + dt_bias)
    dtx = _dot(dt, E, precision=HI)
    a = dtx * (-jnp.exp(alog_e))
    acs = _dot(tril, a, precision=HI)
    rowv = _dot(jnp.ones((CHUNK, CHUNK), F32), acs * eye_t, precision=HI)
    seg = acs - rowv
    Lc = jnp.where(causal, jnp.exp(jnp.where(causal, seg, 0.0)), 0.0)
    xdt = xs * dtx
    Bt = jnp.concatenate([Bm] * 4, axis=0)
    CBc = _bdot(Cm, Bt, "nt")
    Xbd = jnp.where(blockdiag, jnp.concatenate([xdt] * 4, axis=0), 0.0)
    y_intra = _bdot(CBc * Lc, Xbd, "nn")
    y_inter = _bdot(Cm, S, "nn") * jnp.exp(acs)
    y = y_intra + y_inter + d_e * xs
    last = jnp.sum(a, axis=0, keepdims=True)
    dec_end = jnp.exp(last - acs)
    S_new = S * jnp.exp(last) + _bdot(Bm, xdt * dec_end, "tn")
    return S_new, y


def _ssd_in_specs(cmap):
    return [
        pl.BlockSpec((CHUNK, GROUP_W), lambda c, g: (cmap(c), g)),
        pl.BlockSpec((CHUNK, DT_PAD), lambda c, g: (cmap(c), DT_COL)),
        pl.BlockSpec((1, DT_PAD), lambda c, g: (0, 0)),
        pl.BlockSpec((1, GROUP_W), lambda c, g: (0, g)),
        pl.BlockSpec((1, GROUP_W), lambda c, g: (0, g)),
        pl.BlockSpec((CHUNK, D_STATE), lambda c, g: (cmap(c), B_COL0 + g)),
        pl.BlockSpec((CHUNK, D_STATE), lambda c, g: (cmap(c), C_COL0 + g)),
    ]


def _ssd_fwd(xbc, zx, dt_bias, alog_e, d_e, name):
    L = xbc.shape[0]
    nc = L // CHUNK

    def body(xs_ref, dtr_ref, bias_ref, alog_ref, d_ref, b_ref, c_ref, y_ref, st_ref, S_ref):
        c = pl.program_id(0)
        g = pl.program_id(1)

        @pl.when(c == 0)
        def _():
            S_ref[g] = jnp.zeros((D_STATE, GROUP_W), F32)

        S = S_ref[g]
        st_ref[...] = S
        S_new, y = _ssd_chunk(S, xs_ref[...], dtr_ref[...], bias_ref[...], alog_ref[...], d_ref[...],
                              b_ref[...], c_ref[...], _ssd_expand(g))
        y_ref[...] = y
        S_ref[g] = S_new

    return pl.pallas_call(
        body,
        name=name,
        grid=(nc, SSM_GROUPS),
        in_specs=_ssd_in_specs(lambda c: c),
        out_specs=[
            pl.BlockSpec((CHUNK, GROUP_W), lambda c, g: (c, g)),
            pl.BlockSpec((None, None, D_STATE, GROUP_W), lambda c, g: (c, g, 0, 0)),
        ],
        out_shape=[
            jax.ShapeDtypeStruct((L, D_INNER), F32),
            jax.ShapeDtypeStruct((nc, SSM_GROUPS, D_STATE, GROUP_W), F32),
        ],
        scratch_shapes=[pltpu.VMEM((SSM_GROUPS, D_STATE, GROUP_W), F32)],
        compiler_params=_cparams(("arbitrary", "arbitrary")),
    )(xbc, zx, dt_bias, alog_e, d_e, xbc, xbc)


def _ssd_bwd(xbc, zx, dt_bias, alog_e, d_e, states, dy, name):
    L = xbc.shape[0]
    nc = L // CHUNK
    rev = lambda c: nc - 1 - c

    def body(xs_ref, dtr_ref, bias_ref, alog_ref, d_ref, b_ref, c_ref, st_ref, dy_ref,
             dxs_ref, db_ref, dc_ref, ddt_ref, dbias_ref, dalog_ref, dd_ref, dS_ref):
        c = pl.program_id(0)
        g = pl.program_id(1)
        first = jnp.logical_and(c == 0, g == 0)

        @pl.when(c == 0)
        def _():
            dS_ref[g] = jnp.zeros((D_STATE, GROUP_W), F32)
            dalog_ref[g] = jnp.zeros((1, GROUP_W), F32)
            dd_ref[g] = jnp.zeros((1, GROUP_W), F32)

        @pl.when(first)
        def _():
            dbias_ref[...] = jnp.zeros_like(dbias_ref)

        E = _ssd_expand(g)
        _, vjp = jax.vjp(lambda S, xs, dtr, bias, alog, dsk, Bm, Cm: _ssd_chunk(S, xs, dtr, bias, alog, dsk, Bm, Cm, E),
                         st_ref[...], xs_ref[...], dtr_ref[...], bias_ref[...], alog_ref[...], d_ref[...],
                         b_ref[...], c_ref[...])
        dS, dxs, ddtr, dbias, dalog, dd, dB, dC = vjp((dS_ref[g], dy_ref[...]))
        dS_ref[g] = dS
        dxs_ref[...] = dxs
        db_ref[...] = dB
        dc_ref[...] = dC

        @pl.when(g == 0)
        def _():
            ddt_ref[...] = jnp.zeros_like(ddt_ref)

        ddt_ref[...] += ddtr
        dbias_ref[...] += dbias
        dalog_ref[g] += dalog
        dd_ref[g] += dd

    whole3 = pl.BlockSpec((SSM_GROUPS, 1, GROUP_W), lambda c, g: (0, 0, 0))
    return pl.pallas_call(
        body,
        name=name,
        grid=(nc, SSM_GROUPS),
        in_specs=_ssd_in_specs(rev) + [
            pl.BlockSpec((None, None, D_STATE, GROUP_W), lambda c, g: (rev(c), g, 0, 0)),
            pl.BlockSpec((CHUNK, GROUP_W), lambda c, g: (rev(c), g)),
        ],
        out_specs=[
            pl.BlockSpec((CHUNK, GROUP_W), lambda c, g: (rev(c), g)),
            pl.BlockSpec((CHUNK, D_STATE), lambda c, g: (rev(c), g)),
            pl.BlockSpec((CHUNK, D_STATE), lambda c, g: (rev(c), g)),
            pl.BlockSpec((CHUNK, DT_PAD), lambda c, g: (rev(c), 0)),
            pl.BlockSpec((1, DT_PAD), lambda c, g: (0, 0)),
            whole3,
            whole3,
        ],
        out_shape=[
            jax.ShapeDtypeStruct((L, D_INNER), F32),
            jax.ShapeDtypeStruct((L, SSM_GROUPS * D_STATE), F32),
            jax.ShapeDtypeStruct((L, SSM_GROUPS * D_STATE), F32),
            jax.ShapeDtypeStruct((L, DT_PAD), F32),
            jax.ShapeDtypeStruct((1, DT_PAD), F32),
            jax.ShapeDtypeStruct((SSM_GROUPS, 1, GROUP_W), F32),
            jax.ShapeDtypeStruct((SSM_GROUPS, 1, GROUP_W), F32),
        ],
        scratch_shapes=[pltpu.VMEM((SSM_GROUPS, D_STATE, GROUP_W), F32)],
        compiler_params=_cparams(("arbitrary", "arbitrary")),
    )(xbc, zx, dt_bias, alog_e, d_e, xbc, xbc, states, dy)


SB_T = 128


def _split3_dot(x, u):
    x1 = x.astype(BF16)
    r1 = x - x1.astype(F32)
    x2 = r1.astype(BF16)
    x3 = (r1 - x2.astype(F32)).astype(BF16)
    T = x.shape[0]
    out = _dot(jnp.concatenate([x1, x2, x3], axis=0), u)
    return out[:T] + out[T:2 * T] + out[2 * T:]


def _sb_scores(q, kj):
    z = _dot(q, kj, dims=(((1,), (1,)), ((), ())))
    l1p = jnp.log(1.0 + jnp.exp(-jnp.abs(z)))
    return jnp.minimum(z, 0.0) - l1p, jnp.minimum(-z, 0.0) - l1p


def _sb_fwd(qs, kn, v, name):
    H, L, d = qs.shape
    nq = L // SB_T

    def body(q_ref, k_ref, v_ref, o_ref):
        i = pl.program_id(1)
        q = q_ref[...]
        row = lax.broadcasted_iota(jnp.int32, (SB_T, SB_T), 0)
        col = lax.broadcasted_iota(jnp.int32, (SB_T, SB_T), 1)
        after = (row > col).astype(BF16)
        tri = col < row

        def block(j, R, acc, diag):
            start = pl.multiple_of(j * SB_T, SB_T)
            kj = k_ref[pl.ds(start, SB_T), :]
            vj = v_ref[pl.ds(start, SB_T), :]
            lb, lk = _sb_scores(q, kj)
            if diag:
                lk = jnp.where(tri, lk, 0.0)
            w = jnp.exp(lb + _split3_dot(lk, after) + R)
            if diag:
                w = jnp.where(tri, w, 0.0)
            acc = acc + _dot(w.astype(BF16), vj)
            return R + jnp.sum(lk, axis=1, keepdims=True), acc

        R, acc = block(i, jnp.zeros((SB_T, 1), F32), jnp.zeros((SB_T, d), F32), True)

        def loop(n, carry):
            return block(i - 1 - n, carry[0], carry[1], False)

        R, acc = lax.fori_loop(0, i, loop, (R, acc))
        o_ref[...] = acc

    return pl.pallas_call(
        body,
        name=name,
        grid=(H, nq),
        in_specs=[
            pl.BlockSpec((None, SB_T, d), lambda h, i: (h, i, 0)),
            pl.BlockSpec((None, L, d), lambda h, i: (h, 0, 0)),
            pl.BlockSpec((None, L, d), lambda h, i: (h, 0, 0)),
        ],
        out_specs=pl.BlockSpec((None, SB_T, d), lambda h, i: (h, i, 0)),
        out_shape=jax.ShapeDtypeStruct((H, L, d), F32),
        compiler_params=_cparams(("parallel", "arbitrary")),
    )(qs, kn, v)


def _sb_bwd(qs, kn, v, o, do, name):
    H, L, d = qs.shape
    nq = L // SB_T

    def body(q_ref, k_ref, v_ref, o_ref, do_ref, dq_ref, dk_ref, dv_ref):
        i = pl.program_id(1)

        @pl.when(i == 0)
        def _():
            dk_ref[...] = jnp.zeros_like(dk_ref)
            dv_ref[...] = jnp.zeros_like(dv_ref)

        q = q_ref[...]
        do16 = do_ref[...].astype(BF16)
        D = jnp.sum(o_ref[...] * do16.astype(F32), axis=1, keepdims=True)
        row = lax.broadcasted_iota(jnp.int32, (SB_T, SB_T), 0)
        col = lax.broadcasted_iota(jnp.int32, (SB_T, SB_T), 1)
        after = (row > col).astype(BF16)
        from_s = (row >= col).astype(BF16)
        tri = col < row

        def block(j, R, Gs, dq, diag):
            start = pl.multiple_of(j * SB_T, SB_T)
            kj = k_ref[pl.ds(start, SB_T), :]
            vj = v_ref[pl.ds(start, SB_T), :]
            lb, lk = _sb_scores(q, kj)
            if diag:
                lk = jnp.where(tri, lk, 0.0)
            w = jnp.exp(lb + _split3_dot(lk, after) + R)
            if diag:
                w = jnp.where(tri, w, 0.0)
            w16 = w.astype(BF16)
            dw = _dot(do16, vj, dims=(((1,), (1,)), ((), ())))
            g = w16.astype(F32) * dw
            dv_ref[pl.ds(start, SB_T), :] += _dot(w16, do16, dims=(((0,), (0,)), ((), ())))
            G = D - (Gs + _split3_dot(g, from_s))
            sig = jnp.exp(lb)
            dz = g * (1.0 - sig) - G * sig
            if diag:
                dz = jnp.where(tri, dz, 0.0)
            dz16 = dz.astype(BF16)
            dq = dq + _dot(dz16, kj)
            dk_ref[pl.ds(start, SB_T), :] += _dot(dz16, q, dims=(((0,), (0,)), ((), ())))
            return (R + jnp.sum(lk, axis=1, keepdims=True), Gs + jnp.sum(g, axis=1, keepdims=True), dq)

        zero1 = jnp.zeros((SB_T, 1), F32)
        R, Gs, dq = block(i, zero1, zero1, jnp.zeros((SB_T, d), F32), True)

        def loop(n, carry):
            return block(i - 1 - n, carry[0], carry[1], carry[2], False)

        R, Gs, dq = lax.fori_loop(0, i, loop, (R, Gs, dq))
        dq_ref[...] = dq

    blk = pl.BlockSpec((None, SB_T, d), lambda h, i: (h, i, 0))
    full = pl.BlockSpec((None, L, d), lambda h, i: (h, 0, 0))
    return pl.pallas_call(
        body,
        name=name,
        grid=(H, nq),
        in_specs=[blk, full, full, blk, blk],
        out_specs=[blk, full, full],
        out_shape=[jax.ShapeDtypeStruct((H, L, d), F32)] * 3,
        compiler_params=_cparams(("parallel", "arbitrary")),
    )(qs, kn, v, o, do)


def _fold_heads(rows, name):
    def body(x_ref, o_ref):
        c = lax.broadcasted_iota(jnp.int32, (D_INNER, DT_PAD), 0)
        h = lax.broadcasted_iota(jnp.int32, (D_INNER, DT_PAD), 1)
        o_ref[...] = _dot(x_ref[...], (c // HEAD_P == h).astype(F32), precision=HI)

    return pl.pallas_call(
        body, name=name, out_shape=jax.ShapeDtypeStruct((SUBLANE, DT_PAD), F32),
        compiler_params=_cparams(),
    )(rows)


def _ffn_fwd(x, r, norm_w, w_in, w_out, tag):
    x1, h = _norm_fwd(x, r, norm_w, f"norm_ffn_fwd{tag}")
    gu = _mm(h, w_in, "nn", F32, f"ffn_in_fwd{tag}")
    act = _swiglu_fwd(gu, f"swiglu_fwd{tag}")
    f = _mm(act, w_out, "nn", F32, f"ffn_out_fwd{tag}")
    return x1, h, gu, act, f


def _ffn_bwd(x1, h, gu, act, norm_w, w_in, w_out, dres, tag):
    d16 = dres.astype(BF16)
    dact = _mm(d16, w_out, "nt", BF16, f"ffn_out_dx{tag}")
    g_w_out = _mm(act, d16, "tn", F32, f"ffn_out_dw{tag}")
    dgu = _swiglu_bwd(gu, dact, f"swiglu_bwd{tag}")
    dh = _mm(dgu, w_in, "nt", BF16, f"ffn_in_dx{tag}")
    g_w_in = _mm(h, dgu, "tn", F32, f"ffn_in_dw{tag}")
    dres, g_norm = _norm_bwd(x1, norm_w, dh, dres, f"norm_ffn_bwd{tag}")
    return dres, g_norm, g_w_in, g_w_out


def _local_step(x, target, W):
    L = x.shape[0]
    H = SB_HEADS
    nm0, nm1 = W["norm_mix"][0:1], W["norm_mix"][1:2]
    nf0, nf1 = W["norm_ffn"][0:1], W["norm_ffn"][1:2]

    _, h0 = _norm_fwd(x, None, nm0, "norm_mix_fwd0")
    zx = _mm(h0, W["ssm_w_in"], "nn", F32, "ssm_in_fwd")
    xbc = _conv_fwd(zx, W["conv_w8"], W["conv_b"], "conv_fwd")
    y, states = _ssd_fwd(xbc, zx, W["dt_bias"], W["alog_e"], W["d_e"], "ssd_fwd")
    yn = _gated_norm_fwd(y, zx, W["ssm_norm_w"], "gated_norm_fwd")
    mix0 = _mm(yn, W["ssm_w_out"], "nn", F32, "ssm_out_fwd")
    x1, h1, gu0, act0, f0 = _ffn_fwd(x, mix0, nf0, W["ffn_w_in"][0], W["ffn_w_out"][0], "0")

    x2, h2 = _norm_fwd(x1, f0, nm1, "norm_mix_fwd1")
    qkv = _mm(h2, W["sb_w_qkv"], "nn", F32, "sb_qkv_fwd")
    qkv_t = qkv.reshape(L, 3, H, SB_D).transpose(1, 2, 0, 3)
    q_r = qkv_t[0].reshape(H * L, SB_D)
    k_r = qkv_t[1].reshape(H * L, SB_D)
    qs = _qk_norm_fwd(q_r, W["sb_q_gain"], SB_D ** -0.5, "q_norm_fwd").reshape(H, L, SB_D)
    kn = _qk_norm_fwd(k_r, W["sb_k_gain"], 1.0, "k_norm_fwd").reshape(H, L, SB_D)
    vb = qkv_t[2].astype(BF16)
    o = _sb_fwd(qs, kn, vb, "sb_fwd")
    o_flat = o.transpose(1, 0, 2).reshape(L, D_MODEL).astype(BF16)
    mix1 = _mm(o_flat, W["sb_w_o"], "nn", F32, "sb_o_fwd")
    x3, h3, gu1, act1, f1 = _ffn_fwd(x2, mix1, nf1, W["ffn_w_in"][1], W["ffn_w_out"][1], "1")

    dres, sq = _loss_bwd(x3, f1, target, "loss")

    dres, g_nf1, g_fin1, g_fout1 = _ffn_bwd(x3, h3, gu1, act1, nf1, W["ffn_w_in"][1], W["ffn_w_out"][1], dres, "1")
    d16 = dres.astype(BF16)
    do_flat = _mm(d16, W["sb_w_o"], "nt", F32, "sb_o_dx")
    g_w_o = _mm(o_flat, d16, "tn", F32, "sb_o_dw")
    do = do_flat.reshape(L, H, SB_D).transpose(1, 0, 2)
    dqs, dkn, dv = _sb_bwd(qs, kn, vb, o, do, "sb_bwd")
    dq, g_qg = _qk_norm_bwd(q_r, W["sb_q_gain"], dqs.reshape(H * L, SB_D), SB_D ** -0.5, "q_norm_bwd")
    dk, g_kg = _qk_norm_bwd(k_r, W["sb_k_gain"], dkn.reshape(H * L, SB_D), 1.0, "k_norm_bwd")
    dqkv = jnp.stack([dq.reshape(H, L, SB_D), dk.reshape(H, L, SB_D), dv])
    dqkv = dqkv.transpose(2, 0, 1, 3).reshape(L, 3 * D_MODEL).astype(BF16)
    dh2 = _mm(dqkv, W["sb_w_qkv"], "nt", BF16, "sb_qkv_dx")
    g_w_qkv = _mm(h2, dqkv, "tn", F32, "sb_qkv_dw")
    dres, g_nm1 = _norm_bwd(x2, nm1, dh2, dres, "norm_mix_bwd1")

    dres, g_nf0, g_fin0, g_fout0 = _ffn_bwd(x1, h1, gu0, act0, nf0, W["ffn_w_in"][0], W["ffn_w_out"][0], dres, "0")
    d16 = dres.astype(BF16)
    dyn = _mm(d16, W["ssm_w_out"], "nt", BF16, "ssm_out_dx")
    g_w_out = _mm(yn, d16, "tn", F32, "ssm_out_dw")
    dy, dz, g_snw = _gated_norm_bwd(y, zx, W["ssm_norm_w"], dyn, "gated_norm_bwd")
    dxs, dB, dC, ddt, g_dtb, g_alog_e, g_d_e = _ssd_bwd(
        xbc, zx, W["dt_bias"], W["alog_e"], W["d_e"], states, dy, "ssd_bwd")
    dact = jnp.concatenate([dxs, dB, dC], axis=1)
    dxbc, g_cw8, g_cb = _conv_bwd(zx, W["conv_w8"], W["conv_b"], dact, "conv_bwd")
    dzx = jnp.concatenate([dz, dxbc, ddt.astype(BF16)], axis=1)
    dh0 = _mm(dzx, W["ssm_w_in"], "nt", BF16, "ssm_in_dx")
    g_w_in = _mm(h0, dzx, "tn", F32, "ssm_in_dw")
    grad_x, g_nm0 = _norm_bwd(x, nm0, dh0, dres, "norm_mix_bwd0")

    per_head = jnp.concatenate(
        [g_alog_e.reshape(1, D_INNER), g_d_e.reshape(1, D_INNER), jnp.zeros((SUBLANE - 2, D_INNER), F32)], axis=0)
    folded = _fold_heads(per_head, "fold_heads")
    grads = {
        "norm_mix": jnp.concatenate([g_nm0, g_nm1], axis=0),
        "norm_ffn": jnp.concatenate([g_nf0, g_nf1], axis=0),
        "ssm_w_in": g_w_in[:, :D_IN_PROJ],
        "ssm_conv_w": g_cw8[:CONV_W],
        "ssm_conv_b": g_cb,
        "ssm_dt_bias": g_dtb[:, :SSM_HEADS],
        "ssm_a_log": folded[0:1, :SSM_HEADS],
        "ssm_d": folded[1:2, :SSM_HEADS],
        "ssm_norm_w": g_snw,
        "ssm_w_out": g_w_out,
        "sb_w_qkv": g_w_qkv,
        "sb_q_gain": g_qg,
        "sb_k_gain": g_kg,
        "sb_w_o": g_w_o,
        "ffn_w_in": jnp.stack([g_fin0, g_fin1]),
        "ffn_w_out": jnp.stack([g_fout0, g_fout1]),
    }
    return sq, grad_x, grads


def _prep_weights(full):
    W = {}
    W["norm_mix"] = full["norm_mix"]
    W["norm_ffn"] = full["norm_ffn"]
    w_in = full["ssm_w_in"]
    W["ssm_w_in"] = jnp.pad(w_in, ((0, 0), (0, D_IN_PAD - D_IN_PROJ))).astype(BF16)
    W["conv_w8"] = jnp.pad(full["ssm_conv_w"], ((0, SUBLANE - CONV_W), (0, 0)))
    W["conv_b"] = full["ssm_conv_b"]
    W["dt_bias"] = jnp.pad(full["ssm_dt_bias"], ((0, 0), (0, DT_PAD - SSM_HEADS)))
    W["alog_e"] = jnp.repeat(full["ssm_a_log"], HEAD_P, axis=1)
    W["d_e"] = jnp.repeat(full["ssm_d"], HEAD_P, axis=1)
    W["ssm_norm_w"] = full["ssm_norm_w"]
    W["ssm_w_out"] = full["ssm_w_out"].astype(BF16)
    W["sb_w_qkv"] = full["sb_w_qkv"].astype(BF16)
    W["sb_q_gain"] = full["sb_q_gain"]
    W["sb_k_gain"] = full["sb_k_gain"]
    W["sb_w_o"] = full["sb_w_o"].astype(BF16)
    W["ffn_w_in"] = full["ffn_w_in"].astype(BF16)
    W["ffn_w_out"] = full["ffn_w_out"].astype(BF16)
    return W


N_CHIPS = 4
N_DEV = 8
FLAT_W = D_MODEL
SHARDED = (
    ("ssm_w_in", (1, D_MODEL, D_IN_PROJ // N_CHIPS), 2),
    ("ssm_conv_w", (1, CONV_W, CONV_DIM // N_CHIPS), 2),
    ("ssm_w_out", (1, D_INNER // N_CHIPS, D_MODEL), 1),
    ("sb_w_qkv", (1, D_MODEL, 3 * D_MODEL // N_CHIPS), 2),
    ("sb_w_o", (1, D_MODEL // N_CHIPS, D_MODEL), 1),
    ("ffn_w_in", (2, D_MODEL, 2 * D_FF // N_CHIPS), 2),
    ("ffn_w_out", (2, D_FF // N_CHIPS, D_MODEL), 1),
)
REPLICATED = (
    ("norm_mix", (2, D_MODEL)), ("norm_ffn", (2, D_MODEL)), ("ssm_conv_b", (1, CONV_DIM)),
    ("ssm_norm_w", (1, D_INNER)), ("ssm_dt_bias", (1, SSM_HEADS)), ("ssm_a_log", (1, SSM_HEADS)),
    ("ssm_d", (1, SSM_HEADS)), ("sb_q_gain", (1, SB_D)), ("sb_k_gain", (1, SB_D)),
)
ADAM_TILE = 256
_USED_ROWS = sum(math.prod(s) for _, s, _ in SHARDED) // FLAT_W
FLAT_ROWS = -(-_USED_ROWS // ADAM_TILE) * ADAM_TILE
_USED_SMALL = sum(math.prod(s) for _, s in REPLICATED)
SMALL_ROWS = -(-_USED_SMALL // (SUBLANE * FLAT_W)) * SUBLANE


def _pack_shard(d):
    parts = [d[n].reshape(-1, FLAT_W) for n, _, _ in SHARDED]
    parts.append(jnp.zeros((FLAT_ROWS - _USED_ROWS, FLAT_W), parts[0].dtype))
    return jnp.concatenate(parts, axis=0)


def _unpack_shard(flat):
    out, r = {}, 0
    for n, s, _ in SHARDED:
        k = math.prod(s) // FLAT_W
        out[n] = flat[r:r + k].reshape(s)
        r += k
    return out


def _pack_small(d):
    parts = [d[n].reshape(-1) for n, _ in REPLICATED]
    parts.append(jnp.zeros((SMALL_ROWS * FLAT_W - _USED_SMALL,), parts[0].dtype))
    return jnp.concatenate(parts).reshape(SMALL_ROWS, FLAT_W)


def _unpack_small(flat):
    flat = flat.reshape(-1)
    out, r = {}, 0
    for n, s in REPLICATED:
        k = math.prod(s)
        out[n] = flat[r:r + k].reshape(s)
        r += k
    return out


ANY = pl.BlockSpec(memory_space=pl.ANY)


def _other_chips():
    x, y, c = lax.axis_index("x"), lax.axis_index("y"), lax.axis_index("c")
    return x, y, c, [(1 - x, y), (x, 1 - y), (1 - x, 1 - y)]


def _remote(src, dst, send_sem, recv_sem, dev):
    return pltpu.make_async_remote_copy(src_ref=src, dst_ref=dst, send_sem=send_sem, recv_sem=recv_sem,
                                        device_id=dev, device_id_type=MESH_T)


def _gather_weights(wb, ws):
    def body(wb_ref, ws_ref, gb_ref, gs_ref, send_sems, recv_sems, loc_sems):
        x, y, c, others = _other_chips()
        me = 2 * x + y
        pairs = ((wb_ref, gb_ref), (ws_ref, gs_ref))
        local = [pltpu.make_async_copy(src, dst.at[me], loc_sems.at[t]) for t, (src, dst) in enumerate(pairs)]
        for cp in local:
            cp.start()
        sends = []
        for k, (px, py) in enumerate(others):
            for t, (src, dst) in enumerate(pairs):
                cp = _remote(src, dst.at[me], send_sems.at[2 * k + t], recv_sems.at[2 * k + t], (px, py, c))
                cp.start()
                sends.append(cp)
        for k, (px, py) in enumerate(others):
            for t, (src, dst) in enumerate(pairs):
                _remote(src, dst.at[2 * px + py], send_sems.at[2 * k + t], recv_sems.at[2 * k + t],
                        (px, py, c)).wait_recv()
        for cp in sends:
            cp.wait_send()
        for cp in local:
            cp.wait()

    return pl.pallas_call(
        body, name="gather_weights",
        in_specs=[ANY, ANY], out_specs=[ANY, ANY],
        out_shape=[jax.ShapeDtypeStruct((N_CHIPS,) + wb.shape, wb.dtype),
                   jax.ShapeDtypeStruct((N_CHIPS,) + ws.shape, ws.dtype)],
        scratch_shapes=[pltpu.SemaphoreType.DMA((6,)), pltpu.SemaphoreType.DMA((6,)), pltpu.SemaphoreType.DMA((2,))],
    )(wb, ws)


def _scatter_grads(gpack):
    def body(g_ref, r_ref, send_sems, recv_sems):
        x, y, c, others = _other_chips()
        sends = []
        for k, (px, py) in enumerate(others):
            cp = _remote(g_ref.at[2 * px + py], r_ref.at[k], send_sems.at[k], recv_sems.at[k], (px, py, c))
            cp.start()
            sends.append(cp)
        for cp in sends:
            cp.wait_recv()
        for cp in sends:
            cp.wait_send()

    return pl.pallas_call(
        body, name="scatter_grads",
        in_specs=[ANY], out_specs=ANY,
        out_shape=jax.ShapeDtypeStruct((N_CHIPS - 1,) + gpack.shape[1:], gpack.dtype),
        scratch_shapes=[pltpu.SemaphoreType.DMA((3,)), pltpu.SemaphoreType.DMA((3,))],
    )(gpack)


def _exchange_partials(part, small):
    def body(p_ref, s_ref, ps_ref, sa_ref, send_sems, recv_sems, loc_sem):
        x, y, c = lax.axis_index("x"), lax.axis_index("y"), lax.axis_index("c")
        me = 4 * x + 2 * y + c
        local = pltpu.make_async_copy(s_ref, sa_ref.at[me], loc_sem)
        local.start()
        sib = _remote(p_ref, ps_ref, send_sems.at[0], recv_sems.at[0], (x, y, 1 - c))
        sib.start()
        sends = [sib]
        for k in range(1, N_DEV):
            fx, fy, fc = (k >> 2) & 1, (k >> 1) & 1, k & 1
            px, py, pc = x ^ fx, y ^ fy, c ^ fc
            cp = _remote(s_ref, sa_ref.at[me], send_sems.at[k], recv_sems.at[k], (px, py, pc))
            cp.start()
            sends.append(cp)
        sib.wait_recv()
        for k in range(1, N_DEV):
            fx, fy, fc = (k >> 2) & 1, (k >> 1) & 1, k & 1
            px, py, pc = x ^ fx, y ^ fy, c ^ fc
            _remote(s_ref, sa_ref.at[4 * px + 2 * py + pc], send_sems.at[k], recv_sems.at[k], (px, py, pc)).wait_recv()
        for cp in sends:
            cp.wait_send()
        local.wait()

    return pl.pallas_call(
        body, name="exchange_partials",
        in_specs=[ANY, ANY], out_specs=[ANY, ANY],
        out_shape=[jax.ShapeDtypeStruct(part.shape, part.dtype),
                   jax.ShapeDtypeStruct((N_DEV,) + small.shape, small.dtype)],
        scratch_shapes=[pltpu.SemaphoreType.DMA((N_DEV,)), pltpu.SemaphoreType.DMA((N_DEV,)), pltpu.SemaphoreType.DMA],
    )(part, small)


def _partial_sum(own, recv):
    R = own.shape[0]

    def body(o_ref, r_ref, p_ref):
        acc = o_ref[...]
        for k in range(N_CHIPS - 1):
            acc = acc + r_ref[k].astype(F32)
        p_ref[...] = acc

    return pl.pallas_call(
        body, name="grad_partial_sum", grid=(R // ADAM_TILE,),
        in_specs=[pl.BlockSpec((ADAM_TILE, FLAT_W), lambda i: (i, 0)),
                  pl.BlockSpec((N_CHIPS - 1, ADAM_TILE, FLAT_W), lambda i: (0, i, 0))],
        out_specs=pl.BlockSpec((ADAM_TILE, FLAT_W), lambda i: (i, 0)),
        out_shape=jax.ShapeDtypeStruct(own.shape, F32),
        compiler_params=_cparams(("parallel",)),
    )(own, recv)


def _adamw_math(w, g, m, v):
    m = ADAM_B1 * m + (1.0 - ADAM_B1) * g
    v = ADAM_B2 * v + (1.0 - ADAM_B2) * jnp.square(g)
    m_hat = m / (1.0 - ADAM_B1 ** ADAM_STEP)
    v_hat = v / (1.0 - ADAM_B2 ** ADAM_STEP)
    delta = -ADAM_LR * (m_hat / (jnp.sqrt(v_hat) + ADAM_EPS) + ADAM_WD * w)
    return delta, m, v


def _adamw(w, m, v, parts, tile, name):
    n, R, _ = parts.shape

    def body(w_ref, m_ref, v_ref, p_ref, g_ref, d_ref, nm_ref, nv_ref):
        g = p_ref[0]
        for k in range(1, n):
            g = g + p_ref[k]
        delta, nm, nv = _adamw_math(w_ref[...], g, m_ref[...], v_ref[...])
        g_ref[...] = g
        d_ref[...] = delta
        nm_ref[...] = nm
        nv_ref[...] = nv

    blk = pl.BlockSpec((tile, FLAT_W), lambda i: (i, 0))
    return pl.pallas_call(
        body, name=name, grid=(R // tile,),
        in_specs=[blk, blk, blk, pl.BlockSpec((n, tile, FLAT_W), lambda i: (0, i, 0))],
        out_specs=[blk] * 4,
        out_shape=[jax.ShapeDtypeStruct((R, FLAT_W), F32)] * 4,
        compiler_params=_cparams(("parallel",)),
    )(w, m, v, parts)


_NAMES = ("norm_mix", "norm_ffn", "ssm_w_in", "ssm_conv_w", "ssm_conv_b", "ssm_dt_bias", "ssm_a_log", "ssm_d",
          "ssm_norm_w", "ssm_w_out", "sb_w_qkv", "sb_q_gain", "sb_k_gain", "sb_w_o", "ffn_w_in", "ffn_w_out")


def _step(x, loss_target, w, m, v):
    cx, cy, cc = lax.axis_index("x"), lax.axis_index("y"), lax.axis_index("c")
    chip = 2 * cx + cy

    w_flat = _pack_shard(w)
    conv8 = jnp.pad(w["ssm_conv_w"][0], ((0, SUBLANE - CONV_W), (0, 0)))
    gb, gs = _gather_weights(w_flat.astype(BF16), conv8)
    shards = [_unpack_shard(gb[p]) for p in range(N_CHIPS)]
    full = {n: jnp.concatenate([s[n] for s in shards], axis=ax) for n, _, ax in SHARDED}
    full = {n: (a if n.startswith("ffn") else a[0]) for n, a in full.items()}
    full["ssm_conv_w"] = jnp.concatenate([gs[p, :CONV_W] for p in range(N_CHIPS)], axis=1)
    for n, _ in REPLICATED:
        full[n] = w[n]

    sq, grad_x, grads = _local_step(x[0], loss_target[0], _prep_weights(full))
    loss = lax.psum(0.5 * jnp.sum(sq) / D_MODEL, ("x", "y", "c"))

    blocks = []
    for s in range(N_CHIPS):
        d = {}
        for n, shape, ax in SHARDED:
            g = grads[n] if n.startswith("ffn") else grads[n][None]
            d[n] = lax.slice_in_dim(g, s * shape[ax], (s + 1) * shape[ax], axis=ax)
        blocks.append(_pack_shard(d))
    gpack = jnp.stack(blocks)
    own = lax.dynamic_index_in_dim(gpack, chip, axis=0, keepdims=False)
    recv = _scatter_grads(gpack.astype(BF16))
    part = _partial_sum(own, recv)
    part_sib, small_all = _exchange_partials(part, _pack_small(grads))
    parts = jnp.stack([part, part_sib])
    g_f, d_f, m_f, v_f = _adamw(w_flat, _pack_shard(m), _pack_shard(v), parts, ADAM_TILE, "adamw_sharded")
    g_s, d_s, m_s, v_s = _adamw(_pack_small(w), _pack_small(m), _pack_small(v), small_all, SUBLANE, "adamw_replicated")

    outs = []
    for big, small in ((g_f, g_s), (d_f, d_s), (m_f, m_s), (v_f, v_s)):
        d = {**_unpack_shard(big), **_unpack_small(small)}
        outs.append([d[n] for n in _NAMES])
    return loss, grad_x[None], outs


def kernel(x, norm_mix, norm_ffn, ssm_w_in, ssm_conv_w, ssm_conv_b, ssm_dt_bias, ssm_a_log, ssm_d, ssm_norm_w, ssm_w_out, sb_w_qkv, sb_q_gain, sb_k_gain, sb_w_o, ffn_w_in, ffn_w_out, loss_target, m_norm_mix, m_norm_ffn, m_ssm_w_in, m_ssm_conv_w, m_ssm_conv_b, m_ssm_dt_bias, m_ssm_a_log, m_ssm_d, m_ssm_norm_w, m_ssm_w_out, m_sb_w_qkv, m_sb_q_gain, m_sb_k_gain, m_sb_w_o, m_ffn_w_in, m_ffn_w_out, v_norm_mix, v_norm_ffn, v_ssm_w_in, v_ssm_conv_w, v_ssm_conv_b, v_ssm_dt_bias, v_ssm_a_log, v_ssm_d, v_ssm_norm_w, v_ssm_w_out, v_sb_w_qkv, v_sb_q_gain, v_sb_k_gain, v_sb_w_o, v_ffn_w_in, v_ffn_w_out):
    w = dict(zip(_NAMES, (norm_mix, norm_ffn, ssm_w_in, ssm_conv_w, ssm_conv_b, ssm_dt_bias, ssm_a_log, ssm_d,
                          ssm_norm_w, ssm_w_out, sb_w_qkv, sb_q_gain, sb_k_gain, sb_w_o, ffn_w_in, ffn_w_out)))
    m = dict(zip(_NAMES, (m_norm_mix, m_norm_ffn, m_ssm_w_in, m_ssm_conv_w, m_ssm_conv_b, m_ssm_dt_bias, m_ssm_a_log,
                          m_ssm_d, m_ssm_norm_w, m_ssm_w_out, m_sb_w_qkv, m_sb_q_gain, m_sb_k_gain, m_sb_w_o,
                          m_ffn_w_in, m_ffn_w_out)))
    v = dict(zip(_NAMES, (v_norm_mix, v_norm_ffn, v_ssm_w_in, v_ssm_conv_w, v_ssm_conv_b, v_ssm_dt_bias, v_ssm_a_log,
                          v_ssm_d, v_ssm_norm_w, v_ssm_w_out, v_sb_w_qkv, v_sb_q_gain, v_sb_k_gain, v_sb_w_o,
                          v_ffn_w_in, v_ffn_w_out)))
    loss, grad_x, (g, d, nm, nv) = _step(x, loss_target, w, m, v)
    return (loss, grad_x, *g, *d, *nm, *nv)
```

```python
import functools
import math

import jax
import jax.numpy as jnp
from jax import lax
from jax.experimental import pallas as pl
from jax.experimental.pallas import tpu as pltpu

F32 = jnp.float32
BF16 = jnp.bfloat16

VMEM_LIMIT_BYTES = 48 * 1024 * 1024
LANE = 128
SUBLANE = 8

D_MODEL = 1024
CHUNK = 64
D_INNER = 2048
SSM_HEADS = 32
SSM_GROUPS = 8
GROUP_W = D_INNER // SSM_GROUPS
HEAD_P = 64
D_STATE = 128
CONV_W = 4
CONV_DIM = D_INNER + 2 * SSM_GROUPS * D_STATE
D_IN_PROJ = D_INNER + CONV_DIM + SSM_HEADS
DT_PAD = LANE
D_IN_PAD = D_INNER + CONV_DIM + DT_PAD
SB_HEADS = 16
SB_D = 64
D_FF = 2816
NORM_EPS = 1e-6
GATED_NORM_EPS = 1e-5

ADAM_LR = 0.001
ADAM_B1 = 0.9
ADAM_B2 = 0.999
ADAM_EPS = 1e-08
ADAM_WD = 0.01
ADAM_STEP = 10

MESH_T = pl.DeviceIdType.MESH


def _cparams(sem=None):
    return pltpu.CompilerParams(dimension_semantics=sem, vmem_limit_bytes=VMEM_LIMIT_BYTES)


def _tile(n, target, align):
    best = None
    for t in range(align, min(n, target) + 1, align):
        if n % t == 0:
            best = t
    return best or n


def _mm(a, b, mode, out_dtype, name):
    if mode == "nn":
        (M, K), N = a.shape, b.shape[1]
    elif mode == "nt":
        (M, K), N = a.shape, b.shape[0]
    else:
        (K, M), N = a.shape, b.shape[1]
    tm = _tile(M, 512, LANE)
    tn = _tile(N, 1536, LANE)
    tk = _tile(K, 1024, LANE)
    nk = K // tk
    if mode == "nn":
        a_spec = pl.BlockSpec((tm, tk), lambda i, j, k: (i, k))
        b_spec = pl.BlockSpec((tk, tn), lambda i, j, k: (k, j))
        dims = (((1,), (0,)), ((), ()))
    elif mode == "nt":
        a_spec = pl.BlockSpec((tm, tk), lambda i, j, k: (i, k))
        b_spec = pl.BlockSpec((tn, tk), lambda i, j, k: (j, k))
        dims = (((1,), (1,)), ((), ()))
    else:
        a_spec = pl.BlockSpec((tk, tm), lambda i, j, k: (k, i))
        b_spec = pl.BlockSpec((tk, tn), lambda i, j, k: (k, j))
        dims = (((0,), (0,)), ((), ()))

    def body(a_ref, b_ref, o_ref, acc_ref):
        k = pl.program_id(2)

        @pl.when(k == 0)
        def _():
            acc_ref[...] = jnp.zeros_like(acc_ref)

        acc_ref[...] += lax.dot_general(a_ref[...], b_ref[...], dims, preferred_element_type=F32)

        @pl.when(k == nk - 1)
        def _():
            o_ref[...] = acc_ref[...].astype(o_ref.dtype)

    return pl.pallas_call(
        body,
        name=name,
        grid=(M // tm, N // tn, nk),
        in_specs=[a_spec, b_spec],
        out_specs=pl.BlockSpec((tm, tn), lambda i, j, k: (i, j)),
        out_shape=jax.ShapeDtypeStruct((M, N), out_dtype),
        scratch_shapes=[pltpu.VMEM((tm, tn), F32)],
        compiler_params=_cparams(("parallel", "parallel", "arbitrary")),
    )(a, b)


def _rows(fn, name, tile, row_ins, const_ins, row_outs, acc_outs=(), ncol=1):
    L = row_ins[0][0].shape[0]
    nrow = L // tile
    n_ri, n_ci, n_ro, n_ao = len(row_ins), len(const_ins), len(row_outs), len(acc_outs)

    def body(*refs):
        i = pl.program_id(1)
        j = pl.program_id(0)
        ins = [r[...] for r in refs[: n_ri + n_ci]]
        outs = fn(*ins, j=j)
        o_refs = refs[n_ri + n_ci:]
        for r, v in zip(o_refs[:n_ro], outs[:n_ro]):
            r[...] = v.astype(r.dtype)

        @pl.when(i == 0)
        def _():
            for r in o_refs[n_ro:]:
                r[...] = jnp.zeros_like(r)

        for r, v in zip(o_refs[n_ro:], outs[n_ro:]):
            r[...] += v

    def rspec(bc, cf):
        return pl.BlockSpec((tile, bc), lambda j, i: (i, cf(j)))

    def cspec(r, bc, cf):
        return pl.BlockSpec((r, bc), lambda j, i: (0, cf(j)))

    in_specs = [rspec(bc, cf) for (_, bc, cf) in row_ins]
    in_specs += [cspec(a.shape[0], bc, cf) for (a, bc, cf) in const_ins]
    out_specs = [rspec(bc, cf) for (_, _, bc, cf) in row_outs]
    out_specs += [cspec(1, bc, cf) for (_, bc, cf) in acc_outs]
    out_shape = [jax.ShapeDtypeStruct((L, c), dt) for (c, dt, _, _) in row_outs]
    out_shape += [jax.ShapeDtypeStruct((1, c), F32) for (c, _, _) in acc_outs]
    res = pl.pallas_call(
        body,
        name=name,
        grid=(ncol, nrow),
        in_specs=in_specs,
        out_specs=out_specs,
        out_shape=out_shape,
        compiler_params=_cparams(("arbitrary", "arbitrary")),
    )(*[a for (a, _, _) in row_ins], *[a for (a, _, _) in const_ins])
    return res


def _zero(j):
    return 0


def _whole(a):
    return (a, a.shape[1], _zero)


def _rms(x, w, eps):
    return x * lax.rsqrt(jnp.mean(x * x, axis=-1, keepdims=True) + eps) * w


ROW_TILE = 256


def _norm_fwd(x, r, w, name):
    C = x.shape[1]
    if r is None:
        def fn(x_, w_, j):
            return (_rms(x_, w_, NORM_EPS),)
        (h,) = _rows(fn, name, ROW_TILE, [_whole(x)], [_whole(w)], [(C, BF16, C, _zero)])
        return x, h

    def fn(x_, r_, w_, j):
        x1 = x_ + r_
        return x1, _rms(x1, w_, NORM_EPS)

    x1, h = _rows(fn, name, ROW_TILE, [_whole(x), _whole(r)], [_whole(w)],
                  [(C, F32, C, _zero), (C, BF16, C, _zero)])
    return x1, h


def _norm_bwd(x, w, dh, dres, name):
    C = x.shape[1]

    def fn(x_, dh_, dres_, w_, j):
        _, vjp = jax.vjp(lambda a, b: _rms(a, b, NORM_EPS), x_, w_)
        dx, dw = vjp(dh_.astype(F32))
        return dres_ + dx, dw

    return _rows(fn, name, ROW_TILE, [_whole(x), _whole(dh), _whole(dres)], [_whole(w)],
                 [(C, F32, C, _zero)], [(C, C, _zero)])


def _silu(x):
    return x * jax.nn.sigmoid(x)


FF_BLK = 256
FF_NB = D_FF // FF_BLK


def _swiglu_fwd(gu, name):
    def fn(g, u, j):
        return (_silu(g) * u,)
    (act,) = _rows(fn, name, ROW_TILE,
                   [(gu, FF_BLK, lambda j: j), (gu, FF_BLK, lambda j: j + FF_NB)], [],
                   [(D_FF, BF16, FF_BLK, lambda j: j)], ncol=FF_NB)
    return act


def _swiglu_bwd(gu, dact, name):
    def fn(g, u, da, j):
        _, vjp = jax.vjp(lambda a, b: _silu(a) * b, g, u)
        dg, du = vjp(da.astype(F32))
        return (jnp.where(j < FF_NB, dg, du),)
    (dgu,) = _rows(fn, name, ROW_TILE,
                   [(gu, FF_BLK, lambda j: j % FF_NB), (gu, FF_BLK, lambda j: j % FF_NB + FF_NB),
                    (dact, FF_BLK, lambda j: j % FF_NB)], [],
                   [(2 * D_FF, BF16, FF_BLK, lambda j: j)], ncol=2 * FF_NB)
    return dgu


def _gated_norm_fwd(y, zx, w, name):
    def fn(y_, z_, w_, j):
        return (_rms(y_ * _silu(z_), w_, GATED_NORM_EPS),)
    (yn,) = _rows(fn, name, ROW_TILE,
                  [(y, GROUP_W, lambda j: j), (zx, GROUP_W, lambda j: j)], [(w, GROUP_W, lambda j: j)],
                  [(D_INNER, BF16, GROUP_W, lambda j: j)], ncol=SSM_GROUPS)
    return yn


def _gated_norm_bwd(y, zx, w, dyn, name):
    def fn(y_, z_, dyn_, w_, j):
        _, vjp = jax.vjp(lambda a, b, c: _rms(a * _silu(b), c, GATED_NORM_EPS), y_, z_, w_)
        return vjp(dyn_.astype(F32))
    cj = lambda j: j
    return _rows(fn, name, ROW_TILE,
                 [(y, GROUP_W, cj), (zx, GROUP_W, cj), (dyn, GROUP_W, cj)], [(w, GROUP_W, cj)],
                 [(D_INNER, F32, GROUP_W, cj), (D_INNER, BF16, GROUP_W, cj)], [(D_INNER, GROUP_W, cj)],
                 ncol=SSM_GROUPS)


def _loss_bwd(x3, f, target, name):
    C = x3.shape[1]

    def fn(x_, f_, t_, j):
        err = (x_ + f_) - t_
        return err * (1.0 / C), jnp.sum(err * err, axis=0, keepdims=True)

    return _rows(fn, name, ROW_TILE, [_whole(x3), _whole(f), _whole(target)], [],
                 [(C, F32, C, _zero)], [(C, C, _zero)])


def _qk_norm_fwd(q, gain, scale, name):
    def fn(q_, g_, j):
        return (_rms(q_, g_, NORM_EPS) * scale,)
    (qn,) = _rows(fn, name, 2048, [_whole(q)], [_whole(gain)], [(SB_D, BF16, SB_D, _zero)])
    return qn


def _qk_norm_bwd(q, gain, dqn, scale, name):
    def fn(q_, dqn_, g_, j):
        _, vjp = jax.vjp(lambda a, b: _rms(a, b, NORM_EPS) * scale, q_, g_)
        return vjp(dqn_)
    return _rows(fn, name, 2048, [_whole(q), _whole(dqn)], [_whole(gain)],
                 [(SB_D, F32, SB_D, _zero)], [(SB_D, SB_D, _zero)])


CONV_TILE = 256
CONV_BLK = 512
XBC_COL0 = D_INNER // CONV_BLK


def _shift_down(cur, prev8, k):
    if k == 0:
        return cur
    rolled = pltpu.roll(cur, k, 0)
    head_prev = pltpu.roll(prev8, k, 0)
    rid = lax.broadcasted_iota(jnp.int32, (SUBLANE, cur.shape[1]), 0)
    head = jnp.where(rid < k, head_prev, rolled[:SUBLANE])
    if cur.shape[0] == SUBLANE:
        return head
    return jnp.concatenate([head, rolled[SUBLANE:]], axis=0)


def _shift_up(cur, next8, k):
    if k == 0:
        return cur
    T = cur.shape[0]
    rolled = pltpu.roll(cur, T - k, 0)
    tail_next = pltpu.roll(next8, SUBLANE - k, 0)
    rid = lax.broadcasted_iota(jnp.int32, (SUBLANE, cur.shape[1]), 0)
    tail = jnp.where(rid >= SUBLANE - k, tail_next, rolled[T - SUBLANE:])
    return jnp.concatenate([rolled[: T - SUBLANE], tail], axis=0)


def _conv_pre(cur, prev8, w, b):
    pre = b
    for i in range(CONV_W):
        pre = pre + w[i:i + 1, :] * _shift_down(cur, prev8, CONV_W - 1 - i)
    return pre


def _conv_fwd(zx, w8, b, name):
    L = zx.shape[0]
    nrow = L // CONV_TILE
    r8 = CONV_TILE // SUBLANE

    def body(cur_ref, prev_ref, w_ref, b_ref, o_ref):
        i = pl.program_id(1)
        prev8 = jnp.where(i > 0, prev_ref[...], 0.0)
        pre = _conv_pre(cur_ref[...], prev8, w_ref[...], b_ref[...])
        o_ref[...] = _silu(pre)

    return pl.pallas_call(
        body,
        name=name,
        grid=(CONV_DIM // CONV_BLK, nrow),
        in_specs=[
            pl.BlockSpec((CONV_TILE, CONV_BLK), lambda j, i: (i, j + XBC_COL0)),
            pl.BlockSpec((SUBLANE, CONV_BLK), lambda j, i: (jnp.maximum(i * r8 - 1, 0), j + XBC_COL0)),
            pl.BlockSpec((SUBLANE, CONV_BLK), lambda j, i: (0, j)),
            pl.BlockSpec((1, CONV_BLK), lambda j, i: (0, j)),
        ],
        out_specs=pl.BlockSpec((CONV_TILE, CONV_BLK), lambda j, i: (i, j)),
        out_shape=jax.ShapeDtypeStruct((L, CONV_DIM), F32),
        compiler_params=_cparams(("arbitrary", "arbitrary")),
    )(zx, zx, w8, b)


def _conv_bwd(zx, w8, b, dact, name):
    L = zx.shape[0]
    nrow = L // CONV_TILE
    r8 = CONV_TILE // SUBLANE
    last8 = L // SUBLANE - 1

    def body(cur_ref, prev_ref, next_ref, da_ref, dan_ref, w_ref, b_ref, du_ref, dw_ref, db_ref):
        i = pl.program_id(1)
        w = w_ref[...]
        b_ = b_ref[...]
        cur = cur_ref[...]
        prev8 = jnp.where(i > 0, prev_ref[...], 0.0)
        pre = _conv_pre(cur, prev8, w, b_)
        _, vjp = jax.vjp(_silu, pre)
        (dpre,) = vjp(da_ref[...])
        nxt = next_ref[...]
        pre_n = _conv_pre(nxt, cur[CONV_TILE - SUBLANE:], w, b_)
        _, vjp_n = jax.vjp(_silu, pre_n)
        (dpre_n,) = vjp_n(dan_ref[...])
        dpre_n = jnp.where(i < nrow - 1, dpre_n, 0.0)
        du = jnp.zeros_like(cur)
        dws = []
        for k in range(CONV_W):
            wk = w[CONV_W - 1 - k:CONV_W - k, :]
            du = du + wk * _shift_up(dpre, dpre_n, k)
            dws.append(jnp.sum(dpre * _shift_down(cur, prev8, k), axis=0, keepdims=True))
        du_ref[...] = du.astype(du_ref.dtype)
        dw_tile = jnp.concatenate([dws[3], dws[2], dws[1], dws[0]] + [jnp.zeros_like(dws[0])] * 4, axis=0)

        @pl.when(i == 0)
        def _():
            dw_ref[...] = jnp.zeros_like(dw_ref)
            db_ref[...] = jnp.zeros_like(db_ref)

        dw_ref[...] += dw_tile
        db_ref[...] += jnp.sum(dpre, axis=0, keepdims=True)

    return pl.pallas_call(
        body,
        name=name,
        grid=(CONV_DIM // CONV_BLK, nrow),
        in_specs=[
            pl.BlockSpec((CONV_TILE, CONV_BLK), lambda j, i: (i, j + XBC_COL0)),
            pl.BlockSpec((SUBLANE, CONV_BLK), lambda j, i: (jnp.maximum(i * r8 - 1, 0), j + XBC_COL0)),
            pl.BlockSpec((SUBLANE, CONV_BLK), lambda j, i: (jnp.minimum((i + 1) * r8, last8), j + XBC_COL0)),
            pl.BlockSpec((CONV_TILE, CONV_BLK), lambda j, i: (i, j)),
            pl.BlockSpec((SUBLANE, CONV_BLK), lambda j, i: (jnp.minimum((i + 1) * r8, last8), j)),
            pl.BlockSpec((SUBLANE, CONV_BLK), lambda j, i: (0, j)),
            pl.BlockSpec((1, CONV_BLK), lambda j, i: (0, j)),
        ],
        out_specs=[
            pl.BlockSpec((CONV_TILE, CONV_BLK), lambda j, i: (i, j)),
            pl.BlockSpec((SUBLANE, CONV_BLK), lambda j, i: (0, j)),
            pl.BlockSpec((1, CONV_BLK), lambda j, i: (0, j)),
        ],
        out_shape=[
            jax.ShapeDtypeStruct((L, CONV_DIM), BF16),
            jax.ShapeDtypeStruct((SUBLANE, CONV_DIM), F32),
            jax.ShapeDtypeStruct((1, CONV_DIM), F32),
        ],
        compiler_params=_cparams(("arbitrary", "arbitrary")),
    )(zx, zx, zx, dact, dact, w8, b)


HI = lax.Precision.HIGHEST
XS_COL0 = 0
B_COL0 = D_INNER // D_STATE
C_COL0 = B_COL0 + SSM_GROUPS
DT_COL = (D_INNER + CONV_DIM) // DT_PAD


def _dot(a, b, dims=(((1,), (0,)), ((), ())), precision=None):
    return lax.dot_general(a, b, dims, precision=precision, preferred_element_type=F32)


_DOT_DIMS = {
    "nn": (((1,), (0,)), ((), ())),
    "nt": (((1,), (1,)), ((), ())),
    "tn": (((0,), (0,)), ((), ())),
}


@functools.partial(jax.custom_vjp, nondiff_argnums=(2,))
def _bdot(a, b, mode):
    return _dot(a.astype(BF16), b.astype(BF16), _DOT_DIMS[mode])


def _bdot_fwd(a, b, mode):
    return _bdot(a, b, mode), (a, b)


def _bdot_bwd(mode, res, g):
    a, b = res
    if mode == "nn":
        return _bdot(g, b, "nt"), _bdot(a, g, "tn")
    if mode == "nt":
        return _bdot(g, b, "nn"), _bdot(g, a, "tn")
    return _bdot(b, g, "nt"), _bdot(a, g, "nn")


_bdot.defvjp(_bdot_fwd, _bdot_bwd)


def _softplus(x):
    return jnp.maximum(x, 0.0) + jnp.log(1.0 + jnp.exp(-jnp.abs(x)))


def _ssd_consts():
    l_ = lax.broadcasted_iota(jnp.int32, (CHUNK, GROUP_W), 0)
    c_ = lax.broadcasted_iota(jnp.int32, (CHUNK, GROUP_W), 1)
    s_ = c_ % CHUNK
    causal = s_ <= l_
    eye_t = (s_ == l_).astype(F32)
    r0 = lax.broadcasted_iota(jnp.int32, (CHUNK, CHUNK), 0)
    c0 = lax.broadcasted_iota(jnp.int32, (CHUNK, CHUNK), 1)
    tril = (c0 <= r0).astype(F32)
    rb = lax.broadcasted_iota(jnp.int32, (GROUP_W, GROUP_W), 0) // HEAD_P
    cb = lax.broadcasted_iota(jnp.int32, (GROUP_W, GROUP_W), 1) // HEAD_P
    blockdiag = rb == cb
    return causal, eye_t, tril, blockdiag


def _ssd_expand(g):
    h = lax.broadcasted_iota(jnp.int32, (DT_PAD, GROUP_W), 0)
    c = lax.broadcasted_iota(jnp.int32, (DT_PAD, GROUP_W), 1)
    return (h == g * (GROUP_W // HEAD_P) + c // HEAD_P).astype(F32)


def _ssd_chunk(S, xs, dt_raw, dt_bias, alog_e, d_e, Bm, Cm, E):
    causal, eye_t, tril, blockdiag = _ssd_consts()
    dt = _softplus(dt_raw + dt_bias)
    dtx = _dot(dt, E, precision=HI)
    a = dtx * (-jnp.exp(alog_e))
    acs = _dot(tril, a, precision=HI)
    rowv = _dot(jnp.ones((CHUNK, CHUNK), F32), acs * eye_t, precision=HI)
    seg = acs - rowv
    Lc = jnp.where(causal, jnp.exp(jnp.where(causal, seg, 0.0)), 0.0)
    xdt = xs * dtx
    Bt = jnp.concatenate([Bm] * 4, axis=0)
    CBc = _bdot(Cm, Bt, "nt")
    Xbd = jnp.where(blockdiag, jnp.concatenate([xdt] * 4, axis=0), 0.0)
    y_intra = _bdot(CBc * Lc, Xbd, "nn")
    y_inter = _bdot(Cm, S, "nn") * jnp.exp(acs)
    y = y_intra + y_inter + d_e * xs
    last = jnp.sum(a, axis=0, keepdims=True)
    dec_end = jnp.exp(last - acs)
    S_new = S * jnp.exp(last) + _bdot(Bm, xdt * dec_end, "tn")
    return S_new, y


def _ssd_in_specs(cmap):
    return [
        pl.BlockSpec((CHUNK, GROUP_W), lambda c, g: (cmap(c), g)),
        pl.BlockSpec((CHUNK, DT_PAD), lambda c, g: (cmap(c), DT_COL)),
        pl.BlockSpec((1, DT_PAD), lambda c, g: (0, 0)),
        pl.BlockSpec((1, GROUP_W), lambda c, g: (0, g)),
        pl.BlockSpec((1, GROUP_W), lambda c, g: (0, g)),
        pl.BlockSpec((CHUNK, D_STATE), lambda c, g: (cmap(c), B_COL0 + g)),
        pl.BlockSpec((CHUNK, D_STATE), lambda c, g: (cmap(c), C_COL0 + g)),
    ]


def _ssd_fwd(xbc, zx, dt_bias, alog_e, d_e, name):
    L = xbc.shape[0]
    nc = L // CHUNK

    def body(xs_ref, dtr_ref, bias_ref, alog_ref, d_ref, b_ref, c_ref, y_ref, st_ref, S_ref):
        c = pl.program_id(0)
        g = pl.program_id(1)

        @pl.when(c == 0)
        def _():
            S_ref[g] = jnp.zeros((D_STATE, GROUP_W), F32)

        S = S_ref[g]
        st_ref[...] = S
        S_new, y = _ssd_chunk(S, xs_ref[...], dtr_ref[...], bias_ref[...], alog_ref[...], d_ref[...],
                              b_ref[...], c_ref[...], _ssd_expand(g))
        y_ref[...] = y
        S_ref[g] = S_new

    return pl.pallas_call(
        body,
        name=name,
        grid=(nc, SSM_GROUPS),
        in_specs=_ssd_in_specs(lambda c: c),
        out_specs=[
            pl.BlockSpec((CHUNK, GROUP_W), lambda c, g: (c, g)),
            pl.BlockSpec((None, None, D_STATE, GROUP_W), lambda c, g: (c, g, 0, 0)),
        ],
        out_shape=[
            jax.ShapeDtypeStruct((L, D_INNER), F32),
            jax.ShapeDtypeStruct((nc, SSM_GROUPS, D_STATE, GROUP_W), F32),
        ],
        scratch_shapes=[pltpu.VMEM((SSM_GROUPS, D_STATE, GROUP_W), F32)],
        compiler_params=_cparams(("arbitrary", "arbitrary")),
    )(xbc, zx, dt_bias, alog_e, d_e, xbc, xbc)


def _ssd_bwd(xbc, zx, dt_bias, alog_e, d_e, states, dy, name):
    L = xbc.shape[0]
    nc = L // CHUNK
    rev = lambda c: nc - 1 - c

    def body(xs_ref, dtr_ref, bias_ref, alog_ref, d_ref, b_ref, c_ref, st_ref, dy_ref,
             dxs_ref, db_ref, dc_ref, ddt_ref, dbias_ref, dalog_ref, dd_ref, dS_ref):
        c = pl.program_id(0)
        g = pl.program_id(1)
        first = jnp.logical_and(c == 0, g == 0)

        @pl.when(c == 0)
        def _():
            dS_ref[g] = jnp.zeros((D_STATE, GROUP_W), F32)
            dalog_ref[g] = jnp.zeros((1, GROUP_W), F32)
            dd_ref[g] = jnp.zeros((1, GROUP_W), F32)

        @pl.when(first)
        def _():
            dbias_ref[...] = jnp.zeros_like(dbias_ref)

        E = _ssd_expand(g)
        _, vjp = jax.vjp(lambda S, xs, dtr, bias, alog, dsk, Bm, Cm: _ssd_chunk(S, xs, dtr, bias, alog, dsk, Bm, Cm, E),
                         st_ref[...], xs_ref[...], dtr_ref[...], bias_ref[...], alog_ref[...], d_ref[...],
                         b_ref[...], c_ref[...])
        dS, dxs, ddtr, dbias, dalog, dd, dB, dC = vjp((dS_ref[g], dy_ref[...]))
        dS_ref[g] = dS
        dxs_ref[...] = dxs
        db_ref[...] = dB
        dc_ref[...] = dC

        @pl.when(g == 0)
        def _():
            ddt_ref[...] = jnp.zeros_like(ddt_ref)

        ddt_ref[...] += ddtr
        dbias_ref[...] += dbias
        dalog_ref[g] += dalog
        dd_ref[g] += dd

    whole3 = pl.BlockSpec((SSM_GROUPS, 1, GROUP_W), lambda c, g: (0, 0, 0))
    return pl.pallas_call(
        body,
        name=name,
        grid=(nc, SSM_GROUPS),
        in_specs=_ssd_in_specs(rev) + [
            pl.BlockSpec((None, None, D_STATE, GROUP_W), lambda c, g: (rev(c), g, 0, 0)),
            pl.BlockSpec((CHUNK, GROUP_W), lambda c, g: (rev(c), g)),
        ],
        out_specs=[
            pl.BlockSpec((CHUNK, GROUP_W), lambda c, g: (rev(c), g)),
            pl.BlockSpec((CHUNK, D_STATE), lambda c, g: (rev(c), g)),
            pl.BlockSpec((CHUNK, D_STATE), lambda c, g: (rev(c), g)),
            pl.BlockSpec((CHUNK, DT_PAD), lambda c, g: (rev(c), 0)),
            pl.BlockSpec((1, DT_PAD), lambda c, g: (0, 0)),
            whole3,
            whole3,
        ],
        out_shape=[
            jax.ShapeDtypeStruct((L, D_INNER), F32),
            jax.ShapeDtypeStruct((L, SSM_GROUPS * D_STATE), F32),
            jax.ShapeDtypeStruct((L, SSM_GROUPS * D_STATE), F32),
            jax.ShapeDtypeStruct((L, DT_PAD), F32),
            jax.ShapeDtypeStruct((1, DT_PAD), F32),
            jax.ShapeDtypeStruct((SSM_GROUPS, 1, GROUP_W), F32),
            jax.ShapeDtypeStruct((SSM_GROUPS, 1, GROUP_W), F32),
        ],
        scratch_shapes=[pltpu.VMEM((SSM_GROUPS, D_STATE, GROUP_W), F32)],
        compiler_params=_cparams(("arbitrary", "arbitrary")),
    )(xbc, zx, dt_bias, alog_e, d_e, xbc, xbc, states, dy)


def _sb_scores(q, kj):
    z = _dot(q, kj, dims=(((1,), (1,)), ((), ())))
    lb = jnp.minimum(z, 0.0) - jnp.log(1.0 + jnp.exp(-jnp.abs(z)))
    return lb, lb - z


SB_TQ = 256
SB_TK = 512


SB_TC = SB_TK // 2


def _sb_after():
    j = lax.broadcasted_iota(jnp.int32, (SB_TC, SB_TC), 0)
    s = lax.broadcasted_iota(jnp.int32, (SB_TC, SB_TC), 1)
    return (j > s).astype(BF16)


def _sb_suffix(x, after):
    T = x.shape[0]
    hi = x.astype(BF16)
    lo = (x - hi.astype(F32)).astype(BF16)
    stacked = jnp.concatenate([hi[:, :SB_TC], hi[:, SB_TC:], lo[:, :SB_TC], lo[:, SB_TC:]], axis=0)
    out = _dot(stacked, after)
    rs_right = jnp.sum(x[:, SB_TC:], axis=1, keepdims=True)
    left = out[:T] + out[2 * T:3 * T] + rs_right
    right = out[T:2 * T] + out[3 * T:]
    return jnp.concatenate([left, right], axis=1), rs_right + jnp.sum(x[:, :SB_TC], axis=1, keepdims=True)


def _sb_seen(i, start):
    t = lax.broadcasted_iota(jnp.int32, (SB_TQ, SB_TK), 0) + i * SB_TQ
    s = lax.broadcasted_iota(jnp.int32, (SB_TQ, SB_TK), 1) + start
    return s < t


def _sb_walk(gd, groups, carry):
    carry = groups([gd], carry, True)
    carry = lax.cond(gd % 2 == 1, lambda c: groups([gd - 1], c, False), lambda c: c, carry)
    base = gd - 1 - gd % 2
    return lax.fori_loop(0, gd // 2, lambda n, c: groups([base - 2 * n, base - 2 * n - 1], c, False), carry)


def _sb_fwd(qs, kn, v, name):
    H, L, d = qs.shape
    nq = L // SB_TQ
    after = _sb_after()

    def body(q_ref, k_ref, v_ref, aft_ref, o_ref):
        i = pl.program_id(1)
        q = q_ref[...]
        gd = (i * SB_TQ) // SB_TK

        def prep(G, diag):
            start = pl.multiple_of(G * SB_TK, SB_TK)
            lb, lk = _sb_scores(q, k_ref[pl.ds(start, SB_TK), :])
            seen = _sb_seen(i, start) if diag else None
            if diag:
                lk = jnp.where(seen, lk, 0.0)
            between, rs = _sb_suffix(lk, aft_ref[...])
            return lb + between, rs, seen, v_ref[pl.ds(start, SB_TK), :]

        def groups(Gs, carry, diag):
            R, acc = carry
            for s, rs, seen, vg in [prep(G, diag) for G in Gs]:
                w = jnp.exp(s + R)
                if diag:
                    w = jnp.where(seen, w, 0.0)
                acc = acc + _dot(w.astype(BF16), vg)
                R = R + rs
            return R, acc

        _, acc = _sb_walk(gd, groups, (jnp.zeros((SB_TQ, 1), F32), jnp.zeros((SB_TQ, d), F32)))
        o_ref[...] = acc

    return pl.pallas_call(
        body,
        name=name,
        grid=(H, nq),
        in_specs=[
            pl.BlockSpec((None, SB_TQ, d), lambda h, i: (h, i, 0)),
            pl.BlockSpec((None, L, d), lambda h, i: (h, 0, 0)),
            pl.BlockSpec((None, L, d), lambda h, i: (h, 0, 0)),
            pl.BlockSpec((SB_TC, SB_TC), lambda h, i: (0, 0)),
        ],
        out_specs=pl.BlockSpec((None, SB_TQ, d), lambda h, i: (h, i, 0)),
        out_shape=jax.ShapeDtypeStruct((H, L, d), F32),
        compiler_params=_cparams(("parallel", "arbitrary")),
    )(qs, kn, v, after)


def _sb_bwd(qs, kn, v, o, do, name):
    H, L, d = qs.shape
    nq = L // SB_TQ
    after = _sb_after()
    qsT = qs.transpose(0, 2, 1)
    do16 = do.astype(BF16)
    doT = do16.transpose(0, 2, 1)

    def body(q_ref, qT_ref, k_ref, v_ref, o_ref, do_ref, doT_ref, aft_ref, dq_ref, dkT_ref, dvT_ref):
        i = pl.program_id(1)

        @pl.when(i == 0)
        def _():
            dkT_ref[...] = jnp.zeros_like(dkT_ref)
            dvT_ref[...] = jnp.zeros_like(dvT_ref)

        q = q_ref[...]
        qT = qT_ref[...]
        do_ = do_ref[...]
        doT_ = doT_ref[...]
        D = jnp.sum(o_ref[...] * do_.astype(F32), axis=1, keepdims=True)
        gd = (i * SB_TQ) // SB_TK

        def prep(G, diag):
            start = pl.multiple_of(G * SB_TK, SB_TK)
            kg = k_ref[pl.ds(start, SB_TK), :]
            lb, lk = _sb_scores(q, kg)
            seen = _sb_seen(i, start) if diag else None
            if diag:
                lk = jnp.where(seen, lk, 0.0)
            between, rs = _sb_suffix(lk, aft_ref[...])
            dw = _dot(do_, v_ref[pl.ds(start, SB_TK), :], dims=(((1,), (1,)), ((), ())))
            return start, kg, lb + between, rs, dw, jnp.exp(lb), seen

        def groups(Gs_, carry, diag):
            R, Gs, dq = carry
            for start, kg, s, rs, dw, sig, seen in [prep(G, diag) for G in Gs_]:
                w = jnp.exp(s + R)
                if diag:
                    w = jnp.where(seen, w, 0.0)
                w16 = w.astype(BF16)
                g = w16.astype(F32) * dw
                dvT_ref[:, pl.ds(start, SB_TK)] += _dot(doT_, w16)
                g_right, gsum = _sb_suffix(g, aft_ref[...])
                dz = g - sig * (D - Gs - g_right)
                if diag:
                    dz = jnp.where(seen, dz, 0.0)
                dz16 = dz.astype(BF16)
                dq = dq + _dot(dz16, kg)
                dkT_ref[:, pl.ds(start, SB_TK)] += _dot(qT, dz16)
                R = R + rs
                Gs = Gs + gsum
            return R, Gs, dq

        zero1 = jnp.zeros((SB_TQ, 1), F32)
        _, _, dq = _sb_walk(gd, groups, (zero1, zero1, jnp.zeros((SB_TQ, d), F32)))
        dq_ref[...] = dq

    blk = pl.BlockSpec((None, SB_TQ, d), lambda h, i: (h, i, 0))
    blkT = pl.BlockSpec((None, d, SB_TQ), lambda h, i: (h, 0, i))
    full = pl.BlockSpec((None, L, d), lambda h, i: (h, 0, 0))
    fullT = pl.BlockSpec((None, d, L), lambda h, i: (h, 0, 0))
    return pl.pallas_call(
        body,
        name=name,
        grid=(H, nq),
        in_specs=[blk, blkT, full, full, blk, blk, blkT, pl.BlockSpec((SB_TC, SB_TC), lambda h, i: (0, 0))],
        out_specs=[blk, fullT, fullT],
        out_shape=[jax.ShapeDtypeStruct((H, L, d), F32), jax.ShapeDtypeStruct((H, d, L), F32),
                   jax.ShapeDtypeStruct((H, d, L), F32)],
        compiler_params=_cparams(("parallel", "arbitrary")),
    )(qs, qsT, kn, v, o, do16, doT, after)


def _fold_heads(rows, name):
    def body(x_ref, o_ref):
        c = lax.broadcasted_iota(jnp.int32, (D_INNER, DT_PAD), 0)
        h = lax.broadcasted_iota(jnp.int32, (D_INNER, DT_PAD), 1)
        o_ref[...] = _dot(x_ref[...], (c // HEAD_P == h).astype(F32), precision=HI)

    return pl.pallas_call(
        body, name=name, out_shape=jax.ShapeDtypeStruct((SUBLANE, DT_PAD), F32),
        compiler_params=_cparams(),
    )(rows)


def _ffn_fwd(x, r, norm_w, w_in, w_out, tag):
    x1, h = _norm_fwd(x, r, norm_w, f"norm_ffn_fwd{tag}")
    gu = _mm(h, w_in, "nn", F32, f"ffn_in_fwd{tag}")
    act = _swiglu_fwd(gu, f"swiglu_fwd{tag}")
    f = _mm(act, w_out, "nn", F32, f"ffn_out_fwd{tag}")
    return x1, h, gu, act, f


def _ffn_bwd(x1, h, gu, act, norm_w, w_in, w_out, dres, tag):
    d16 = dres.astype(BF16)
    dact = _mm(d16, w_out, "nt", BF16, f"ffn_out_dx{tag}")
    g_w_out = _mm(act, d16, "tn", F32, f"ffn_out_dw{tag}")
    dgu = _swiglu_bwd(gu, dact, f"swiglu_bwd{tag}")
    dh = _mm(dgu, w_in, "nt", BF16, f"ffn_in_dx{tag}")
    g_w_in = _mm(h, dgu, "tn", F32, f"ffn_in_dw{tag}")
    dres, g_norm = _norm_bwd(x1, norm_w, dh, dres, f"norm_ffn_bwd{tag}")
    return dres, g_norm, g_w_in, g_w_out


def _local_step(x, target, W):
    L = x.shape[0]
    H = SB_HEADS
    nm0, nm1 = W["norm_mix"][0:1], W["norm_mix"][1:2]
    nf0, nf1 = W["norm_ffn"][0:1], W["norm_ffn"][1:2]

    _, h0 = _norm_fwd(x, None, nm0, "norm_mix_fwd0")
    zx = _mm(h0, W["ssm_w_in"], "nn", F32, "ssm_in_fwd")
    xbc = _conv_fwd(zx, W["conv_w8"], W["conv_b"], "conv_fwd")
    y, states = _ssd_fwd(xbc, zx, W["dt_bias"], W["alog_e"], W["d_e"], "ssd_fwd")
    yn = _gated_norm_fwd(y, zx, W["ssm_norm_w"], "gated_norm_fwd")
    mix0 = _mm(yn, W["ssm_w_out"], "nn", F32, "ssm_out_fwd")
    x1, h1, gu0, act0, f0 = _ffn_fwd(x, mix0, nf0, W["ffn_w_in"][0], W["ffn_w_out"][0], "0")

    x2, h2 = _norm_fwd(x1, f0, nm1, "norm_mix_fwd1")
    qkv = _mm(h2, W["sb_w_qkv"], "nn", F32, "sb_qkv_fwd")
    qkv_t = qkv.reshape(L, 3, H, SB_D).transpose(1, 2, 0, 3)
    q_r = qkv_t[0].reshape(H * L, SB_D)
    k_r = qkv_t[1].reshape(H * L, SB_D)
    qs = _qk_norm_fwd(q_r, W["sb_q_gain"], SB_D ** -0.5, "q_norm_fwd").reshape(H, L, SB_D)
    kn = _qk_norm_fwd(k_r, W["sb_k_gain"], 1.0, "k_norm_fwd").reshape(H, L, SB_D)
    vb = qkv_t[2].astype(BF16)
    o = _sb_fwd(qs, kn, vb, "sb_fwd")
    o_flat = o.transpose(1, 0, 2).reshape(L, D_MODEL).astype(BF16)
    mix1 = _mm(o_flat, W["sb_w_o"], "nn", F32, "sb_o_fwd")
    x3, h3, gu1, act1, f1 = _ffn_fwd(x2, mix1, nf1, W["ffn_w_in"][1], W["ffn_w_out"][1], "1")

    dres, sq = _loss_bwd(x3, f1, target, "loss")

    dres, g_nf1, g_fin1, g_fout1 = _ffn_bwd(x3, h3, gu1, act1, nf1, W["ffn_w_in"][1], W["ffn_w_out"][1], dres, "1")
    d16 = dres.astype(BF16)
    do_flat = _mm(d16, W["sb_w_o"], "nt", F32, "sb_o_dx")
    g_w_o = _mm(o_flat, d16, "tn", F32, "sb_o_dw")
    do = do_flat.reshape(L, H, SB_D).transpose(1, 0, 2)
    dqs, dknT, dvT = _sb_bwd(qs, kn, vb, o, do, "sb_bwd")
    dkn = dknT.transpose(0, 2, 1)
    dq, g_qg = _qk_norm_bwd(q_r, W["sb_q_gain"], dqs.reshape(H * L, SB_D), SB_D ** -0.5, "q_norm_bwd")
    dk, g_kg = _qk_norm_bwd(k_r, W["sb_k_gain"], dkn.reshape(H * L, SB_D), 1.0, "k_norm_bwd")
    dqkv = jnp.stack([dq.reshape(H, L, SB_D), dk.reshape(H, L, SB_D), dvT.transpose(0, 2, 1)])
    dqkv = dqkv.transpose(2, 0, 1, 3).reshape(L, 3 * D_MODEL).astype(BF16)
    dh2 = _mm(dqkv, W["sb_w_qkv"], "nt", BF16, "sb_qkv_dx")
    g_w_qkv = _mm(h2, dqkv, "tn", F32, "sb_qkv_dw")
    dres, g_nm1 = _norm_bwd(x2, nm1, dh2, dres, "norm_mix_bwd1")

    dres, g_nf0, g_fin0, g_fout0 = _ffn_bwd(x1, h1, gu0, act0, nf0, W["ffn_w_in"][0], W["ffn_w_out"][0], dres, "0")
    d16 = dres.astype(BF16)
    dyn = _mm(d16, W["ssm_w_out"], "nt", BF16, "ssm_out_dx")
    g_w_out = _mm(yn, d16, "tn", F32, "ssm_out_dw")
    dy, dz, g_snw = _gated_norm_bwd(y, zx, W["ssm_norm_w"], dyn, "gated_norm_bwd")
    dxs, dB, dC, ddt, g_dtb, g_alog_e, g_d_e = _ssd_bwd(
        xbc, zx, W["dt_bias"], W["alog_e"], W["d_e"], states, dy, "ssd_bwd")
    dact = jnp.concatenate([dxs, dB, dC], axis=1)
    dxbc, g_cw8, g_cb = _conv_bwd(zx, W["conv_w8"], W["conv_b"], dact, "conv_bwd")
    dzx = jnp.concatenate([dz, dxbc, ddt.astype(BF16)], axis=1)
    dh0 = _mm(dzx, W["ssm_w_in"], "nt", BF16, "ssm_in_dx")
    g_w_in = _mm(h0, dzx, "tn", F32, "ssm_in_dw")
    grad_x, g_nm0 = _norm_bwd(x, nm0, dh0, dres, "norm_mix_bwd0")

    per_head = jnp.concatenate(
        [g_alog_e.reshape(1, D_INNER), g_d_e.reshape(1, D_INNER), jnp.zeros((SUBLANE - 2, D_INNER), F32)], axis=0)
    folded = _fold_heads(per_head, "fold_heads")
    grads = {
        "norm_mix": jnp.concatenate([g_nm0, g_nm1], axis=0),
        "norm_ffn": jnp.concatenate([g_nf0, g_nf1], axis=0),
        "ssm_w_in": g_w_in[:, :D_IN_PROJ],
        "ssm_conv_w": g_cw8[:CONV_W],
        "ssm_conv_b": g_cb,
        "ssm_dt_bias": g_dtb[:, :SSM_HEADS],
        "ssm_a_log": folded[0:1, :SSM_HEADS],
        "ssm_d": folded[1:2, :SSM_HEADS],
        "ssm_norm_w": g_snw,
        "ssm_w_out": g_w_out,
        "sb_w_qkv": g_w_qkv,
        "sb_q_gain": g_qg,
        "sb_k_gain": g_kg,
        "sb_w_o": g_w_o,
        "ffn_w_in": jnp.stack([g_fin0, g_fin1]),
        "ffn_w_out": jnp.stack([g_fout0, g_fout1]),
    }
    return sq, grad_x, grads


def _prep_weights(full):
    W = {}
    W["norm_mix"] = full["norm_mix"]
    W["norm_ffn"] = full["norm_ffn"]
    w_in = full["ssm_w_in"]
    W["ssm_w_in"] = jnp.pad(w_in, ((0, 0), (0, D_IN_PAD - D_IN_PROJ))).astype(BF16)
    W["conv_w8"] = jnp.pad(full["ssm_conv_w"], ((0, SUBLANE - CONV_W), (0, 0)))
    W["conv_b"] = full["ssm_conv_b"]
    W["dt_bias"] = jnp.pad(full["ssm_dt_bias"], ((0, 0), (0, DT_PAD - SSM_HEADS)))
    W["alog_e"] = jnp.repeat(full["ssm_a_log"], HEAD_P, axis=1)
    W["d_e"] = jnp.repeat(full["ssm_d"], HEAD_P, axis=1)
    W["ssm_norm_w"] = full["ssm_norm_w"]
    W["ssm_w_out"] = full["ssm_w_out"].astype(BF16)
    W["sb_w_qkv"] = full["sb_w_qkv"].astype(BF16)
    W["sb_q_gain"] = full["sb_q_gain"]
    W["sb_k_gain"] = full["sb_k_gain"]
    W["sb_w_o"] = full["sb_w_o"].astype(BF16)
    W["ffn_w_in"] = full["ffn_w_in"].astype(BF16)
    W["ffn_w_out"] = full["ffn_w_out"].astype(BF16)
    return W


N_CHIPS = 4
N_DEV = 8
FLAT_W = D_MODEL
SHARDED = (
    ("ssm_w_in", (1, D_MODEL, D_IN_PROJ // N_CHIPS), 2),
    ("ssm_conv_w", (1, CONV_W, CONV_DIM // N_CHIPS), 2),
    ("ssm_w_out", (1, D_INNER // N_CHIPS, D_MODEL), 1),
    ("sb_w_qkv", (1, D_MODEL, 3 * D_MODEL // N_CHIPS), 2),
    ("sb_w_o", (1, D_MODEL // N_CHIPS, D_MODEL), 1),
    ("ffn_w_in", (2, D_MODEL, 2 * D_FF // N_CHIPS), 2),
    ("ffn_w_out", (2, D_FF // N_CHIPS, D_MODEL), 1),
)
REPLICATED = (
    ("norm_mix", (2, D_MODEL)), ("norm_ffn", (2, D_MODEL)), ("ssm_conv_b", (1, CONV_DIM)),
    ("ssm_norm_w", (1, D_INNER)), ("ssm_dt_bias", (1, SSM_HEADS)), ("ssm_a_log", (1, SSM_HEADS)),
    ("ssm_d", (1, SSM_HEADS)), ("sb_q_gain", (1, SB_D)), ("sb_k_gain", (1, SB_D)),
)
ADAM_TILE = 256
_USED_ROWS = sum(math.prod(s) for _, s, _ in SHARDED) // FLAT_W
FLAT_ROWS = -(-_USED_ROWS // ADAM_TILE) * ADAM_TILE
_USED_SMALL = sum(math.prod(s) for _, s in REPLICATED)
SMALL_ROWS = -(-_USED_SMALL // (SUBLANE * FLAT_W)) * SUBLANE


def _pack_shard(d):
    parts = [d[n].reshape(-1, FLAT_W) for n, _, _ in SHARDED]
    parts.append(jnp.zeros((FLAT_ROWS - _USED_ROWS, FLAT_W), parts[0].dtype))
    return jnp.concatenate(parts, axis=0)


def _unpack_shard(flat):
    out, r = {}, 0
    for n, s, _ in SHARDED:
        k = math.prod(s) // FLAT_W
        out[n] = flat[r:r + k].reshape(s)
        r += k
    return out


def _pack_small(d):
    parts = [d[n].reshape(-1) for n, _ in REPLICATED]
    parts.append(jnp.zeros((SMALL_ROWS * FLAT_W - _USED_SMALL,), parts[0].dtype))
    return jnp.concatenate(parts).reshape(SMALL_ROWS, FLAT_W)


def _unpack_small(flat):
    flat = flat.reshape(-1)
    out, r = {}, 0
    for n, s in REPLICATED:
        k = math.prod(s)
        out[n] = flat[r:r + k].reshape(s)
        r += k
    return out


ANY = pl.BlockSpec(memory_space=pl.ANY)


def _other_chips():
    x, y, c = lax.axis_index("x"), lax.axis_index("y"), lax.axis_index("c")
    return x, y, c, [(1 - x, y), (x, 1 - y), (1 - x, 1 - y)]


def _remote(src, dst, send_sem, recv_sem, dev):
    return pltpu.make_async_remote_copy(src_ref=src, dst_ref=dst, send_sem=send_sem, recv_sem=recv_sem,
                                        device_id=dev, device_id_type=MESH_T)


def _gather_weights(wb, ws):
    def body(wb_ref, ws_ref, gb_ref, gs_ref, send_sems, recv_sems, loc_sems):
        x, y, c, others = _other_chips()
        me = 2 * x + y
        pairs = ((wb_ref, gb_ref), (ws_ref, gs_ref))
        local = [pltpu.make_async_copy(src, dst.at[me], loc_sems.at[t]) for t, (src, dst) in enumerate(pairs)]
        for cp in local:
            cp.start()
        sends = []
        for k, (px, py) in enumerate(others):
            for t, (src, dst) in enumerate(pairs):
                cp = _remote(src, dst.at[me], send_sems.at[2 * k + t], recv_sems.at[2 * k + t], (px, py, c))
                cp.start()
                sends.append(cp)
        for k, (px, py) in enumerate(others):
            for t, (src, dst) in enumerate(pairs):
                _remote(src, dst.at[2 * px + py], send_sems.at[2 * k + t], recv_sems.at[2 * k + t],
                        (px, py, c)).wait_recv()
        for cp in sends:
            cp.wait_send()
        for cp in local:
            cp.wait()

    return pl.pallas_call(
        body, name="gather_weights",
        in_specs=[ANY, ANY], out_specs=[ANY, ANY],
        out_shape=[jax.ShapeDtypeStruct((N_CHIPS,) + wb.shape, wb.dtype),
                   jax.ShapeDtypeStruct((N_CHIPS,) + ws.shape, ws.dtype)],
        scratch_shapes=[pltpu.SemaphoreType.DMA((6,)), pltpu.SemaphoreType.DMA((6,)), pltpu.SemaphoreType.DMA((2,))],
    )(wb, ws)


def _scatter_grads(gpack):
    def body(g_ref, r_ref, send_sems, recv_sems):
        x, y, c, others = _other_chips()
        sends = []
        for k, (px, py) in enumerate(others):
            cp = _remote(g_ref.at[2 * px + py], r_ref.at[k], send_sems.at[k], recv_sems.at[k], (px, py, c))
            cp.start()
            sends.append(cp)
        for cp in sends:
            cp.wait_recv()
        for cp in sends:
            cp.wait_send()

    return pl.pallas_call(
        body, name="scatter_grads",
        in_specs=[ANY], out_specs=ANY,
        out_shape=jax.ShapeDtypeStruct((N_CHIPS - 1,) + gpack.shape[1:], gpack.dtype),
        scratch_shapes=[pltpu.SemaphoreType.DMA((3,)), pltpu.SemaphoreType.DMA((3,))],
    )(gpack)


def _exchange_partials(part, small):
    def body(p_ref, s_ref, ps_ref, sa_ref, send_sems, recv_sems, loc_sem):
        x, y, c = lax.axis_index("x"), lax.axis_index("y"), lax.axis_index("c")
        me = 4 * x + 2 * y + c
        local = pltpu.make_async_copy(s_ref, sa_ref.at[me], loc_sem)
        local.start()
        sib = _remote(p_ref, ps_ref, send_sems.at[0], recv_sems.at[0], (x, y, 1 - c))
        sib.start()
        sends = [sib]
        for k in range(1, N_DEV):
            fx, fy, fc = (k >> 2) & 1, (k >> 1) & 1, k & 1
            px, py, pc = x ^ fx, y ^ fy, c ^ fc
            cp = _remote(s_ref, sa_ref.at[me], send_sems.at[k], recv_sems.at[k], (px, py, pc))
            cp.start()
            sends.append(cp)
        sib.wait_recv()
        for k in range(1, N_DEV):
            fx, fy, fc = (k >> 2) & 1, (k >> 1) & 1, k & 1
            px, py, pc = x ^ fx, y ^ fy, c ^ fc
            _remote(s_ref, sa_ref.at[4 * px + 2 * py + pc], send_sems.at[k], recv_sems.at[k], (px, py, pc)).wait_recv()
        for cp in sends:
            cp.wait_send()
        local.wait()

    return pl.pallas_call(
        body, name="exchange_partials",
        in_specs=[ANY, ANY], out_specs=[ANY, ANY],
        out_shape=[jax.ShapeDtypeStruct(part.shape, part.dtype),
                   jax.ShapeDtypeStruct((N_DEV,) + small.shape, small.dtype)],
        scratch_shapes=[pltpu.SemaphoreType.DMA((N_DEV,)), pltpu.SemaphoreType.DMA((N_DEV,)), pltpu.SemaphoreType.DMA],
    )(part, small)


def _partial_sum(own, recv):
    R = own.shape[0]

    def body(o_ref, r_ref, p_ref):
        acc = o_ref[...]
        for k in range(N_CHIPS - 1):
            acc = acc + r_ref[k].astype(F32)
        p_ref[...] = acc

    return pl.pallas_call(
        body, name="grad_partial_sum", grid=(R // ADAM_TILE,),
        in_specs=[pl.BlockSpec((ADAM_TILE, FLAT_W), lambda i: (i, 0)),
                  pl.BlockSpec((N_CHIPS - 1, ADAM_TILE, FLAT_W), lambda i: (0, i, 0))],
        out_specs=pl.BlockSpec((ADAM_TILE, FLAT_W), lambda i: (i, 0)),
        out_shape=jax.ShapeDtypeStruct(own.shape, F32),
        compiler_params=_cparams(("parallel",)),
    )(own, recv)


def _adamw_math(w, g, m, v):
    m = ADAM_B1 * m + (1.0 - ADAM_B1) * g
    v = ADAM_B2 * v + (1.0 - ADAM_B2) * jnp.square(g)
    m_hat = m / (1.0 - ADAM_B1 ** ADAM_STEP)
    v_hat = v / (1.0 - ADAM_B2 ** ADAM_STEP)
    delta = -ADAM_LR * (m_hat / (jnp.sqrt(v_hat) + ADAM_EPS) + ADAM_WD * w)
    return delta, m, v


def _adamw(w, m, v, parts, tile, name):
    n, R, _ = parts.shape

    def body(w_ref, m_ref, v_ref, p_ref, g_ref, d_ref, nm_ref, nv_ref):
        g = p_ref[0]
        for k in range(1, n):
            g = g + p_ref[k]
        delta, nm, nv = _adamw_math(w_ref[...], g, m_ref[...], v_ref[...])
        g_ref[...] = g
        d_ref[...] = delta
        nm_ref[...] = nm
        nv_ref[...] = nv

    blk = pl.BlockSpec((tile, FLAT_W), lambda i: (i, 0))
    return pl.pallas_call(
        body, name=name, grid=(R // tile,),
        in_specs=[blk, blk, blk, pl.BlockSpec((n, tile, FLAT_W), lambda i: (0, i, 0))],
        out_specs=[blk] * 4,
        out_shape=[jax.ShapeDtypeStruct((R, FLAT_W), F32)] * 4,
        compiler_params=_cparams(("parallel",)),
    )(w, m, v, parts)


_NAMES = ("norm_mix", "norm_ffn", "ssm_w_in", "ssm_conv_w", "ssm_conv_b", "ssm_dt_bias", "ssm_a_log", "ssm_d",
          "ssm_norm_w", "ssm_w_out", "sb_w_qkv", "sb_q_gain", "sb_k_gain", "sb_w_o", "ffn_w_in", "ffn_w_out")


def _step(x, loss_target, w, m, v):
    cx, cy, cc = lax.axis_index("x"), lax.axis_index("y"), lax.axis_index("c")
    chip = 2 * cx + cy

    w_flat = _pack_shard(w)
    conv8 = jnp.pad(w["ssm_conv_w"][0], ((0, SUBLANE - CONV_W), (0, 0)))
    gb, gs = _gather_weights(w_flat.astype(BF16), conv8)
    shards = [_unpack_shard(gb[p]) for p in range(N_CHIPS)]
    full = {n: jnp.concatenate([s[n] for s in shards], axis=ax) for n, _, ax in SHARDED}
    full = {n: (a if n.startswith("ffn") else a[0]) for n, a in full.items()}
    full["ssm_conv_w"] = jnp.concatenate([gs[p, :CONV_W] for p in range(N_CHIPS)], axis=1)
    for n, _ in REPLICATED:
        full[n] = w[n]

    sq, grad_x, grads = _local_step(x[0], loss_target[0], _prep_weights(full))
    loss = lax.psum(0.5 * jnp.sum(sq) / D_MODEL, ("x", "y", "c"))

    blocks = []
    for s in range(N_CHIPS):
        d = {}
        for n, shape, ax in SHARDED:
            g = grads[n] if n.startswith("ffn") else grads[n][None]
            d[n] = lax.slice_in_dim(g, s * shape[ax], (s + 1) * shape[ax], axis=ax)
        blocks.append(_pack_shard(d))
    gpack = jnp.stack(blocks)
    own = lax.dynamic_index_in_dim(gpack, chip, axis=0, keepdims=False)
    recv = _scatter_grads(gpack.astype(BF16))
    part = _partial_sum(own, recv)
    part_sib, small_all = _exchange_partials(part, _pack_small(grads))
    parts = jnp.stack([part, part_sib])
    g_f, d_f, m_f, v_f = _adamw(w_flat, _pack_shard(m), _pack_shard(v), parts, ADAM_TILE, "adamw_sharded")
    g_s, d_s, m_s, v_s = _adamw(_pack_small(w), _pack_small(m), _pack_small(v), small_all, SUBLANE, "adamw_replicated")

    outs = []
    for big, small in ((g_f, g_s), (d_f, d_s), (m_f, m_s), (v_f, v_s)):
        d = {**_unpack_shard(big), **_unpack_small(small)}
        outs.append([d[n] for n in _NAMES])
    return loss, grad_x[None], outs


def kernel(x, norm_mix, norm_ffn, ssm_w_in, ssm_conv_w, ssm_conv_b, ssm_dt_bias, ssm_a_log, ssm_d, ssm_norm_w, ssm_w_out, sb_w_qkv, sb_q_gain, sb_k_gain, sb_w_o, ffn_w_in, ffn_w_out, loss_target, m_norm_mix, m_norm_ffn, m_ssm_w_in, m_ssm_conv_w, m_ssm_conv_b, m_ssm_dt_bias, m_ssm_a_log, m_ssm_d, m_ssm_norm_w, m_ssm_w_out, m_sb_w_qkv, m_sb_q_gain, m_sb_k_gain, m_sb_w_o, m_ffn_w_in, m_ffn_w_out, v_norm_mix, v_norm_ffn, v_ssm_w_in, v_ssm_conv_w, v_ssm_conv_b, v_ssm_dt_bias, v_ssm_a_log, v_ssm_d, v_ssm_norm_w, v_ssm_w_out, v_sb_w_qkv, v_sb_q_gain, v_sb_k_gain, v_sb_w_o, v_ffn_w_in, v_ffn_w_out):
    w = dict(zip(_NAMES, (norm_mix, norm_ffn, ssm_w_in, ssm_conv_w, ssm_conv_b, ssm_dt_bias, ssm_a_log, ssm_d,
                          ssm_norm_w, ssm_w_out, sb_w_qkv, sb_q_gain, sb_k_gain, sb_w_o, ffn_w_in, ffn_w_out)))
    m = dict(zip(_NAMES, (m_norm_mix, m_norm_ffn, m_ssm_w_in, m_ssm_conv_w, m_ssm_conv_b, m_ssm_dt_bias, m_ssm_a_log,
                          m_ssm_d, m_ssm_norm_w, m_ssm_w_out, m_sb_w_qkv, m_sb_q_gain, m_sb_k_gain, m_sb_w_o,
                          m_ffn_w_in, m_ffn_w_out)))
    v = dict(zip(_NAMES, (v_norm_mix, v_norm_ffn, v_ssm_w_in, v_ssm_conv_w, v_ssm_conv_b, v_ssm_dt_bias, v_ssm_a_log,
                          v_ssm_d, v_ssm_norm_w, v_ssm_w_out, v_sb_w_qkv, v_sb_q_gain, v_sb_k_gain, v_sb_w_o,
                          v_ffn_w_in, v_ffn_w_out)))
    loss, grad_x, (g, d, nm, nv) = _step(x, loss_target, w, m, v)
    return (loss, grad_x, *g, *d, *nm, *nv)
```

```python
import functools
import math

import jax
import jax.numpy as jnp
from jax import lax
from jax.experimental import pallas as pl
from jax.experimental.pallas import tpu as pltpu

F32 = jnp.float32
BF16 = jnp.bfloat16

VMEM_LIMIT_BYTES = 48 * 1024 * 1024
LANE = 128
SUBLANE = 8

D_MODEL = 1024
CHUNK = 64
D_INNER = 2048
SSM_HEADS = 32
SSM_GROUPS = 8
GROUP_W = D_INNER // SSM_GROUPS
HEAD_P = 64
D_STATE = 128
CONV_W = 4
CONV_DIM = D_INNER + 2 * SSM_GROUPS * D_STATE
D_IN_PROJ = D_INNER + CONV_DIM + SSM_HEADS
DT_PAD = LANE
D_IN_PAD = D_INNER + CONV_DIM + DT_PAD
SB_HEADS = 16
SB_D = 64
D_FF = 2816
NORM_EPS = 1e-6
GATED_NORM_EPS = 1e-5

ADAM_LR = 0.001
ADAM_B1 = 0.9
ADAM_B2 = 0.999
ADAM_EPS = 1e-08
ADAM_WD = 0.01
ADAM_STEP = 10

MESH_T = pl.DeviceIdType.MESH


def _cparams(sem=None):
    return pltpu.CompilerParams(dimension_semantics=sem, vmem_limit_bytes=VMEM_LIMIT_BYTES)


def _tile(n, target, align):
    best = None
    for t in range(align, min(n, target) + 1, align):
        if n % t == 0:
            best = t
    return best or n


def _mm(a, b, mode, out_dtype, name):
    if mode == "nn":
        (M, K), N = a.shape, b.shape[1]
    elif mode == "nt":
        (M, K), N = a.shape, b.shape[0]
    else:
        (K, M), N = a.shape, b.shape[1]
    tm = _tile(M, 512, LANE)
    tn = _tile(N, 1536, LANE)
    tk = _tile(K, 1536, LANE)
    nk = K // tk
    if mode == "nn":
        a_spec = pl.BlockSpec((tm, tk), lambda i, j, k: (i, k))
        b_spec = pl.BlockSpec((tk, tn), lambda i, j, k: (k, j))
        dims = (((1,), (0,)), ((), ()))
    elif mode == "nt":
        a_spec = pl.BlockSpec((tm, tk), lambda i, j, k: (i, k))
        b_spec = pl.BlockSpec((tn, tk), lambda i, j, k: (j, k))
        dims = (((1,), (1,)), ((), ()))
    else:
        a_spec = pl.BlockSpec((tk, tm), lambda i, j, k: (k, i))
        b_spec = pl.BlockSpec((tk, tn), lambda i, j, k: (k, j))
        dims = (((0,), (0,)), ((), ()))

    def body(a_ref, b_ref, o_ref, acc_ref):
        k = pl.program_id(2)

        @pl.when(k == 0)
        def _():
            acc_ref[...] = jnp.zeros_like(acc_ref)

        acc_ref[...] += lax.dot_general(a_ref[...], b_ref[...], dims, preferred_element_type=F32)

        @pl.when(k == nk - 1)
        def _():
            o_ref[...] = acc_ref[...].astype(o_ref.dtype)

    return pl.pallas_call(
        body,
        name=name,
        grid=(M // tm, N // tn, nk),
        in_specs=[a_spec, b_spec],
        out_specs=pl.BlockSpec((tm, tn), lambda i, j, k: (i, j)),
        out_shape=jax.ShapeDtypeStruct((M, N), out_dtype),
        scratch_shapes=[pltpu.VMEM((tm, tn), F32)],
        compiler_params=_cparams(("parallel", "parallel", "arbitrary")),
    )(a, b)


def _rows(fn, name, tile, row_ins, const_ins, row_outs, acc_outs=(), ncol=1):
    L = row_ins[0][0].shape[0]
    tile = min(tile, L)
    nrow = L // tile
    n_ri, n_ci, n_ro, n_ao = len(row_ins), len(const_ins), len(row_outs), len(acc_outs)

    def body(*refs):
        i = pl.program_id(1)
        j = pl.program_id(0)
        ins = [r[...] for r in refs[: n_ri + n_ci]]
        outs = fn(*ins, j=j)
        o_refs = refs[n_ri + n_ci:]
        for r, v in zip(o_refs[:n_ro], outs[:n_ro]):
            r[...] = v.astype(r.dtype)

        @pl.when(i == 0)
        def _():
            for r in o_refs[n_ro:]:
                r[...] = jnp.zeros_like(r)

        for r, v in zip(o_refs[n_ro:], outs[n_ro:]):
            r[...] += v

    def rspec(bc, cf):
        return pl.BlockSpec((tile, bc), lambda j, i: (i, cf(j)))

    def cspec(r, bc, cf):
        return pl.BlockSpec((r, bc), lambda j, i: (0, cf(j)))

    in_specs = [rspec(bc, cf) for (_, bc, cf) in row_ins]
    in_specs += [cspec(a.shape[0], bc, cf) for (a, bc, cf) in const_ins]
    out_specs = [rspec(bc, cf) for (_, _, bc, cf) in row_outs]
    out_specs += [cspec(1, bc, cf) for (_, bc, cf) in acc_outs]
    out_shape = [jax.ShapeDtypeStruct((L, c), dt) for (c, dt, _, _) in row_outs]
    out_shape += [jax.ShapeDtypeStruct((1, c), F32) for (c, _, _) in acc_outs]
    res = pl.pallas_call(
        body,
        name=name,
        grid=(ncol, nrow),
        in_specs=in_specs,
        out_specs=out_specs,
        out_shape=out_shape,
        compiler_params=_cparams(("arbitrary", "arbitrary")),
    )(*[a for (a, _, _) in row_ins], *[a for (a, _, _) in const_ins])
    return res


def _zero(j):
    return 0


def _whole(a):
    return (a, a.shape[1], _zero)


def _rms(x, w, eps):
    return x * lax.rsqrt(jnp.mean(x * x, axis=-1, keepdims=True) + eps) * w


ROW_TILE = 256
COL_ROW_TILE = 1024


def _norm_fwd(x, r, w, name):
    C = x.shape[1]
    if r is None:
        def fn(x_, w_, j):
            return (_rms(x_, w_, NORM_EPS),)
        (h,) = _rows(fn, name, ROW_TILE, [_whole(x)], [_whole(w)], [(C, BF16, C, _zero)])
        return x, h

    def fn(x_, r_, w_, j):
        x1 = x_ + r_
        return x1, _rms(x1, w_, NORM_EPS)

    x1, h = _rows(fn, name, ROW_TILE, [_whole(x), _whole(r)], [_whole(w)],
                  [(C, F32, C, _zero), (C, BF16, C, _zero)])
    return x1, h


def _norm_bwd(x, w, dh, dres, name):
    C = x.shape[1]

    def fn(x_, dh_, dres_, w_, j):
        _, vjp = jax.vjp(lambda a, b: _rms(a, b, NORM_EPS), x_, w_)
        dx, dw = vjp(dh_.astype(F32))
        return dres_ + dx, dw

    return _rows(fn, name, ROW_TILE, [_whole(x), _whole(dh), _whole(dres)], [_whole(w)],
                 [(C, F32, C, _zero)], [(C, C, _zero)])


def _silu(x):
    return x * jax.nn.sigmoid(x)


FF_BLK = 256
FF_NB = D_FF // FF_BLK


def _swiglu_fwd(gu, name):
    def fn(g, u, j):
        return (_silu(g) * u,)
    (act,) = _rows(fn, name, COL_ROW_TILE,
                   [(gu, FF_BLK, lambda j: j), (gu, FF_BLK, lambda j: j + FF_NB)], [],
                   [(D_FF, BF16, FF_BLK, lambda j: j)], ncol=FF_NB)
    return act


def _swiglu_bwd(gu, dact, name):
    def fn(g, u, da, j):
        _, vjp = jax.vjp(lambda a, b: _silu(a) * b, g, u)
        dg, du = vjp(da.astype(F32))
        return (jnp.where(j < FF_NB, dg, du),)
    (dgu,) = _rows(fn, name, COL_ROW_TILE,
                   [(gu, FF_BLK, lambda j: j % FF_NB), (gu, FF_BLK, lambda j: j % FF_NB + FF_NB),
                    (dact, FF_BLK, lambda j: j % FF_NB)], [],
                   [(2 * D_FF, BF16, FF_BLK, lambda j: j)], ncol=2 * FF_NB)
    return dgu


def _gated_norm_fwd(y, zx, w, name):
    def fn(y_, z_, w_, j):
        return (_rms(y_ * _silu(z_), w_, GATED_NORM_EPS),)
    (yn,) = _rows(fn, name, COL_ROW_TILE,
                  [(y, GROUP_W, lambda j: j), (zx, GROUP_W, lambda j: j)], [(w, GROUP_W, lambda j: j)],
                  [(D_INNER, BF16, GROUP_W, lambda j: j)], ncol=SSM_GROUPS)
    return yn


def _gated_norm_bwd(y, zx, w, dyn, name):
    def fn(y_, z_, dyn_, w_, j):
        _, vjp = jax.vjp(lambda a, b, c: _rms(a * _silu(b), c, GATED_NORM_EPS), y_, z_, w_)
        return vjp(dyn_.astype(F32))
    cj = lambda j: j
    return _rows(fn, name, COL_ROW_TILE,
                 [(y, GROUP_W, cj), (zx, GROUP_W, cj), (dyn, GROUP_W, cj)], [(w, GROUP_W, cj)],
                 [(D_INNER, F32, GROUP_W, cj), (D_INNER, BF16, GROUP_W, cj)], [(D_INNER, GROUP_W, cj)],
                 ncol=SSM_GROUPS)


def _loss_bwd(x3, f, target, name):
    C = x3.shape[1]

    def fn(x_, f_, t_, j):
        err = (x_ + f_) - t_
        return err * (1.0 / C), jnp.sum(err * err, axis=0, keepdims=True)

    return _rows(fn, name, ROW_TILE, [_whole(x3), _whole(f), _whole(target)], [],
                 [(C, F32, C, _zero)], [(C, C, _zero)])


def _qk_norm_fwd(q, gain, scale, name):
    def fn(q_, g_, j):
        return (_rms(q_, g_, NORM_EPS) * scale,)
    (qn,) = _rows(fn, name, 2048, [_whole(q)], [_whole(gain)], [(SB_D, BF16, SB_D, _zero)])
    return qn


def _qk_norm_bwd(q, gain, dqn, scale, cot_scale, name):
    def fn(q_, dqn_, g_, j):
        _, vjp = jax.vjp(lambda a, b: _rms(a, b, NORM_EPS) * scale, q_, g_)
        return vjp(dqn_ * cot_scale)
    return _rows(fn, name, 2048, [_whole(q), _whole(dqn)], [_whole(gain)],
                 [(SB_D, F32, SB_D, _zero)], [(SB_D, SB_D, _zero)])


CONV_TILE = 512
CONV_BLK = 512
XBC_COL0 = D_INNER // CONV_BLK


def _shift_down(cur, prev8, k):
    if k == 0:
        return cur
    rolled = pltpu.roll(cur, k, 0)
    head_prev = pltpu.roll(prev8, k, 0)
    rid = lax.broadcasted_iota(jnp.int32, (SUBLANE, cur.shape[1]), 0)
    head = jnp.where(rid < k, head_prev, rolled[:SUBLANE])
    if cur.shape[0] == SUBLANE:
        return head
    return jnp.concatenate([head, rolled[SUBLANE:]], axis=0)


def _shift_up(cur, next8, k):
    if k == 0:
        return cur
    T = cur.shape[0]
    rolled = pltpu.roll(cur, T - k, 0)
    tail_next = pltpu.roll(next8, SUBLANE - k, 0)
    rid = lax.broadcasted_iota(jnp.int32, (SUBLANE, cur.shape[1]), 0)
    tail = jnp.where(rid >= SUBLANE - k, tail_next, rolled[T - SUBLANE:])
    return jnp.concatenate([rolled[: T - SUBLANE], tail], axis=0)


def _conv_pre(cur, prev8, w, b):
    pre = b
    for i in range(CONV_W):
        pre = pre + w[i:i + 1, :] * _shift_down(cur, prev8, CONV_W - 1 - i)
    return pre


def _conv_fwd(zx, w8, b, name):
    L = zx.shape[0]
    nrow = L // CONV_TILE
    r8 = CONV_TILE // SUBLANE

    def body(cur_ref, prev_ref, w_ref, b_ref, o_ref):
        i = pl.program_id(1)
        prev8 = jnp.where(i > 0, prev_ref[...], 0.0)
        pre = _conv_pre(cur_ref[...], prev8, w_ref[...], b_ref[...])
        o_ref[...] = _silu(pre)

    return pl.pallas_call(
        body,
        name=name,
        grid=(CONV_DIM // CONV_BLK, nrow),
        in_specs=[
            pl.BlockSpec((CONV_TILE, CONV_BLK), lambda j, i: (i, j + XBC_COL0)),
            pl.BlockSpec((SUBLANE, CONV_BLK), lambda j, i: (jnp.maximum(i * r8 - 1, 0), j + XBC_COL0)),
            pl.BlockSpec((SUBLANE, CONV_BLK), lambda j, i: (0, j)),
            pl.BlockSpec((1, CONV_BLK), lambda j, i: (0, j)),
        ],
        out_specs=pl.BlockSpec((CONV_TILE, CONV_BLK), lambda j, i: (i, j)),
        out_shape=jax.ShapeDtypeStruct((L, CONV_DIM), F32),
        compiler_params=_cparams(("arbitrary", "arbitrary")),
    )(zx, zx, w8, b)


def _conv_bwd(zx, w8, b, dact, name):
    L = zx.shape[0]
    nrow = L // CONV_TILE
    r8 = CONV_TILE // SUBLANE
    last8 = L // SUBLANE - 1

    def body(cur_ref, prev_ref, next_ref, da_ref, dan_ref, w_ref, b_ref, du_ref, dw_ref, db_ref):
        i = pl.program_id(1)
        w = w_ref[...]
        b_ = b_ref[...]
        cur = cur_ref[...]
        prev8 = jnp.where(i > 0, prev_ref[...], 0.0)
        pre = _conv_pre(cur, prev8, w, b_)
        _, vjp = jax.vjp(_silu, pre)
        (dpre,) = vjp(da_ref[...])
        nxt = next_ref[...]
        pre_n = _conv_pre(nxt, cur[CONV_TILE - SUBLANE:], w, b_)
        _, vjp_n = jax.vjp(_silu, pre_n)
        (dpre_n,) = vjp_n(dan_ref[...])
        dpre_n = jnp.where(i < nrow - 1, dpre_n, 0.0)
        du = jnp.zeros_like(cur)
        dws = []
        for k in range(CONV_W):
            wk = w[CONV_W - 1 - k:CONV_W - k, :]
            du = du + wk * _shift_up(dpre, dpre_n, k)
            dws.append(jnp.sum(dpre * _shift_down(cur, prev8, k), axis=0, keepdims=True))
        du_ref[...] = du.astype(du_ref.dtype)
        dw_tile = jnp.concatenate([dws[3], dws[2], dws[1], dws[0]] + [jnp.zeros_like(dws[0])] * 4, axis=0)

        @pl.when(i == 0)
        def _():
            dw_ref[...] = jnp.zeros_like(dw_ref)
            db_ref[...] = jnp.zeros_like(db_ref)

        dw_ref[...] += dw_tile
        db_ref[...] += jnp.sum(dpre, axis=0, keepdims=True)

    return pl.pallas_call(
        body,
        name=name,
        grid=(CONV_DIM // CONV_BLK, nrow),
        in_specs=[
            pl.BlockSpec((CONV_TILE, CONV_BLK), lambda j, i: (i, j + XBC_COL0)),
            pl.BlockSpec((SUBLANE, CONV_BLK), lambda j, i: (jnp.maximum(i * r8 - 1, 0), j + XBC_COL0)),
            pl.BlockSpec((SUBLANE, CONV_BLK), lambda j, i: (jnp.minimum((i + 1) * r8, last8), j + XBC_COL0)),
            pl.BlockSpec((CONV_TILE, CONV_BLK), lambda j, i: (i, j)),
            pl.BlockSpec((SUBLANE, CONV_BLK), lambda j, i: (jnp.minimum((i + 1) * r8, last8), j)),
            pl.BlockSpec((SUBLANE, CONV_BLK), lambda j, i: (0, j)),
            pl.BlockSpec((1, CONV_BLK), lambda j, i: (0, j)),
        ],
        out_specs=[
            pl.BlockSpec((CONV_TILE, CONV_BLK), lambda j, i: (i, j)),
            pl.BlockSpec((SUBLANE, CONV_BLK), lambda j, i: (0, j)),
            pl.BlockSpec((1, CONV_BLK), lambda j, i: (0, j)),
        ],
        out_shape=[
            jax.ShapeDtypeStruct((L, CONV_DIM), BF16),
            jax.ShapeDtypeStruct((SUBLANE, CONV_DIM), F32),
            jax.ShapeDtypeStruct((1, CONV_DIM), F32),
        ],
        compiler_params=_cparams(("arbitrary", "arbitrary")),
    )(zx, zx, zx, dact, dact, w8, b)


HI = lax.Precision.HIGHEST
XS_COL0 = 0
B_COL0 = D_INNER // D_STATE
C_COL0 = B_COL0 + SSM_GROUPS
DT_COL = (D_INNER + CONV_DIM) // DT_PAD


def _dot(a, b, dims=(((1,), (0,)), ((), ())), precision=None):
    return lax.dot_general(a, b, dims, precision=precision, preferred_element_type=F32)


_DOT_DIMS = {
    "nn": (((1,), (0,)), ((), ())),
    "nt": (((1,), (1,)), ((), ())),
    "tn": (((0,), (0,)), ((), ())),
}


@functools.partial(jax.custom_vjp, nondiff_argnums=(2,))
def _bdot(a, b, mode):
    return _dot(a.astype(BF16), b.astype(BF16), _DOT_DIMS[mode])


def _bdot_fwd(a, b, mode):
    return _bdot(a, b, mode), (a, b)


def _bdot_bwd(mode, res, g):
    a, b = res
    if mode == "nn":
        return _bdot(g, b, "nt"), _bdot(a, g, "tn")
    if mode == "nt":
        return _bdot(g, b, "nn"), _bdot(g, a, "tn")
    return _bdot(b, g, "nt"), _bdot(a, g, "nn")


_bdot.defvjp(_bdot_fwd, _bdot_bwd)


def _softplus(x):
    return jnp.maximum(x, 0.0) + jnp.log(1.0 + jnp.exp(-jnp.abs(x)))


def _split2(x):
    hi = x.astype(BF16)
    return hi, (x - hi.astype(F32)).astype(BF16)


@jax.custom_vjp
def _sel_left(m, mT, x):
    hi, lo = _split2(x)
    out = _dot(m, jnp.concatenate([hi, lo], axis=1))
    n = x.shape[1]
    return out[:, :n] + out[:, n:]


_sel_left.defvjp(lambda m, mT, x: (_sel_left(m, mT, x), (m, mT)),
                 lambda res, g: (jnp.zeros_like(res[0]), jnp.zeros_like(res[1]), _sel_left(res[1], res[0], g)))


@jax.custom_vjp
def _sel_right(x, m, mT):
    hi, lo = _split2(x)
    out = _dot(jnp.concatenate([hi, lo], axis=0), m)
    r = x.shape[0]
    return out[:r] + out[r:]


_sel_right.defvjp(lambda x, m, mT: (_sel_right(x, m, mT), (m, mT)),
                  lambda res, g: (_sel_right(g, res[1], res[0]), jnp.zeros_like(res[0]), jnp.zeros_like(res[1])))


def _ssd_consts():
    l_ = lax.broadcasted_iota(jnp.int32, (CHUNK, GROUP_W), 0)
    c_ = lax.broadcasted_iota(jnp.int32, (CHUNK, GROUP_W), 1)
    s_ = c_ % CHUNK
    causal = s_ <= l_
    eye_t = (s_ == l_).astype(F32)
    r0 = lax.broadcasted_iota(jnp.int32, (CHUNK, CHUNK), 0)
    c0 = lax.broadcasted_iota(jnp.int32, (CHUNK, CHUNK), 1)
    tril = (c0 <= r0).astype(BF16)
    triu = (r0 <= c0).astype(BF16)
    rb = lax.broadcasted_iota(jnp.int32, (GROUP_W, GROUP_W), 0) // HEAD_P
    cb = lax.broadcasted_iota(jnp.int32, (GROUP_W, GROUP_W), 1) // HEAD_P
    blockdiag = rb == cb
    return causal, eye_t, tril, triu, blockdiag


def _ssd_expand(g):
    h = lax.broadcasted_iota(jnp.int32, (DT_PAD, GROUP_W), 0)
    c = lax.broadcasted_iota(jnp.int32, (DT_PAD, GROUP_W), 1)
    cT = lax.broadcasted_iota(jnp.int32, (GROUP_W, DT_PAD), 0)
    hT = lax.broadcasted_iota(jnp.int32, (GROUP_W, DT_PAD), 1)
    hpg = GROUP_W // HEAD_P
    return (h == g * hpg + c // HEAD_P).astype(BF16), (hT == g * hpg + cT // HEAD_P).astype(BF16)


def _ssd_chunk(S, xs, dt_raw, dt_bias, alog_e, d_e, Bm, Cm, E):
    causal, eye_t, tril, triu, blockdiag = _ssd_consts()
    ones = jnp.ones((CHUNK, CHUNK), BF16)
    dt = _softplus(dt_raw + dt_bias)
    dtx = _sel_right(dt, E[0], E[1])
    a = dtx * (-jnp.exp(alog_e))
    acs = _sel_left(tril, triu, a)
    rowv = _sel_left(ones, ones, acs * eye_t)
    seg = acs - rowv
    Lc = jnp.where(causal, jnp.exp(jnp.where(causal, seg, 0.0)), 0.0)
    xdt = xs * dtx
    Bt = jnp.concatenate([Bm] * 4, axis=0)
    CBc = _bdot(Cm, Bt, "nt")
    Xbd = jnp.where(blockdiag, jnp.concatenate([xdt] * 4, axis=0), 0.0)
    y_intra = _bdot(CBc * Lc, Xbd, "nn")
    y_inter = _bdot(Cm, S, "nn") * jnp.exp(acs)
    y = y_intra + y_inter + d_e * xs
    last = jnp.sum(a, axis=0, keepdims=True)
    dec_end = jnp.exp(last - acs)
    S_new = S * jnp.exp(last) + _bdot(Bm, xdt * dec_end, "tn")
    return S_new, y


SSD_GPS = 8
SSD_W = SSD_GPS * GROUP_W
SSD_N = SSD_GPS * D_STATE
SSD_STEPS = SSM_GROUPS // SSD_GPS


def _ssd_in_specs(cmap):
    return [
        pl.BlockSpec((CHUNK, SSD_W), lambda c, g: (cmap(c), g)),
        pl.BlockSpec((CHUNK, DT_PAD), lambda c, g: (cmap(c), DT_COL)),
        pl.BlockSpec((1, DT_PAD), lambda c, g: (0, 0)),
        pl.BlockSpec((1, SSD_W), lambda c, g: (0, g)),
        pl.BlockSpec((1, SSD_W), lambda c, g: (0, g)),
        pl.BlockSpec((CHUNK, SSD_N), lambda c, g: (cmap(c), B_COL0 // SSD_GPS + g)),
        pl.BlockSpec((CHUNK, SSD_N), lambda c, g: (cmap(c), C_COL0 // SSD_GPS + g)),
    ]


def _gw(u):
    return slice(u * GROUP_W, (u + 1) * GROUP_W)


def _gn(u):
    return slice(u * D_STATE, (u + 1) * D_STATE)


def _ssd_fwd(xbc, zx, dt_bias, alog_e, d_e, name):
    L = xbc.shape[0]
    nc = L // CHUNK

    def body(xs_ref, dtr_ref, bias_ref, alog_ref, d_ref, b_ref, c_ref, y_ref, st_ref, S_ref):
        c = pl.program_id(0)
        g0 = pl.program_id(1) * SSD_GPS if SSD_STEPS > 1 else 0

        @pl.when(c == 0)
        def _():
            for u in range(SSD_GPS):
                S_ref[g0 + u] = jnp.zeros((D_STATE, GROUP_W), F32)

        dtr, bias = dtr_ref[...], bias_ref[...]
        for u in range(SSD_GPS):
            S = S_ref[g0 + u]
            st_ref[u] = S
            S_new, y = _ssd_chunk(S, xs_ref[:, _gw(u)], dtr, bias, alog_ref[:, _gw(u)], d_ref[:, _gw(u)],
                                  b_ref[:, _gn(u)], c_ref[:, _gn(u)], _ssd_expand(g0 + u))
            y_ref[:, _gw(u)] = y
            S_ref[g0 + u] = S_new

    return pl.pallas_call(
        body,
        name=name,
        grid=(nc, SSD_STEPS),
        in_specs=_ssd_in_specs(lambda c: c),
        out_specs=[
            pl.BlockSpec((CHUNK, SSD_W), lambda c, g: (c, g)),
            pl.BlockSpec((None, SSD_GPS, D_STATE, GROUP_W), lambda c, g: (c, g, 0, 0)),
        ],
        out_shape=[
            jax.ShapeDtypeStruct((L, D_INNER), F32),
            jax.ShapeDtypeStruct((nc, SSM_GROUPS, D_STATE, GROUP_W), F32),
        ],
        scratch_shapes=[pltpu.VMEM((SSM_GROUPS, D_STATE, GROUP_W), F32)],
        compiler_params=_cparams(("arbitrary", "arbitrary")),
    )(xbc, zx, dt_bias, alog_e, d_e, xbc, xbc)


def _ssd_bwd(xbc, zx, dt_bias, alog_e, d_e, states, dy, name):
    L = xbc.shape[0]
    nc = L // CHUNK
    rev = lambda c: nc - 1 - c

    def body(xs_ref, dtr_ref, bias_ref, alog_ref, d_ref, b_ref, c_ref, st_ref, dy_ref,
             dxs_ref, db_ref, dc_ref, ddt_ref, dbias_ref, dalog_ref, dd_ref, dS_ref):
        c = pl.program_id(0)
        step = pl.program_id(1)
        g0 = step * SSD_GPS if SSD_STEPS > 1 else 0
        first = jnp.logical_and(c == 0, step == 0)

        @pl.when(c == 0)
        def _():
            for u in range(SSD_GPS):
                dS_ref[g0 + u] = jnp.zeros((D_STATE, GROUP_W), F32)
                dalog_ref[g0 + u] = jnp.zeros((1, GROUP_W), F32)
                dd_ref[g0 + u] = jnp.zeros((1, GROUP_W), F32)

        @pl.when(first)
        def _():
            dbias_ref[...] = jnp.zeros_like(dbias_ref)

        dtr, bias = dtr_ref[...], bias_ref[...]
        ddt_sum = jnp.zeros((CHUNK, DT_PAD), F32)
        dbias_sum = jnp.zeros((1, DT_PAD), F32)
        for u in range(SSD_GPS):
            E = _ssd_expand(g0 + u)
            _, vjp = jax.vjp(
                lambda S, xs, dtr_, bias_, alog, dsk, Bm, Cm: _ssd_chunk(S, xs, dtr_, bias_, alog, dsk, Bm, Cm, E),
                st_ref[u], xs_ref[:, _gw(u)], dtr, bias, alog_ref[:, _gw(u)], d_ref[:, _gw(u)],
                b_ref[:, _gn(u)], c_ref[:, _gn(u)])
            dS, dxs, ddtr, dbias, dalog, dd, dB, dC = vjp((dS_ref[g0 + u], dy_ref[:, _gw(u)]))
            dS_ref[g0 + u] = dS
            dxs_ref[:, _gw(u)] = dxs
            db_ref[:, _gn(u)] = dB
            dc_ref[:, _gn(u)] = dC
            dalog_ref[g0 + u] += dalog
            dd_ref[g0 + u] += dd
            ddt_sum = ddt_sum + ddtr
            dbias_sum = dbias_sum + dbias

        @pl.when(step == 0)
        def _():
            ddt_ref[...] = jnp.zeros_like(ddt_ref)

        ddt_ref[...] += ddt_sum
        dbias_ref[...] += dbias_sum

    whole3 = pl.BlockSpec((SSM_GROUPS, 1, GROUP_W), lambda c, g: (0, 0, 0))
    return pl.pallas_call(
        body,
        name=name,
        grid=(nc, SSD_STEPS),
        in_specs=_ssd_in_specs(rev) + [
            pl.BlockSpec((None, SSD_GPS, D_STATE, GROUP_W), lambda c, g: (rev(c), g, 0, 0)),
            pl.BlockSpec((CHUNK, SSD_W), lambda c, g: (rev(c), g)),
        ],
        out_specs=[
            pl.BlockSpec((CHUNK, SSD_W), lambda c, g: (rev(c), g)),
            pl.BlockSpec((CHUNK, SSD_N), lambda c, g: (rev(c), g)),
            pl.BlockSpec((CHUNK, SSD_N), lambda c, g: (rev(c), g)),
            pl.BlockSpec((CHUNK, DT_PAD), lambda c, g: (rev(c), 0)),
            pl.BlockSpec((1, DT_PAD), lambda c, g: (0, 0)),
            whole3,
            whole3,
        ],
        out_shape=[
            jax.ShapeDtypeStruct((L, D_INNER), F32),
            jax.ShapeDtypeStruct((L, SSM_GROUPS * D_STATE), F32),
            jax.ShapeDtypeStruct((L, SSM_GROUPS * D_STATE), F32),
            jax.ShapeDtypeStruct((L, DT_PAD), F32),
            jax.ShapeDtypeStruct((1, DT_PAD), F32),
            jax.ShapeDtypeStruct((SSM_GROUPS, 1, GROUP_W), F32),
            jax.ShapeDtypeStruct((SSM_GROUPS, 1, GROUP_W), F32),
        ],
        scratch_shapes=[pltpu.VMEM((SSM_GROUPS, D_STATE, GROUP_W), F32)],
        compiler_params=_cparams(("arbitrary", "arbitrary")),
    )(xbc, zx, dt_bias, alog_e, d_e, xbc, xbc, states, dy)


LOG2E = math.log2(math.e)
LN2 = math.log(2.0)
SB_Q_SCALE = SB_D ** -0.5 * LOG2E


def _sb_scores(q, kj):
    z2 = _dot(q, kj, dims=(((1,), (1,)), ((), ())))
    lb = jnp.minimum(z2, 0.0) - jnp.log2(1.0 + jnp.exp2(-jnp.abs(z2)))
    return lb, lb - z2


SB_TQ = 256
SB_TK = 512
SB_TC = SB_TK // 2


def _sb_after():
    j = lax.broadcasted_iota(jnp.int32, (2 * SB_TC, SB_TC), 0) % SB_TC
    s = lax.broadcasted_iota(jnp.int32, (2 * SB_TC, SB_TC), 1)
    return (j > s).astype(BF16)


def _sb_suffix(x, after):
    T = x.shape[0]
    hi = x.astype(BF16)
    lo = (x - hi.astype(F32)).astype(BF16)
    stacked = jnp.concatenate([jnp.concatenate([hi[:, :SB_TC], lo[:, :SB_TC]], axis=1),
                               jnp.concatenate([hi[:, SB_TC:], lo[:, SB_TC:]], axis=1)], axis=0)
    out = _dot(stacked, after)
    rs_right = jnp.sum(x[:, SB_TC:], axis=1, keepdims=True)
    return (jnp.concatenate([out[:T] + rs_right, out[T:]], axis=1),
            rs_right + jnp.sum(x[:, :SB_TC], axis=1, keepdims=True))


SB_ROWS_FWD = 256
SB_ROWS_BWD = 128


def _sb_seen(rows, row0, start):
    t = lax.broadcasted_iota(jnp.int32, (rows, SB_TK), 0) + row0
    s = lax.broadcasted_iota(jnp.int32, (rows, SB_TK), 1) + start
    return s < t


def _sb_walk(gd, groups, carry):
    carry = groups([gd], carry, True)
    carry = lax.cond(gd % 2 == 1, lambda c: groups([gd - 1], c, False), lambda c: c, carry)
    base = gd - 1 - gd % 2
    return lax.fori_loop(0, gd // 2, lambda n, c: groups([base - 2 * n, base - 2 * n - 1], c, False), carry)


def _sb_fwd(qs, kn, v, name):
    H, L, d = qs.shape
    nq = L // SB_TQ
    after = _sb_after()
    SB_ROWS = SB_ROWS_FWD

    def body(q_ref, k_ref, v_ref, aft_ref, o_ref):
        i = pl.program_id(1)
        gd = (i * SB_TQ) // SB_TK
        subs = range(SB_TQ // SB_ROWS)
        qr = [q_ref[r * SB_ROWS:(r + 1) * SB_ROWS, :] for r in subs]

        def prep(G, diag, r):
            start = pl.multiple_of(G * SB_TK, SB_TK)
            lb, lk = _sb_scores(qr[r], k_ref[pl.ds(start, SB_TK), :])
            seen = _sb_seen(SB_ROWS, i * SB_TQ + r * SB_ROWS, start) if diag else None
            if diag:
                lk = jnp.where(seen, lk, 0.0)
            between, rs = _sb_suffix(lk, aft_ref[...])
            return lb + between, rs, seen, v_ref[pl.ds(start, SB_TK), :]

        def groups(Gs, carry, diag):
            carry = list(carry)
            pre = [[prep(G, diag, r) for G in Gs] for r in subs]
            for r in subs:
                R, acc = carry[r]
                for s, rs, seen, vg in pre[r]:
                    w = jnp.exp2(s + R)
                    if diag:
                        w = jnp.where(seen, w, 0.0)
                    acc = acc + _dot(w.astype(BF16), vg)
                    R = R + rs
                carry[r] = (R, acc)
            return tuple(carry)

        zero = (jnp.zeros((SB_ROWS, 1), F32), jnp.zeros((SB_ROWS, d), F32))
        out = _sb_walk(gd, groups, tuple(zero for _ in subs))
        for r in subs:
            o_ref[r * SB_ROWS:(r + 1) * SB_ROWS, :] = out[r][1]

    return pl.pallas_call(
        body,
        name=name,
        grid=(H, nq),
        in_specs=[
            pl.BlockSpec((None, SB_TQ, d), lambda h, i: (h, i, 0)),
            pl.BlockSpec((None, L, d), lambda h, i: (h, 0, 0)),
            pl.BlockSpec((None, L, d), lambda h, i: (h, 0, 0)),
            pl.BlockSpec((2 * SB_TC, SB_TC), lambda h, i: (0, 0)),
        ],
        out_specs=pl.BlockSpec((None, SB_TQ, d), lambda h, i: (h, i, 0)),
        out_shape=jax.ShapeDtypeStruct((H, L, d), F32),
        compiler_params=_cparams(("parallel", "arbitrary")),
    )(qs, kn, v, after)


def _sb_bwd(qs, kn, v, o, do, name):
    H, L, d = qs.shape
    nq = L // SB_TQ
    after = _sb_after()
    SB_ROWS = SB_ROWS_BWD
    qsT = qs.transpose(0, 2, 1)
    do16 = do.astype(BF16)
    doT = do16.transpose(0, 2, 1)

    def body(q_ref, qT_ref, k_ref, v_ref, o_ref, do_ref, doT_ref, aft_ref, dq_ref, dkT_ref, dvT_ref):
        i = pl.program_id(1)

        @pl.when(i == 0)
        def _():
            dkT_ref[...] = jnp.zeros_like(dkT_ref)
            dvT_ref[...] = jnp.zeros_like(dvT_ref)

        gd = (i * SB_TQ) // SB_TK
        subs = range(SB_TQ // SB_ROWS)
        rows = [slice(r * SB_ROWS, (r + 1) * SB_ROWS) for r in subs]
        qr = [q_ref[rows[r], :] for r in subs]
        qT = [qT_ref[:, rows[r]] for r in subs]
        do_ = [do_ref[rows[r], :] for r in subs]
        doT_ = [doT_ref[:, rows[r]] for r in subs]
        D = [jnp.sum(o_ref[rows[r], :] * do_[r].astype(F32), axis=1, keepdims=True) for r in subs]

        def prep(G, diag, r):
            start = pl.multiple_of(G * SB_TK, SB_TK)
            kg = k_ref[pl.ds(start, SB_TK), :]
            lb, lk = _sb_scores(qr[r], kg)
            seen = _sb_seen(SB_ROWS, i * SB_TQ + r * SB_ROWS, start) if diag else None
            if diag:
                lk = jnp.where(seen, lk, 0.0)
            between, rs = _sb_suffix(lk, aft_ref[...])
            dw = _dot(do_[r], v_ref[pl.ds(start, SB_TK), :], dims=(((1,), (1,)), ((), ())))
            return start, kg, lb + between, rs, dw, jnp.exp2(lb), seen

        def groups(Gs_, carry, diag):
            carry = list(carry)
            pre = [[prep(G, diag, r) for G in Gs_] for r in subs]
            for r in subs:
                R, Gs, dq = carry[r]
                for start, kg, s, rs, dw, sig, seen in pre[r]:
                    w = jnp.exp2(s + R)
                    if diag:
                        w = jnp.where(seen, w, 0.0)
                    w16 = w.astype(BF16)
                    g = w16.astype(F32) * dw
                    dvT_ref[:, pl.ds(start, SB_TK)] += _dot(doT_[r], w16)
                    g_right, gsum = _sb_suffix(g, aft_ref[...])
                    dz = g - sig * (D[r] - Gs - g_right)
                    if diag:
                        dz = jnp.where(seen, dz, 0.0)
                    dz16 = dz.astype(BF16)
                    dq = dq + _dot(dz16, kg)
                    dkT_ref[:, pl.ds(start, SB_TK)] += _dot(qT[r], dz16)
                    R = R + rs
                    Gs = Gs + gsum
                carry[r] = (R, Gs, dq)
            return tuple(carry)

        zero1 = jnp.zeros((SB_ROWS, 1), F32)
        out = _sb_walk(gd, groups, tuple((zero1, zero1, jnp.zeros((SB_ROWS, d), F32)) for _ in subs))
        for r in subs:
            dq_ref[rows[r], :] = out[r][2]

    blk = pl.BlockSpec((None, SB_TQ, d), lambda h, i: (h, i, 0))
    blkT = pl.BlockSpec((None, d, SB_TQ), lambda h, i: (h, 0, i))
    full = pl.BlockSpec((None, L, d), lambda h, i: (h, 0, 0))
    fullT = pl.BlockSpec((None, d, L), lambda h, i: (h, 0, 0))
    return pl.pallas_call(
        body,
        name=name,
        grid=(H, nq),
        in_specs=[blk, blkT, full, full, blk, blk, blkT, pl.BlockSpec((2 * SB_TC, SB_TC), lambda h, i: (0, 0))],
        out_specs=[blk, fullT, fullT],
        out_shape=[jax.ShapeDtypeStruct((H, L, d), F32), jax.ShapeDtypeStruct((H, d, L), F32),
                   jax.ShapeDtypeStruct((H, d, L), F32)],
        compiler_params=_cparams(("parallel", "arbitrary")),
    )(qs, qsT, kn, v, o, do16, doT, after)


def _fold_heads(rows, name):
    def body(x_ref, o_ref):
        c = lax.broadcasted_iota(jnp.int32, (D_INNER, DT_PAD), 0)
        h = lax.broadcasted_iota(jnp.int32, (D_INNER, DT_PAD), 1)
        o_ref[...] = _dot(x_ref[...], (c // HEAD_P == h).astype(F32), precision=HI)

    return pl.pallas_call(
        body, name=name, out_shape=jax.ShapeDtypeStruct((SUBLANE, DT_PAD), F32),
        compiler_params=_cparams(),
    )(rows)


def _ffn_fwd(x, r, norm_w, w_in, w_out, tag):
    x1, h = _norm_fwd(x, r, norm_w, f"norm_ffn_fwd{tag}")
    gu = _mm(h, w_in, "nn", F32, f"ffn_in_fwd{tag}")
    act = _swiglu_fwd(gu, f"swiglu_fwd{tag}")
    f = _mm(act, w_out, "nn", F32, f"ffn_out_fwd{tag}")
    return x1, h, gu, act, f


def _ffn_bwd(x1, h, gu, act, norm_w, w_in, w_out, dres, tag):
    d16 = dres.astype(BF16)
    dact = _mm(d16, w_out, "nt", BF16, f"ffn_out_dx{tag}")
    g_w_out = _mm(act, d16, "tn", F32, f"ffn_out_dw{tag}")
    dgu = _swiglu_bwd(gu, dact, f"swiglu_bwd{tag}")
    dh = _mm(dgu, w_in, "nt", BF16, f"ffn_in_dx{tag}")
    g_w_in = _mm(h, dgu, "tn", F32, f"ffn_in_dw{tag}")
    dres, g_norm = _norm_bwd(x1, norm_w, dh, dres, f"norm_ffn_bwd{tag}")
    return dres, g_norm, g_w_in, g_w_out


def _local_step(x, target, W):
    L = x.shape[0]
    H = SB_HEADS
    nm0, nm1 = W["norm_mix"][0:1], W["norm_mix"][1:2]
    nf0, nf1 = W["norm_ffn"][0:1], W["norm_ffn"][1:2]

    _, h0 = _norm_fwd(x, None, nm0, "norm_mix_fwd0")
    zx = _mm(h0, W["ssm_w_in"], "nn", F32, "ssm_in_fwd")
    xbc = _conv_fwd(zx, W["conv_w8"], W["conv_b"], "conv_fwd")
    y, states = _ssd_fwd(xbc, zx, W["dt_bias"], W["alog_e"], W["d_e"], "ssd_fwd")
    yn = _gated_norm_fwd(y, zx, W["ssm_norm_w"], "gated_norm_fwd")
    mix0 = _mm(yn, W["ssm_w_out"], "nn", F32, "ssm_out_fwd")
    x1, h1, gu0, act0, f0 = _ffn_fwd(x, mix0, nf0, W["ffn_w_in"][0], W["ffn_w_out"][0], "0")

    x2, h2 = _norm_fwd(x1, f0, nm1, "norm_mix_fwd1")
    qkv = _mm(h2, W["sb_w_qkv"], "nn", F32, "sb_qkv_fwd")
    qkv_t = qkv.reshape(L, 3, H, SB_D).transpose(1, 2, 0, 3)
    q_r = qkv_t[0].reshape(H * L, SB_D)
    k_r = qkv_t[1].reshape(H * L, SB_D)
    qs = _qk_norm_fwd(q_r, W["sb_q_gain"], SB_Q_SCALE, "q_norm_fwd").reshape(H, L, SB_D)
    kn = _qk_norm_fwd(k_r, W["sb_k_gain"], 1.0, "k_norm_fwd").reshape(H, L, SB_D)
    vb = qkv_t[2].astype(BF16)
    o = _sb_fwd(qs, kn, vb, "sb_fwd")
    o_flat = o.transpose(1, 0, 2).reshape(L, D_MODEL).astype(BF16)
    mix1 = _mm(o_flat, W["sb_w_o"], "nn", F32, "sb_o_fwd")
    x3, h3, gu1, act1, f1 = _ffn_fwd(x2, mix1, nf1, W["ffn_w_in"][1], W["ffn_w_out"][1], "1")

    dres, sq = _loss_bwd(x3, f1, target, "loss")

    dres, g_nf1, g_fin1, g_fout1 = _ffn_bwd(x3, h3, gu1, act1, nf1, W["ffn_w_in"][1], W["ffn_w_out"][1], dres, "1")
    d16 = dres.astype(BF16)
    do_flat = _mm(d16, W["sb_w_o"], "nt", F32, "sb_o_dx")
    g_w_o = _mm(o_flat, d16, "tn", F32, "sb_o_dw")
    do = do_flat.reshape(L, H, SB_D).transpose(1, 0, 2)
    dqs, dknT, dvT = _sb_bwd(qs, kn, vb, o, do, "sb_bwd")
    dkn = dknT.transpose(0, 2, 1)
    dq, g_qg = _qk_norm_bwd(q_r, W["sb_q_gain"], dqs.reshape(H * L, SB_D), SB_Q_SCALE, LN2, "q_norm_bwd")
    dk, g_kg = _qk_norm_bwd(k_r, W["sb_k_gain"], dkn.reshape(H * L, SB_D), 1.0, LN2, "k_norm_bwd")
    dqkv = jnp.stack([dq.reshape(H, L, SB_D), dk.reshape(H, L, SB_D), dvT.transpose(0, 2, 1)])
    dqkv = dqkv.transpose(2, 0, 1, 3).reshape(L, 3 * D_MODEL).astype(BF16)
    dh2 = _mm(dqkv, W["sb_w_qkv"], "nt", BF16, "sb_qkv_dx")
    g_w_qkv = _mm(h2, dqkv, "tn", F32, "sb_qkv_dw")
    dres, g_nm1 = _norm_bwd(x2, nm1, dh2, dres, "norm_mix_bwd1")

    dres, g_nf0, g_fin0, g_fout0 = _ffn_bwd(x1, h1, gu0, act0, nf0, W["ffn_w_in"][0], W["ffn_w_out"][0], dres, "0")
    d16 = dres.astype(BF16)
    dyn = _mm(d16, W["ssm_w_out"], "nt", BF16, "ssm_out_dx")
    g_w_out = _mm(yn, d16, "tn", F32, "ssm_out_dw")
    dy, dz, g_snw = _gated_norm_bwd(y, zx, W["ssm_norm_w"], dyn, "gated_norm_bwd")
    dxs, dB, dC, ddt, g_dtb, g_alog_e, g_d_e = _ssd_bwd(
        xbc, zx, W["dt_bias"], W["alog_e"], W["d_e"], states, dy, "ssd_bwd")
    dact = jnp.concatenate([dxs, dB, dC], axis=1)
    dxbc, g_cw8, g_cb = _conv_bwd(zx, W["conv_w8"], W["conv_b"], dact, "conv_bwd")
    dzx = jnp.concatenate([dz, dxbc, ddt.astype(BF16)], axis=1)
    dh0 = _mm(dzx, W["ssm_w_in"], "nt", BF16, "ssm_in_dx")
    g_w_in = _mm(h0, dzx, "tn", F32, "ssm_in_dw")
    grad_x, g_nm0 = _norm_bwd(x, nm0, dh0, dres, "norm_mix_bwd0")

    per_head = jnp.concatenate(
        [g_alog_e.reshape(1, D_INNER), g_d_e.reshape(1, D_INNER), jnp.zeros((SUBLANE - 2, D_INNER), F32)], axis=0)
    folded = _fold_heads(per_head, "fold_heads")
    grads = {
        "norm_mix": jnp.concatenate([g_nm0, g_nm1], axis=0),
        "norm_ffn": jnp.concatenate([g_nf0, g_nf1], axis=0),
        "ssm_w_in": g_w_in[:, :D_IN_PROJ],
        "ssm_conv_w": g_cw8[:CONV_W],
        "ssm_conv_b": g_cb,
        "ssm_dt_bias": g_dtb[:, :SSM_HEADS],
        "ssm_a_log": folded[0:1, :SSM_HEADS],
        "ssm_d": folded[1:2, :SSM_HEADS],
        "ssm_norm_w": g_snw,
        "ssm_w_out": g_w_out,
        "sb_w_qkv": g_w_qkv,
        "sb_q_gain": g_qg,
        "sb_k_gain": g_kg,
        "sb_w_o": g_w_o,
        "ffn_w_in": jnp.stack([g_fin0, g_fin1]),
        "ffn_w_out": jnp.stack([g_fout0, g_fout1]),
    }
    return sq, grad_x, grads


def _prep_weights(full):
    W = {}
    W["norm_mix"] = full["norm_mix"]
    W["norm_ffn"] = full["norm_ffn"]
    w_in = full["ssm_w_in"]
    W["ssm_w_in"] = jnp.pad(w_in, ((0, 0), (0, D_IN_PAD - D_IN_PROJ))).astype(BF16)
    W["conv_w8"] = jnp.pad(full["ssm_conv_w"], ((0, SUBLANE - CONV_W), (0, 0)))
    W["conv_b"] = full["ssm_conv_b"]
    W["dt_bias"] = jnp.pad(full["ssm_dt_bias"], ((0, 0), (0, DT_PAD - SSM_HEADS)))
    W["alog_e"] = jnp.repeat(full["ssm_a_log"], HEAD_P, axis=1)
    W["d_e"] = jnp.repeat(full["ssm_d"], HEAD_P, axis=1)
    W["ssm_norm_w"] = full["ssm_norm_w"]
    W["ssm_w_out"] = full["ssm_w_out"].astype(BF16)
    W["sb_w_qkv"] = full["sb_w_qkv"].astype(BF16)
    W["sb_q_gain"] = full["sb_q_gain"]
    W["sb_k_gain"] = full["sb_k_gain"]
    W["sb_w_o"] = full["sb_w_o"].astype(BF16)
    W["ffn_w_in"] = full["ffn_w_in"].astype(BF16)
    W["ffn_w_out"] = full["ffn_w_out"].astype(BF16)
    return W


N_CHIPS = 4
N_DEV = 8
FLAT_W = D_MODEL
SHARDED = (
    ("ssm_w_in", (1, D_MODEL, D_IN_PROJ // N_CHIPS), 2),
    ("ssm_conv_w", (1, CONV_W, CONV_DIM // N_CHIPS), 2),
    ("ssm_w_out", (1, D_INNER // N_CHIPS, D_MODEL), 1),
    ("sb_w_qkv", (1, D_MODEL, 3 * D_MODEL // N_CHIPS), 2),
    ("sb_w_o", (1, D_MODEL // N_CHIPS, D_MODEL), 1),
    ("ffn_w_in", (2, D_MODEL, 2 * D_FF // N_CHIPS), 2),
    ("ffn_w_out", (2, D_FF // N_CHIPS, D_MODEL), 1),
)
REPLICATED = (
    ("norm_mix", (2, D_MODEL)), ("norm_ffn", (2, D_MODEL)), ("ssm_conv_b", (1, CONV_DIM)),
    ("ssm_norm_w", (1, D_INNER)), ("ssm_dt_bias", (1, SSM_HEADS)), ("ssm_a_log", (1, SSM_HEADS)),
    ("ssm_d", (1, SSM_HEADS)), ("sb_q_gain", (1, SB_D)), ("sb_k_gain", (1, SB_D)),
)
ADAM_TILE = 256
_USED_ROWS = sum(math.prod(s) for _, s, _ in SHARDED) // FLAT_W
FLAT_ROWS = -(-_USED_ROWS // ADAM_TILE) * ADAM_TILE
_USED_SMALL = sum(math.prod(s) for _, s in REPLICATED)
SMALL_ROWS = -(-_USED_SMALL // (SUBLANE * FLAT_W)) * SUBLANE


def _pack_shard(d):
    parts = [d[n].reshape(-1, FLAT_W) for n, _, _ in SHARDED]
    parts.append(jnp.zeros((FLAT_ROWS - _USED_ROWS, FLAT_W), parts[0].dtype))
    return jnp.concatenate(parts, axis=0)


def _unpack_shard(flat):
    out, r = {}, 0
    for n, s, _ in SHARDED:
        k = math.prod(s) // FLAT_W
        out[n] = flat[r:r + k].reshape(s)
        r += k
    return out


def _pack_small(d):
    parts = [d[n].reshape(-1) for n, _ in REPLICATED]
    parts.append(jnp.zeros((SMALL_ROWS * FLAT_W - _USED_SMALL,), parts[0].dtype))
    return jnp.concatenate(parts).reshape(SMALL_ROWS, FLAT_W)


def _unpack_small(flat):
    flat = flat.reshape(-1)
    out, r = {}, 0
    for n, s in REPLICATED:
        k = math.prod(s)
        out[n] = flat[r:r + k].reshape(s)
        r += k
    return out


ANY = pl.BlockSpec(memory_space=pl.ANY)


def _other_chips():
    x, y, c = lax.axis_index("x"), lax.axis_index("y"), lax.axis_index("c")
    return x, y, c, [(1 - x, y), (x, 1 - y), (1 - x, 1 - y)]


def _remote(src, dst, send_sem, recv_sem, dev):
    return pltpu.make_async_remote_copy(src_ref=src, dst_ref=dst, send_sem=send_sem, recv_sem=recv_sem,
                                        device_id=dev, device_id_type=MESH_T)


def _gather_weights(wb, ws):
    def body(wb_ref, ws_ref, gb_ref, gs_ref, send_sems, recv_sems, loc_sems):
        x, y, c, others = _other_chips()
        me = 2 * x + y
        pairs = ((wb_ref, gb_ref), (ws_ref, gs_ref))
        local = [pltpu.make_async_copy(src, dst.at[me], loc_sems.at[t]) for t, (src, dst) in enumerate(pairs)]
        for cp in local:
            cp.start()
        sends = []
        for k, (px, py) in enumerate(others):
            for t, (src, dst) in enumerate(pairs):
                cp = _remote(src, dst.at[me], send_sems.at[2 * k + t], recv_sems.at[2 * k + t], (px, py, c))
                cp.start()
                sends.append(cp)
        for k, (px, py) in enumerate(others):
            for t, (src, dst) in enumerate(pairs):
                _remote(src, dst.at[2 * px + py], send_sems.at[2 * k + t], recv_sems.at[2 * k + t],
                        (px, py, c)).wait_recv()
        for cp in sends:
            cp.wait_send()
        for cp in local:
            cp.wait()

    return pl.pallas_call(
        body, name="gather_weights",
        in_specs=[ANY, ANY], out_specs=[ANY, ANY],
        out_shape=[jax.ShapeDtypeStruct((N_CHIPS,) + wb.shape, wb.dtype),
                   jax.ShapeDtypeStruct((N_CHIPS,) + ws.shape, ws.dtype)],
        scratch_shapes=[pltpu.SemaphoreType.DMA((6,)), pltpu.SemaphoreType.DMA((6,)), pltpu.SemaphoreType.DMA((2,))],
    )(wb, ws)


def _scatter_grads(gpack):
    def body(g_ref, r_ref, send_sems, recv_sems):
        x, y, c, others = _other_chips()
        sends = []
        for k, (px, py) in enumerate(others):
            cp = _remote(g_ref.at[2 * px + py], r_ref.at[k], send_sems.at[k], recv_sems.at[k], (px, py, c))
            cp.start()
            sends.append(cp)
        for cp in sends:
            cp.wait_recv()
        for cp in sends:
            cp.wait_send()

    return pl.pallas_call(
        body, name="scatter_grads",
        in_specs=[ANY], out_specs=ANY,
        out_shape=jax.ShapeDtypeStruct((N_CHIPS - 1,) + gpack.shape[1:], gpack.dtype),
        scratch_shapes=[pltpu.SemaphoreType.DMA((3,)), pltpu.SemaphoreType.DMA((3,))],
    )(gpack)


def _exchange_partials(part, small):
    def body(p_ref, s_ref, ps_ref, sa_ref, send_sems, recv_sems, loc_sem):
        x, y, c = lax.axis_index("x"), lax.axis_index("y"), lax.axis_index("c")
        me = 4 * x + 2 * y + c
        local = pltpu.make_async_copy(s_ref, sa_ref.at[me], loc_sem)
        local.start()
        sib = _remote(p_ref, ps_ref, send_sems.at[0], recv_sems.at[0], (x, y, 1 - c))
        sib.start()
        sends = [sib]
        for k in range(1, N_DEV):
            fx, fy, fc = (k >> 2) & 1, (k >> 1) & 1, k & 1
            px, py, pc = x ^ fx, y ^ fy, c ^ fc
            cp = _remote(s_ref, sa_ref.at[me], send_sems.at[k], recv_sems.at[k], (px, py, pc))
            cp.start()
            sends.append(cp)
        sib.wait_recv()
        for k in range(1, N_DEV):
            fx, fy, fc = (k >> 2) & 1, (k >> 1) & 1, k & 1
            px, py, pc = x ^ fx, y ^ fy, c ^ fc
            _remote(s_ref, sa_ref.at[4 * px + 2 * py + pc], send_sems.at[k], recv_sems.at[k], (px, py, pc)).wait_recv()
        for cp in sends:
            cp.wait_send()
        local.wait()

    return pl.pallas_call(
        body, name="exchange_partials",
        in_specs=[ANY, ANY], out_specs=[ANY, ANY],
        out_shape=[jax.ShapeDtypeStruct(part.shape, part.dtype),
                   jax.ShapeDtypeStruct((N_DEV,) + small.shape, small.dtype)],
        scratch_shapes=[pltpu.SemaphoreType.DMA((N_DEV,)), pltpu.SemaphoreType.DMA((N_DEV,)), pltpu.SemaphoreType.DMA],
    )(part, small)


def _partial_sum(own, recv):
    R = own.shape[0]

    def body(o_ref, r_ref, p_ref):
        acc = o_ref[...]
        for k in range(N_CHIPS - 1):
            acc = acc + r_ref[k].astype(F32)
        p_ref[...] = acc

    return pl.pallas_call(
        body, name="grad_partial_sum", grid=(R // ADAM_TILE,),
        in_specs=[pl.BlockSpec((ADAM_TILE, FLAT_W), lambda i: (i, 0)),
                  pl.BlockSpec((N_CHIPS - 1, ADAM_TILE, FLAT_W), lambda i: (0, i, 0))],
        out_specs=pl.BlockSpec((ADAM_TILE, FLAT_W), lambda i: (i, 0)),
        out_shape=jax.ShapeDtypeStruct(own.shape, F32),
        compiler_params=_cparams(("parallel",)),
    )(own, recv)


def _adamw_math(w, g, m, v):
    m = ADAM_B1 * m + (1.0 - ADAM_B1) * g
    v = ADAM_B2 * v + (1.0 - ADAM_B2) * jnp.square(g)
    m_hat = m / (1.0 - ADAM_B1 ** ADAM_STEP)
    v_hat = v / (1.0 - ADAM_B2 ** ADAM_STEP)
    delta = -ADAM_LR * (m_hat / (jnp.sqrt(v_hat) + ADAM_EPS) + ADAM_WD * w)
    return delta, m, v


def _adamw(w, m, v, parts, tile, name):
    n, R, _ = parts.shape

    def body(w_ref, m_ref, v_ref, p_ref, g_ref, d_ref, nm_ref, nv_ref):
        g = p_ref[0]
        for k in range(1, n):
            g = g + p_ref[k]
        delta, nm, nv = _adamw_math(w_ref[...], g, m_ref[...], v_ref[...])
        g_ref[...] = g
        d_ref[...] = delta
        nm_ref[...] = nm
        nv_ref[...] = nv

    blk = pl.BlockSpec((tile, FLAT_W), lambda i: (i, 0))
    return pl.pallas_call(
        body, name=name, grid=(R // tile,),
        in_specs=[blk, blk, blk, pl.BlockSpec((n, tile, FLAT_W), lambda i: (0, i, 0))],
        out_specs=[blk] * 4,
        out_shape=[jax.ShapeDtypeStruct((R, FLAT_W), F32)] * 4,
        compiler_params=_cparams(("parallel",)),
    )(w, m, v, parts)


_NAMES = ("norm_mix", "norm_ffn", "ssm_w_in", "ssm_conv_w", "ssm_conv_b", "ssm_dt_bias", "ssm_a_log", "ssm_d",
          "ssm_norm_w", "ssm_w_out", "sb_w_qkv", "sb_q_gain", "sb_k_gain", "sb_w_o", "ffn_w_in", "ffn_w_out")


def _step(x, loss_target, w, m, v):
    cx, cy, cc = lax.axis_index("x"), lax.axis_index("y"), lax.axis_index("c")
    chip = 2 * cx + cy

    w_flat = _pack_shard(w)
    conv8 = jnp.pad(w["ssm_conv_w"][0], ((0, SUBLANE - CONV_W), (0, 0)))
    gb, gs = _gather_weights(w_flat.astype(BF16), conv8)
    shards = [_unpack_shard(gb[p]) for p in range(N_CHIPS)]
    full = {n: jnp.concatenate([s[n] for s in shards], axis=ax) for n, _, ax in SHARDED}
    full = {n: (a if n.startswith("ffn") else a[0]) for n, a in full.items()}
    full["ssm_conv_w"] = jnp.concatenate([gs[p, :CONV_W] for p in range(N_CHIPS)], axis=1)
    for n, _ in REPLICATED:
        full[n] = w[n]

    sq, grad_x, grads = _local_step(x[0], loss_target[0], _prep_weights(full))
    loss = lax.psum(0.5 * jnp.sum(sq) / D_MODEL, ("x", "y", "c"))

    blocks = []
    for s in range(N_CHIPS):
        d = {}
        for n, shape, ax in SHARDED:
            g = grads[n] if n.startswith("ffn") else grads[n][None]
            d[n] = lax.slice_in_dim(g, s * shape[ax], (s + 1) * shape[ax], axis=ax)
        blocks.append(_pack_shard(d))
    gpack = jnp.stack(blocks)
    own = lax.dynamic_index_in_dim(gpack, chip, axis=0, keepdims=False)
    recv = _scatter_grads(gpack.astype(BF16))
    part = _partial_sum(own, recv)
    part_sib, small_all = _exchange_partials(part, _pack_small(grads))
    parts = jnp.stack([part, part_sib])
    g_f, d_f, m_f, v_f = _adamw(w_flat, _pack_shard(m), _pack_shard(v), parts, ADAM_TILE, "adamw_sharded")
    g_s, d_s, m_s, v_s = _adamw(_pack_small(w), _pack_small(m), _pack_small(v), small_all, SUBLANE, "adamw_replicated")

    outs = []
    for big, small in ((g_f, g_s), (d_f, d_s), (m_f, m_s), (v_f, v_s)):
        d = {**_unpack_shard(big), **_unpack_small(small)}
        outs.append([d[n] for n in _NAMES])
    return loss, grad_x[None], outs


def kernel(x, norm_mix, norm_ffn, ssm_w_in, ssm_conv_w, ssm_conv_b, ssm_dt_bias, ssm_a_log, ssm_d, ssm_norm_w, ssm_w_out, sb_w_qkv, sb_q_gain, sb_k_gain, sb_w_o, ffn_w_in, ffn_w_out, loss_target, m_norm_mix, m_norm_ffn, m_ssm_w_in, m_ssm_conv_w, m_ssm_conv_b, m_ssm_dt_bias, m_ssm_a_log, m_ssm_d, m_ssm_norm_w, m_ssm_w_out, m_sb_w_qkv, m_sb_q_gain, m_sb_k_gain, m_sb_w_o, m_ffn_w_in, m_ffn_w_out, v_norm_mix, v_norm_ffn, v_ssm_w_in, v_ssm_conv_w, v_ssm_conv_b, v_ssm_dt_bias, v_ssm_a_log, v_ssm_d, v_ssm_norm_w, v_ssm_w_out, v_sb_w_qkv, v_sb_q_gain, v_sb_k_gain, v_sb_w_o, v_ffn_w_in, v_ffn_w_out):
    w = dict(zip(_NAMES, (norm_mix, norm_ffn, ssm_w_in, ssm_conv_w, ssm_conv_b, ssm_dt_bias, ssm_a_log, ssm_d,
                          ssm_norm_w, ssm_w_out, sb_w_qkv, sb_q_gain, sb_k_gain, sb_w_o, ffn_w_in, ffn_w_out)))
    m = dict(zip(_NAMES, (m_norm_mix, m_norm_ffn, m_ssm_w_in, m_ssm_conv_w, m_ssm_conv_b, m_ssm_dt_bias, m_ssm_a_log,
                          m_ssm_d, m_ssm_norm_w, m_ssm_w_out, m_sb_w_qkv, m_sb_q_gain, m_sb_k_gain, m_sb_w_o,
                          m_ffn_w_in, m_ffn_w_out)))
    v = dict(zip(_NAMES, (v_norm_mix, v_norm_ffn, v_ssm_w_in, v_ssm_conv_w, v_ssm_conv_b, v_ssm_dt_bias, v_ssm_a_log,
                          v_ssm_d, v_ssm_norm_w, v_ssm_w_out, v_sb_w_qkv, v_sb_q_gain, v_sb_k_gain, v_sb_w_o,
                          v_ffn_w_in, v_ffn_w_out)))
    loss, grad_x, (g, d, nm, nv) = _step(x, loss_target, w, m, v)
    return (loss, grad_x, *g, *d, *nm, *nv)
```

```python
import functools
import math

import jax
import jax.numpy as jnp
from jax import lax
from jax.experimental import pallas as pl
from jax.experimental.pallas import tpu as pltpu

F32 = jnp.float32
BF16 = jnp.bfloat16

VMEM_LIMIT_BYTES = 48 * 1024 * 1024
LANE = 128
SUBLANE = 8

D_MODEL = 1024
CHUNK = 64
D_INNER = 2048
SSM_HEADS = 32
SSM_GROUPS = 8
GROUP_W = D_INNER // SSM_GROUPS
HEAD_P = 64
D_STATE = 128
CONV_W = 4
CONV_DIM = D_INNER + 2 * SSM_GROUPS * D_STATE
D_IN_PROJ = D_INNER + CONV_DIM + SSM_HEADS
DT_PAD = LANE
D_IN_PAD = D_INNER + CONV_DIM + DT_PAD
SB_HEADS = 16
SB_D = 64
D_FF = 2816
NORM_EPS = 1e-6
GATED_NORM_EPS = 1e-5

ADAM_LR = 0.001
ADAM_B1 = 0.9
ADAM_B2 = 0.999
ADAM_EPS = 1e-08
ADAM_WD = 0.01
ADAM_STEP = 10

MESH_T = pl.DeviceIdType.MESH


def _cparams(sem=None):
    return pltpu.CompilerParams(dimension_semantics=sem, vmem_limit_bytes=VMEM_LIMIT_BYTES)


def _tile(n, target, align):
    best = None
    for t in range(align, min(n, target) + 1, align):
        if n % t == 0:
            best = t
    return best or n


def _mm(a, b, mode, out_dtype, name):
    if mode == "nn":
        (M, K), N = a.shape, b.shape[1]
    elif mode == "nt":
        (M, K), N = a.shape, b.shape[0]
    else:
        (K, M), N = a.shape, b.shape[1]
    tm = _tile(M, 512, LANE)
    tn = _tile(N, 1536, LANE)
    tk = _tile(K, 1536, LANE)
    nk = K // tk
    if mode == "nn":
        a_spec = pl.BlockSpec((tm, tk), lambda i, j, k: (i, k))
        b_spec = pl.BlockSpec((tk, tn), lambda i, j, k: (k, j))
        dims = (((1,), (0,)), ((), ()))
    elif mode == "nt":
        a_spec = pl.BlockSpec((tm, tk), lambda i, j, k: (i, k))
        b_spec = pl.BlockSpec((tn, tk), lambda i, j, k: (j, k))
        dims = (((1,), (1,)), ((), ()))
    else:
        a_spec = pl.BlockSpec((tk, tm), lambda i, j, k: (k, i))
        b_spec = pl.BlockSpec((tk, tn), lambda i, j, k: (k, j))
        dims = (((0,), (0,)), ((), ()))

    def body(a_ref, b_ref, o_ref, acc_ref):
        k = pl.program_id(2)

        @pl.when(k == 0)
        def _():
            acc_ref[...] = jnp.zeros_like(acc_ref)

        acc_ref[...] += lax.dot_general(a_ref[...], b_ref[...], dims, preferred_element_type=F32)

        @pl.when(k == nk - 1)
        def _():
            o_ref[...] = acc_ref[...].astype(o_ref.dtype)

    return pl.pallas_call(
        body,
        name=name,
        grid=(M // tm, N // tn, nk),
        in_specs=[a_spec, b_spec],
        out_specs=pl.BlockSpec((tm, tn), lambda i, j, k: (i, j)),
        out_shape=jax.ShapeDtypeStruct((M, N), out_dtype),
        scratch_shapes=[pltpu.VMEM((tm, tn), F32)],
        compiler_params=_cparams(("parallel", "parallel", "arbitrary")),
    )(a, b)


def _rows(fn, name, tile, row_ins, const_ins, row_outs, acc_outs=(), ncol=1):
    L = row_ins[0][0].shape[0]
    tile = min(tile, L)
    nrow = L // tile
    n_ri, n_ci, n_ro, n_ao = len(row_ins), len(const_ins), len(row_outs), len(acc_outs)

    def body(*refs):
        i = pl.program_id(1)
        j = pl.program_id(0)
        ins = [r[...] for r in refs[: n_ri + n_ci]]
        outs = fn(*ins, j=j)
        o_refs = refs[n_ri + n_ci:]
        for r, v in zip(o_refs[:n_ro], outs[:n_ro]):
            r[...] = v.astype(r.dtype)

        @pl.when(i == 0)
        def _():
            for r in o_refs[n_ro:]:
                r[...] = jnp.zeros_like(r)

        for r, v in zip(o_refs[n_ro:], outs[n_ro:]):
            r[...] += v

    def rspec(bc, cf):
        return pl.BlockSpec((tile, bc), lambda j, i: (i, cf(j)))

    def cspec(r, bc, cf):
        return pl.BlockSpec((r, bc), lambda j, i: (0, cf(j)))

    in_specs = [rspec(bc, cf) for (_, bc, cf) in row_ins]
    in_specs += [cspec(a.shape[0], bc, cf) for (a, bc, cf) in const_ins]
    out_specs = [rspec(bc, cf) for (_, _, bc, cf) in row_outs]
    out_specs += [cspec(1, bc, cf) for (_, bc, cf) in acc_outs]
    out_shape = [jax.ShapeDtypeStruct((L, c), dt) for (c, dt, _, _) in row_outs]
    out_shape += [jax.ShapeDtypeStruct((1, c), F32) for (c, _, _) in acc_outs]
    res = pl.pallas_call(
        body,
        name=name,
        grid=(ncol, nrow),
        in_specs=in_specs,
        out_specs=out_specs,
        out_shape=out_shape,
        compiler_params=_cparams(("arbitrary", "arbitrary")),
    )(*[a for (a, _, _) in row_ins], *[a for (a, _, _) in const_ins])
    return res


def _zero(j):
    return 0


def _whole(a):
    return (a, a.shape[1], _zero)


def _rms(x, w, eps):
    return x * lax.rsqrt(jnp.mean(x * x, axis=-1, keepdims=True) + eps) * w


ROW_TILE = 256
COL_ROW_TILE = 1024


def _norm_fwd(x, r, w, name):
    C = x.shape[1]
    if r is None:
        def fn(x_, w_, j):
            return (_rms(x_, w_, NORM_EPS),)
        (h,) = _rows(fn, name, ROW_TILE, [_whole(x)], [_whole(w)], [(C, BF16, C, _zero)])
        return x, h

    def fn(x_, r_, w_, j):
        x1 = x_ + r_
        return x1, _rms(x1, w_, NORM_EPS)

    x1, h = _rows(fn, name, ROW_TILE, [_whole(x), _whole(r)], [_whole(w)],
                  [(C, F32, C, _zero), (C, BF16, C, _zero)])
    return x1, h


def _norm_bwd(x, w, dh, dres, name):
    C = x.shape[1]

    def fn(x_, dh_, dres_, w_, j):
        _, vjp = jax.vjp(lambda a, b: _rms(a, b, NORM_EPS), x_, w_)
        dx, dw = vjp(dh_.astype(F32))
        return dres_ + dx, dw

    return _rows(fn, name, ROW_TILE, [_whole(x), _whole(dh), _whole(dres)], [_whole(w)],
                 [(C, F32, C, _zero)], [(C, C, _zero)])


def _silu(x):
    return x * jax.nn.sigmoid(x)


FF_BLK = 256
FF_NB = D_FF // FF_BLK


def _swiglu_fwd(gu, name):
    def fn(g, u, j):
        return (_silu(g) * u,)
    (act,) = _rows(fn, name, COL_ROW_TILE,
                   [(gu, FF_BLK, lambda j: j), (gu, FF_BLK, lambda j: j + FF_NB)], [],
                   [(D_FF, BF16, FF_BLK, lambda j: j)], ncol=FF_NB)
    return act


def _swiglu_bwd(gu, dact, name):
    def fn(g, u, da, j):
        _, vjp = jax.vjp(lambda a, b: _silu(a) * b, g, u)
        dg, du = vjp(da.astype(F32))
        return (jnp.where(j < FF_NB, dg, du),)
    (dgu,) = _rows(fn, name, COL_ROW_TILE,
                   [(gu, FF_BLK, lambda j: j % FF_NB), (gu, FF_BLK, lambda j: j % FF_NB + FF_NB),
                    (dact, FF_BLK, lambda j: j % FF_NB)], [],
                   [(2 * D_FF, BF16, FF_BLK, lambda j: j)], ncol=2 * FF_NB)
    return dgu


def _gated_norm_fwd(y, zx, w, name):
    def fn(y_, z_, w_, j):
        return (_rms(y_ * _silu(z_), w_, GATED_NORM_EPS),)
    (yn,) = _rows(fn, name, COL_ROW_TILE,
                  [(y, GROUP_W, lambda j: j), (zx, GROUP_W, lambda j: j)], [(w, GROUP_W, lambda j: j)],
                  [(D_INNER, BF16, GROUP_W, lambda j: j)], ncol=SSM_GROUPS)
    return yn


def _gated_norm_bwd(y, zx, w, dyn, name):
    def fn(y_, z_, dyn_, w_, j):
        _, vjp = jax.vjp(lambda a, b, c: _rms(a * _silu(b), c, GATED_NORM_EPS), y_, z_, w_)
        return vjp(dyn_.astype(F32))
    cj = lambda j: j
    return _rows(fn, name, COL_ROW_TILE,
                 [(y, GROUP_W, cj), (zx, GROUP_W, cj), (dyn, GROUP_W, cj)], [(w, GROUP_W, cj)],
                 [(D_INNER, F32, GROUP_W, cj), (D_INNER, BF16, GROUP_W, cj)], [(D_INNER, GROUP_W, cj)],
                 ncol=SSM_GROUPS)


def _loss_bwd(x3, f, target, name):
    C = x3.shape[1]

    def fn(x_, f_, t_, j):
        err = (x_ + f_) - t_
        return err * (1.0 / C), jnp.sum(err * err, axis=0, keepdims=True)

    return _rows(fn, name, ROW_TILE, [_whole(x3), _whole(f), _whole(target)], [],
                 [(C, F32, C, _zero)], [(C, C, _zero)])


def _qk_norm_fwd(q, gain, scale, name):
    def fn(q_, g_, j):
        return (_rms(q_, g_, NORM_EPS) * scale,)
    (qn,) = _rows(fn, name, 2048, [_whole(q)], [_whole(gain)], [(SB_D, BF16, SB_D, _zero)])
    return qn


def _qk_norm_bwd(q, gain, dqn, scale, cot_scale, name):
    def fn(q_, dqn_, g_, j):
        _, vjp = jax.vjp(lambda a, b: _rms(a, b, NORM_EPS) * scale, q_, g_)
        return vjp(dqn_ * cot_scale)
    return _rows(fn, name, 2048, [_whole(q), _whole(dqn)], [_whole(gain)],
                 [(SB_D, F32, SB_D, _zero)], [(SB_D, SB_D, _zero)])


CONV_TILE = 512
CONV_BLK = 512
XBC_COL0 = D_INNER // CONV_BLK


def _shift_down(cur, prev8, k):
    if k == 0:
        return cur
    rolled = pltpu.roll(cur, k, 0)
    head_prev = pltpu.roll(prev8, k, 0)
    rid = lax.broadcasted_iota(jnp.int32, (SUBLANE, cur.shape[1]), 0)
    head = jnp.where(rid < k, head_prev, rolled[:SUBLANE])
    if cur.shape[0] == SUBLANE:
        return head
    return jnp.concatenate([head, rolled[SUBLANE:]], axis=0)


def _shift_up(cur, next8, k):
    if k == 0:
        return cur
    T = cur.shape[0]
    rolled = pltpu.roll(cur, T - k, 0)
    tail_next = pltpu.roll(next8, SUBLANE - k, 0)
    rid = lax.broadcasted_iota(jnp.int32, (SUBLANE, cur.shape[1]), 0)
    tail = jnp.where(rid >= SUBLANE - k, tail_next, rolled[T - SUBLANE:])
    return jnp.concatenate([rolled[: T - SUBLANE], tail], axis=0)


def _conv_pre(cur, prev8, w, b):
    pre = b
    for i in range(CONV_W):
        pre = pre + w[i:i + 1, :] * _shift_down(cur, prev8, CONV_W - 1 - i)
    return pre


def _conv_fwd(zx, w8, b, name):
    L = zx.shape[0]
    nrow = L // CONV_TILE
    r8 = CONV_TILE // SUBLANE

    def body(cur_ref, prev_ref, w_ref, b_ref, o_ref):
        i = pl.program_id(1)
        prev8 = jnp.where(i > 0, prev_ref[...], 0.0)
        pre = _conv_pre(cur_ref[...], prev8, w_ref[...], b_ref[...])
        o_ref[...] = _silu(pre)

    return pl.pallas_call(
        body,
        name=name,
        grid=(CONV_DIM // CONV_BLK, nrow),
        in_specs=[
            pl.BlockSpec((CONV_TILE, CONV_BLK), lambda j, i: (i, j + XBC_COL0)),
            pl.BlockSpec((SUBLANE, CONV_BLK), lambda j, i: (jnp.maximum(i * r8 - 1, 0), j + XBC_COL0)),
            pl.BlockSpec((SUBLANE, CONV_BLK), lambda j, i: (0, j)),
            pl.BlockSpec((1, CONV_BLK), lambda j, i: (0, j)),
        ],
        out_specs=pl.BlockSpec((CONV_TILE, CONV_BLK), lambda j, i: (i, j)),
        out_shape=jax.ShapeDtypeStruct((L, CONV_DIM), F32),
        compiler_params=_cparams(("arbitrary", "arbitrary")),
    )(zx, zx, w8, b)


def _conv_bwd(zx, w8, b, dact, name):
    L = zx.shape[0]
    nrow = L // CONV_TILE
    r8 = CONV_TILE // SUBLANE
    last8 = L // SUBLANE - 1

    def body(cur_ref, prev_ref, next_ref, da_ref, dan_ref, w_ref, b_ref, du_ref, dw_ref, db_ref):
        i = pl.program_id(1)
        w = w_ref[...]
        b_ = b_ref[...]
        cur = cur_ref[...]
        prev8 = jnp.where(i > 0, prev_ref[...], 0.0)
        pre = _conv_pre(cur, prev8, w, b_)
        _, vjp = jax.vjp(_silu, pre)
        (dpre,) = vjp(da_ref[...])
        nxt = next_ref[...]
        pre_n = _conv_pre(nxt, cur[CONV_TILE - SUBLANE:], w, b_)
        _, vjp_n = jax.vjp(_silu, pre_n)
        (dpre_n,) = vjp_n(dan_ref[...])
        dpre_n = jnp.where(i < nrow - 1, dpre_n, 0.0)
        du = jnp.zeros_like(cur)
        dws = []
        for k in range(CONV_W):
            wk = w[CONV_W - 1 - k:CONV_W - k, :]
            du = du + wk * _shift_up(dpre, dpre_n, k)
            dws.append(jnp.sum(dpre * _shift_down(cur, prev8, k), axis=0, keepdims=True))
        du_ref[...] = du.astype(du_ref.dtype)
        dw_tile = jnp.concatenate([dws[3], dws[2], dws[1], dws[0]] + [jnp.zeros_like(dws[0])] * 4, axis=0)

        @pl.when(i == 0)
        def _():
            dw_ref[...] = jnp.zeros_like(dw_ref)
            db_ref[...] = jnp.zeros_like(db_ref)

        dw_ref[...] += dw_tile
        db_ref[...] += jnp.sum(dpre, axis=0, keepdims=True)

    return pl.pallas_call(
        body,
        name=name,
        grid=(CONV_DIM // CONV_BLK, nrow),
        in_specs=[
            pl.BlockSpec((CONV_TILE, CONV_BLK), lambda j, i: (i, j + XBC_COL0)),
            pl.BlockSpec((SUBLANE, CONV_BLK), lambda j, i: (jnp.maximum(i * r8 - 1, 0), j + XBC_COL0)),
            pl.BlockSpec((SUBLANE, CONV_BLK), lambda j, i: (jnp.minimum((i + 1) * r8, last8), j + XBC_COL0)),
            pl.BlockSpec((CONV_TILE, CONV_BLK), lambda j, i: (i, j)),
            pl.BlockSpec((SUBLANE, CONV_BLK), lambda j, i: (jnp.minimum((i + 1) * r8, last8), j)),
            pl.BlockSpec((SUBLANE, CONV_BLK), lambda j, i: (0, j)),
            pl.BlockSpec((1, CONV_BLK), lambda j, i: (0, j)),
        ],
        out_specs=[
            pl.BlockSpec((CONV_TILE, CONV_BLK), lambda j, i: (i, j)),
            pl.BlockSpec((SUBLANE, CONV_BLK), lambda j, i: (0, j)),
            pl.BlockSpec((1, CONV_BLK), lambda j, i: (0, j)),
        ],
        out_shape=[
            jax.ShapeDtypeStruct((L, CONV_DIM), BF16),
            jax.ShapeDtypeStruct((SUBLANE, CONV_DIM), F32),
            jax.ShapeDtypeStruct((1, CONV_DIM), F32),
        ],
        compiler_params=_cparams(("arbitrary", "arbitrary")),
    )(zx, zx, zx, dact, dact, w8, b)


HI = lax.Precision.HIGHEST
XS_COL0 = 0
B_COL0 = D_INNER // D_STATE
C_COL0 = B_COL0 + SSM_GROUPS
DT_COL = (D_INNER + CONV_DIM) // DT_PAD


def _dot(a, b, dims=(((1,), (0,)), ((), ())), precision=None):
    return lax.dot_general(a, b, dims, precision=precision, preferred_element_type=F32)


_DOT_DIMS = {
    "nn": (((1,), (0,)), ((), ())),
    "nt": (((1,), (1,)), ((), ())),
    "tn": (((0,), (0,)), ((), ())),
}


@functools.partial(jax.custom_vjp, nondiff_argnums=(2,))
def _bdot(a, b, mode):
    return _dot(a.astype(BF16), b.astype(BF16), _DOT_DIMS[mode])


def _bdot_fwd(a, b, mode):
    return _bdot(a, b, mode), (a, b)


def _bdot_bwd(mode, res, g):
    a, b = res
    if mode == "nn":
        return _bdot(g, b, "nt"), _bdot(a, g, "tn")
    if mode == "nt":
        return _bdot(g, b, "nn"), _bdot(g, a, "tn")
    return _bdot(b, g, "nt"), _bdot(a, g, "nn")


_bdot.defvjp(_bdot_fwd, _bdot_bwd)


def _softplus(x):
    return jnp.maximum(x, 0.0) + jnp.log(1.0 + jnp.exp(-jnp.abs(x)))


def _split2(x):
    hi = x.astype(BF16)
    return hi, (x - hi.astype(F32)).astype(BF16)


@jax.custom_vjp
def _sel_left(m, mT, x):
    hi, lo = _split2(x)
    out = _dot(m, jnp.concatenate([hi, lo], axis=1))
    n = x.shape[1]
    return out[:, :n] + out[:, n:]


_sel_left.defvjp(lambda m, mT, x: (_sel_left(m, mT, x), (m, mT)),
                 lambda res, g: (jnp.zeros_like(res[0]), jnp.zeros_like(res[1]), _sel_left(res[1], res[0], g)))


@jax.custom_vjp
def _sel_right(x, m, mT):
    hi, lo = _split2(x)
    out = _dot(jnp.concatenate([hi, lo], axis=0), m)
    r = x.shape[0]
    return out[:r] + out[r:]


_sel_right.defvjp(lambda x, m, mT: (_sel_right(x, m, mT), (m, mT)),
                  lambda res, g: (_sel_right(g, res[1], res[0]), jnp.zeros_like(res[0]), jnp.zeros_like(res[1])))


def _ssd_consts():
    l_ = lax.broadcasted_iota(jnp.int32, (CHUNK, GROUP_W), 0)
    c_ = lax.broadcasted_iota(jnp.int32, (CHUNK, GROUP_W), 1)
    s_ = c_ % CHUNK
    causal = s_ <= l_
    eye_t = (s_ == l_).astype(F32)
    r0 = lax.broadcasted_iota(jnp.int32, (CHUNK, CHUNK), 0)
    c0 = lax.broadcasted_iota(jnp.int32, (CHUNK, CHUNK), 1)
    tril = (c0 <= r0).astype(BF16)
    triu = (r0 <= c0).astype(BF16)
    rb = lax.broadcasted_iota(jnp.int32, (GROUP_W, GROUP_W), 0) // HEAD_P
    cb = lax.broadcasted_iota(jnp.int32, (GROUP_W, GROUP_W), 1) // HEAD_P
    blockdiag = rb == cb
    return causal, eye_t, tril, triu, blockdiag


def _ssd_expand(g):
    h = lax.broadcasted_iota(jnp.int32, (DT_PAD, GROUP_W), 0)
    c = lax.broadcasted_iota(jnp.int32, (DT_PAD, GROUP_W), 1)
    cT = lax.broadcasted_iota(jnp.int32, (GROUP_W, DT_PAD), 0)
    hT = lax.broadcasted_iota(jnp.int32, (GROUP_W, DT_PAD), 1)
    hpg = GROUP_W // HEAD_P
    return (h == g * hpg + c // HEAD_P).astype(BF16), (hT == g * hpg + cT // HEAD_P).astype(BF16)


def _ssd_chunk(S, xs, dt_raw, dt_bias, alog_e, d_e, Bm, Cm, E):
    causal, eye_t, tril, triu, blockdiag = _ssd_consts()
    ones = jnp.ones((CHUNK, CHUNK), BF16)
    dt = _softplus(dt_raw + dt_bias)
    dtx = _sel_right(dt, E[0], E[1])
    a = dtx * (-jnp.exp(alog_e))
    acs = _sel_left(tril, triu, a)
    rowv = _sel_left(ones, ones, acs * eye_t)
    seg = acs - rowv
    Lc = jnp.where(causal, jnp.exp(jnp.where(causal, seg, 0.0)), 0.0)
    xdt = xs * dtx
    Bt = jnp.concatenate([Bm] * 4, axis=0)
    CBc = _bdot(Cm, Bt, "nt")
    Xbd = jnp.where(blockdiag, jnp.concatenate([xdt] * 4, axis=0), 0.0)
    y_intra = _bdot(CBc * Lc, Xbd, "nn")
    y_inter = _bdot(Cm, S, "nn") * jnp.exp(acs)
    y = y_intra + y_inter + d_e * xs
    last = jnp.sum(a, axis=0, keepdims=True)
    dec_end = jnp.exp(last - acs)
    S_new = S * jnp.exp(last) + _bdot(Bm, xdt * dec_end, "tn")
    return S_new, y


SSD_GPS = 8
SSD_W = SSD_GPS * GROUP_W
SSD_N = SSD_GPS * D_STATE
SSD_STEPS = SSM_GROUPS // SSD_GPS


def _ssd_in_specs(cmap):
    return [
        pl.BlockSpec((CHUNK, SSD_W), lambda c, g: (cmap(c), g)),
        pl.BlockSpec((CHUNK, DT_PAD), lambda c, g: (cmap(c), DT_COL)),
        pl.BlockSpec((1, DT_PAD), lambda c, g: (0, 0)),
        pl.BlockSpec((1, SSD_W), lambda c, g: (0, g)),
        pl.BlockSpec((1, SSD_W), lambda c, g: (0, g)),
        pl.BlockSpec((CHUNK, SSD_N), lambda c, g: (cmap(c), B_COL0 // SSD_GPS + g)),
        pl.BlockSpec((CHUNK, SSD_N), lambda c, g: (cmap(c), C_COL0 // SSD_GPS + g)),
    ]


def _gw(u):
    return slice(u * GROUP_W, (u + 1) * GROUP_W)


def _gn(u):
    return slice(u * D_STATE, (u + 1) * D_STATE)


def _ssd_fwd(xbc, zx, dt_bias, alog_e, d_e, name):
    L = xbc.shape[0]
    nc = L // CHUNK

    def body(xs_ref, dtr_ref, bias_ref, alog_ref, d_ref, b_ref, c_ref, y_ref, st_ref, S_ref):
        c = pl.program_id(0)
        g0 = pl.program_id(1) * SSD_GPS if SSD_STEPS > 1 else 0

        @pl.when(c == 0)
        def _():
            for u in range(SSD_GPS):
                S_ref[g0 + u] = jnp.zeros((D_STATE, GROUP_W), F32)

        dtr, bias = dtr_ref[...], bias_ref[...]
        for u in range(SSD_GPS):
            S = S_ref[g0 + u]
            st_ref[u] = S
            S_new, y = _ssd_chunk(S, xs_ref[:, _gw(u)], dtr, bias, alog_ref[:, _gw(u)], d_ref[:, _gw(u)],
                                  b_ref[:, _gn(u)], c_ref[:, _gn(u)], _ssd_expand(g0 + u))
            y_ref[:, _gw(u)] = y
            S_ref[g0 + u] = S_new

    return pl.pallas_call(
        body,
        name=name,
        grid=(nc, SSD_STEPS),
        in_specs=_ssd_in_specs(lambda c: c),
        out_specs=[
            pl.BlockSpec((CHUNK, SSD_W), lambda c, g: (c, g)),
            pl.BlockSpec((None, SSD_GPS, D_STATE, GROUP_W), lambda c, g: (c, g, 0, 0)),
        ],
        out_shape=[
            jax.ShapeDtypeStruct((L, D_INNER), F32),
            jax.ShapeDtypeStruct((nc, SSM_GROUPS, D_STATE, GROUP_W), F32),
        ],
        scratch_shapes=[pltpu.VMEM((SSM_GROUPS, D_STATE, GROUP_W), F32)],
        compiler_params=_cparams(("arbitrary", "arbitrary")),
    )(xbc, zx, dt_bias, alog_e, d_e, xbc, xbc)


def _ssd_bwd(xbc, zx, dt_bias, alog_e, d_e, states, dy, name):
    L = xbc.shape[0]
    nc = L // CHUNK
    rev = lambda c: nc - 1 - c

    def body(xs_ref, dtr_ref, bias_ref, alog_ref, d_ref, b_ref, c_ref, st_ref, dy_ref,
             dxs_ref, db_ref, dc_ref, ddt_ref, dbias_ref, dalog_ref, dd_ref, dS_ref):
        c = pl.program_id(0)
        step = pl.program_id(1)
        g0 = step * SSD_GPS if SSD_STEPS > 1 else 0
        first = jnp.logical_and(c == 0, step == 0)

        @pl.when(c == 0)
        def _():
            for u in range(SSD_GPS):
                dS_ref[g0 + u] = jnp.zeros((D_STATE, GROUP_W), F32)
                dalog_ref[g0 + u] = jnp.zeros((1, GROUP_W), F32)
                dd_ref[g0 + u] = jnp.zeros((1, GROUP_W), F32)

        @pl.when(first)
        def _():
            dbias_ref[...] = jnp.zeros_like(dbias_ref)

        dtr, bias = dtr_ref[...], bias_ref[...]
        ddt_sum = jnp.zeros((CHUNK, DT_PAD), F32)
        dbias_sum = jnp.zeros((1, DT_PAD), F32)
        for u in range(SSD_GPS):
            E = _ssd_expand(g0 + u)
            _, vjp = jax.vjp(
                lambda S, xs, dtr_, bias_, alog, dsk, Bm, Cm: _ssd_chunk(S, xs, dtr_, bias_, alog, dsk, Bm, Cm, E),
                st_ref[u], xs_ref[:, _gw(u)], dtr, bias, alog_ref[:, _gw(u)], d_ref[:, _gw(u)],
                b_ref[:, _gn(u)], c_ref[:, _gn(u)])
            dS, dxs, ddtr, dbias, dalog, dd, dB, dC = vjp((dS_ref[g0 + u], dy_ref[:, _gw(u)]))
            dS_ref[g0 + u] = dS
            dxs_ref[:, _gw(u)] = dxs
            db_ref[:, _gn(u)] = dB
            dc_ref[:, _gn(u)] = dC
            dalog_ref[g0 + u] += dalog
            dd_ref[g0 + u] += dd
            ddt_sum = ddt_sum + ddtr
            dbias_sum = dbias_sum + dbias

        @pl.when(step == 0)
        def _():
            ddt_ref[...] = jnp.zeros_like(ddt_ref)

        ddt_ref[...] += ddt_sum
        dbias_ref[...] += dbias_sum

    whole3 = pl.BlockSpec((SSM_GROUPS, 1, GROUP_W), lambda c, g: (0, 0, 0))
    return pl.pallas_call(
        body,
        name=name,
        grid=(nc, SSD_STEPS),
        in_specs=_ssd_in_specs(rev) + [
            pl.BlockSpec((None, SSD_GPS, D_STATE, GROUP_W), lambda c, g: (rev(c), g, 0, 0)),
            pl.BlockSpec((CHUNK, SSD_W), lambda c, g: (rev(c), g)),
        ],
        out_specs=[
            pl.BlockSpec((CHUNK, SSD_W), lambda c, g: (rev(c), g)),
            pl.BlockSpec((CHUNK, SSD_N), lambda c, g: (rev(c), g)),
            pl.BlockSpec((CHUNK, SSD_N), lambda c, g: (rev(c), g)),
            pl.BlockSpec((CHUNK, DT_PAD), lambda c, g: (rev(c), 0)),
            pl.BlockSpec((1, DT_PAD), lambda c, g: (0, 0)),
            whole3,
            whole3,
        ],
        out_shape=[
            jax.ShapeDtypeStruct((L, D_INNER), F32),
            jax.ShapeDtypeStruct((L, SSM_GROUPS * D_STATE), F32),
            jax.ShapeDtypeStruct((L, SSM_GROUPS * D_STATE), F32),
            jax.ShapeDtypeStruct((L, DT_PAD), F32),
            jax.ShapeDtypeStruct((1, DT_PAD), F32),
            jax.ShapeDtypeStruct((SSM_GROUPS, 1, GROUP_W), F32),
            jax.ShapeDtypeStruct((SSM_GROUPS, 1, GROUP_W), F32),
        ],
        scratch_shapes=[pltpu.VMEM((SSM_GROUPS, D_STATE, GROUP_W), F32)],
        compiler_params=_cparams(("arbitrary", "arbitrary")),
    )(xbc, zx, dt_bias, alog_e, d_e, xbc, xbc, states, dy)


LOG2E = math.log2(math.e)
LN2 = math.log(2.0)
SB_Q_SCALE = SB_D ** -0.5 * LOG2E


def _sb_scores(q, kj):
    z2 = _dot(q, kj, dims=(((1,), (1,)), ((), ())))
    lb = jnp.minimum(z2, 0.0) - jnp.log2(1.0 + jnp.exp2(-jnp.abs(z2)))
    return lb, lb - z2


SB_TQ = 256
SB_TK = 512
SB_TC = SB_TK // 2


def _sb_after():
    j = lax.broadcasted_iota(jnp.int32, (2 * SB_TC, SB_TC), 0) % SB_TC
    s = lax.broadcasted_iota(jnp.int32, (2 * SB_TC, SB_TC), 1)
    return (j > s).astype(BF16)


def _sb_suffix(x, after):
    T = x.shape[0]
    hi = x.astype(BF16)
    lo = (x - hi.astype(F32)).astype(BF16)
    stacked = jnp.concatenate([jnp.concatenate([hi[:, :SB_TC], lo[:, :SB_TC]], axis=1),
                               jnp.concatenate([hi[:, SB_TC:], lo[:, SB_TC:]], axis=1)], axis=0)
    out = _dot(stacked, after)
    rs_right = jnp.sum(x[:, SB_TC:], axis=1, keepdims=True)
    return (jnp.concatenate([out[:T] + rs_right, out[T:]], axis=1),
            rs_right + jnp.sum(x[:, :SB_TC], axis=1, keepdims=True))


SB_ROWS_FWD = 256
SB_ROWS_BWD = 128


def _sb_seen(rows, row0, start):
    t = lax.broadcasted_iota(jnp.int32, (rows, SB_TK), 0) + row0
    s = lax.broadcasted_iota(jnp.int32, (rows, SB_TK), 1) + start
    return s < t


def _sb_walk(gd, groups, carry):
    carry = groups([gd], carry, True)
    carry = lax.cond(gd % 2 == 1, lambda c: groups([gd - 1], c, False), lambda c: c, carry)
    base = gd - 1 - gd % 2
    return lax.fori_loop(0, gd // 2, lambda n, c: groups([base - 2 * n, base - 2 * n - 1], c, False), carry)


def _sb_fwd(qs, kn, v, name):
    H, L, d = qs.shape
    nq = L // SB_TQ
    after = _sb_after()
    SB_ROWS = SB_ROWS_FWD

    def body(q_ref, k_ref, v_ref, aft_ref, o_ref):
        i = pl.program_id(1)
        gd = (i * SB_TQ) // SB_TK
        subs = range(SB_TQ // SB_ROWS)
        qr = [q_ref[r * SB_ROWS:(r + 1) * SB_ROWS, :] for r in subs]

        def prep(G, diag, r):
            start = pl.multiple_of(G * SB_TK, SB_TK)
            lb, lk = _sb_scores(qr[r], k_ref[pl.ds(start, SB_TK), :])
            seen = _sb_seen(SB_ROWS, i * SB_TQ + r * SB_ROWS, start) if diag else None
            if diag:
                lk = jnp.where(seen, lk, 0.0)
            between, rs = _sb_suffix(lk, aft_ref[...])
            return lb + between, rs, seen, v_ref[pl.ds(start, SB_TK), :]

        def groups(Gs, carry, diag):
            carry = list(carry)
            pre = [[prep(G, diag, r) for G in Gs] for r in subs]
            for r in subs:
                R, acc = carry[r]
                for s, rs, seen, vg in pre[r]:
                    w = jnp.exp2(s + R)
                    if diag:
                        w = jnp.where(seen, w, 0.0)
                    acc = acc + _dot(w.astype(BF16), vg)
                    R = R + rs
                carry[r] = (R, acc)
            return tuple(carry)

        zero = (jnp.zeros((SB_ROWS, 1), F32), jnp.zeros((SB_ROWS, d), F32))
        out = _sb_walk(gd, groups, tuple(zero for _ in subs))
        for r in subs:
            o_ref[r * SB_ROWS:(r + 1) * SB_ROWS, :] = out[r][1]

    return pl.pallas_call(
        body,
        name=name,
        grid=(H, nq),
        in_specs=[
            pl.BlockSpec((None, SB_TQ, d), lambda h, i: (h, i, 0)),
            pl.BlockSpec((None, L, d), lambda h, i: (h, 0, 0)),
            pl.BlockSpec((None, L, d), lambda h, i: (h, 0, 0)),
            pl.BlockSpec((2 * SB_TC, SB_TC), lambda h, i: (0, 0)),
        ],
        out_specs=pl.BlockSpec((None, SB_TQ, d), lambda h, i: (h, i, 0)),
        out_shape=jax.ShapeDtypeStruct((H, L, d), F32),
        compiler_params=_cparams(("parallel", "arbitrary")),
    )(qs, kn, v, after)


def _sb_bwd(qs, kn, v, o, do, name):
    H, L, d = qs.shape
    nq = L // SB_TQ
    after = _sb_after()
    SB_ROWS = SB_ROWS_BWD
    qsT = qs.transpose(0, 2, 1)
    do16 = do.astype(BF16)
    doT = do16.transpose(0, 2, 1)

    def body(q_ref, qT_ref, k_ref, v_ref, o_ref, do_ref, doT_ref, aft_ref, dq_ref, dkT_ref, dvT_ref):
        i = pl.program_id(1)

        @pl.when(i == 0)
        def _():
            dkT_ref[...] = jnp.zeros_like(dkT_ref)
            dvT_ref[...] = jnp.zeros_like(dvT_ref)

        gd = (i * SB_TQ) // SB_TK
        subs = range(SB_TQ // SB_ROWS)
        rows = [slice(r * SB_ROWS, (r + 1) * SB_ROWS) for r in subs]
        qr = [q_ref[rows[r], :] for r in subs]
        qT = [qT_ref[:, rows[r]] for r in subs]
        do_ = [do_ref[rows[r], :] for r in subs]
        doT_ = [doT_ref[:, rows[r]] for r in subs]
        D = [jnp.sum(o_ref[rows[r], :] * do_[r].astype(F32), axis=1, keepdims=True) for r in subs]

        def prep(G, diag, r):
            start = pl.multiple_of(G * SB_TK, SB_TK)
            kg = k_ref[pl.ds(start, SB_TK), :]
            lb, lk = _sb_scores(qr[r], kg)
            seen = _sb_seen(SB_ROWS, i * SB_TQ + r * SB_ROWS, start) if diag else None
            if diag:
                lk = jnp.where(seen, lk, 0.0)
            between, rs = _sb_suffix(lk, aft_ref[...])
            dw = _dot(do_[r], v_ref[pl.ds(start, SB_TK), :], dims=(((1,), (1,)), ((), ())))
            return start, kg, lb + between, rs, dw, jnp.exp2(lb), seen

        def groups(Gs_, carry, diag):
            carry = list(carry)
            pre = [[prep(G, diag, r) for G in Gs_] for r in subs]
            for r in subs:
                R, Gs, dq = carry[r]
                for start, kg, s, rs, dw, sig, seen in pre[r]:
                    w = jnp.exp2(s + R)
                    if diag:
                        w = jnp.where(seen, w, 0.0)
                    w16 = w.astype(BF16)
                    g = w16.astype(F32) * dw
                    dvT_ref[:, pl.ds(start, SB_TK)] += _dot(doT_[r], w16)
                    g_right, gsum = _sb_suffix(g, aft_ref[...])
                    dz = g - sig * (D[r] - Gs - g_right)
                    if diag:
                        dz = jnp.where(seen, dz, 0.0)
                    dz16 = dz.astype(BF16)
                    dq = dq + _dot(dz16, kg)
                    dkT_ref[:, pl.ds(start, SB_TK)] += _dot(qT[r], dz16)
                    R = R + rs
                    Gs = Gs + gsum
                carry[r] = (R, Gs, dq)
            return tuple(carry)

        zero1 = jnp.zeros((SB_ROWS, 1), F32)
        out = _sb_walk(gd, groups, tuple((zero1, zero1, jnp.zeros((SB_ROWS, d), F32)) for _ in subs))
        for r in subs:
            dq_ref[rows[r], :] = out[r][2]

    blk = pl.BlockSpec((None, SB_TQ, d), lambda h, i: (h, i, 0))
    blkT = pl.BlockSpec((None, d, SB_TQ), lambda h, i: (h, 0, i))
    full = pl.BlockSpec((None, L, d), lambda h, i: (h, 0, 0))
    fullT = pl.BlockSpec((None, d, L), lambda h, i: (h, 0, 0))
    return pl.pallas_call(
        body,
        name=name,
        grid=(H, nq),
        in_specs=[blk, blkT, full, full, blk, blk, blkT, pl.BlockSpec((2 * SB_TC, SB_TC), lambda h, i: (0, 0))],
        out_specs=[blk, fullT, fullT],
        out_shape=[jax.ShapeDtypeStruct((H, L, d), F32), jax.ShapeDtypeStruct((H, d, L), F32),
                   jax.ShapeDtypeStruct((H, d, L), F32)],
        compiler_params=_cparams(("parallel", "arbitrary")),
    )(qs, qsT, kn, v, o, do16, doT, after)


def _fold_heads(rows, name):
    def body(x_ref, o_ref):
        c = lax.broadcasted_iota(jnp.int32, (D_INNER, DT_PAD), 0)
        h = lax.broadcasted_iota(jnp.int32, (D_INNER, DT_PAD), 1)
        o_ref[...] = _dot(x_ref[...], (c // HEAD_P == h).astype(F32), precision=HI)

    return pl.pallas_call(
        body, name=name, out_shape=jax.ShapeDtypeStruct((SUBLANE, DT_PAD), F32),
        compiler_params=_cparams(),
    )(rows)


def _ffn_fwd(x, r, norm_w, w_in_t, w_out, tag):
    x1, h = _norm_fwd(x, r, norm_w, f"norm_ffn_fwd{tag}")
    gu = _mm(h, w_in_t, "nt", F32, f"ffn_in_fwd{tag}")
    act = _swiglu_fwd(gu, f"swiglu_fwd{tag}")
    f = _mm(act, w_out, "nn", F32, f"ffn_out_fwd{tag}")
    return x1, h, gu, act, f


def _ffn_bwd(x1, h, gu, act, norm_w, w_in_t, w_out, dres, tag):
    d16 = dres.astype(BF16)
    dact = _mm(d16, w_out, "nt", BF16, f"ffn_out_dx{tag}")
    g_w_out = _mm(act, d16, "tn", F32, f"ffn_out_dw{tag}")
    dgu = _swiglu_bwd(gu, dact, f"swiglu_bwd{tag}")
    dh = _mm(dgu, w_in_t, "nn", BF16, f"ffn_in_dx{tag}")
    g_w_in_t = _mm(dgu, h, "tn", F32, f"ffn_in_dw{tag}")
    dres, g_norm = _norm_bwd(x1, norm_w, dh, dres, f"norm_ffn_bwd{tag}")
    return dres, g_norm, g_w_in_t, g_w_out


def _local_step(x, target, W):
    L = x.shape[0]
    H = SB_HEADS
    nm0, nm1 = W["norm_mix"][0:1], W["norm_mix"][1:2]
    nf0, nf1 = W["norm_ffn"][0:1], W["norm_ffn"][1:2]

    _, h0 = _norm_fwd(x, None, nm0, "norm_mix_fwd0")
    zx = _mm(h0, W["ssm_w_in_t"], "nt", F32, "ssm_in_fwd")
    xbc = _conv_fwd(zx, W["conv_w8"], W["conv_b"], "conv_fwd")
    y, states = _ssd_fwd(xbc, zx, W["dt_bias"], W["alog_e"], W["d_e"], "ssd_fwd")
    yn = _gated_norm_fwd(y, zx, W["ssm_norm_w"], "gated_norm_fwd")
    mix0 = _mm(yn, W["ssm_w_out"], "nn", F32, "ssm_out_fwd")
    x1, h1, gu0, act0, f0 = _ffn_fwd(x, mix0, nf0, W["ffn_w_in_t"][0], W["ffn_w_out"][0], "0")

    x2, h2 = _norm_fwd(x1, f0, nm1, "norm_mix_fwd1")
    qkv = _mm(h2, W["sb_w_qkv_t"], "nt", F32, "sb_qkv_fwd")
    qkv_t = qkv.reshape(L, 3, H, SB_D).transpose(1, 2, 0, 3)
    q_r = qkv_t[0].reshape(H * L, SB_D)
    k_r = qkv_t[1].reshape(H * L, SB_D)
    qs = _qk_norm_fwd(q_r, W["sb_q_gain"], SB_Q_SCALE, "q_norm_fwd").reshape(H, L, SB_D)
    kn = _qk_norm_fwd(k_r, W["sb_k_gain"], 1.0, "k_norm_fwd").reshape(H, L, SB_D)
    vb = qkv_t[2].astype(BF16)
    o = _sb_fwd(qs, kn, vb, "sb_fwd")
    o_flat = o.transpose(1, 0, 2).reshape(L, D_MODEL).astype(BF16)
    mix1 = _mm(o_flat, W["sb_w_o"], "nn", F32, "sb_o_fwd")
    x3, h3, gu1, act1, f1 = _ffn_fwd(x2, mix1, nf1, W["ffn_w_in_t"][1], W["ffn_w_out"][1], "1")

    dres, sq = _loss_bwd(x3, f1, target, "loss")

    dres, g_nf1, g_fin1, g_fout1 = _ffn_bwd(x3, h3, gu1, act1, nf1, W["ffn_w_in_t"][1], W["ffn_w_out"][1], dres, "1")
    d16 = dres.astype(BF16)
    do_flat = _mm(d16, W["sb_w_o"], "nt", F32, "sb_o_dx")
    g_w_o = _mm(o_flat, d16, "tn", F32, "sb_o_dw")
    do = do_flat.reshape(L, H, SB_D).transpose(1, 0, 2)
    dqs, dknT, dvT = _sb_bwd(qs, kn, vb, o, do, "sb_bwd")
    dkn = dknT.transpose(0, 2, 1)
    dq, g_qg = _qk_norm_bwd(q_r, W["sb_q_gain"], dqs.reshape(H * L, SB_D), SB_Q_SCALE, LN2, "q_norm_bwd")
    dk, g_kg = _qk_norm_bwd(k_r, W["sb_k_gain"], dkn.reshape(H * L, SB_D), 1.0, LN2, "k_norm_bwd")
    dqkv = jnp.stack([dq.reshape(H, L, SB_D), dk.reshape(H, L, SB_D), dvT.transpose(0, 2, 1)])
    dqkv = dqkv.transpose(2, 0, 1, 3).reshape(L, 3 * D_MODEL).astype(BF16)
    dh2 = _mm(dqkv, W["sb_w_qkv_t"], "nn", BF16, "sb_qkv_dx")
    g_w_qkv_t = _mm(dqkv, h2, "tn", F32, "sb_qkv_dw")
    dres, g_nm1 = _norm_bwd(x2, nm1, dh2, dres, "norm_mix_bwd1")

    dres, g_nf0, g_fin0, g_fout0 = _ffn_bwd(x1, h1, gu0, act0, nf0, W["ffn_w_in_t"][0], W["ffn_w_out"][0], dres, "0")
    d16 = dres.astype(BF16)
    dyn = _mm(d16, W["ssm_w_out"], "nt", BF16, "ssm_out_dx")
    g_w_out = _mm(yn, d16, "tn", F32, "ssm_out_dw")
    dy, dz, g_snw = _gated_norm_bwd(y, zx, W["ssm_norm_w"], dyn, "gated_norm_bwd")
    dxs, dB, dC, ddt, g_dtb, g_alog_e, g_d_e = _ssd_bwd(
        xbc, zx, W["dt_bias"], W["alog_e"], W["d_e"], states, dy, "ssd_bwd")
    dact = jnp.concatenate([dxs, dB, dC], axis=1)
    dxbc, g_cw8, g_cb = _conv_bwd(zx, W["conv_w8"], W["conv_b"], dact, "conv_bwd")
    dzx = jnp.concatenate([dz, dxbc, ddt.astype(BF16)], axis=1)
    dh0 = _mm(dzx, W["ssm_w_in_t"], "nn", BF16, "ssm_in_dx")
    g_w_in_t = _mm(dzx, h0, "tn", F32, "ssm_in_dw")
    grad_x, g_nm0 = _norm_bwd(x, nm0, dh0, dres, "norm_mix_bwd0")

    per_head = jnp.concatenate(
        [g_alog_e.reshape(1, D_INNER), g_d_e.reshape(1, D_INNER), jnp.zeros((SUBLANE - 2, D_INNER), F32)], axis=0)
    folded = _fold_heads(per_head, "fold_heads")
    grads = {
        "norm_mix": jnp.concatenate([g_nm0, g_nm1], axis=0),
        "norm_ffn": jnp.concatenate([g_nf0, g_nf1], axis=0),
        "ssm_w_in": [g_w_in_t],
        "ssm_conv_w": [g_cw8],
        "ssm_conv_b": g_cb,
        "ssm_dt_bias": g_dtb[:, :SSM_HEADS],
        "ssm_a_log": folded[0:1, :SSM_HEADS],
        "ssm_d": folded[1:2, :SSM_HEADS],
        "ssm_norm_w": g_snw,
        "ssm_w_out": [g_w_out],
        "sb_w_qkv": [g_w_qkv_t],
        "sb_q_gain": g_qg,
        "sb_k_gain": g_kg,
        "sb_w_o": [g_w_o],
        "ffn_w_in": [g_fin0, g_fin1],
        "ffn_w_out": [g_fout0, g_fout1],
    }
    return sq, grad_x, grads


def _prep_weights(full):
    W = _prep_vectors(full)
    w_in_t = full["ssm_w_in"].T
    W["ssm_w_in_t"] = jnp.pad(w_in_t, ((0, D_IN_PAD - D_IN_PROJ), (0, 0))).astype(BF16)
    W["conv_w8"] = jnp.pad(full["ssm_conv_w"], ((0, SUBLANE - CONV_W), (0, 0)))
    W["ssm_w_out"] = full["ssm_w_out"].astype(BF16)
    W["sb_w_qkv_t"] = full["sb_w_qkv"].T.astype(BF16)
    W["sb_w_o"] = full["sb_w_o"].astype(BF16)
    W["ffn_w_in_t"] = jnp.swapaxes(full["ffn_w_in"], 1, 2).astype(BF16)
    W["ffn_w_out"] = full["ffn_w_out"].astype(BF16)
    return W


def _prep_vectors(full):
    W = {}
    W["norm_mix"] = full["norm_mix"]
    W["norm_ffn"] = full["norm_ffn"]
    W["conv_b"] = full["ssm_conv_b"]
    W["dt_bias"] = jnp.pad(full["ssm_dt_bias"], ((0, 0), (0, DT_PAD - SSM_HEADS)))
    W["alog_e"] = jnp.repeat(full["ssm_a_log"], HEAD_P, axis=1)
    W["d_e"] = jnp.repeat(full["ssm_d"], HEAD_P, axis=1)
    W["ssm_norm_w"] = full["ssm_norm_w"]
    W["sb_q_gain"] = full["sb_q_gain"]
    W["sb_k_gain"] = full["sb_k_gain"]
    return W


N_CHIPS = 4
N_DEV = 8
FLAT_W = D_MODEL
SHARDED = (
    ("ssm_w_in", (1, D_MODEL, D_IN_PROJ // N_CHIPS), True),
    ("ssm_w_out", (1, D_INNER // N_CHIPS, D_MODEL), False),
    ("sb_w_qkv", (1, D_MODEL, 3 * D_MODEL // N_CHIPS), True),
    ("sb_w_o", (1, D_MODEL // N_CHIPS, D_MODEL), False),
    ("ffn_w_in", (2, D_MODEL, 2 * D_FF // N_CHIPS), True),
    ("ffn_w_out", (2, D_FF // N_CHIPS, D_MODEL), False),
    ("ssm_conv_w", (1, CONV_W, CONV_DIM // N_CHIPS), False),
)
REPLICATED = (
    ("norm_mix", (2, D_MODEL)), ("norm_ffn", (2, D_MODEL)), ("ssm_conv_b", (1, CONV_DIM)),
    ("ssm_norm_w", (1, D_INNER)), ("ssm_dt_bias", (1, SSM_HEADS)), ("ssm_a_log", (1, SSM_HEADS)),
    ("ssm_d", (1, SSM_HEADS)), ("sb_q_gain", (1, SB_D)), ("sb_k_gain", (1, SB_D)),
)
ADAM_TILE = 256
_USED_ROWS = sum(math.prod(s) for _, s, _ in SHARDED) // FLAT_W
FLAT_ROWS = -(-_USED_ROWS // ADAM_TILE) * ADAM_TILE
_USED_SMALL = sum(math.prod(s) for _, s in REPLICATED)
SMALL_ROWS = -(-_USED_SMALL // (SUBLANE * FLAT_W)) * SUBLANE


def _flat_rows(shape):
    return math.prod(shape) // FLAT_W


def _pack_shard(d, dtype):
    parts = []
    for n, _, transposed in SHARDED:
        a = jnp.swapaxes(d[n], 1, 2) if transposed else d[n]
        parts.append(a.reshape(-1, FLAT_W).astype(dtype))
    parts.append(jnp.zeros((FLAT_ROWS - _USED_ROWS, FLAT_W), dtype))
    return jnp.concatenate(parts, axis=0)


def _unpack_shard(flat):
    out, r = {}, 0
    for n, s, transposed in SHARDED:
        k = _flat_rows(s)
        a = flat[r:r + k]
        out[n] = jnp.swapaxes(a.reshape(s[0], s[2], s[1]), 1, 2) if transposed else a.reshape(s)
        r += k
    return out


def _shard_of_whole(whole, chip):
    parts = []
    for n, s, _ in SHARDED:
        k = _flat_rows(s) // s[0]
        if n == "ssm_conv_w":
            parts.append(whole[n][0][:CONV_W, chip * FLAT_W:(chip + 1) * FLAT_W])
        else:
            parts.extend(a[chip * k:(chip + 1) * k] for a in whole[n])
    parts.append(jnp.zeros((FLAT_ROWS - _USED_ROWS, FLAT_W), parts[0].dtype))
    return jnp.concatenate(parts, axis=0)


def _whole_of_shards(gb):
    out, r = {}, 0
    for n, s, _ in SHARDED:
        k = _flat_rows(s) // s[0]
        out[n] = [jnp.concatenate([gb[p, r + l * k:r + (l + 1) * k] for p in range(N_CHIPS)], axis=0)
                  for l in range(s[0])]
        r += k * s[0]
    return out


def _pack_small(d):
    parts = [d[n].reshape(-1) for n, _ in REPLICATED]
    parts.append(jnp.zeros((SMALL_ROWS * FLAT_W - _USED_SMALL,), parts[0].dtype))
    return jnp.concatenate(parts).reshape(SMALL_ROWS, FLAT_W)


def _unpack_small(flat):
    flat = flat.reshape(-1)
    out, r = {}, 0
    for n, s in REPLICATED:
        k = math.prod(s)
        out[n] = flat[r:r + k].reshape(s)
        r += k
    return out


ANY = pl.BlockSpec(memory_space=pl.ANY)


def _other_chips():
    x, y, c = lax.axis_index("x"), lax.axis_index("y"), lax.axis_index("c")
    return x, y, c, [(1 - x, y), (x, 1 - y), (1 - x, 1 - y)]


def _remote(src, dst, send_sem, recv_sem, dev):
    return pltpu.make_async_remote_copy(src_ref=src, dst_ref=dst, send_sem=send_sem, recv_sem=recv_sem,
                                        device_id=dev, device_id_type=MESH_T)


def _gather_weights(wb, ws):
    def body(wb_ref, ws_ref, gb_ref, gs_ref, send_sems, recv_sems, loc_sems):
        x, y, c, others = _other_chips()
        me = 2 * x + y
        pairs = ((wb_ref, gb_ref), (ws_ref, gs_ref))
        local = [pltpu.make_async_copy(src, dst.at[me], loc_sems.at[t]) for t, (src, dst) in enumerate(pairs)]
        for cp in local:
            cp.start()
        sends = []
        for k, (px, py) in enumerate(others):
            for t, (src, dst) in enumerate(pairs):
                cp = _remote(src, dst.at[me], send_sems.at[2 * k + t], recv_sems.at[2 * k + t], (px, py, c))
                cp.start()
                sends.append(cp)
        for k, (px, py) in enumerate(others):
            for t, (src, dst) in enumerate(pairs):
                _remote(src, dst.at[2 * px + py], send_sems.at[2 * k + t], recv_sems.at[2 * k + t],
                        (px, py, c)).wait_recv()
        for cp in sends:
            cp.wait_send()
        for cp in local:
            cp.wait()

    return pl.pallas_call(
        body, name="gather_weights",
        in_specs=[ANY, ANY], out_specs=[ANY, ANY],
        out_shape=[jax.ShapeDtypeStruct((N_CHIPS,) + wb.shape, wb.dtype),
                   jax.ShapeDtypeStruct((N_CHIPS,) + ws.shape, ws.dtype)],
        scratch_shapes=[pltpu.SemaphoreType.DMA((6,)), pltpu.SemaphoreType.DMA((6,)), pltpu.SemaphoreType.DMA((2,))],
    )(wb, ws)


def _scatter_grads(gpack):
    def body(g_ref, r_ref, send_sems, recv_sems):
        x, y, c, others = _other_chips()
        sends = []
        for k, (px, py) in enumerate(others):
            cp = _remote(g_ref.at[2 * px + py], r_ref.at[k], send_sems.at[k], recv_sems.at[k], (px, py, c))
            cp.start()
            sends.append(cp)
        for cp in sends:
            cp.wait_recv()
        for cp in sends:
            cp.wait_send()

    return pl.pallas_call(
        body, name="scatter_grads",
        in_specs=[ANY], out_specs=ANY,
        out_shape=jax.ShapeDtypeStruct((N_CHIPS - 1,) + gpack.shape[1:], gpack.dtype),
        scratch_shapes=[pltpu.SemaphoreType.DMA((3,)), pltpu.SemaphoreType.DMA((3,))],
    )(gpack)


def _exchange_partials(part, small):
    def body(p_ref, s_ref, ps_ref, sa_ref, send_sems, recv_sems, loc_sem):
        x, y, c = lax.axis_index("x"), lax.axis_index("y"), lax.axis_index("c")
        me = 4 * x + 2 * y + c
        local = pltpu.make_async_copy(s_ref, sa_ref.at[me], loc_sem)
        local.start()
        sib = _remote(p_ref, ps_ref, send_sems.at[0], recv_sems.at[0], (x, y, 1 - c))
        sib.start()
        sends = [sib]
        for k in range(1, N_DEV):
            fx, fy, fc = (k >> 2) & 1, (k >> 1) & 1, k & 1
            px, py, pc = x ^ fx, y ^ fy, c ^ fc
            cp = _remote(s_ref, sa_ref.at[me], send_sems.at[k], recv_sems.at[k], (px, py, pc))
            cp.start()
            sends.append(cp)
        sib.wait_recv()
        for k in range(1, N_DEV):
            fx, fy, fc = (k >> 2) & 1, (k >> 1) & 1, k & 1
            px, py, pc = x ^ fx, y ^ fy, c ^ fc
            _remote(s_ref, sa_ref.at[4 * px + 2 * py + pc], send_sems.at[k], recv_sems.at[k], (px, py, pc)).wait_recv()
        for cp in sends:
            cp.wait_send()
        local.wait()

    return pl.pallas_call(
        body, name="exchange_partials",
        in_specs=[ANY, ANY], out_specs=[ANY, ANY],
        out_shape=[jax.ShapeDtypeStruct(part.shape, part.dtype),
                   jax.ShapeDtypeStruct((N_DEV,) + small.shape, small.dtype)],
        scratch_shapes=[pltpu.SemaphoreType.DMA((N_DEV,)), pltpu.SemaphoreType.DMA((N_DEV,)), pltpu.SemaphoreType.DMA],
    )(part, small)


def _partial_sum(own, recv):
    R = own.shape[0]

    def body(o_ref, r_ref, p_ref):
        acc = o_ref[...]
        for k in range(N_CHIPS - 1):
            acc = acc + r_ref[k].astype(F32)
        p_ref[...] = acc

    return pl.pallas_call(
        body, name="grad_partial_sum", grid=(R // ADAM_TILE,),
        in_specs=[pl.BlockSpec((ADAM_TILE, FLAT_W), lambda i: (i, 0)),
                  pl.BlockSpec((N_CHIPS - 1, ADAM_TILE, FLAT_W), lambda i: (0, i, 0))],
        out_specs=pl.BlockSpec((ADAM_TILE, FLAT_W), lambda i: (i, 0)),
        out_shape=jax.ShapeDtypeStruct(own.shape, F32),
        compiler_params=_cparams(("parallel",)),
    )(own, recv)


def _adamw_math(w, g, m, v):
    m = ADAM_B1 * m + (1.0 - ADAM_B1) * g
    v = ADAM_B2 * v + (1.0 - ADAM_B2) * jnp.square(g)
    m_hat = m / (1.0 - ADAM_B1 ** ADAM_STEP)
    v_hat = v / (1.0 - ADAM_B2 ** ADAM_STEP)
    delta = -ADAM_LR * (m_hat / (jnp.sqrt(v_hat) + ADAM_EPS) + ADAM_WD * w)
    return delta, m, v


def _adamw(w, m, v, parts, tile, name):
    n, R, _ = parts.shape

    def body(w_ref, m_ref, v_ref, p_ref, g_ref, d_ref, nm_ref, nv_ref):
        g = p_ref[0]
        for k in range(1, n):
            g = g + p_ref[k]
        delta, nm, nv = _adamw_math(w_ref[...], g, m_ref[...], v_ref[...])
        g_ref[...] = g
        d_ref[...] = delta
        nm_ref[...] = nm
        nv_ref[...] = nv

    blk = pl.BlockSpec((tile, FLAT_W), lambda i: (i, 0))
    return pl.pallas_call(
        body, name=name, grid=(R // tile,),
        in_specs=[blk, blk, blk, pl.BlockSpec((n, tile, FLAT_W), lambda i: (0, i, 0))],
        out_specs=[blk] * 4,
        out_shape=[jax.ShapeDtypeStruct((R, FLAT_W), F32)] * 4,
        compiler_params=_cparams(("parallel",)),
    )(w, m, v, parts)


def _add2(a, b):
    def body(a_ref, b_ref, o_ref):
        o_ref[...] = a_ref[...] + b_ref[...]

    blk = pl.BlockSpec((ADAM_TILE, FLAT_W), lambda i: (i, 0))
    return pl.pallas_call(
        body, name="grad_final_sum", grid=(a.shape[0] // ADAM_TILE,), in_specs=[blk, blk], out_specs=blk,
        out_shape=jax.ShapeDtypeStruct(a.shape, F32), compiler_params=_cparams(("parallel",)),
    )(a, b)


def _adamw_block(w, m, v, g, name):
    shape = w.shape
    R, C = math.prod(shape[:-1]), shape[-1]
    tile = _tile(R, ADAM_TILE, SUBLANE)

    def body(w_ref, m_ref, v_ref, g_ref, d_ref, nm_ref, nv_ref):
        delta, nm, nv = _adamw_math(w_ref[...], g_ref[...], m_ref[...], v_ref[...])
        d_ref[...] = delta
        nm_ref[...] = nm
        nv_ref[...] = nv

    blk = pl.BlockSpec((tile, C), lambda i: (i, 0))
    outs = pl.pallas_call(
        body, name=name, grid=(R // tile,), in_specs=[blk] * 4, out_specs=[blk] * 3,
        out_shape=[jax.ShapeDtypeStruct((R, C), F32)] * 3, compiler_params=_cparams(("parallel",)),
    )(*(a.reshape(R, C) for a in (w, m, v, g)))
    return tuple(a.reshape(shape) for a in outs)


_NAMES = ("norm_mix", "norm_ffn", "ssm_w_in", "ssm_conv_w", "ssm_conv_b", "ssm_dt_bias", "ssm_a_log", "ssm_d",
          "ssm_norm_w", "ssm_w_out", "sb_w_qkv", "sb_q_gain", "sb_k_gain", "sb_w_o", "ffn_w_in", "ffn_w_out")


def _step(x, loss_target, w, m, v):
    cx, cy, cc = lax.axis_index("x"), lax.axis_index("y"), lax.axis_index("c")
    chip = 2 * cx + cy

    conv8 = jnp.pad(w["ssm_conv_w"][0], ((0, SUBLANE - CONV_W), (0, 0)))
    gb, gs = _gather_weights(_pack_shard(w, BF16), conv8)
    whole = _whole_of_shards(gb)
    W = _prep_vectors(w)
    W["ssm_w_in_t"] = jnp.pad(whole["ssm_w_in"][0], ((0, D_IN_PAD - D_IN_PROJ), (0, 0)))
    W["conv_w8"] = jnp.concatenate([gs[p] for p in range(N_CHIPS)], axis=1)
    W["ssm_w_out"] = whole["ssm_w_out"][0]
    W["sb_w_qkv_t"] = whole["sb_w_qkv"][0]
    W["sb_w_o"] = whole["sb_w_o"][0]
    W["ffn_w_in_t"] = whole["ffn_w_in"]
    W["ffn_w_out"] = whole["ffn_w_out"]

    sq, grad_x, grads = _local_step(x[0], loss_target[0], W)
    loss = lax.psum(0.5 * jnp.sum(sq) / D_MODEL, ("x", "y", "c"))

    gpack = jnp.stack([_shard_of_whole(grads, s) for s in range(N_CHIPS)])
    own = lax.dynamic_index_in_dim(gpack, chip, axis=0, keepdims=False)
    recv = _scatter_grads(gpack.astype(BF16))
    part = _partial_sum(own, recv)
    part_sib, small_all = _exchange_partials(part, _pack_small(grads))
    g_blocks = _unpack_shard(_add2(part, part_sib))
    g_s, d_s, m_s, v_s = _adamw(_pack_small(w), _pack_small(m), _pack_small(v), small_all, SUBLANE, "adamw_replicated")

    g, d, nm, nv = (_unpack_small(a) for a in (g_s, d_s, m_s, v_s))
    for n, _, _ in SHARDED:
        g[n] = g_blocks[n]
        d[n], nm[n], nv[n] = _adamw_block(w[n], m[n], v[n], g[n], "adamw_" + n)
    return loss, grad_x[None], [[t[n] for n in _NAMES] for t in (g, d, nm, nv)]


def kernel(x, norm_mix, norm_ffn, ssm_w_in, ssm_conv_w, ssm_conv_b, ssm_dt_bias, ssm_a_log, ssm_d, ssm_norm_w, ssm_w_out, sb_w_qkv, sb_q_gain, sb_k_gain, sb_w_o, ffn_w_in, ffn_w_out, loss_target, m_norm_mix, m_norm_ffn, m_ssm_w_in, m_ssm_conv_w, m_ssm_conv_b, m_ssm_dt_bias, m_ssm_a_log, m_ssm_d, m_ssm_norm_w, m_ssm_w_out, m_sb_w_qkv, m_sb_q_gain, m_sb_k_gain, m_sb_w_o, m_ffn_w_in, m_ffn_w_out, v_norm_mix, v_norm_ffn, v_ssm_w_in, v_ssm_conv_w, v_ssm_conv_b, v_ssm_dt_bias, v_ssm_a_log, v_ssm_d, v_ssm_norm_w, v_ssm_w_out, v_sb_w_qkv, v_sb_q_gain, v_sb_k_gain, v_sb_w_o, v_ffn_w_in, v_ffn_w_out):
    w = dict(zip(_NAMES, (norm_mix, norm_ffn, ssm_w_in, ssm_conv_w, ssm_conv_b, ssm_dt_bias, ssm_a_log, ssm_d,
                          ssm_norm_w, ssm_w_out, sb_w_qkv, sb_q_gain, sb_k_gain, sb_w_o, ffn_w_in, ffn_w_out)))
    m = dict(zip(_NAMES, (m_norm_mix, m_norm_ffn, m_ssm_w_in, m_ssm_conv_w, m_ssm_conv_b, m_ssm_dt_bias, m_ssm_a_log,
                          m_ssm_d, m_ssm_norm_w, m_ssm_w_out, m_sb_w_qkv, m_sb_q_gain, m_sb_k_gain, m_sb_w_o,
                          m_ffn_w_in, m_ffn_w_out)))
    v = dict(zip(_NAMES, (v_norm_mix, v_norm_ffn, v_ssm_w_in, v_ssm_conv_w, v_ssm_conv_b, v_ssm_dt_bias, v_ssm_a_log,
                          v_ssm_d, v_ssm_norm_w, v_ssm_w_out, v_sb_w_qkv, v_sb_q_gain, v_sb_k_gain, v_sb_w_o,
                          v_ffn_w_in, v_ffn_w_out)))
    loss, grad_x, (g, d, nm, nv) = _step(x, loss_target, w, m, v)
    return (loss, grad_x, *g, *d, *nm, *nv)
```

```python
import functools
import math

import jax
import jax.numpy as jnp
from jax import lax
from jax.experimental import pallas as pl
from jax.experimental.pallas import tpu as pltpu

F32 = jnp.float32
BF16 = jnp.bfloat16

VMEM_LIMIT_BYTES = 48 * 1024 * 1024
LANE = 128
SUBLANE = 8

D_MODEL = 1024
CHUNK = 64
D_INNER = 2048
SSM_HEADS = 32
SSM_GROUPS = 8
GROUP_W = D_INNER // SSM_GROUPS
HEAD_P = 64
D_STATE = 128
CONV_W = 4
CONV_DIM = D_INNER + 2 * SSM_GROUPS * D_STATE
D_IN_PROJ = D_INNER + CONV_DIM + SSM_HEADS
DT_PAD = LANE
D_IN_PAD = D_INNER + CONV_DIM + DT_PAD
SB_HEADS = 16
SB_D = 64
D_FF = 2816
NORM_EPS = 1e-6
GATED_NORM_EPS = 1e-5

ADAM_LR = 0.001
ADAM_B1 = 0.9
ADAM_B2 = 0.999
ADAM_EPS = 1e-08
ADAM_WD = 0.01
ADAM_STEP = 10

MESH_T = pl.DeviceIdType.MESH


def _cparams(sem=None):
    return pltpu.CompilerParams(dimension_semantics=sem, vmem_limit_bytes=VMEM_LIMIT_BYTES)


def _tile(n, target, align):
    best = None
    for t in range(align, min(n, target) + 1, align):
        if n % t == 0:
            best = t
    return best or n


def _mm(a, b, mode, out_dtype, name):
    if mode == "nn":
        (M, K), N = a.shape, b.shape[1]
    elif mode == "nt":
        (M, K), N = a.shape, b.shape[0]
    else:
        (K, M), N = a.shape, b.shape[1]
    tm = _tile(M, 1408, LANE)
    tn = _tile(N, 1536, LANE)
    tk = _tile(K, 1536, LANE)
    nk = K // tk
    if mode == "nn":
        a_spec = pl.BlockSpec((tm, tk), lambda i, j, k: (i, k))
        b_spec = pl.BlockSpec((tk, tn), lambda i, j, k: (k, j))
        dims = (((1,), (0,)), ((), ()))
    elif mode == "nt":
        a_spec = pl.BlockSpec((tm, tk), lambda i, j, k: (i, k))
        b_spec = pl.BlockSpec((tn, tk), lambda i, j, k: (j, k))
        dims = (((1,), (1,)), ((), ()))
    else:
        a_spec = pl.BlockSpec((tk, tm), lambda i, j, k: (k, i))
        b_spec = pl.BlockSpec((tk, tn), lambda i, j, k: (k, j))
        dims = (((0,), (0,)), ((), ()))

    def body(a_ref, b_ref, o_ref, acc_ref):
        k = pl.program_id(2)

        @pl.when(k == 0)
        def _():
            acc_ref[...] = jnp.zeros_like(acc_ref)

        acc_ref[...] += lax.dot_general(a_ref[...], b_ref[...], dims, preferred_element_type=F32)

        @pl.when(k == nk - 1)
        def _():
            o_ref[...] = acc_ref[...].astype(o_ref.dtype)

    return pl.pallas_call(
        body,
        name=name,
        grid=(M // tm, N // tn, nk),
        in_specs=[a_spec, b_spec],
        out_specs=pl.BlockSpec((tm, tn), lambda i, j, k: (i, j)),
        out_shape=jax.ShapeDtypeStruct((M, N), out_dtype),
        scratch_shapes=[pltpu.VMEM((tm, tn), F32)],
        compiler_params=_cparams(("parallel", "parallel", "arbitrary")),
    )(a, b)


def _rows(fn, name, tile, row_ins, const_ins, row_outs, acc_outs=(), ncol=1):
    L = row_ins[0][0].shape[0]
    tile = min(tile, L)
    nrow = L // tile
    n_ri, n_ci, n_ro, n_ao = len(row_ins), len(const_ins), len(row_outs), len(acc_outs)

    def body(*refs):
        i = pl.program_id(1)
        j = pl.program_id(0)
        ins = [r[...] for r in refs[: n_ri + n_ci]]
        outs = fn(*ins, j=j)
        o_refs = refs[n_ri + n_ci:]
        for r, v in zip(o_refs[:n_ro], outs[:n_ro]):
            r[...] = v.astype(r.dtype)

        @pl.when(i == 0)
        def _():
            for r in o_refs[n_ro:]:
                r[...] = jnp.zeros_like(r)

        for r, v in zip(o_refs[n_ro:], outs[n_ro:]):
            r[...] += v

    def rspec(bc, cf):
        return pl.BlockSpec((tile, bc), lambda j, i: (i, cf(j)))

    def cspec(r, bc, cf):
        return pl.BlockSpec((r, bc), lambda j, i: (0, cf(j)))

    in_specs = [rspec(bc, cf) for (_, bc, cf) in row_ins]
    in_specs += [cspec(a.shape[0], bc, cf) for (a, bc, cf) in const_ins]
    out_specs = [rspec(bc, cf) for (_, _, bc, cf) in row_outs]
    out_specs += [cspec(1, bc, cf) for (_, bc, cf) in acc_outs]
    out_shape = [jax.ShapeDtypeStruct((L, c), dt) for (c, dt, _, _) in row_outs]
    out_shape += [jax.ShapeDtypeStruct((1, c), F32) for (c, _, _) in acc_outs]
    res = pl.pallas_call(
        body,
        name=name,
        grid=(ncol, nrow),
        in_specs=in_specs,
        out_specs=out_specs,
        out_shape=out_shape,
        compiler_params=_cparams(("arbitrary", "arbitrary")),
    )(*[a for (a, _, _) in row_ins], *[a for (a, _, _) in const_ins])
    return res


def _zero(j):
    return 0


def _whole(a):
    return (a, a.shape[1], _zero)


def _rms(x, w, eps):
    return x * lax.rsqrt(jnp.mean(x * x, axis=-1, keepdims=True) + eps) * w


ROW_TILE = 256
COL_ROW_TILE = 1024


def _norm_fwd(x, r, w, name):
    C = x.shape[1]
    if r is None:
        def fn(x_, w_, j):
            return (_rms(x_, w_, NORM_EPS),)
        (h,) = _rows(fn, name, ROW_TILE, [_whole(x)], [_whole(w)], [(C, BF16, C, _zero)])
        return x, h

    def fn(x_, r_, w_, j):
        x1 = x_ + r_
        return x1, _rms(x1, w_, NORM_EPS)

    x1, h = _rows(fn, name, ROW_TILE, [_whole(x), _whole(r)], [_whole(w)],
                  [(C, F32, C, _zero), (C, BF16, C, _zero)])
    return x1, h


def _norm_bwd(x, w, dh, dres, name):
    C = x.shape[1]

    def fn(x_, dh_, dres_, w_, j):
        _, vjp = jax.vjp(lambda a, b: _rms(a, b, NORM_EPS), x_, w_)
        dx, dw = vjp(dh_.astype(F32))
        return dres_ + dx, dw

    return _rows(fn, name, ROW_TILE, [_whole(x), _whole(dh), _whole(dres)], [_whole(w)],
                 [(C, F32, C, _zero)], [(C, C, _zero)])


def _silu(x):
    return x * jax.nn.sigmoid(x)


FF_BLK = 256
FF_NB = D_FF // FF_BLK


def _swiglu_fwd(gu, name):
    def fn(g, u, j):
        return (_silu(g) * u,)
    (act,) = _rows(fn, name, COL_ROW_TILE,
                   [(gu, FF_BLK, lambda j: j), (gu, FF_BLK, lambda j: j + FF_NB)], [],
                   [(D_FF, BF16, FF_BLK, lambda j: j)], ncol=FF_NB)
    return act


def _swiglu_bwd(gu, dact, name):
    def fn(g, u, da, j):
        _, vjp = jax.vjp(lambda a, b: _silu(a) * b, g, u)
        dg, du = vjp(da.astype(F32))
        return (jnp.where(j < FF_NB, dg, du),)
    (dgu,) = _rows(fn, name, COL_ROW_TILE,
                   [(gu, FF_BLK, lambda j: j % FF_NB), (gu, FF_BLK, lambda j: j % FF_NB + FF_NB),
                    (dact, FF_BLK, lambda j: j % FF_NB)], [],
                   [(2 * D_FF, BF16, FF_BLK, lambda j: j)], ncol=2 * FF_NB)
    return dgu


def _gated_norm_fwd(y, zx, w, name):
    def fn(y_, z_, w_, j):
        return (_rms(y_ * _silu(z_), w_, GATED_NORM_EPS),)
    (yn,) = _rows(fn, name, COL_ROW_TILE,
                  [(y, GROUP_W, lambda j: j), (zx, GROUP_W, lambda j: j)], [(w, GROUP_W, lambda j: j)],
                  [(D_INNER, BF16, GROUP_W, lambda j: j)], ncol=SSM_GROUPS)
    return yn


def _gated_norm_bwd(y, zx, w, dyn, name):
    def fn(y_, z_, dyn_, w_, j):
        _, vjp = jax.vjp(lambda a, b, c: _rms(a * _silu(b), c, GATED_NORM_EPS), y_, z_, w_)
        return vjp(dyn_.astype(F32))
    cj = lambda j: j
    return _rows(fn, name, COL_ROW_TILE,
                 [(y, GROUP_W, cj), (zx, GROUP_W, cj), (dyn, GROUP_W, cj)], [(w, GROUP_W, cj)],
                 [(D_INNER, F32, GROUP_W, cj), (D_INNER, BF16, GROUP_W, cj)], [(D_INNER, GROUP_W, cj)],
                 ncol=SSM_GROUPS)


def _loss_bwd(x3, f, target, name):
    C = x3.shape[1]

    def fn(x_, f_, t_, j):
        err = (x_ + f_) - t_
        return err * (1.0 / C), jnp.sum(err * err, axis=0, keepdims=True)

    return _rows(fn, name, ROW_TILE, [_whole(x3), _whole(f), _whole(target)], [],
                 [(C, F32, C, _zero)], [(C, C, _zero)])


def _qk_norm_fwd(q, gain, scale, name):
    def fn(q_, g_, j):
        return (_rms(q_, g_, NORM_EPS) * scale,)
    (qn,) = _rows(fn, name, 2048, [_whole(q)], [_whole(gain)], [(SB_D, BF16, SB_D, _zero)])
    return qn


def _qk_norm_bwd(q, gain, dqn, scale, cot_scale, name):
    def fn(q_, dqn_, g_, j):
        _, vjp = jax.vjp(lambda a, b: _rms(a, b, NORM_EPS) * scale, q_, g_)
        return vjp(dqn_ * cot_scale)
    return _rows(fn, name, 2048, [_whole(q), _whole(dqn)], [_whole(gain)],
                 [(SB_D, F32, SB_D, _zero)], [(SB_D, SB_D, _zero)])


CONV_TILE = 512
CONV_BLK = 512
XBC_COL0 = D_INNER // CONV_BLK


def _shift_down(cur, prev8, k):
    if k == 0:
        return cur
    rolled = pltpu.roll(cur, k, 0)
    head_prev = pltpu.roll(prev8, k, 0)
    rid = lax.broadcasted_iota(jnp.int32, (SUBLANE, cur.shape[1]), 0)
    head = jnp.where(rid < k, head_prev, rolled[:SUBLANE])
    if cur.shape[0] == SUBLANE:
        return head
    return jnp.concatenate([head, rolled[SUBLANE:]], axis=0)


def _shift_up(cur, next8, k):
    if k == 0:
        return cur
    T = cur.shape[0]
    rolled = pltpu.roll(cur, T - k, 0)
    tail_next = pltpu.roll(next8, SUBLANE - k, 0)
    rid = lax.broadcasted_iota(jnp.int32, (SUBLANE, cur.shape[1]), 0)
    tail = jnp.where(rid >= SUBLANE - k, tail_next, rolled[T - SUBLANE:])
    return jnp.concatenate([rolled[: T - SUBLANE], tail], axis=0)


def _conv_pre(cur, prev8, w, b):
    pre = b
    for i in range(CONV_W):
        pre = pre + w[i:i + 1, :] * _shift_down(cur, prev8, CONV_W - 1 - i)
    return pre


def _conv_fwd(zx, w8, b, name):
    L = zx.shape[0]
    nrow = L // CONV_TILE
    r8 = CONV_TILE // SUBLANE

    def body(cur_ref, prev_ref, w_ref, b_ref, o_ref):
        i = pl.program_id(1)
        prev8 = jnp.where(i > 0, prev_ref[...], 0.0)
        pre = _conv_pre(cur_ref[...], prev8, w_ref[...], b_ref[...])
        o_ref[...] = _silu(pre)

    return pl.pallas_call(
        body,
        name=name,
        grid=(CONV_DIM // CONV_BLK, nrow),
        in_specs=[
            pl.BlockSpec((CONV_TILE, CONV_BLK), lambda j, i: (i, j + XBC_COL0)),
            pl.BlockSpec((SUBLANE, CONV_BLK), lambda j, i: (jnp.maximum(i * r8 - 1, 0), j + XBC_COL0)),
            pl.BlockSpec((SUBLANE, CONV_BLK), lambda j, i: (0, j)),
            pl.BlockSpec((1, CONV_BLK), lambda j, i: (0, j)),
        ],
        out_specs=pl.BlockSpec((CONV_TILE, CONV_BLK), lambda j, i: (i, j)),
        out_shape=jax.ShapeDtypeStruct((L, CONV_DIM), F32),
        compiler_params=_cparams(("arbitrary", "arbitrary")),
    )(zx, zx, w8, b)


def _conv_bwd(zx, w8, b, dact, name):
    L = zx.shape[0]
    nrow = L // CONV_TILE
    r8 = CONV_TILE // SUBLANE
    last8 = L // SUBLANE - 1

    def body(cur_ref, prev_ref, next_ref, da_ref, dan_ref, w_ref, b_ref, du_ref, dw_ref, db_ref):
        i = pl.program_id(1)
        w = w_ref[...]
        b_ = b_ref[...]
        cur = cur_ref[...]
        prev8 = jnp.where(i > 0, prev_ref[...], 0.0)
        pre = _conv_pre(cur, prev8, w, b_)
        _, vjp = jax.vjp(_silu, pre)
        (dpre,) = vjp(da_ref[...])
        nxt = next_ref[...]
        pre_n = _conv_pre(nxt, cur[CONV_TILE - SUBLANE:], w, b_)
        _, vjp_n = jax.vjp(_silu, pre_n)
        (dpre_n,) = vjp_n(dan_ref[...])
        dpre_n = jnp.where(i < nrow - 1, dpre_n, 0.0)
        du = jnp.zeros_like(cur)
        dws = []
        for k in range(CONV_W):
            wk = w[CONV_W - 1 - k:CONV_W - k, :]
            du = du + wk * _shift_up(dpre, dpre_n, k)
            dws.append(jnp.sum(dpre * _shift_down(cur, prev8, k), axis=0, keepdims=True))
        du_ref[...] = du.astype(du_ref.dtype)
        dw_tile = jnp.concatenate([dws[3], dws[2], dws[1], dws[0]] + [jnp.zeros_like(dws[0])] * 4, axis=0)

        @pl.when(i == 0)
        def _():
            dw_ref[...] = jnp.zeros_like(dw_ref)
            db_ref[...] = jnp.zeros_like(db_ref)

        dw_ref[...] += dw_tile
        db_ref[...] += jnp.sum(dpre, axis=0, keepdims=True)

    return pl.pallas_call(
        body,
        name=name,
        grid=(CONV_DIM // CONV_BLK, nrow),
        in_specs=[
            pl.BlockSpec((CONV_TILE, CONV_BLK), lambda j, i: (i, j + XBC_COL0)),
            pl.BlockSpec((SUBLANE, CONV_BLK), lambda j, i: (jnp.maximum(i * r8 - 1, 0), j + XBC_COL0)),
            pl.BlockSpec((SUBLANE, CONV_BLK), lambda j, i: (jnp.minimum((i + 1) * r8, last8), j + XBC_COL0)),
            pl.BlockSpec((CONV_TILE, CONV_BLK), lambda j, i: (i, j)),
            pl.BlockSpec((SUBLANE, CONV_BLK), lambda j, i: (jnp.minimum((i + 1) * r8, last8), j)),
            pl.BlockSpec((SUBLANE, CONV_BLK), lambda j, i: (0, j)),
            pl.BlockSpec((1, CONV_BLK), lambda j, i: (0, j)),
        ],
        out_specs=[
            pl.BlockSpec((CONV_TILE, CONV_BLK), lambda j, i: (i, j)),
            pl.BlockSpec((SUBLANE, CONV_BLK), lambda j, i: (0, j)),
            pl.BlockSpec((1, CONV_BLK), lambda j, i: (0, j)),
        ],
        out_shape=[
            jax.ShapeDtypeStruct((L, CONV_DIM), BF16),
            jax.ShapeDtypeStruct((SUBLANE, CONV_DIM), F32),
            jax.ShapeDtypeStruct((1, CONV_DIM), F32),
        ],
        compiler_params=_cparams(("arbitrary", "arbitrary")),
    )(zx, zx, zx, dact, dact, w8, b)


HI = lax.Precision.HIGHEST
XS_COL0 = 0
B_COL0 = D_INNER // D_STATE
C_COL0 = B_COL0 + SSM_GROUPS
DT_COL = (D_INNER + CONV_DIM) // DT_PAD


def _dot(a, b, dims=(((1,), (0,)), ((), ())), precision=None):
    return lax.dot_general(a, b, dims, precision=precision, preferred_element_type=F32)


_DOT_DIMS = {
    "nn": (((1,), (0,)), ((), ())),
    "nt": (((1,), (1,)), ((), ())),
    "tn": (((0,), (0,)), ((), ())),
}


@functools.partial(jax.custom_vjp, nondiff_argnums=(2,))
def _bdot(a, b, mode):
    return _dot(a.astype(BF16), b.astype(BF16), _DOT_DIMS[mode])


def _bdot_fwd(a, b, mode):
    return _bdot(a, b, mode), (a, b)


def _bdot_bwd(mode, res, g):
    a, b = res
    if mode == "nn":
        return _bdot(g, b, "nt"), _bdot(a, g, "tn")
    if mode == "nt":
        return _bdot(g, b, "nn"), _bdot(g, a, "tn")
    return _bdot(b, g, "nt"), _bdot(a, g, "nn")


_bdot.defvjp(_bdot_fwd, _bdot_bwd)


def _softplus(x):
    return jnp.maximum(x, 0.0) + jnp.log(1.0 + jnp.exp(-jnp.abs(x)))


def _split2(x):
    hi = x.astype(BF16)
    return hi, (x - hi.astype(F32)).astype(BF16)


@jax.custom_vjp
def _sel_left(m, mT, x):
    hi, lo = _split2(x)
    out = _dot(m, jnp.concatenate([hi, lo], axis=1))
    n = x.shape[1]
    return out[:, :n] + out[:, n:]


_sel_left.defvjp(lambda m, mT, x: (_sel_left(m, mT, x), (m, mT)),
                 lambda res, g: (jnp.zeros_like(res[0]), jnp.zeros_like(res[1]), _sel_left(res[1], res[0], g)))


@jax.custom_vjp
def _sel_right(x, m, mT):
    hi, lo = _split2(x)
    out = _dot(jnp.concatenate([hi, lo], axis=0), m)
    r = x.shape[0]
    return out[:r] + out[r:]


_sel_right.defvjp(lambda x, m, mT: (_sel_right(x, m, mT), (m, mT)),
                  lambda res, g: (_sel_right(g, res[1], res[0]), jnp.zeros_like(res[0]), jnp.zeros_like(res[1])))


def _ssd_consts():
    l_ = lax.broadcasted_iota(jnp.int32, (CHUNK, GROUP_W), 0)
    c_ = lax.broadcasted_iota(jnp.int32, (CHUNK, GROUP_W), 1)
    s_ = c_ % CHUNK
    causal = s_ <= l_
    eye_t = (s_ == l_).astype(F32)
    r0 = lax.broadcasted_iota(jnp.int32, (CHUNK, CHUNK), 0)
    c0 = lax.broadcasted_iota(jnp.int32, (CHUNK, CHUNK), 1)
    tril = (c0 <= r0).astype(BF16)
    triu = (r0 <= c0).astype(BF16)
    rb = lax.broadcasted_iota(jnp.int32, (GROUP_W, GROUP_W), 0) // HEAD_P
    cb = lax.broadcasted_iota(jnp.int32, (GROUP_W, GROUP_W), 1) // HEAD_P
    blockdiag = rb == cb
    return causal, eye_t, tril, triu, blockdiag


def _ssd_expand(g):
    h = lax.broadcasted_iota(jnp.int32, (DT_PAD, GROUP_W), 0)
    c = lax.broadcasted_iota(jnp.int32, (DT_PAD, GROUP_W), 1)
    cT = lax.broadcasted_iota(jnp.int32, (GROUP_W, DT_PAD), 0)
    hT = lax.broadcasted_iota(jnp.int32, (GROUP_W, DT_PAD), 1)
    hpg = GROUP_W // HEAD_P
    return (h == g * hpg + c // HEAD_P).astype(BF16), (hT == g * hpg + cT // HEAD_P).astype(BF16)


def _ssd_chunk(S, xs, dt_raw, dt_bias, alog_e, d_e, Bm, Cm, E):
    causal, eye_t, tril, triu, blockdiag = _ssd_consts()
    ones = jnp.ones((CHUNK, CHUNK), BF16)
    dt = _softplus(dt_raw + dt_bias)
    dtx = _sel_right(dt, E[0], E[1])
    a = dtx * (-jnp.exp(alog_e))
    acs = _sel_left(tril, triu, a)
    rowv = _sel_left(ones, ones, acs * eye_t)
    seg = acs - rowv
    Lc = jnp.where(causal, jnp.exp(jnp.where(causal, seg, 0.0)), 0.0)
    xdt = xs * dtx
    Bt = jnp.concatenate([Bm] * 4, axis=0)
    CBc = _bdot(Cm, Bt, "nt")
    Xbd = jnp.where(blockdiag, jnp.concatenate([xdt] * 4, axis=0), 0.0)
    y_intra = _bdot(CBc * Lc, Xbd, "nn")
    y_inter = _bdot(Cm, S, "nn") * jnp.exp(acs)
    y = y_intra + y_inter + d_e * xs
    last = jnp.sum(a, axis=0, keepdims=True)
    dec_end = jnp.exp(last - acs)
    S_new = S * jnp.exp(last) + _bdot(Bm, xdt * dec_end, "tn")
    return S_new, y


SSD_GPS = 8
SSD_W = SSD_GPS * GROUP_W
SSD_N = SSD_GPS * D_STATE
SSD_STEPS = SSM_GROUPS // SSD_GPS


def _ssd_in_specs(cmap):
    return [
        pl.BlockSpec((CHUNK, SSD_W), lambda c, g: (cmap(c), g)),
        pl.BlockSpec((CHUNK, DT_PAD), lambda c, g: (cmap(c), DT_COL)),
        pl.BlockSpec((1, DT_PAD), lambda c, g: (0, 0)),
        pl.BlockSpec((1, SSD_W), lambda c, g: (0, g)),
        pl.BlockSpec((1, SSD_W), lambda c, g: (0, g)),
        pl.BlockSpec((CHUNK, SSD_N), lambda c, g: (cmap(c), B_COL0 // SSD_GPS + g)),
        pl.BlockSpec((CHUNK, SSD_N), lambda c, g: (cmap(c), C_COL0 // SSD_GPS + g)),
    ]


def _gw(u):
    return slice(u * GROUP_W, (u + 1) * GROUP_W)


def _gn(u):
    return slice(u * D_STATE, (u + 1) * D_STATE)


def _ssd_fwd(xbc, zx, dt_bias, alog_e, d_e, name, travel=None):
    L = xbc.shape[0]
    nc = L // CHUNK
    n_in = 7 + (travel is not None)

    def body(*refs):
        xs_ref, dtr_ref, bias_ref, alog_ref, d_ref, b_ref, c_ref = refs[:7]
        y_ref, st_ref = refs[n_in:n_in + 2]
        S_ref = refs[n_in + 2 + (travel is not None)]
        c = pl.program_id(0)
        step = pl.program_id(1)
        g0 = step * SSD_GPS if SSD_STEPS > 1 else 0
        if travel is not None:
            moves = lambda: _gather_moves(refs[7], refs[n_in + 2], *refs[n_in + 4:])
            pl.when(jnp.logical_and(c == 0, step == 0))(lambda: moves().start())

        @pl.when(c == 0)
        def _():
            for u in range(SSD_GPS):
                S_ref[g0 + u] = jnp.zeros((D_STATE, GROUP_W), F32)

        dtr, bias = dtr_ref[...], bias_ref[...]
        for u in range(SSD_GPS):
            S = S_ref[g0 + u]
            st_ref[u] = S
            S_new, y = _ssd_chunk(S, xs_ref[:, _gw(u)], dtr, bias, alog_ref[:, _gw(u)], d_ref[:, _gw(u)],
                                  b_ref[:, _gn(u)], c_ref[:, _gn(u)], _ssd_expand(g0 + u))
            y_ref[:, _gw(u)] = y
            S_ref[g0 + u] = S_new

        if travel is not None:
            pl.when(jnp.logical_and(c == nc - 1, step == SSD_STEPS - 1))(lambda: moves().wait())

    extra = travel is not None
    return pl.pallas_call(
        body,
        name=name,
        grid=(nc, SSD_STEPS),
        in_specs=_ssd_in_specs(lambda c: c) + [ANY] * extra,
        out_specs=[
            pl.BlockSpec((CHUNK, SSD_W), lambda c, g: (c, g)),
            pl.BlockSpec((None, SSD_GPS, D_STATE, GROUP_W), lambda c, g: (c, g, 0, 0)),
        ] + [ANY] * extra,
        out_shape=[
            jax.ShapeDtypeStruct((L, D_INNER), F32),
            jax.ShapeDtypeStruct((nc, SSM_GROUPS, D_STATE, GROUP_W), F32),
        ] + ([jax.ShapeDtypeStruct((N_CHIPS,) + travel.shape, travel.dtype)] if extra else []),
        scratch_shapes=[pltpu.VMEM((SSM_GROUPS, D_STATE, GROUP_W), F32)]
        + ((_MOVE_SEMS + [pltpu.SemaphoreType.DMA]) if extra else []),
        compiler_params=_cparams(("arbitrary", "arbitrary")),
    )(xbc, zx, dt_bias, alog_e, d_e, xbc, xbc, *([travel] if extra else []))


def _ssd_bwd(xbc, zx, dt_bias, alog_e, d_e, states, dy, name, travel=None):
    assert SSD_STEPS == 1, "one grid step writes the whole d (xs | B | C) row block"
    L = xbc.shape[0]
    nc = L // CHUNK
    rev = lambda c: nc - 1 - c
    n_in = 9 + (travel is not None)

    def body(*refs):
        xs_ref, dtr_ref, bias_ref, alog_ref, d_ref, b_ref, c_ref, st_ref, dy_ref = refs[:9]
        dact_ref, ddt_ref, dbias_ref, dalog_ref, dd_ref = refs[n_in:n_in + 5]
        dS_ref = refs[n_in + 5 + (travel is not None)]
        c = pl.program_id(0)
        step = pl.program_id(1)
        g0 = 0
        first = jnp.logical_and(c == 0, step == 0)
        if travel is not None:
            moves = lambda: _scatter_moves(refs[9], refs[n_in + 5], *refs[n_in + 7:])
            pl.when(first)(lambda: moves().start())

        @pl.when(c == 0)
        def _():
            for u in range(SSD_GPS):
                dS_ref[g0 + u] = jnp.zeros((D_STATE, GROUP_W), F32)
                dalog_ref[g0 + u] = jnp.zeros((1, GROUP_W), F32)
                dd_ref[g0 + u] = jnp.zeros((1, GROUP_W), F32)

        @pl.when(first)
        def _():
            dbias_ref[...] = jnp.zeros_like(dbias_ref)

        dtr, bias = dtr_ref[...], bias_ref[...]
        ddt_sum = jnp.zeros((CHUNK, DT_PAD), F32)
        dbias_sum = jnp.zeros((1, DT_PAD), F32)
        for u in range(SSD_GPS):
            E = _ssd_expand(g0 + u)
            _, vjp = jax.vjp(
                lambda S, xs, dtr_, bias_, alog, dsk, Bm, Cm: _ssd_chunk(S, xs, dtr_, bias_, alog, dsk, Bm, Cm, E),
                st_ref[u], xs_ref[:, _gw(u)], dtr, bias, alog_ref[:, _gw(u)], d_ref[:, _gw(u)],
                b_ref[:, _gn(u)], c_ref[:, _gn(u)])
            dS, dxs, ddtr, dbias, dalog, dd, dB, dC = vjp((dS_ref[g0 + u], dy_ref[:, _gw(u)]))
            dS_ref[g0 + u] = dS
            dact_ref[:, _gw(u)] = dxs
            dact_ref[:, slice(D_INNER + u * D_STATE, D_INNER + (u + 1) * D_STATE)] = dB
            dact_ref[:, slice(D_INNER + SSD_N + u * D_STATE, D_INNER + SSD_N + (u + 1) * D_STATE)] = dC
            dalog_ref[g0 + u] += dalog
            dd_ref[g0 + u] += dd
            ddt_sum = ddt_sum + ddtr
            dbias_sum = dbias_sum + dbias

        ddt_ref[...] = ddt_sum
        dbias_ref[...] += dbias_sum
        if travel is not None:
            pl.when(jnp.logical_and(c == nc - 1, step == SSD_STEPS - 1))(lambda: moves().wait())

    extra = travel is not None
    whole3 = pl.BlockSpec((SSM_GROUPS, 1, GROUP_W), lambda c, g: (0, 0, 0))
    return pl.pallas_call(
        body,
        name=name,
        grid=(nc, SSD_STEPS),
        in_specs=_ssd_in_specs(rev) + [
            pl.BlockSpec((None, SSD_GPS, D_STATE, GROUP_W), lambda c, g: (rev(c), g, 0, 0)),
            pl.BlockSpec((CHUNK, SSD_W), lambda c, g: (rev(c), g)),
        ] + [ANY] * extra,
        out_specs=[
            pl.BlockSpec((CHUNK, CONV_DIM), lambda c, g: (rev(c), 0)),
            pl.BlockSpec((CHUNK, DT_PAD), lambda c, g: (rev(c), 0)),
            pl.BlockSpec((1, DT_PAD), lambda c, g: (0, 0)),
            whole3,
            whole3,
        ] + [ANY] * extra,
        out_shape=[
            jax.ShapeDtypeStruct((L, CONV_DIM), F32),
            jax.ShapeDtypeStruct((L, DT_PAD), F32),
            jax.ShapeDtypeStruct((1, DT_PAD), F32),
            jax.ShapeDtypeStruct((SSM_GROUPS, 1, GROUP_W), F32),
            jax.ShapeDtypeStruct((SSM_GROUPS, 1, GROUP_W), F32),
        ] + ([jax.ShapeDtypeStruct((N_CHIPS - 1,) + travel.shape[1:], travel.dtype)] if extra else []),
        scratch_shapes=[pltpu.VMEM((SSM_GROUPS, D_STATE, GROUP_W), F32)] + (_MOVE_SEMS if extra else []),
        compiler_params=_cparams(("arbitrary", "arbitrary")),
    )(xbc, zx, dt_bias, alog_e, d_e, xbc, xbc, states, dy, *([travel] if extra else []))


LOG2E = math.log2(math.e)
LN2 = math.log(2.0)
SB_Q_SCALE = SB_D ** -0.5 * LOG2E


def _sb_scores(q, kj):
    z2 = _dot(q, kj, dims=(((1,), (1,)), ((), ())))
    lb = jnp.minimum(z2, 0.0) - jnp.log2(1.0 + jnp.exp2(-jnp.abs(z2)))
    return lb, lb - z2


SB_TQ = 256
SB_TK = 512
SB_TC = SB_TK // 2


def _sb_after():
    j = lax.broadcasted_iota(jnp.int32, (2 * SB_TC, SB_TC), 0) % SB_TC
    s = lax.broadcasted_iota(jnp.int32, (2 * SB_TC, SB_TC), 1)
    return (j > s).astype(BF16)


def _sb_suffix(x, after):
    T = x.shape[0]
    hi = x.astype(BF16)
    lo = (x - hi.astype(F32)).astype(BF16)
    stacked = jnp.concatenate([jnp.concatenate([hi[:, :SB_TC], lo[:, :SB_TC]], axis=1),
                               jnp.concatenate([hi[:, SB_TC:], lo[:, SB_TC:]], axis=1)], axis=0)
    out = _dot(stacked, after)
    rs_right = jnp.sum(x[:, SB_TC:], axis=1, keepdims=True)
    return (jnp.concatenate([out[:T] + rs_right, out[T:]], axis=1),
            rs_right + jnp.sum(x[:, :SB_TC], axis=1, keepdims=True))


SB_ROWS_FWD = 256
SB_ROWS_BWD = 128


def _sb_seen(rows, row0, start):
    t = lax.broadcasted_iota(jnp.int32, (rows, SB_TK), 0) + row0
    s = lax.broadcasted_iota(jnp.int32, (rows, SB_TK), 1) + start
    return s < t


def _sb_walk(gd, groups, carry):
    carry = groups([gd], carry, True)
    carry = lax.cond(gd % 2 == 1, lambda c: groups([gd - 1], c, False), lambda c: c, carry)
    base = gd - 1 - gd % 2
    return lax.fori_loop(0, gd // 2, lambda n, c: groups([base - 2 * n, base - 2 * n - 1], c, False), carry)


def _sb_fwd(qs, kn, v, name):
    H, L, d = qs.shape
    nq = L // SB_TQ
    after = _sb_after()
    SB_ROWS = SB_ROWS_FWD

    def body(q_ref, k_ref, v_ref, aft_ref, o_ref):
        i = pl.program_id(1)
        gd = (i * SB_TQ) // SB_TK
        subs = range(SB_TQ // SB_ROWS)
        qr = [q_ref[r * SB_ROWS:(r + 1) * SB_ROWS, :] for r in subs]

        def prep(G, diag, r):
            start = pl.multiple_of(G * SB_TK, SB_TK)
            lb, lk = _sb_scores(qr[r], k_ref[pl.ds(start, SB_TK), :])
            seen = _sb_seen(SB_ROWS, i * SB_TQ + r * SB_ROWS, start) if diag else None
            if diag:
                lk = jnp.where(seen, lk, 0.0)
            between, rs = _sb_suffix(lk, aft_ref[...])
            return lb + between, rs, seen, v_ref[pl.ds(start, SB_TK), :]

        def groups(Gs, carry, diag):
            carry = list(carry)
            pre = [[prep(G, diag, r) for G in Gs] for r in subs]
            for r in subs:
                R, acc = carry[r]
                for s, rs, seen, vg in pre[r]:
                    w = jnp.exp2(s + R)
                    if diag:
                        w = jnp.where(seen, w, 0.0)
                    acc = acc + _dot(w.astype(BF16), vg)
                    R = R + rs
                carry[r] = (R, acc)
            return tuple(carry)

        zero = (jnp.zeros((SB_ROWS, 1), F32), jnp.zeros((SB_ROWS, d), F32))
        out = _sb_walk(gd, groups, tuple(zero for _ in subs))
        for r in subs:
            o_ref[r * SB_ROWS:(r + 1) * SB_ROWS, :] = out[r][1]

    return pl.pallas_call(
        body,
        name=name,
        grid=(H, nq),
        in_specs=[
            pl.BlockSpec((None, SB_TQ, d), lambda h, i: (h, i, 0)),
            pl.BlockSpec((None, L, d), lambda h, i: (h, 0, 0)),
            pl.BlockSpec((None, L, d), lambda h, i: (h, 0, 0)),
            pl.BlockSpec((2 * SB_TC, SB_TC), lambda h, i: (0, 0)),
        ],
        out_specs=pl.BlockSpec((None, SB_TQ, d), lambda h, i: (h, i, 0)),
        out_shape=jax.ShapeDtypeStruct((H, L, d), F32),
        compiler_params=_cparams(("parallel", "arbitrary")),
    )(qs, kn, v, after)


def _sb_bwd(qs, kn, v, o, do, name):
    H, L, d = qs.shape
    nq = L // SB_TQ
    after = _sb_after()
    SB_ROWS = SB_ROWS_BWD
    qsT = qs.transpose(0, 2, 1)
    do16 = do.astype(BF16)
    doT = do16.transpose(0, 2, 1)

    def body(q_ref, qT_ref, k_ref, v_ref, o_ref, do_ref, doT_ref, aft_ref, dq_ref, dkT_ref, dvT_ref):
        i = pl.program_id(1)

        @pl.when(i == 0)
        def _():
            dkT_ref[...] = jnp.zeros_like(dkT_ref)
            dvT_ref[...] = jnp.zeros_like(dvT_ref)

        gd = (i * SB_TQ) // SB_TK
        subs = range(SB_TQ // SB_ROWS)
        rows = [slice(r * SB_ROWS, (r + 1) * SB_ROWS) for r in subs]
        qr = [q_ref[rows[r], :] for r in subs]
        qT = [qT_ref[:, rows[r]] for r in subs]
        do_ = [do_ref[rows[r], :] for r in subs]
        doT_ = [doT_ref[:, rows[r]] for r in subs]
        D = [jnp.sum(o_ref[rows[r], :] * do_[r].astype(F32), axis=1, keepdims=True) for r in subs]

        def prep(G, diag, r):
            start = pl.multiple_of(G * SB_TK, SB_TK)
            kg = k_ref[pl.ds(start, SB_TK), :]
            lb, lk = _sb_scores(qr[r], kg)
            seen = _sb_seen(SB_ROWS, i * SB_TQ + r * SB_ROWS, start) if diag else None
            if diag:
                lk = jnp.where(seen, lk, 0.0)
            between, rs = _sb_suffix(lk, aft_ref[...])
            dw = _dot(do_[r], v_ref[pl.ds(start, SB_TK), :], dims=(((1,), (1,)), ((), ())))
            return start, kg, lb + between, rs, dw, jnp.exp2(lb), seen

        def groups(Gs_, carry, diag):
            carry = list(carry)
            pre = [[prep(G, diag, r) for G in Gs_] for r in subs]
            for r in subs:
                R, Gs, dq = carry[r]
                for start, kg, s, rs, dw, sig, seen in pre[r]:
                    w = jnp.exp2(s + R)
                    if diag:
                        w = jnp.where(seen, w, 0.0)
                    w16 = w.astype(BF16)
                    g = w16.astype(F32) * dw
                    dvT_ref[:, pl.ds(start, SB_TK)] += _dot(doT_[r], w16)
                    g_right, gsum = _sb_suffix(g, aft_ref[...])
                    dz = g - sig * (D[r] - Gs - g_right)
                    if diag:
                        dz = jnp.where(seen, dz, 0.0)
                    dz16 = dz.astype(BF16)
                    dq = dq + _dot(dz16, kg)
                    dkT_ref[:, pl.ds(start, SB_TK)] += _dot(qT[r], dz16)
                    R = R + rs
                    Gs = Gs + gsum
                carry[r] = (R, Gs, dq)
            return tuple(carry)

        zero1 = jnp.zeros((SB_ROWS, 1), F32)
        out = _sb_walk(gd, groups, tuple((zero1, zero1, jnp.zeros((SB_ROWS, d), F32)) for _ in subs))
        for r in subs:
            dq_ref[rows[r], :] = out[r][2]

    blk = pl.BlockSpec((None, SB_TQ, d), lambda h, i: (h, i, 0))
    blkT = pl.BlockSpec((None, d, SB_TQ), lambda h, i: (h, 0, i))
    full = pl.BlockSpec((None, L, d), lambda h, i: (h, 0, 0))
    fullT = pl.BlockSpec((None, d, L), lambda h, i: (h, 0, 0))
    return pl.pallas_call(
        body,
        name=name,
        grid=(H, nq),
        in_specs=[blk, blkT, full, full, blk, blk, blkT, pl.BlockSpec((2 * SB_TC, SB_TC), lambda h, i: (0, 0))],
        out_specs=[blk, fullT, fullT],
        out_shape=[jax.ShapeDtypeStruct((H, L, d), F32), jax.ShapeDtypeStruct((H, d, L), F32),
                   jax.ShapeDtypeStruct((H, d, L), F32)],
        compiler_params=_cparams(("parallel", "arbitrary")),
    )(qs, qsT, kn, v, o, do16, doT, after)


def _fold_heads(rows, name):
    def body(x_ref, o_ref):
        c = lax.broadcasted_iota(jnp.int32, (D_INNER, DT_PAD), 0)
        h = lax.broadcasted_iota(jnp.int32, (D_INNER, DT_PAD), 1)
        o_ref[...] = _dot(x_ref[...], (c // HEAD_P == h).astype(F32), precision=HI)

    return pl.pallas_call(
        body, name=name, out_shape=jax.ShapeDtypeStruct((SUBLANE, DT_PAD), F32),
        compiler_params=_cparams(),
    )(rows)


def _ffn_fwd(x, r, norm_w, w_in_t, w_out, tag):
    x1, h = _norm_fwd(x, r, norm_w, f"norm_ffn_fwd{tag}")
    gu = _mm(h, w_in_t, "nt", F32, f"ffn_in_fwd{tag}")
    act = _swiglu_fwd(gu, f"swiglu_fwd{tag}")
    f = _mm(act, w_out, "nn", F32, f"ffn_out_fwd{tag}")
    return x1, h, gu, act, f


def _ffn_bwd(x1, h, gu, act, norm_w, w_in_t, w_out, dres, tag):
    d16 = dres.astype(BF16)
    dact = _mm(d16, w_out, "nt", BF16, f"ffn_out_dx{tag}")
    g_w_out = _mm(act, d16, "tn", F32, f"ffn_out_dw{tag}")
    dgu = _swiglu_bwd(gu, dact, f"swiglu_bwd{tag}")
    dh = _mm(dgu, w_in_t, "nn", BF16, f"ffn_in_dx{tag}")
    g_w_in_t = _mm(dgu, h, "tn", F32, f"ffn_in_dw{tag}")
    dres, g_norm = _norm_bwd(x1, norm_w, dh, dres, f"norm_ffn_bwd{tag}")
    return dres, g_norm, g_w_in_t, g_w_out


def _rest_weights(gb):
    whole = _whole_of_shards(gb, SEG_REST)
    return {"ssm_w_out": whole["ssm_w_out"][0], "sb_w_qkv_t": whole["sb_w_qkv"][0], "sb_w_o": whole["sb_w_o"][0],
            "ffn_w_in_t": whole["ffn_w_in"], "ffn_w_out": whole["ffn_w_out"]}


def _local_step(x, target, W, wb_rest=None):
    L = x.shape[0]
    H = SB_HEADS
    nm0, nm1 = W["norm_mix"][0:1], W["norm_mix"][1:2]
    nf0, nf1 = W["norm_ffn"][0:1], W["norm_ffn"][1:2]

    _, h0 = _norm_fwd(x, None, nm0, "norm_mix_fwd0")
    zx = _mm(h0, W["ssm_w_in_t"], "nt", F32, "ssm_in_fwd")
    xbc = _conv_fwd(zx, W["conv_w8"], W["conv_b"], "conv_fwd")
    if wb_rest is None:
        y, states = _ssd_fwd(xbc, zx, W["dt_bias"], W["alog_e"], W["d_e"], "ssd_fwd")
    else:
        y, states, gb_rest = _ssd_fwd(xbc, zx, W["dt_bias"], W["alog_e"], W["d_e"], "ssd_fwd", travel=wb_rest)
        W = {**W, **_rest_weights(gb_rest)}
    yn = _gated_norm_fwd(y, zx, W["ssm_norm_w"], "gated_norm_fwd")
    mix0 = _mm(yn, W["ssm_w_out"], "nn", F32, "ssm_out_fwd")
    x1, h1, gu0, act0, f0 = _ffn_fwd(x, mix0, nf0, W["ffn_w_in_t"][0], W["ffn_w_out"][0], "0")

    x2, h2 = _norm_fwd(x1, f0, nm1, "norm_mix_fwd1")
    qkv = _mm(h2, W["sb_w_qkv_t"], "nt", F32, "sb_qkv_fwd")
    qkv_t = qkv.reshape(L, 3, H, SB_D).transpose(1, 2, 0, 3)
    q_r = qkv_t[0].reshape(H * L, SB_D)
    k_r = qkv_t[1].reshape(H * L, SB_D)
    qs = _qk_norm_fwd(q_r, W["sb_q_gain"], SB_Q_SCALE, "q_norm_fwd").reshape(H, L, SB_D)
    kn = _qk_norm_fwd(k_r, W["sb_k_gain"], 1.0, "k_norm_fwd").reshape(H, L, SB_D)
    vb = qkv_t[2].astype(BF16)
    o = _sb_fwd(qs, kn, vb, "sb_fwd")
    o_flat = o.transpose(1, 0, 2).reshape(L, D_MODEL).astype(BF16)
    mix1 = _mm(o_flat, W["sb_w_o"], "nn", F32, "sb_o_fwd")
    x3, h3, gu1, act1, f1 = _ffn_fwd(x2, mix1, nf1, W["ffn_w_in_t"][1], W["ffn_w_out"][1], "1")

    dres, sq = _loss_bwd(x3, f1, target, "loss")

    dres, g_nf1, g_fin1, g_fout1 = _ffn_bwd(x3, h3, gu1, act1, nf1, W["ffn_w_in_t"][1], W["ffn_w_out"][1], dres, "1")
    d16 = dres.astype(BF16)
    do_flat = _mm(d16, W["sb_w_o"], "nt", F32, "sb_o_dx")
    g_w_o = _mm(o_flat, d16, "tn", F32, "sb_o_dw")
    do = do_flat.reshape(L, H, SB_D).transpose(1, 0, 2)
    dqs, dknT, dvT = _sb_bwd(qs, kn, vb, o, do, "sb_bwd")
    dkn = dknT.transpose(0, 2, 1)
    dq, g_qg = _qk_norm_bwd(q_r, W["sb_q_gain"], dqs.reshape(H * L, SB_D), SB_Q_SCALE, LN2, "q_norm_bwd")
    dk, g_kg = _qk_norm_bwd(k_r, W["sb_k_gain"], dkn.reshape(H * L, SB_D), 1.0, LN2, "k_norm_bwd")
    dqkv = jnp.stack([dq.reshape(H, L, SB_D), dk.reshape(H, L, SB_D), dvT.transpose(0, 2, 1)])
    dqkv = dqkv.transpose(2, 0, 1, 3).reshape(L, 3 * D_MODEL).astype(BF16)
    dh2 = _mm(dqkv, W["sb_w_qkv_t"], "nn", BF16, "sb_qkv_dx")
    g_w_qkv_t = _mm(dqkv, h2, "tn", F32, "sb_qkv_dw")
    dres, g_nm1 = _norm_bwd(x2, nm1, dh2, dres, "norm_mix_bwd1")

    dres, g_nf0, g_fin0, g_fout0 = _ffn_bwd(x1, h1, gu0, act0, nf0, W["ffn_w_in_t"][0], W["ffn_w_out"][0], dres, "0")
    d16 = dres.astype(BF16)
    dyn = _mm(d16, W["ssm_w_out"], "nt", BF16, "ssm_out_dx")
    g_w_out = _mm(yn, d16, "tn", F32, "ssm_out_dw")
    dy, dz, g_snw = _gated_norm_bwd(y, zx, W["ssm_norm_w"], dyn, "gated_norm_bwd")
    rest = {"ssm_w_out": [g_w_out], "sb_w_qkv": [g_w_qkv_t], "sb_w_o": [g_w_o],
            "ffn_w_in": [g_fin0, g_fin1], "ffn_w_out": [g_fout0, g_fout1]}
    if wb_rest is None:
        dact, ddt, g_dtb, g_alog_e, g_d_e = _ssd_bwd(
            xbc, zx, W["dt_bias"], W["alog_e"], W["d_e"], states, dy, "ssd_bwd")
        exchanged = None
    else:
        slots = jnp.stack([_shard_of_whole(rest, s, SEG_REST) for s in range(N_CHIPS)])
        dact, ddt, g_dtb, g_alog_e, g_d_e, received = _ssd_bwd(
            xbc, zx, W["dt_bias"], W["alog_e"], W["d_e"], states, dy, "ssd_bwd", travel=slots.astype(BF16))
        exchanged = (slots, received)
    dxbc, g_cw8, g_cb = _conv_bwd(zx, W["conv_w8"], W["conv_b"], dact, "conv_bwd")
    dzx = jnp.concatenate([dz, dxbc, ddt.astype(BF16)], axis=1)
    dh0 = _mm(dzx, W["ssm_w_in_t"], "nn", BF16, "ssm_in_dx")
    g_w_in_t = _mm(dzx, h0, "tn", F32, "ssm_in_dw")
    grad_x, g_nm0 = _norm_bwd(x, nm0, dh0, dres, "norm_mix_bwd0")

    per_head = jnp.concatenate(
        [g_alog_e.reshape(1, D_INNER), g_d_e.reshape(1, D_INNER), jnp.zeros((SUBLANE - 2, D_INNER), F32)], axis=0)
    folded = _fold_heads(per_head, "fold_heads")
    grads = {
        "norm_mix": jnp.concatenate([g_nm0, g_nm1], axis=0),
        "norm_ffn": jnp.concatenate([g_nf0, g_nf1], axis=0),
        "ssm_w_in": [g_w_in_t],
        "ssm_conv_w": [g_cw8],
        "ssm_conv_b": g_cb,
        "ssm_dt_bias": g_dtb[:, :SSM_HEADS],
        "ssm_a_log": folded[0:1, :SSM_HEADS],
        "ssm_d": folded[1:2, :SSM_HEADS],
        "ssm_norm_w": g_snw,
        "sb_q_gain": g_qg,
        "sb_k_gain": g_kg,
        **rest,
    }
    return (sq, grad_x, grads) if wb_rest is None else (sq, grad_x, grads, exchanged)


def _prep_weights(full):
    W = _prep_vectors(full)
    w_in_t = full["ssm_w_in"].T
    W["ssm_w_in_t"] = jnp.pad(w_in_t, ((0, D_IN_PAD - D_IN_PROJ), (0, 0))).astype(BF16)
    W["conv_w8"] = jnp.pad(full["ssm_conv_w"], ((0, SUBLANE - CONV_W), (0, 0)))
    W["ssm_w_out"] = full["ssm_w_out"].astype(BF16)
    W["sb_w_qkv_t"] = full["sb_w_qkv"].T.astype(BF16)
    W["sb_w_o"] = full["sb_w_o"].astype(BF16)
    W["ffn_w_in_t"] = jnp.swapaxes(full["ffn_w_in"], 1, 2).astype(BF16)
    W["ffn_w_out"] = full["ffn_w_out"].astype(BF16)
    return W


def _prep_vectors(full):
    W = {}
    W["norm_mix"] = full["norm_mix"]
    W["norm_ffn"] = full["norm_ffn"]
    W["conv_b"] = full["ssm_conv_b"]
    W["dt_bias"] = jnp.pad(full["ssm_dt_bias"], ((0, 0), (0, DT_PAD - SSM_HEADS)))
    W["alog_e"] = jnp.repeat(full["ssm_a_log"], HEAD_P, axis=1)
    W["d_e"] = jnp.repeat(full["ssm_d"], HEAD_P, axis=1)
    W["ssm_norm_w"] = full["ssm_norm_w"]
    W["sb_q_gain"] = full["sb_q_gain"]
    W["sb_k_gain"] = full["sb_k_gain"]
    return W


N_CHIPS = 4
N_DEV = 8
FLAT_W = D_MODEL

SEG_FIRST = (
    ("ssm_w_in", (1, D_MODEL, D_IN_PROJ // N_CHIPS), True),
    ("ssm_conv_w", (1, CONV_W, CONV_DIM // N_CHIPS), False),
)
SEG_REST = (
    ("ssm_w_out", (1, D_INNER // N_CHIPS, D_MODEL), False),
    ("sb_w_qkv", (1, D_MODEL, 3 * D_MODEL // N_CHIPS), True),
    ("sb_w_o", (1, D_MODEL // N_CHIPS, D_MODEL), False),
    ("ffn_w_in", (2, D_MODEL, 2 * D_FF // N_CHIPS), True),
    ("ffn_w_out", (2, D_FF // N_CHIPS, D_MODEL), False),
)
SHARDED = SEG_FIRST + SEG_REST
REPLICATED = (
    ("norm_mix", (2, D_MODEL)), ("norm_ffn", (2, D_MODEL)), ("ssm_conv_b", (1, CONV_DIM)),
    ("ssm_norm_w", (1, D_INNER)), ("ssm_dt_bias", (1, SSM_HEADS)), ("ssm_a_log", (1, SSM_HEADS)),
    ("ssm_d", (1, SSM_HEADS)), ("sb_q_gain", (1, SB_D)), ("sb_k_gain", (1, SB_D)),
)
ADAM_TILE = 256
_USED_SMALL = sum(math.prod(s) for _, s in REPLICATED)
SMALL_ROWS = -(-_USED_SMALL // (SUBLANE * FLAT_W)) * SUBLANE


def _flat_rows(shape):
    return math.prod(shape) // FLAT_W


def _seg_rows(seg):
    used = sum(_flat_rows(s) for _, s, _ in seg)
    return used, -(-used // ADAM_TILE) * ADAM_TILE


def _pad_rows(parts, seg, dtype):
    used, rows = _seg_rows(seg)
    return parts + [jnp.zeros((rows - used, FLAT_W), dtype)] if rows > used else parts


def _pack_shard(d, dtype, seg):
    parts = []
    for n, _, transposed in seg:
        a = jnp.swapaxes(d[n], 1, 2) if transposed else d[n]
        parts.append(a.reshape(-1, FLAT_W).astype(dtype))
    return jnp.concatenate(_pad_rows(parts, seg, dtype), axis=0)


def _unpack_shard(flat, seg):
    out, r = {}, 0
    for n, s, transposed in seg:
        k = _flat_rows(s)
        a = flat[r:r + k]
        out[n] = jnp.swapaxes(a.reshape(s[0], s[2], s[1]), 1, 2) if transposed else a.reshape(s)
        r += k
    return out


def _shard_of_whole(whole, chip, seg):
    parts = []
    for n, s, _ in seg:
        k = _flat_rows(s) // s[0]
        if n == "ssm_conv_w":
            parts.append(whole[n][0][:CONV_W, chip * FLAT_W:(chip + 1) * FLAT_W])
        else:
            parts.extend(a[chip * k:(chip + 1) * k] for a in whole[n])
    return jnp.concatenate(_pad_rows(parts, seg, parts[0].dtype), axis=0)


def _whole_of_shards(gb, seg):
    out, r = {}, 0
    for n, s, _ in seg:
        k = _flat_rows(s) // s[0]
        if n != "ssm_conv_w":
            out[n] = [jnp.concatenate([gb[p, r + l * k:r + (l + 1) * k] for p in range(N_CHIPS)], axis=0)
                      for l in range(s[0])]
        r += k * s[0]
    return out


def _pack_small(d):
    parts = [d[n].reshape(-1) for n, _ in REPLICATED]
    parts.append(jnp.zeros((SMALL_ROWS * FLAT_W - _USED_SMALL,), parts[0].dtype))
    return jnp.concatenate(parts).reshape(SMALL_ROWS, FLAT_W)


def _unpack_small(flat):
    flat = flat.reshape(-1)
    out, r = {}, 0
    for n, s in REPLICATED:
        k = math.prod(s)
        out[n] = flat[r:r + k].reshape(s)
        r += k
    return out


ANY = pl.BlockSpec(memory_space=pl.ANY)


def _other_chips():
    x, y, c = lax.axis_index("x"), lax.axis_index("y"), lax.axis_index("c")
    return x, y, c, [(1 - x, y), (x, 1 - y), (1 - x, 1 - y)]


def _remote(src, dst, send_sem, recv_sem, dev):
    return pltpu.make_async_remote_copy(src_ref=src, dst_ref=dst, send_sem=send_sem, recv_sem=recv_sem,
                                        device_id=dev, device_id_type=MESH_T)


def _gather_weights(wb, ws):
    def body(wb_ref, ws_ref, gb_ref, gs_ref, send_sems, recv_sems, loc_sems):
        x, y, c, others = _other_chips()
        me = 2 * x + y
        pairs = ((wb_ref, gb_ref), (ws_ref, gs_ref))
        local = [pltpu.make_async_copy(src, dst.at[me], loc_sems.at[t]) for t, (src, dst) in enumerate(pairs)]
        for cp in local:
            cp.start()
        sends = []
        for k, (px, py) in enumerate(others):
            for t, (src, dst) in enumerate(pairs):
                cp = _remote(src, dst.at[me], send_sems.at[2 * k + t], recv_sems.at[2 * k + t], (px, py, c))
                cp.start()
                sends.append(cp)
        for k, (px, py) in enumerate(others):
            for t, (src, dst) in enumerate(pairs):
                _remote(src, dst.at[2 * px + py], send_sems.at[2 * k + t], recv_sems.at[2 * k + t],
                        (px, py, c)).wait_recv()
        for cp in sends:
            cp.wait_send()
        for cp in local:
            cp.wait()

    return pl.pallas_call(
        body, name="gather_weights",
        in_specs=[ANY, ANY], out_specs=[ANY, ANY],
        out_shape=[jax.ShapeDtypeStruct((N_CHIPS,) + wb.shape, wb.dtype),
                   jax.ShapeDtypeStruct((N_CHIPS,) + ws.shape, ws.dtype)],
        scratch_shapes=[pltpu.SemaphoreType.DMA((6,)), pltpu.SemaphoreType.DMA((6,)), pltpu.SemaphoreType.DMA((2,))],
    )(wb, ws)


def _scatter_grads(gpack):
    def body(g_ref, r_ref, send_sems, recv_sems):
        x, y, c, others = _other_chips()
        sends = []
        for k, (px, py) in enumerate(others):
            cp = _remote(g_ref.at[2 * px + py], r_ref.at[k], send_sems.at[k], recv_sems.at[k], (px, py, c))
            cp.start()
            sends.append(cp)
        for cp in sends:
            cp.wait_recv()
        for cp in sends:
            cp.wait_send()

    return pl.pallas_call(
        body, name="scatter_grads",
        in_specs=[ANY], out_specs=ANY,
        out_shape=jax.ShapeDtypeStruct((N_CHIPS - 1,) + gpack.shape[1:], gpack.dtype),
        scratch_shapes=[pltpu.SemaphoreType.DMA((3,)), pltpu.SemaphoreType.DMA((3,))],
    )(gpack)


class _Moves:
    def __init__(self, sends, recvs, local=None):
        self.sends, self.recvs, self.local = sends, recvs, local

    def start(self):
        if self.local is not None:
            self.local.start()
        for cp in self.sends:
            cp.start()

    def wait(self):
        for cp in self.recvs:
            cp.wait_recv()
        for cp in self.sends:
            cp.wait_send()
        if self.local is not None:
            self.local.wait()


def _gather_moves(wb_ref, gb_ref, send_sems, recv_sems, loc_sem):
    x, y, c, others = _other_chips()
    me = 2 * x + y
    sends = [_remote(wb_ref, gb_ref.at[me], send_sems.at[k], recv_sems.at[k], (px, py, c))
             for k, (px, py) in enumerate(others)]
    recvs = [_remote(wb_ref, gb_ref.at[2 * px + py], send_sems.at[k], recv_sems.at[k], (px, py, c))
             for k, (px, py) in enumerate(others)]
    return _Moves(sends, recvs, pltpu.make_async_copy(wb_ref, gb_ref.at[me], loc_sem))


def _scatter_moves(g_ref, r_ref, send_sems, recv_sems):
    x, y, c, others = _other_chips()
    sends = [_remote(g_ref.at[2 * px + py], r_ref.at[k], send_sems.at[k], recv_sems.at[k], (px, py, c))
             for k, (px, py) in enumerate(others)]
    return _Moves(sends, sends)


_MOVE_SEMS = [pltpu.SemaphoreType.DMA((N_CHIPS - 1,)), pltpu.SemaphoreType.DMA((N_CHIPS - 1,))]


def _exchange_partials(part_a, part_b, small):
    def body(pa_ref, pb_ref, s_ref, psa_ref, psb_ref, sa_ref, send_sems, recv_sems, loc_sem):
        x, y, c = lax.axis_index("x"), lax.axis_index("y"), lax.axis_index("c")
        me = 4 * x + 2 * y + c
        local = pltpu.make_async_copy(s_ref, sa_ref.at[me], loc_sem)
        local.start()
        sib = _remote(pa_ref, psa_ref, send_sems.at[0], recv_sems.at[0], (x, y, 1 - c))
        sib_b = _remote(pb_ref, psb_ref, send_sems.at[N_DEV], recv_sems.at[N_DEV], (x, y, 1 - c))
        sib.start()
        sib_b.start()
        sends = [sib, sib_b]
        for k in range(1, N_DEV):
            fx, fy, fc = (k >> 2) & 1, (k >> 1) & 1, k & 1
            px, py, pc = x ^ fx, y ^ fy, c ^ fc
            cp = _remote(s_ref, sa_ref.at[me], send_sems.at[k], recv_sems.at[k], (px, py, pc))
            cp.start()
            sends.append(cp)
        sib.wait_recv()
        sib_b.wait_recv()
        for k in range(1, N_DEV):
            fx, fy, fc = (k >> 2) & 1, (k >> 1) & 1, k & 1
            px, py, pc = x ^ fx, y ^ fy, c ^ fc
            _remote(s_ref, sa_ref.at[4 * px + 2 * py + pc], send_sems.at[k], recv_sems.at[k], (px, py, pc)).wait_recv()
        for cp in sends:
            cp.wait_send()
        local.wait()

    return pl.pallas_call(
        body, name="exchange_partials",
        in_specs=[ANY, ANY, ANY], out_specs=[ANY, ANY, ANY],
        out_shape=[jax.ShapeDtypeStruct(part_a.shape, part_a.dtype),
                   jax.ShapeDtypeStruct(part_b.shape, part_b.dtype),
                   jax.ShapeDtypeStruct((N_DEV,) + small.shape, small.dtype)],
        scratch_shapes=[pltpu.SemaphoreType.DMA((N_DEV + 1,)), pltpu.SemaphoreType.DMA((N_DEV + 1,)),
                        pltpu.SemaphoreType.DMA],
    )(part_a, part_b, small)


def _partial_sum(own, recv, name):
    R = own.shape[0]

    def body(o_ref, r_ref, p_ref):
        acc = o_ref[...]
        for k in range(N_CHIPS - 1):
            acc = acc + r_ref[k].astype(F32)
        p_ref[...] = acc

    return pl.pallas_call(
        body, name=name, grid=(R // ADAM_TILE,),
        in_specs=[pl.BlockSpec((ADAM_TILE, FLAT_W), lambda i: (i, 0)),
                  pl.BlockSpec((N_CHIPS - 1, ADAM_TILE, FLAT_W), lambda i: (0, i, 0))],
        out_specs=pl.BlockSpec((ADAM_TILE, FLAT_W), lambda i: (i, 0)),
        out_shape=jax.ShapeDtypeStruct(own.shape, F32),
        compiler_params=_cparams(("parallel",)),
    )(own, recv)


def _adamw_math(w, g, m, v):
    m = ADAM_B1 * m + (1.0 - ADAM_B1) * g
    v = ADAM_B2 * v + (1.0 - ADAM_B2) * jnp.square(g)
    m_hat = m / (1.0 - ADAM_B1 ** ADAM_STEP)
    v_hat = v / (1.0 - ADAM_B2 ** ADAM_STEP)
    delta = -ADAM_LR * (m_hat / (jnp.sqrt(v_hat) + ADAM_EPS) + ADAM_WD * w)
    return delta, m, v


def _adamw(w, m, v, parts, tile, name):
    n, R, _ = parts.shape

    def body(w_ref, m_ref, v_ref, p_ref, g_ref, d_ref, nm_ref, nv_ref):
        g = p_ref[0]
        for k in range(1, n):
            g = g + p_ref[k]
        delta, nm, nv = _adamw_math(w_ref[...], g, m_ref[...], v_ref[...])
        g_ref[...] = g
        d_ref[...] = delta
        nm_ref[...] = nm
        nv_ref[...] = nv

    blk = pl.BlockSpec((tile, FLAT_W), lambda i: (i, 0))
    return pl.pallas_call(
        body, name=name, grid=(R // tile,),
        in_specs=[blk, blk, blk, pl.BlockSpec((n, tile, FLAT_W), lambda i: (0, i, 0))],
        out_specs=[blk] * 4,
        out_shape=[jax.ShapeDtypeStruct((R, FLAT_W), F32)] * 4,
        compiler_params=_cparams(("parallel",)),
    )(w, m, v, parts)


def _add2(a, b, name):
    def body(a_ref, b_ref, o_ref):
        o_ref[...] = a_ref[...] + b_ref[...]

    blk = pl.BlockSpec((ADAM_TILE, FLAT_W), lambda i: (i, 0))
    return pl.pallas_call(
        body, name=name, grid=(a.shape[0] // ADAM_TILE,), in_specs=[blk, blk], out_specs=blk,
        out_shape=jax.ShapeDtypeStruct(a.shape, F32), compiler_params=_cparams(("parallel",)),
    )(a, b)


def _adamw_block(w, m, v, g, name):
    shape = w.shape
    R, C = math.prod(shape[:-1]), shape[-1]
    tile = _tile(R, ADAM_TILE, SUBLANE)

    def body(w_ref, m_ref, v_ref, g_ref, d_ref, nm_ref, nv_ref):
        delta, nm, nv = _adamw_math(w_ref[...], g_ref[...], m_ref[...], v_ref[...])
        d_ref[...] = delta
        nm_ref[...] = nm
        nv_ref[...] = nv

    blk = pl.BlockSpec((tile, C), lambda i: (i, 0))
    outs = pl.pallas_call(
        body, name=name, grid=(R // tile,), in_specs=[blk] * 4, out_specs=[blk] * 3,
        out_shape=[jax.ShapeDtypeStruct((R, C), F32)] * 3, compiler_params=_cparams(("parallel",)),
    )(*(a.reshape(R, C) for a in (w, m, v, g)))
    return tuple(a.reshape(shape) for a in outs)


_NAMES = ("norm_mix", "norm_ffn", "ssm_w_in", "ssm_conv_w", "ssm_conv_b", "ssm_dt_bias", "ssm_a_log", "ssm_d",
          "ssm_norm_w", "ssm_w_out", "sb_w_qkv", "sb_q_gain", "sb_k_gain", "sb_w_o", "ffn_w_in", "ffn_w_out")


def _step(x, loss_target, w, m, v):
    cx, cy, cc = lax.axis_index("x"), lax.axis_index("y"), lax.axis_index("c")
    chip = 2 * cx + cy

    conv8 = jnp.pad(w["ssm_conv_w"][0], ((0, SUBLANE - CONV_W), (0, 0)))
    gb, gs = _gather_weights(_pack_shard(w, BF16, SEG_FIRST), conv8)
    W = _prep_vectors(w)
    W["ssm_w_in_t"] = jnp.pad(_whole_of_shards(gb, SEG_FIRST)["ssm_w_in"][0], ((0, D_IN_PAD - D_IN_PROJ), (0, 0)))
    W["conv_w8"] = jnp.concatenate([gs[p] for p in range(N_CHIPS)], axis=1)

    sq, grad_x, grads, (slots_rest, recv_rest) = _local_step(
        x[0], loss_target[0], W, wb_rest=_pack_shard(w, BF16, SEG_REST))
    loss = lax.psum(0.5 * jnp.sum(sq) / D_MODEL, ("x", "y", "c"))

    slots_first = jnp.stack([_shard_of_whole(grads, s, SEG_FIRST) for s in range(N_CHIPS)])
    recv_first = _scatter_grads(slots_first.astype(BF16))
    own = lambda slots: lax.dynamic_index_in_dim(slots, chip, axis=0, keepdims=False)
    part_first = _partial_sum(own(slots_first), recv_first, "grad_partial_sum_first")
    part_rest = _partial_sum(own(slots_rest), recv_rest, "grad_partial_sum_rest")
    sib_first, sib_rest, small_all = _exchange_partials(part_first, part_rest, _pack_small(grads))
    g_blocks = {**_unpack_shard(_add2(part_first, sib_first, "grad_final_sum_first"), SEG_FIRST),
                **_unpack_shard(_add2(part_rest, sib_rest, "grad_final_sum_rest"), SEG_REST)}
    g_s, d_s, m_s, v_s = _adamw(_pack_small(w), _pack_small(m), _pack_small(v), small_all, SUBLANE, "adamw_replicated")

    g, d, nm, nv = (_unpack_small(a) for a in (g_s, d_s, m_s, v_s))
    for n, _, _ in SHARDED:
        g[n] = g_blocks[n]
        d[n], nm[n], nv[n] = _adamw_block(w[n], m[n], v[n], g[n], "adamw_" + n)
    return loss, grad_x[None], [[t[n] for n in _NAMES] for t in (g, d, nm, nv)]


def kernel(x, norm_mix, norm_ffn, ssm_w_in, ssm_conv_w, ssm_conv_b, ssm_dt_bias, ssm_a_log, ssm_d, ssm_norm_w, ssm_w_out, sb_w_qkv, sb_q_gain, sb_k_gain, sb_w_o, ffn_w_in, ffn_w_out, loss_target, m_norm_mix, m_norm_ffn, m_ssm_w_in, m_ssm_conv_w, m_ssm_conv_b, m_ssm_dt_bias, m_ssm_a_log, m_ssm_d, m_ssm_norm_w, m_ssm_w_out, m_sb_w_qkv, m_sb_q_gain, m_sb_k_gain, m_sb_w_o, m_ffn_w_in, m_ffn_w_out, v_norm_mix, v_norm_ffn, v_ssm_w_in, v_ssm_conv_w, v_ssm_conv_b, v_ssm_dt_bias, v_ssm_a_log, v_ssm_d, v_ssm_norm_w, v_ssm_w_out, v_sb_w_qkv, v_sb_q_gain, v_sb_k_gain, v_sb_w_o, v_ffn_w_in, v_ffn_w_out):
    w = dict(zip(_NAMES, (norm_mix, norm_ffn, ssm_w_in, ssm_conv_w, ssm_conv_b, ssm_dt_bias, ssm_a_log, ssm_d,
                          ssm_norm_w, ssm_w_out, sb_w_qkv, sb_q_gain, sb_k_gain, sb_w_o, ffn_w_in, ffn_w_out)))
    m = dict(zip(_NAMES, (m_norm_mix, m_norm_ffn, m_ssm_w_in, m_ssm_conv_w, m_ssm_conv_b, m_ssm_dt_bias, m_ssm_a_log,
                          m_ssm_d, m_ssm_norm_w, m_ssm_w_out, m_sb_w_qkv, m_sb_q_gain, m_sb_k_gain, m_sb_w_o,
                          m_ffn_w_in, m_ffn_w_out)))
    v = dict(zip(_NAMES, (v_norm_mix, v_norm_ffn, v_ssm_w_in, v_ssm_conv_w, v_ssm_conv_b, v_ssm_dt_bias, v_ssm_a_log,
                          v_ssm_d, v_ssm_norm_w, v_ssm_w_out, v_sb_w_qkv, v_sb_q_gain, v_sb_k_gain, v_sb_w_o,
                          v_ffn_w_in, v_ffn_w_out)))
    loss, grad_x, (g, d, nm, nv) = _step(x, loss_target, w, m, v)
    return (loss, grad_x, *g, *d, *nm, *nv)
```

```python
import functools
import math

import jax
import jax.numpy as jnp
from jax import lax
from jax.experimental import pallas as pl
from jax.experimental.pallas import tpu as pltpu

F32 = jnp.float32
BF16 = jnp.bfloat16

VMEM_LIMIT_BYTES = 48 * 1024 * 1024
LANE = 128
SUBLANE = 8

D_MODEL = 1024
CHUNK = 64
D_INNER = 2048
SSM_HEADS = 32
SSM_GROUPS = 8
GROUP_W = D_INNER // SSM_GROUPS
HEAD_P = 64
D_STATE = 128
CONV_W = 4
CONV_DIM = D_INNER + 2 * SSM_GROUPS * D_STATE
D_IN_PROJ = D_INNER + CONV_DIM + SSM_HEADS
DT_PAD = LANE
D_IN_PAD = D_INNER + CONV_DIM + DT_PAD
SB_HEADS = 16
SB_D = 64
D_FF = 2816
NORM_EPS = 1e-6
GATED_NORM_EPS = 1e-5

ADAM_LR = 0.001
ADAM_B1 = 0.9
ADAM_B2 = 0.999
ADAM_EPS = 1e-08
ADAM_WD = 0.01
ADAM_STEP = 10

MESH_T = pl.DeviceIdType.MESH


def _cparams(sem=None):
    return pltpu.CompilerParams(dimension_semantics=sem, vmem_limit_bytes=VMEM_LIMIT_BYTES)


def _tile(n, target, align):
    best = None
    for t in range(align, min(n, target) + 1, align):
        if n % t == 0:
            best = t
    return best or n


def _mm(a, b, mode, out_dtype, name):
    if mode == "nn":
        (M, K), N = a.shape, b.shape[1]
    elif mode == "nt":
        (M, K), N = a.shape, b.shape[0]
    else:
        (K, M), N = a.shape, b.shape[1]
    tm = _tile(M, 1408, LANE)
    tn = _tile(N, 1536, LANE)
    tk = _tile(K, 1536, LANE)
    nk = K // tk
    if mode == "nn":
        a_spec = pl.BlockSpec((tm, tk), lambda i, j, k: (i, k))
        b_spec = pl.BlockSpec((tk, tn), lambda i, j, k: (k, j))
        dims = (((1,), (0,)), ((), ()))
    elif mode == "nt":
        a_spec = pl.BlockSpec((tm, tk), lambda i, j, k: (i, k))
        b_spec = pl.BlockSpec((tn, tk), lambda i, j, k: (j, k))
        dims = (((1,), (1,)), ((), ()))
    else:
        a_spec = pl.BlockSpec((tk, tm), lambda i, j, k: (k, i))
        b_spec = pl.BlockSpec((tk, tn), lambda i, j, k: (k, j))
        dims = (((0,), (0,)), ((), ()))

    def body(a_ref, b_ref, o_ref, acc_ref):
        k = pl.program_id(2)

        @pl.when(k == 0)
        def _():
            acc_ref[...] = jnp.zeros_like(acc_ref)

        acc_ref[...] += lax.dot_general(a_ref[...], b_ref[...], dims, preferred_element_type=F32)

        @pl.when(k == nk - 1)
        def _():
            o_ref[...] = acc_ref[...].astype(o_ref.dtype)

    return pl.pallas_call(
        body,
        name=name,
        grid=(M // tm, N // tn, nk),
        in_specs=[a_spec, b_spec],
        out_specs=pl.BlockSpec((tm, tn), lambda i, j, k: (i, j)),
        out_shape=jax.ShapeDtypeStruct((M, N), out_dtype),
        scratch_shapes=[pltpu.VMEM((tm, tn), F32)],
        compiler_params=_cparams(("parallel", "parallel", "arbitrary")),
    )(a, b)


def _rows(fn, name, tile, row_ins, const_ins, row_outs, acc_outs=(), ncol=1):
    L = row_ins[0][0].shape[0]
    tile = min(tile, L)
    nrow = L // tile
    n_ri, n_ci, n_ro, n_ao = len(row_ins), len(const_ins), len(row_outs), len(acc_outs)

    def body(*refs):
        i = pl.program_id(1)
        j = pl.program_id(0)
        ins = [r[...] for r in refs[: n_ri + n_ci]]
        outs = fn(*ins, j=j)
        o_refs = refs[n_ri + n_ci:]
        for r, v in zip(o_refs[:n_ro], outs[:n_ro]):
            r[...] = v.astype(r.dtype)

        @pl.when(i == 0)
        def _():
            for r in o_refs[n_ro:]:
                r[...] = jnp.zeros_like(r)

        for r, v in zip(o_refs[n_ro:], outs[n_ro:]):
            r[...] += v

    def rspec(bc, cf):
        return pl.BlockSpec((tile, bc), lambda j, i: (i, cf(j)))

    def cspec(r, bc, cf):
        return pl.BlockSpec((r, bc), lambda j, i: (0, cf(j)))

    in_specs = [rspec(bc, cf) for (_, bc, cf) in row_ins]
    in_specs += [cspec(a.shape[0], bc, cf) for (a, bc, cf) in const_ins]
    out_specs = [rspec(bc, cf) for (_, _, bc, cf) in row_outs]
    out_specs += [cspec(1, bc, cf) for (_, bc, cf) in acc_outs]
    out_shape = [jax.ShapeDtypeStruct((L, c), dt) for (c, dt, _, _) in row_outs]
    out_shape += [jax.ShapeDtypeStruct((1, c), F32) for (c, _, _) in acc_outs]
    res = pl.pallas_call(
        body,
        name=name,
        grid=(ncol, nrow),
        in_specs=in_specs,
        out_specs=out_specs,
        out_shape=out_shape,
        compiler_params=_cparams(("arbitrary", "arbitrary")),
    )(*[a for (a, _, _) in row_ins], *[a for (a, _, _) in const_ins])
    return res


def _zero(j):
    return 0


def _whole(a):
    return (a, a.shape[1], _zero)


def _rms(x, w, eps):
    return x * lax.rsqrt(jnp.mean(x * x, axis=-1, keepdims=True) + eps) * w


ROW_TILE = 256
COL_ROW_TILE = 1024
QK_ROW_TILE = 8192


def _norm_fwd(x, r, w, name):
    C = x.shape[1]
    if r is None:
        def fn(x_, w_, j):
            return (_rms(x_, w_, NORM_EPS),)
        (h,) = _rows(fn, name, ROW_TILE, [_whole(x)], [_whole(w)], [(C, BF16, C, _zero)])
        return x, h

    def fn(x_, r_, w_, j):
        x1 = x_ + r_
        return x1, _rms(x1, w_, NORM_EPS)

    x1, h = _rows(fn, name, ROW_TILE, [_whole(x), _whole(r)], [_whole(w)],
                  [(C, F32, C, _zero), (C, BF16, C, _zero)])
    return x1, h


def _norm_bwd(x, w, dh, dres, name):
    C = x.shape[1]

    def fn(x_, dh_, dres_, w_, j):
        _, vjp = jax.vjp(lambda a, b: _rms(a, b, NORM_EPS), x_, w_)
        dx, dw = vjp(dh_.astype(F32))
        return dres_ + dx, dw

    return _rows(fn, name, ROW_TILE, [_whole(x), _whole(dh), _whole(dres)], [_whole(w)],
                 [(C, F32, C, _zero)], [(C, C, _zero)])


def _silu(x):
    return x * jax.nn.sigmoid(x)


FF_BLK = 256
FF_NB = D_FF // FF_BLK


def _swiglu_fwd(gu, name):
    def fn(g, u, j):
        return (_silu(g) * u,)
    (act,) = _rows(fn, name, COL_ROW_TILE,
                   [(gu, FF_BLK, lambda j: j), (gu, FF_BLK, lambda j: j + FF_NB)], [],
                   [(D_FF, BF16, FF_BLK, lambda j: j)], ncol=FF_NB)
    return act


def _swiglu_bwd(gu, dact, name):
    def fn(g, u, da, j):
        _, vjp = jax.vjp(lambda a, b: _silu(a) * b, g, u)
        dg, du = vjp(da.astype(F32))
        return (jnp.where(j < FF_NB, dg, du),)
    (dgu,) = _rows(fn, name, COL_ROW_TILE,
                   [(gu, FF_BLK, lambda j: j % FF_NB), (gu, FF_BLK, lambda j: j % FF_NB + FF_NB),
                    (dact, FF_BLK, lambda j: j % FF_NB)], [],
                   [(2 * D_FF, BF16, FF_BLK, lambda j: j)], ncol=2 * FF_NB)
    return dgu


def _gated_norm_fwd(y, zx, w, name):
    def fn(y_, z_, w_, j):
        return (_rms(y_ * _silu(z_), w_, GATED_NORM_EPS),)
    (yn,) = _rows(fn, name, COL_ROW_TILE,
                  [(y, GROUP_W, lambda j: j), (zx, GROUP_W, lambda j: j)], [(w, GROUP_W, lambda j: j)],
                  [(D_INNER, BF16, GROUP_W, lambda j: j)], ncol=SSM_GROUPS)
    return yn


def _gated_norm_bwd(y, zx, w, dyn, name):
    def fn(y_, z_, dyn_, w_, j):
        _, vjp = jax.vjp(lambda a, b, c: _rms(a * _silu(b), c, GATED_NORM_EPS), y_, z_, w_)
        return vjp(dyn_.astype(F32))
    cj = lambda j: j
    return _rows(fn, name, COL_ROW_TILE,
                 [(y, GROUP_W, cj), (zx, GROUP_W, cj), (dyn, GROUP_W, cj)], [(w, GROUP_W, cj)],
                 [(D_INNER, F32, GROUP_W, cj), (D_INNER, BF16, GROUP_W, cj)], [(D_INNER, GROUP_W, cj)],
                 ncol=SSM_GROUPS)


def _loss_bwd(x3, f, target, name):
    C = x3.shape[1]

    def fn(x_, f_, t_, j):
        err = (x_ + f_) - t_
        return err * (1.0 / C), jnp.sum(err * err, axis=0, keepdims=True)

    return _rows(fn, name, ROW_TILE, [_whole(x3), _whole(f), _whole(target)], [],
                 [(C, F32, C, _zero)], [(C, C, _zero)])


def _qk_norm_fwd(q, gain, scale, name):
    def fn(q_, g_, j):
        return (_rms(q_, g_, NORM_EPS) * scale,)
    (qn,) = _rows(fn, name, QK_ROW_TILE, [_whole(q)], [_whole(gain)], [(SB_D, BF16, SB_D, _zero)])
    return qn


def _qk_norm_bwd(q, gain, dqn, scale, cot_scale, name):
    def fn(q_, dqn_, g_, j):
        _, vjp = jax.vjp(lambda a, b: _rms(a, b, NORM_EPS) * scale, q_, g_)
        return vjp(dqn_ * cot_scale)
    return _rows(fn, name, QK_ROW_TILE, [_whole(q), _whole(dqn)], [_whole(gain)],
                 [(SB_D, F32, SB_D, _zero)], [(SB_D, SB_D, _zero)])


CONV_TILE = 512
CONV_BLK = 512
XBC_COL0 = D_INNER // CONV_BLK


def _shift_down(cur, prev8, k):
    if k == 0:
        return cur
    rolled = pltpu.roll(cur, k, 0)
    head_prev = pltpu.roll(prev8, k, 0)
    rid = lax.broadcasted_iota(jnp.int32, (SUBLANE, cur.shape[1]), 0)
    head = jnp.where(rid < k, head_prev, rolled[:SUBLANE])
    if cur.shape[0] == SUBLANE:
        return head
    return jnp.concatenate([head, rolled[SUBLANE:]], axis=0)


def _shift_up(cur, next8, k):
    if k == 0:
        return cur
    T = cur.shape[0]
    rolled = pltpu.roll(cur, T - k, 0)
    tail_next = pltpu.roll(next8, SUBLANE - k, 0)
    rid = lax.broadcasted_iota(jnp.int32, (SUBLANE, cur.shape[1]), 0)
    tail = jnp.where(rid >= SUBLANE - k, tail_next, rolled[T - SUBLANE:])
    return jnp.concatenate([rolled[: T - SUBLANE], tail], axis=0)


def _conv_pre(cur, prev8, w, b):
    pre = b
    for i in range(CONV_W):
        pre = pre + w[i:i + 1, :] * _shift_down(cur, prev8, CONV_W - 1 - i)
    return pre


def _conv_fwd(zx, w8, b, name):
    L = zx.shape[0]
    nrow = L // CONV_TILE
    r8 = CONV_TILE // SUBLANE

    def body(cur_ref, prev_ref, w_ref, b_ref, o_ref):
        i = pl.program_id(1)
        prev8 = jnp.where(i > 0, prev_ref[...], 0.0)
        pre = _conv_pre(cur_ref[...], prev8, w_ref[...], b_ref[...])
        o_ref[...] = _silu(pre)

    return pl.pallas_call(
        body,
        name=name,
        grid=(CONV_DIM // CONV_BLK, nrow),
        in_specs=[
            pl.BlockSpec((CONV_TILE, CONV_BLK), lambda j, i: (i, j + XBC_COL0)),
            pl.BlockSpec((SUBLANE, CONV_BLK), lambda j, i: (jnp.maximum(i * r8 - 1, 0), j + XBC_COL0)),
            pl.BlockSpec((SUBLANE, CONV_BLK), lambda j, i: (0, j)),
            pl.BlockSpec((1, CONV_BLK), lambda j, i: (0, j)),
        ],
        out_specs=pl.BlockSpec((CONV_TILE, CONV_BLK), lambda j, i: (i, j)),
        out_shape=jax.ShapeDtypeStruct((L, CONV_DIM), F32),
        compiler_params=_cparams(("arbitrary", "arbitrary")),
    )(zx, zx, w8, b)


def _conv_bwd(zx, w8, b, dact, name):
    L = zx.shape[0]
    nrow = L // CONV_TILE
    r8 = CONV_TILE // SUBLANE
    last8 = L // SUBLANE - 1

    def body(cur_ref, prev_ref, next_ref, da_ref, dan_ref, w_ref, b_ref, du_ref, dw_ref, db_ref):
        i = pl.program_id(1)
        w = w_ref[...]
        b_ = b_ref[...]
        cur = cur_ref[...]
        prev8 = jnp.where(i > 0, prev_ref[...], 0.0)
        pre = _conv_pre(cur, prev8, w, b_)
        _, vjp = jax.vjp(_silu, pre)
        (dpre,) = vjp(da_ref[...])
        nxt = next_ref[...]
        pre_n = _conv_pre(nxt, cur[CONV_TILE - SUBLANE:], w, b_)
        _, vjp_n = jax.vjp(_silu, pre_n)
        (dpre_n,) = vjp_n(dan_ref[...])
        dpre_n = jnp.where(i < nrow - 1, dpre_n, 0.0)
        du = jnp.zeros_like(cur)
        dws = []
        for k in range(CONV_W):
            wk = w[CONV_W - 1 - k:CONV_W - k, :]
            du = du + wk * _shift_up(dpre, dpre_n, k)
            dws.append(jnp.sum(dpre * _shift_down(cur, prev8, k), axis=0, keepdims=True))
        du_ref[...] = du.astype(du_ref.dtype)
        dw_tile = jnp.concatenate([dws[3], dws[2], dws[1], dws[0]] + [jnp.zeros_like(dws[0])] * 4, axis=0)

        @pl.when(i == 0)
        def _():
            dw_ref[...] = jnp.zeros_like(dw_ref)
            db_ref[...] = jnp.zeros_like(db_ref)

        dw_ref[...] += dw_tile
        db_ref[...] += jnp.sum(dpre, axis=0, keepdims=True)

    return pl.pallas_call(
        body,
        name=name,
        grid=(CONV_DIM // CONV_BLK, nrow),
        in_specs=[
            pl.BlockSpec((CONV_TILE, CONV_BLK), lambda j, i: (i, j + XBC_COL0)),
            pl.BlockSpec((SUBLANE, CONV_BLK), lambda j, i: (jnp.maximum(i * r8 - 1, 0), j + XBC_COL0)),
            pl.BlockSpec((SUBLANE, CONV_BLK), lambda j, i: (jnp.minimum((i + 1) * r8, last8), j + XBC_COL0)),
            pl.BlockSpec((CONV_TILE, CONV_BLK), lambda j, i: (i, j)),
            pl.BlockSpec((SUBLANE, CONV_BLK), lambda j, i: (jnp.minimum((i + 1) * r8, last8), j)),
            pl.BlockSpec((SUBLANE, CONV_BLK), lambda j, i: (0, j)),
            pl.BlockSpec((1, CONV_BLK), lambda j, i: (0, j)),
        ],
        out_specs=[
            pl.BlockSpec((CONV_TILE, CONV_BLK), lambda j, i: (i, j)),
            pl.BlockSpec((SUBLANE, CONV_BLK), lambda j, i: (0, j)),
            pl.BlockSpec((1, CONV_BLK), lambda j, i: (0, j)),
        ],
        out_shape=[
            jax.ShapeDtypeStruct((L, CONV_DIM), BF16),
            jax.ShapeDtypeStruct((SUBLANE, CONV_DIM), F32),
            jax.ShapeDtypeStruct((1, CONV_DIM), F32),
        ],
        compiler_params=_cparams(("arbitrary", "arbitrary")),
    )(zx, zx, zx, dact, dact, w8, b)


HI = lax.Precision.HIGHEST
XS_COL0 = 0
B_COL0 = D_INNER // D_STATE
C_COL0 = B_COL0 + SSM_GROUPS
DT_COL = (D_INNER + CONV_DIM) // DT_PAD


def _dot(a, b, dims=(((1,), (0,)), ((), ())), precision=None):
    return lax.dot_general(a, b, dims, precision=precision, preferred_element_type=F32)


_DOT_DIMS = {
    "nn": (((1,), (0,)), ((), ())),
    "nt": (((1,), (1,)), ((), ())),
    "tn": (((0,), (0,)), ((), ())),
}


@functools.partial(jax.custom_vjp, nondiff_argnums=(2,))
def _bdot(a, b, mode):
    return _dot(a.astype(BF16), b.astype(BF16), _DOT_DIMS[mode])


def _bdot_fwd(a, b, mode):
    return _bdot(a, b, mode), (a, b)


def _bdot_bwd(mode, res, g):
    a, b = res
    if mode == "nn":
        return _bdot(g, b, "nt"), _bdot(a, g, "tn")
    if mode == "nt":
        return _bdot(g, b, "nn"), _bdot(g, a, "tn")
    return _bdot(b, g, "nt"), _bdot(a, g, "nn")


_bdot.defvjp(_bdot_fwd, _bdot_bwd)


def _softplus(x):
    return jnp.maximum(x, 0.0) + jnp.log(1.0 + jnp.exp(-jnp.abs(x)))


def _split2(x):
    hi = x.astype(BF16)
    return hi, (x - hi.astype(F32)).astype(BF16)


@jax.custom_vjp
def _sel_left(m, mT, x):
    hi, lo = _split2(x)
    out = _dot(m, jnp.concatenate([hi, lo], axis=1))
    n = x.shape[1]
    return out[:, :n] + out[:, n:]


_sel_left.defvjp(lambda m, mT, x: (_sel_left(m, mT, x), (m, mT)),
                 lambda res, g: (jnp.zeros_like(res[0]), jnp.zeros_like(res[1]), _sel_left(res[1], res[0], g)))


@jax.custom_vjp
def _sel_right(x, m, mT):
    hi, lo = _split2(x)
    out = _dot(jnp.concatenate([hi, lo], axis=0), m)
    r = x.shape[0]
    return out[:r] + out[r:]


_sel_right.defvjp(lambda x, m, mT: (_sel_right(x, m, mT), (m, mT)),
                  lambda res, g: (_sel_right(g, res[1], res[0]), jnp.zeros_like(res[0]), jnp.zeros_like(res[1])))


def _ssd_consts():
    l_ = lax.broadcasted_iota(jnp.int32, (CHUNK, GROUP_W), 0)
    c_ = lax.broadcasted_iota(jnp.int32, (CHUNK, GROUP_W), 1)
    s_ = c_ % CHUNK
    causal = s_ <= l_
    eye_t = (s_ == l_).astype(F32)
    r0 = lax.broadcasted_iota(jnp.int32, (CHUNK, CHUNK), 0)
    c0 = lax.broadcasted_iota(jnp.int32, (CHUNK, CHUNK), 1)
    tril = (c0 <= r0).astype(BF16)
    triu = (r0 <= c0).astype(BF16)
    rb = lax.broadcasted_iota(jnp.int32, (GROUP_W, GROUP_W), 0) // HEAD_P
    cb = lax.broadcasted_iota(jnp.int32, (GROUP_W, GROUP_W), 1) // HEAD_P
    blockdiag = rb == cb
    return causal, eye_t, tril, triu, blockdiag


def _ssd_expand(g):
    h = lax.broadcasted_iota(jnp.int32, (DT_PAD, GROUP_W), 0)
    c = lax.broadcasted_iota(jnp.int32, (DT_PAD, GROUP_W), 1)
    cT = lax.broadcasted_iota(jnp.int32, (GROUP_W, DT_PAD), 0)
    hT = lax.broadcasted_iota(jnp.int32, (GROUP_W, DT_PAD), 1)
    hpg = GROUP_W // HEAD_P
    return (h == g * hpg + c // HEAD_P).astype(BF16), (hT == g * hpg + cT // HEAD_P).astype(BF16)


def _ssd_chunk(S, xs, dt_raw, dt_bias, alog_e, d_e, Bm, Cm, E):
    causal, eye_t, tril, triu, blockdiag = _ssd_consts()
    ones = jnp.ones((CHUNK, CHUNK), BF16)
    dt = _softplus(dt_raw + dt_bias)
    dtx = _sel_right(dt, E[0], E[1])
    a = dtx * (-jnp.exp(alog_e))
    acs = _sel_left(tril, triu, a)
    rowv = _sel_left(ones, ones, acs * eye_t)
    seg = acs - rowv
    Lc = jnp.where(causal, jnp.exp(jnp.where(causal, seg, 0.0)), 0.0)
    xdt = xs * dtx
    Bt = jnp.concatenate([Bm] * 4, axis=0)
    CBc = _bdot(Cm, Bt, "nt")
    Xbd = jnp.where(blockdiag, jnp.concatenate([xdt] * 4, axis=0), 0.0)
    y_intra = _bdot(CBc * Lc, Xbd, "nn")
    y_inter = _bdot(Cm, S, "nn") * jnp.exp(acs)
    y = y_intra + y_inter + d_e * xs
    last = jnp.sum(a, axis=0, keepdims=True)
    dec_end = jnp.exp(last - acs)
    S_new = S * jnp.exp(last) + _bdot(Bm, xdt * dec_end, "tn")
    return S_new, y


SSD_GPS = 8
SSD_W = SSD_GPS * GROUP_W
SSD_N = SSD_GPS * D_STATE
SSD_STEPS = SSM_GROUPS // SSD_GPS


def _ssd_in_specs(cmap):
    return [
        pl.BlockSpec((CHUNK, SSD_W), lambda c, g: (cmap(c), g)),
        pl.BlockSpec((CHUNK, DT_PAD), lambda c, g: (cmap(c), DT_COL)),
        pl.BlockSpec((1, DT_PAD), lambda c, g: (0, 0)),
        pl.BlockSpec((1, SSD_W), lambda c, g: (0, g)),
        pl.BlockSpec((1, SSD_W), lambda c, g: (0, g)),
        pl.BlockSpec((CHUNK, SSD_N), lambda c, g: (cmap(c), B_COL0 // SSD_GPS + g)),
        pl.BlockSpec((CHUNK, SSD_N), lambda c, g: (cmap(c), C_COL0 // SSD_GPS + g)),
    ]


def _gw(u):
    return slice(u * GROUP_W, (u + 1) * GROUP_W)


def _gn(u):
    return slice(u * D_STATE, (u + 1) * D_STATE)


def _ssd_fwd(xbc, zx, dt_bias, alog_e, d_e, name, travel=None):
    L = xbc.shape[0]
    nc = L // CHUNK
    n_in = 7 + (travel is not None)

    def body(*refs):
        xs_ref, dtr_ref, bias_ref, alog_ref, d_ref, b_ref, c_ref = refs[:7]
        y_ref, st_ref = refs[n_in:n_in + 2]
        S_ref = refs[n_in + 2 + (travel is not None)]
        c = pl.program_id(0)
        step = pl.program_id(1)
        g0 = step * SSD_GPS if SSD_STEPS > 1 else 0
        if travel is not None:
            moves = lambda: _gather_moves(refs[7], refs[n_in + 2], *refs[n_in + 4:])
            pl.when(jnp.logical_and(c == 0, step == 0))(lambda: moves().start())

        @pl.when(c == 0)
        def _():
            for u in range(SSD_GPS):
                S_ref[g0 + u] = jnp.zeros((D_STATE, GROUP_W), F32)

        dtr, bias = dtr_ref[...], bias_ref[...]
        for u in range(SSD_GPS):
            S = S_ref[g0 + u]
            st_ref[u] = S
            S_new, y = _ssd_chunk(S, xs_ref[:, _gw(u)], dtr, bias, alog_ref[:, _gw(u)], d_ref[:, _gw(u)],
                                  b_ref[:, _gn(u)], c_ref[:, _gn(u)], _ssd_expand(g0 + u))
            y_ref[:, _gw(u)] = y
            S_ref[g0 + u] = S_new

        if travel is not None:
            pl.when(jnp.logical_and(c == nc - 1, step == SSD_STEPS - 1))(lambda: moves().wait())

    extra = travel is not None
    return pl.pallas_call(
        body,
        name=name,
        grid=(nc, SSD_STEPS),
        in_specs=_ssd_in_specs(lambda c: c) + [ANY] * extra,
        out_specs=[
            pl.BlockSpec((CHUNK, SSD_W), lambda c, g: (c, g)),
            pl.BlockSpec((None, SSD_GPS, D_STATE, GROUP_W), lambda c, g: (c, g, 0, 0)),
        ] + [ANY] * extra,
        out_shape=[
            jax.ShapeDtypeStruct((L, D_INNER), F32),
            jax.ShapeDtypeStruct((nc, SSM_GROUPS, D_STATE, GROUP_W), F32),
        ] + ([jax.ShapeDtypeStruct((N_CHIPS,) + travel.shape, travel.dtype)] if extra else []),
        scratch_shapes=[pltpu.VMEM((SSM_GROUPS, D_STATE, GROUP_W), F32)]
        + ((_MOVE_SEMS + [pltpu.SemaphoreType.DMA]) if extra else []),
        compiler_params=_cparams(("arbitrary", "arbitrary")),
    )(xbc, zx, dt_bias, alog_e, d_e, xbc, xbc, *([travel] if extra else []))


def _ssd_bwd(xbc, zx, dt_bias, alog_e, d_e, states, dy, name, travel=None):
    assert SSD_STEPS == 1, "one grid step writes the whole d (xs | B | C) row block"
    L = xbc.shape[0]
    nc = L // CHUNK
    rev = lambda c: nc - 1 - c
    n_in = 9 + (travel is not None)

    def body(*refs):
        xs_ref, dtr_ref, bias_ref, alog_ref, d_ref, b_ref, c_ref, st_ref, dy_ref = refs[:9]
        dact_ref, ddt_ref, dbias_ref, dalog_ref, dd_ref = refs[n_in:n_in + 5]
        dS_ref = refs[n_in + 5 + (travel is not None)]
        c = pl.program_id(0)
        step = pl.program_id(1)
        g0 = 0
        first = jnp.logical_and(c == 0, step == 0)
        if travel is not None:
            moves = lambda: _scatter_moves(refs[9], refs[n_in + 5], *refs[n_in + 7:])
            pl.when(first)(lambda: moves().start())

        @pl.when(c == 0)
        def _():
            for u in range(SSD_GPS):
                dS_ref[g0 + u] = jnp.zeros((D_STATE, GROUP_W), F32)
                dalog_ref[g0 + u] = jnp.zeros((1, GROUP_W), F32)
                dd_ref[g0 + u] = jnp.zeros((1, GROUP_W), F32)

        @pl.when(first)
        def _():
            dbias_ref[...] = jnp.zeros_like(dbias_ref)

        dtr, bias = dtr_ref[...], bias_ref[...]
        ddt_sum = jnp.zeros((CHUNK, DT_PAD), F32)
        dbias_sum = jnp.zeros((1, DT_PAD), F32)
        for u in range(SSD_GPS):
            E = _ssd_expand(g0 + u)
            _, vjp = jax.vjp(
                lambda S, xs, dtr_, bias_, alog, dsk, Bm, Cm: _ssd_chunk(S, xs, dtr_, bias_, alog, dsk, Bm, Cm, E),
                st_ref[u], xs_ref[:, _gw(u)], dtr, bias, alog_ref[:, _gw(u)], d_ref[:, _gw(u)],
                b_ref[:, _gn(u)], c_ref[:, _gn(u)])
            dS, dxs, ddtr, dbias, dalog, dd, dB, dC = vjp((dS_ref[g0 + u], dy_ref[:, _gw(u)]))
            dS_ref[g0 + u] = dS
            dact_ref[:, _gw(u)] = dxs
            dact_ref[:, slice(D_INNER + u * D_STATE, D_INNER + (u + 1) * D_STATE)] = dB
            dact_ref[:, slice(D_INNER + SSD_N + u * D_STATE, D_INNER + SSD_N + (u + 1) * D_STATE)] = dC
            dalog_ref[g0 + u] += dalog
            dd_ref[g0 + u] += dd
            ddt_sum = ddt_sum + ddtr
            dbias_sum = dbias_sum + dbias

        ddt_ref[...] = ddt_sum
        dbias_ref[...] += dbias_sum
        if travel is not None:
            pl.when(jnp.logical_and(c == nc - 1, step == SSD_STEPS - 1))(lambda: moves().wait())

    extra = travel is not None
    whole3 = pl.BlockSpec((SSM_GROUPS, 1, GROUP_W), lambda c, g: (0, 0, 0))
    return pl.pallas_call(
        body,
        name=name,
        grid=(nc, SSD_STEPS),
        in_specs=_ssd_in_specs(rev) + [
            pl.BlockSpec((None, SSD_GPS, D_STATE, GROUP_W), lambda c, g: (rev(c), g, 0, 0)),
            pl.BlockSpec((CHUNK, SSD_W), lambda c, g: (rev(c), g)),
        ] + [ANY] * extra,
        out_specs=[
            pl.BlockSpec((CHUNK, CONV_DIM), lambda c, g: (rev(c), 0)),
            pl.BlockSpec((CHUNK, DT_PAD), lambda c, g: (rev(c), 0)),
            pl.BlockSpec((1, DT_PAD), lambda c, g: (0, 0)),
            whole3,
            whole3,
        ] + [ANY] * extra,
        out_shape=[
            jax.ShapeDtypeStruct((L, CONV_DIM), F32),
            jax.ShapeDtypeStruct((L, DT_PAD), F32),
            jax.ShapeDtypeStruct((1, DT_PAD), F32),
            jax.ShapeDtypeStruct((SSM_GROUPS, 1, GROUP_W), F32),
            jax.ShapeDtypeStruct((SSM_GROUPS, 1, GROUP_W), F32),
        ] + ([jax.ShapeDtypeStruct((N_CHIPS - 1,) + travel.shape[1:], travel.dtype)] if extra else []),
        scratch_shapes=[pltpu.VMEM((SSM_GROUPS, D_STATE, GROUP_W), F32)] + (_MOVE_SEMS if extra else []),
        compiler_params=_cparams(("arbitrary", "arbitrary")),
    )(xbc, zx, dt_bias, alog_e, d_e, xbc, xbc, states, dy, *([travel] if extra else []))


LOG2E = math.log2(math.e)
LN2 = math.log(2.0)
SB_Q_SCALE = SB_D ** -0.5 * LOG2E


def _sb_scores(q, kj):
    z2 = _dot(q, kj, dims=(((1,), (1,)), ((), ())))
    lb = jnp.minimum(z2, 0.0) - jnp.log2(1.0 + jnp.exp2(-jnp.abs(z2)))
    return lb, lb - z2


SB_TQ_FWD = 512
SB_TQ_BWD = 256
SB_TK = 512
SB_TC = SB_TK // 2


def _sb_after():
    j = lax.broadcasted_iota(jnp.int32, (2 * SB_TC, SB_TC), 0) % SB_TC
    s = lax.broadcasted_iota(jnp.int32, (2 * SB_TC, SB_TC), 1)
    return (j > s).astype(BF16)


def _sb_suffix(x, after):
    T = x.shape[0]
    hi = x.astype(BF16)
    lo = (x - hi.astype(F32)).astype(BF16)
    stacked = jnp.concatenate([jnp.concatenate([hi[:, :SB_TC], lo[:, :SB_TC]], axis=1),
                               jnp.concatenate([hi[:, SB_TC:], lo[:, SB_TC:]], axis=1)], axis=0)
    out = _dot(stacked, after)
    rs_right = jnp.sum(x[:, SB_TC:], axis=1, keepdims=True)
    return (jnp.concatenate([out[:T] + rs_right, out[T:]], axis=1),
            rs_right + jnp.sum(x[:, :SB_TC], axis=1, keepdims=True))


SB_ROWS_FWD = 256
SB_ROWS_BWD = 128


def _sb_seen(rows, row0, start):
    t = lax.broadcasted_iota(jnp.int32, (rows, SB_TK), 0) + row0
    s = lax.broadcasted_iota(jnp.int32, (rows, SB_TK), 1) + start
    return s < t


def _sb_walk(gd, groups, carry):
    carry = lax.cond(gd % 2 == 1, lambda c: groups([gd, gd - 1], c, True), lambda c: groups([gd], c, True), carry)
    base = gd - 1 - gd % 2
    return lax.fori_loop(0, gd // 2, lambda n, c: groups([base - 2 * n, base - 2 * n - 1], c, False), carry)


def _sb_fwd(qs, kn, v, name):
    H, L, d = qs.shape
    SB_TQ, SB_ROWS = SB_TQ_FWD, SB_ROWS_FWD
    nq = L // SB_TQ
    after = _sb_after()

    def body(q_ref, k_ref, v_ref, aft_ref, o_ref):
        i = pl.program_id(1)
        gd = (i * SB_TQ) // SB_TK
        subs = range(SB_TQ // SB_ROWS)
        qr = [q_ref[r * SB_ROWS:(r + 1) * SB_ROWS, :] for r in subs]

        def prep(G, diag, r):
            start = pl.multiple_of(G * SB_TK, SB_TK)
            lb, lk = _sb_scores(qr[r], k_ref[pl.ds(start, SB_TK), :])
            seen = _sb_seen(SB_ROWS, i * SB_TQ + r * SB_ROWS, start) if diag else None
            if diag:
                lk = jnp.where(seen, lk, 0.0)
            between, rs = _sb_suffix(lk, aft_ref[...])
            return lb + between, rs, seen, v_ref[pl.ds(start, SB_TK), :]

        def groups(Gs, carry, diag):
            carry = list(carry)
            pre = [[prep(G, diag and n == 0, r) for n, G in enumerate(Gs)] for r in subs]
            for r in subs:
                R, acc = carry[r]
                for s, rs, seen, vg in pre[r]:
                    w = jnp.exp2(s + R)
                    if seen is not None:
                        w = jnp.where(seen, w, 0.0)
                    acc = acc + _dot(w.astype(BF16), vg)
                    R = R + rs
                carry[r] = (R, acc)
            return tuple(carry)

        zero = (jnp.zeros((SB_ROWS, 1), F32), jnp.zeros((SB_ROWS, d), F32))
        out = _sb_walk(gd, groups, tuple(zero for _ in subs))
        for r in subs:
            o_ref[r * SB_ROWS:(r + 1) * SB_ROWS, :] = out[r][1]

    return pl.pallas_call(
        body,
        name=name,
        grid=(H, nq),
        in_specs=[
            pl.BlockSpec((None, SB_TQ, d), lambda h, i: (h, i, 0)),
            pl.BlockSpec((None, L, d), lambda h, i: (h, 0, 0)),
            pl.BlockSpec((None, L, d), lambda h, i: (h, 0, 0)),
            pl.BlockSpec((2 * SB_TC, SB_TC), lambda h, i: (0, 0)),
        ],
        out_specs=pl.BlockSpec((None, SB_TQ, d), lambda h, i: (h, i, 0)),
        out_shape=jax.ShapeDtypeStruct((H, L, d), F32),
        compiler_params=_cparams(("parallel", "arbitrary")),
    )(qs, kn, v, after)


def _sb_bwd(qs, kn, v, o, do, name):
    H, L, d = qs.shape
    SB_TQ, SB_ROWS = SB_TQ_BWD, SB_ROWS_BWD
    nq = L // SB_TQ
    after = _sb_after()
    qsT = qs.transpose(0, 2, 1)
    do16 = do.astype(BF16)
    doT = do16.transpose(0, 2, 1)

    def body(q_ref, qT_ref, k_ref, v_ref, o_ref, do_ref, doT_ref, aft_ref, dq_ref, dkT_ref, dvT_ref):
        i = pl.program_id(1)

        @pl.when(i == 0)
        def _():
            dkT_ref[...] = jnp.zeros_like(dkT_ref)
            dvT_ref[...] = jnp.zeros_like(dvT_ref)

        gd = (i * SB_TQ) // SB_TK
        subs = range(SB_TQ // SB_ROWS)
        rows = [slice(r * SB_ROWS, (r + 1) * SB_ROWS) for r in subs]
        qr = [q_ref[rows[r], :] for r in subs]
        qT = [qT_ref[:, rows[r]] for r in subs]
        do_ = [do_ref[rows[r], :] for r in subs]
        doT_ = [doT_ref[:, rows[r]] for r in subs]
        D = [jnp.sum(o_ref[rows[r], :] * do_[r].astype(F32), axis=1, keepdims=True) for r in subs]

        def prep(G, diag, r):
            start = pl.multiple_of(G * SB_TK, SB_TK)
            kg = k_ref[pl.ds(start, SB_TK), :]
            lb, lk = _sb_scores(qr[r], kg)
            seen = _sb_seen(SB_ROWS, i * SB_TQ + r * SB_ROWS, start) if diag else None
            if diag:
                lk = jnp.where(seen, lk, 0.0)
            between, rs = _sb_suffix(lk, aft_ref[...])
            dw = _dot(do_[r], v_ref[pl.ds(start, SB_TK), :], dims=(((1,), (1,)), ((), ())))
            return start, kg, lb + between, rs, dw, jnp.exp2(lb), seen

        def groups(Gs_, carry, diag):
            carry = list(carry)
            pre = [[prep(G, diag and n == 0, r) for n, G in enumerate(Gs_)] for r in subs]
            for r in subs:
                R, Gs, dq = carry[r]
                for start, kg, s, rs, dw, sig, seen in pre[r]:
                    w = jnp.exp2(s + R)
                    if seen is not None:
                        w = jnp.where(seen, w, 0.0)
                    w16 = w.astype(BF16)
                    g = w16.astype(F32) * dw
                    dvT_ref[:, pl.ds(start, SB_TK)] += _dot(doT_[r], w16)
                    g_right, gsum = _sb_suffix(g, aft_ref[...])
                    dz = g - sig * (D[r] - Gs - g_right)
                    if seen is not None:
                        dz = jnp.where(seen, dz, 0.0)
                    dz16 = dz.astype(BF16)
                    dq = dq + _dot(dz16, kg)
                    dkT_ref[:, pl.ds(start, SB_TK)] += _dot(qT[r], dz16)
                    R = R + rs
                    Gs = Gs + gsum
                carry[r] = (R, Gs, dq)
            return tuple(carry)

        zero1 = jnp.zeros((SB_ROWS, 1), F32)
        out = _sb_walk(gd, groups, tuple((zero1, zero1, jnp.zeros((SB_ROWS, d), F32)) for _ in subs))
        for r in subs:
            dq_ref[rows[r], :] = out[r][2]

    blk = pl.BlockSpec((None, SB_TQ, d), lambda h, i: (h, i, 0))
    blkT = pl.BlockSpec((None, d, SB_TQ), lambda h, i: (h, 0, i))
    full = pl.BlockSpec((None, L, d), lambda h, i: (h, 0, 0))
    fullT = pl.BlockSpec((None, d, L), lambda h, i: (h, 0, 0))
    return pl.pallas_call(
        body,
        name=name,
        grid=(H, nq),
        in_specs=[blk, blkT, full, full, blk, blk, blkT, pl.BlockSpec((2 * SB_TC, SB_TC), lambda h, i: (0, 0))],
        out_specs=[blk, fullT, fullT],
        out_shape=[jax.ShapeDtypeStruct((H, L, d), F32), jax.ShapeDtypeStruct((H, d, L), F32),
                   jax.ShapeDtypeStruct((H, d, L), F32)],
        compiler_params=_cparams(("parallel", "arbitrary")),
    )(qs, qsT, kn, v, o, do16, doT, after)


def _fold_heads(rows, name):
    def body(x_ref, o_ref):
        c = lax.broadcasted_iota(jnp.int32, (D_INNER, DT_PAD), 0)
        h = lax.broadcasted_iota(jnp.int32, (D_INNER, DT_PAD), 1)
        o_ref[...] = _dot(x_ref[...], (c // HEAD_P == h).astype(F32), precision=HI)

    return pl.pallas_call(
        body, name=name, out_shape=jax.ShapeDtypeStruct((SUBLANE, DT_PAD), F32),
        compiler_params=_cparams(),
    )(rows)


def _ffn_fwd(x, r, norm_w, w_in_t, w_out, tag):
    x1, h = _norm_fwd(x, r, norm_w, f"norm_ffn_fwd{tag}")
    gu = _mm(h, w_in_t, "nt", F32, f"ffn_in_fwd{tag}")
    act = _swiglu_fwd(gu, f"swiglu_fwd{tag}")
    f = _mm(act, w_out, "nn", F32, f"ffn_out_fwd{tag}")
    return x1, h, gu, act, f


def _ffn_bwd(x1, h, gu, act, norm_w, w_in_t, w_out, dres, tag):
    d16 = dres.astype(BF16)
    dact = _mm(d16, w_out, "nt", BF16, f"ffn_out_dx{tag}")
    g_w_out = _mm(act, d16, "tn", F32, f"ffn_out_dw{tag}")
    dgu = _swiglu_bwd(gu, dact, f"swiglu_bwd{tag}")
    dh = _mm(dgu, w_in_t, "nn", BF16, f"ffn_in_dx{tag}")
    g_w_in_t = _mm(dgu, h, "tn", F32, f"ffn_in_dw{tag}")
    dres, g_norm = _norm_bwd(x1, norm_w, dh, dres, f"norm_ffn_bwd{tag}")
    return dres, g_norm, g_w_in_t, g_w_out


def _rest_weights(gb):
    whole = _whole_of_shards(gb, SEG_REST)
    return {"ssm_w_out": whole["ssm_w_out"][0], "sb_w_qkv_t": whole["sb_w_qkv"][0], "sb_w_o": whole["sb_w_o"][0],
            "ffn_w_in_t": whole["ffn_w_in"], "ffn_w_out": whole["ffn_w_out"]}


def _local_step(x, target, W, wb_rest=None):
    L = x.shape[0]
    H = SB_HEADS
    nm0, nm1 = W["norm_mix"][0:1], W["norm_mix"][1:2]
    nf0, nf1 = W["norm_ffn"][0:1], W["norm_ffn"][1:2]

    _, h0 = _norm_fwd(x, None, nm0, "norm_mix_fwd0")
    zx = _mm(h0, W["ssm_w_in_t"], "nt", F32, "ssm_in_fwd")
    xbc = _conv_fwd(zx, W["conv_w8"], W["conv_b"], "conv_fwd")
    if wb_rest is None:
        y, states = _ssd_fwd(xbc, zx, W["dt_bias"], W["alog_e"], W["d_e"], "ssd_fwd")
    else:
        y, states, gb_rest = _ssd_fwd(xbc, zx, W["dt_bias"], W["alog_e"], W["d_e"], "ssd_fwd", travel=wb_rest)
        W = {**W, **_rest_weights(gb_rest)}
    yn = _gated_norm_fwd(y, zx, W["ssm_norm_w"], "gated_norm_fwd")
    mix0 = _mm(yn, W["ssm_w_out"], "nn", F32, "ssm_out_fwd")
    x1, h1, gu0, act0, f0 = _ffn_fwd(x, mix0, nf0, W["ffn_w_in_t"][0], W["ffn_w_out"][0], "0")

    x2, h2 = _norm_fwd(x1, f0, nm1, "norm_mix_fwd1")
    qkv = _mm(h2, W["sb_w_qkv_t"], "nt", F32, "sb_qkv_fwd")
    qkv_t = qkv.reshape(L, 3, H, SB_D).transpose(1, 2, 0, 3)
    q_r = qkv_t[0].reshape(H * L, SB_D)
    k_r = qkv_t[1].reshape(H * L, SB_D)
    qs = _qk_norm_fwd(q_r, W["sb_q_gain"], SB_Q_SCALE, "q_norm_fwd").reshape(H, L, SB_D)
    kn = _qk_norm_fwd(k_r, W["sb_k_gain"], 1.0, "k_norm_fwd").reshape(H, L, SB_D)
    vb = qkv_t[2].astype(BF16)
    o = _sb_fwd(qs, kn, vb, "sb_fwd")
    o_flat = o.transpose(1, 0, 2).reshape(L, D_MODEL).astype(BF16)
    mix1 = _mm(o_flat, W["sb_w_o"], "nn", F32, "sb_o_fwd")
    x3, h3, gu1, act1, f1 = _ffn_fwd(x2, mix1, nf1, W["ffn_w_in_t"][1], W["ffn_w_out"][1], "1")

    dres, sq = _loss_bwd(x3, f1, target, "loss")

    dres, g_nf1, g_fin1, g_fout1 = _ffn_bwd(x3, h3, gu1, act1, nf1, W["ffn_w_in_t"][1], W["ffn_w_out"][1], dres, "1")
    d16 = dres.astype(BF16)
    do_flat = _mm(d16, W["sb_w_o"], "nt", F32, "sb_o_dx")
    g_w_o = _mm(o_flat, d16, "tn", F32, "sb_o_dw")
    do = do_flat.reshape(L, H, SB_D).transpose(1, 0, 2)
    dqs, dknT, dvT = _sb_bwd(qs, kn, vb, o, do, "sb_bwd")
    dkn = dknT.transpose(0, 2, 1)
    dq, g_qg = _qk_norm_bwd(q_r, W["sb_q_gain"], dqs.reshape(H * L, SB_D), SB_Q_SCALE, LN2, "q_norm_bwd")
    dk, g_kg = _qk_norm_bwd(k_r, W["sb_k_gain"], dkn.reshape(H * L, SB_D), 1.0, LN2, "k_norm_bwd")
    dqkv = jnp.stack([dq.reshape(H, L, SB_D), dk.reshape(H, L, SB_D), dvT.transpose(0, 2, 1)])
    dqkv = dqkv.transpose(2, 0, 1, 3).reshape(L, 3 * D_MODEL).astype(BF16)
    dh2 = _mm(dqkv, W["sb_w_qkv_t"], "nn", BF16, "sb_qkv_dx")
    g_w_qkv_t = _mm(dqkv, h2, "tn", F32, "sb_qkv_dw")
    dres, g_nm1 = _norm_bwd(x2, nm1, dh2, dres, "norm_mix_bwd1")

    dres, g_nf0, g_fin0, g_fout0 = _ffn_bwd(x1, h1, gu0, act0, nf0, W["ffn_w_in_t"][0], W["ffn_w_out"][0], dres, "0")
    d16 = dres.astype(BF16)
    dyn = _mm(d16, W["ssm_w_out"], "nt", BF16, "ssm_out_dx")
    g_w_out = _mm(yn, d16, "tn", F32, "ssm_out_dw")
    dy, dz, g_snw = _gated_norm_bwd(y, zx, W["ssm_norm_w"], dyn, "gated_norm_bwd")
    rest = {"ssm_w_out": [g_w_out], "sb_w_qkv": [g_w_qkv_t], "sb_w_o": [g_w_o],
            "ffn_w_in": [g_fin0, g_fin1], "ffn_w_out": [g_fout0, g_fout1]}
    if wb_rest is None:
        dact, ddt, g_dtb, g_alog_e, g_d_e = _ssd_bwd(
            xbc, zx, W["dt_bias"], W["alog_e"], W["d_e"], states, dy, "ssd_bwd")
        exchanged = None
    else:
        slots = jnp.stack([_shard_of_whole(rest, s, SEG_REST) for s in range(N_CHIPS)])
        dact, ddt, g_dtb, g_alog_e, g_d_e, received = _ssd_bwd(
            xbc, zx, W["dt_bias"], W["alog_e"], W["d_e"], states, dy, "ssd_bwd", travel=slots.astype(BF16))
        exchanged = (slots, received)
    dxbc, g_cw8, g_cb = _conv_bwd(zx, W["conv_w8"], W["conv_b"], dact, "conv_bwd")
    dzx = jnp.concatenate([dz, dxbc, ddt.astype(BF16)], axis=1)
    dh0 = _mm(dzx, W["ssm_w_in_t"], "nn", BF16, "ssm_in_dx")
    g_w_in_t = _mm(dzx, h0, "tn", F32, "ssm_in_dw")
    grad_x, g_nm0 = _norm_bwd(x, nm0, dh0, dres, "norm_mix_bwd0")

    per_head = jnp.concatenate(
        [g_alog_e.reshape(1, D_INNER), g_d_e.reshape(1, D_INNER), jnp.zeros((SUBLANE - 2, D_INNER), F32)], axis=0)
    folded = _fold_heads(per_head, "fold_heads")
    grads = {
        "norm_mix": jnp.concatenate([g_nm0, g_nm1], axis=0),
        "norm_ffn": jnp.concatenate([g_nf0, g_nf1], axis=0),
        "ssm_w_in": [g_w_in_t],
        "ssm_conv_w": [g_cw8],
        "ssm_conv_b": g_cb,
        "ssm_dt_bias": g_dtb[:, :SSM_HEADS],
        "ssm_a_log": folded[0:1, :SSM_HEADS],
        "ssm_d": folded[1:2, :SSM_HEADS],
        "ssm_norm_w": g_snw,
        "sb_q_gain": g_qg,
        "sb_k_gain": g_kg,
        **rest,
    }
    return (sq, grad_x, grads) if wb_rest is None else (sq, grad_x, grads, exchanged)


def _prep_weights(full):
    W = _prep_vectors(full)
    w_in_t = full["ssm_w_in"].T
    W["ssm_w_in_t"] = jnp.pad(w_in_t, ((0, D_IN_PAD - D_IN_PROJ), (0, 0))).astype(BF16)
    W["conv_w8"] = jnp.pad(full["ssm_conv_w"], ((0, SUBLANE - CONV_W), (0, 0)))
    W["ssm_w_out"] = full["ssm_w_out"].astype(BF16)
    W["sb_w_qkv_t"] = full["sb_w_qkv"].T.astype(BF16)
    W["sb_w_o"] = full["sb_w_o"].astype(BF16)
    W["ffn_w_in_t"] = jnp.swapaxes(full["ffn_w_in"], 1, 2).astype(BF16)
    W["ffn_w_out"] = full["ffn_w_out"].astype(BF16)
    return W


def _prep_vectors(full):
    W = {}
    W["norm_mix"] = full["norm_mix"]
    W["norm_ffn"] = full["norm_ffn"]
    W["conv_b"] = full["ssm_conv_b"]
    W["dt_bias"] = jnp.pad(full["ssm_dt_bias"], ((0, 0), (0, DT_PAD - SSM_HEADS)))
    W["alog_e"] = jnp.repeat(full["ssm_a_log"], HEAD_P, axis=1)
    W["d_e"] = jnp.repeat(full["ssm_d"], HEAD_P, axis=1)
    W["ssm_norm_w"] = full["ssm_norm_w"]
    W["sb_q_gain"] = full["sb_q_gain"]
    W["sb_k_gain"] = full["sb_k_gain"]
    return W


N_CHIPS = 4
N_DEV = 8
FLAT_W = D_MODEL

SEG_FIRST = (
    ("ssm_w_in", (1, D_MODEL, D_IN_PROJ // N_CHIPS), True),
    ("ssm_conv_w", (1, CONV_W, CONV_DIM // N_CHIPS), False),
)
SEG_REST = (
    ("ssm_w_out", (1, D_INNER // N_CHIPS, D_MODEL), False),
    ("sb_w_qkv", (1, D_MODEL, 3 * D_MODEL // N_CHIPS), True),
    ("sb_w_o", (1, D_MODEL // N_CHIPS, D_MODEL), False),
    ("ffn_w_in", (2, D_MODEL, 2 * D_FF // N_CHIPS), True),
    ("ffn_w_out", (2, D_FF // N_CHIPS, D_MODEL), False),
)
SHARDED = SEG_FIRST + SEG_REST
REPLICATED = (
    ("norm_mix", (2, D_MODEL)), ("norm_ffn", (2, D_MODEL)), ("ssm_conv_b", (1, CONV_DIM)),
    ("ssm_norm_w", (1, D_INNER)), ("ssm_dt_bias", (1, SSM_HEADS)), ("ssm_a_log", (1, SSM_HEADS)),
    ("ssm_d", (1, SSM_HEADS)), ("sb_q_gain", (1, SB_D)), ("sb_k_gain", (1, SB_D)),
)
ADAM_TILE = 256
_USED_SMALL = sum(math.prod(s) for _, s in REPLICATED)
SMALL_ROWS = -(-_USED_SMALL // (SUBLANE * FLAT_W)) * SUBLANE


def _flat_rows(shape):
    return math.prod(shape) // FLAT_W


def _seg_rows(seg):
    used = sum(_flat_rows(s) for _, s, _ in seg)
    return used, -(-used // ADAM_TILE) * ADAM_TILE


def _pad_rows(parts, seg, dtype):
    used, rows = _seg_rows(seg)
    return parts + [jnp.zeros((rows - used, FLAT_W), dtype)] if rows > used else parts


def _pack_shard(d, dtype, seg):
    parts = []
    for n, _, transposed in seg:
        a = jnp.swapaxes(d[n], 1, 2) if transposed else d[n]
        parts.append(a.reshape(-1, FLAT_W).astype(dtype))
    return jnp.concatenate(_pad_rows(parts, seg, dtype), axis=0)


def _unpack_shard(flat, seg):
    out, r = {}, 0
    for n, s, transposed in seg:
        k = _flat_rows(s)
        a = flat[r:r + k]
        out[n] = jnp.swapaxes(a.reshape(s[0], s[2], s[1]), 1, 2) if transposed else a.reshape(s)
        r += k
    return out


def _shard_of_whole(whole, chip, seg):
    parts = []
    for n, s, _ in seg:
        k = _flat_rows(s) // s[0]
        if n == "ssm_conv_w":
            parts.append(whole[n][0][:CONV_W, chip * FLAT_W:(chip + 1) * FLAT_W])
        else:
            parts.extend(a[chip * k:(chip + 1) * k] for a in whole[n])
    return jnp.concatenate(_pad_rows(parts, seg, parts[0].dtype), axis=0)


def _whole_of_shards(gb, seg):
    out, r = {}, 0
    for n, s, _ in seg:
        k = _flat_rows(s) // s[0]
        if n != "ssm_conv_w":
            out[n] = [jnp.concatenate([gb[p, r + l * k:r + (l + 1) * k] for p in range(N_CHIPS)], axis=0)
                      for l in range(s[0])]
        r += k * s[0]
    return out


def _pack_small(d):
    parts = [d[n].reshape(-1) for n, _ in REPLICATED]
    parts.append(jnp.zeros((SMALL_ROWS * FLAT_W - _USED_SMALL,), parts[0].dtype))
    return jnp.concatenate(parts).reshape(SMALL_ROWS, FLAT_W)


def _unpack_small(flat):
    flat = flat.reshape(-1)
    out, r = {}, 0
    for n, s in REPLICATED:
        k = math.prod(s)
        out[n] = flat[r:r + k].reshape(s)
        r += k
    return out


ANY = pl.BlockSpec(memory_space=pl.ANY)


def _other_chips():
    x, y, c = lax.axis_index("x"), lax.axis_index("y"), lax.axis_index("c")
    return x, y, c, [(1 - x, y), (x, 1 - y), (1 - x, 1 - y)]


def _remote(src, dst, send_sem, recv_sem, dev):
    return pltpu.make_async_remote_copy(src_ref=src, dst_ref=dst, send_sem=send_sem, recv_sem=recv_sem,
                                        device_id=dev, device_id_type=MESH_T)


def _gather_weights(wb, ws):
    def body(wb_ref, ws_ref, gb_ref, gs_ref, send_sems, recv_sems, loc_sems):
        x, y, c, others = _other_chips()
        me = 2 * x + y
        pairs = ((wb_ref, gb_ref), (ws_ref, gs_ref))
        local = [pltpu.make_async_copy(src, dst.at[me], loc_sems.at[t]) for t, (src, dst) in enumerate(pairs)]
        for cp in local:
            cp.start()
        sends = []
        for k, (px, py) in enumerate(others):
            for t, (src, dst) in enumerate(pairs):
                cp = _remote(src, dst.at[me], send_sems.at[2 * k + t], recv_sems.at[2 * k + t], (px, py, c))
                cp.start()
                sends.append(cp)
        for k, (px, py) in enumerate(others):
            for t, (src, dst) in enumerate(pairs):
                _remote(src, dst.at[2 * px + py], send_sems.at[2 * k + t], recv_sems.at[2 * k + t],
                        (px, py, c)).wait_recv()
        for cp in sends:
            cp.wait_send()
        for cp in local:
            cp.wait()

    return pl.pallas_call(
        body, name="gather_weights",
        in_specs=[ANY, ANY], out_specs=[ANY, ANY],
        out_shape=[jax.ShapeDtypeStruct((N_CHIPS,) + wb.shape, wb.dtype),
                   jax.ShapeDtypeStruct((N_CHIPS,) + ws.shape, ws.dtype)],
        scratch_shapes=[pltpu.SemaphoreType.DMA((6,)), pltpu.SemaphoreType.DMA((6,)), pltpu.SemaphoreType.DMA((2,))],
    )(wb, ws)


def _scatter_grads(gpack):
    def body(g_ref, r_ref, send_sems, recv_sems):
        x, y, c, others = _other_chips()
        sends = []
        for k, (px, py) in enumerate(others):
            cp = _remote(g_ref.at[2 * px + py], r_ref.at[k], send_sems.at[k], recv_sems.at[k], (px, py, c))
            cp.start()
            sends.append(cp)
        for cp in sends:
            cp.wait_recv()
        for cp in sends:
            cp.wait_send()

    return pl.pallas_call(
        body, name="scatter_grads",
        in_specs=[ANY], out_specs=ANY,
        out_shape=jax.ShapeDtypeStruct((N_CHIPS - 1,) + gpack.shape[1:], gpack.dtype),
        scratch_shapes=[pltpu.SemaphoreType.DMA((3,)), pltpu.SemaphoreType.DMA((3,))],
    )(gpack)


class _Moves:
    def __init__(self, sends, recvs, local=None):
        self.sends, self.recvs, self.local = sends, recvs, local

    def start(self):
        if self.local is not None:
            self.local.start()
        for cp in self.sends:
            cp.start()

    def wait(self):
        for cp in self.recvs:
            cp.wait_recv()
        for cp in self.sends:
            cp.wait_send()
        if self.local is not None:
            self.local.wait()


def _gather_moves(wb_ref, gb_ref, send_sems, recv_sems, loc_sem):
    x, y, c, others = _other_chips()
    me = 2 * x + y
    sends = [_remote(wb_ref, gb_ref.at[me], send_sems.at[k], recv_sems.at[k], (px, py, c))
             for k, (px, py) in enumerate(others)]
    recvs = [_remote(wb_ref, gb_ref.at[2 * px + py], send_sems.at[k], recv_sems.at[k], (px, py, c))
             for k, (px, py) in enumerate(others)]
    return _Moves(sends, recvs, pltpu.make_async_copy(wb_ref, gb_ref.at[me], loc_sem))


def _scatter_moves(g_ref, r_ref, send_sems, recv_sems):
    x, y, c, others = _other_chips()
    sends = [_remote(g_ref.at[2 * px + py], r_ref.at[k], send_sems.at[k], recv_sems.at[k], (px, py, c))
             for k, (px, py) in enumerate(others)]
    return _Moves(sends, sends)


_MOVE_SEMS = [pltpu.SemaphoreType.DMA((N_CHIPS - 1,)), pltpu.SemaphoreType.DMA((N_CHIPS - 1,))]


def _exchange_partials(part_a, part_b, small):
    def body(pa_ref, pb_ref, s_ref, psa_ref, psb_ref, sa_ref, send_sems, recv_sems, loc_sem):
        x, y, c = lax.axis_index("x"), lax.axis_index("y"), lax.axis_index("c")
        me = 4 * x + 2 * y + c
        local = pltpu.make_async_copy(s_ref, sa_ref.at[me], loc_sem)
        local.start()
        sib = _remote(pa_ref, psa_ref, send_sems.at[0], recv_sems.at[0], (x, y, 1 - c))
        sib_b = _remote(pb_ref, psb_ref, send_sems.at[N_DEV], recv_sems.at[N_DEV], (x, y, 1 - c))
        sib.start()
        sib_b.start()
        sends = [sib, sib_b]
        for k in range(1, N_DEV):
            fx, fy, fc = (k >> 2) & 1, (k >> 1) & 1, k & 1
            px, py, pc = x ^ fx, y ^ fy, c ^ fc
            cp = _remote(s_ref, sa_ref.at[me], send_sems.at[k], recv_sems.at[k], (px, py, pc))
            cp.start()
            sends.append(cp)
        sib.wait_recv()
        sib_b.wait_recv()
        for k in range(1, N_DEV):
            fx, fy, fc = (k >> 2) & 1, (k >> 1) & 1, k & 1
            px, py, pc = x ^ fx, y ^ fy, c ^ fc
            _remote(s_ref, sa_ref.at[4 * px + 2 * py + pc], send_sems.at[k], recv_sems.at[k], (px, py, pc)).wait_recv()
        for cp in sends:
            cp.wait_send()
        local.wait()

    return pl.pallas_call(
        body, name="exchange_partials",
        in_specs=[ANY, ANY, ANY], out_specs=[ANY, ANY, ANY],
        out_shape=[jax.ShapeDtypeStruct(part_a.shape, part_a.dtype),
                   jax.ShapeDtypeStruct(part_b.shape, part_b.dtype),
                   jax.ShapeDtypeStruct((N_DEV,) + small.shape, small.dtype)],
        scratch_shapes=[pltpu.SemaphoreType.DMA((N_DEV + 1,)), pltpu.SemaphoreType.DMA((N_DEV + 1,)),
                        pltpu.SemaphoreType.DMA],
    )(part_a, part_b, small)


def _partial_sum(own, recv, name):
    R = own.shape[0]

    def body(o_ref, r_ref, p_ref):
        acc = o_ref[...]
        for k in range(N_CHIPS - 1):
            acc = acc + r_ref[k].astype(F32)
        p_ref[...] = acc

    return pl.pallas_call(
        body, name=name, grid=(R // ADAM_TILE,),
        in_specs=[pl.BlockSpec((ADAM_TILE, FLAT_W), lambda i: (i, 0)),
                  pl.BlockSpec((N_CHIPS - 1, ADAM_TILE, FLAT_W), lambda i: (0, i, 0))],
        out_specs=pl.BlockSpec((ADAM_TILE, FLAT_W), lambda i: (i, 0)),
        out_shape=jax.ShapeDtypeStruct(own.shape, F32),
        compiler_params=_cparams(("parallel",)),
    )(own, recv)


def _adamw_math(w, g, m, v):
    m = ADAM_B1 * m + (1.0 - ADAM_B1) * g
    v = ADAM_B2 * v + (1.0 - ADAM_B2) * jnp.square(g)
    m_hat = m / (1.0 - ADAM_B1 ** ADAM_STEP)
    v_hat = v / (1.0 - ADAM_B2 ** ADAM_STEP)
    delta = -ADAM_LR * (m_hat / (jnp.sqrt(v_hat) + ADAM_EPS) + ADAM_WD * w)
    return delta, m, v


def _adamw(w, m, v, parts, tile, name):
    n, R, _ = parts.shape

    def body(w_ref, m_ref, v_ref, p_ref, g_ref, d_ref, nm_ref, nv_ref):
        g = p_ref[0]
        for k in range(1, n):
            g = g + p_ref[k]
        delta, nm, nv = _adamw_math(w_ref[...], g, m_ref[...], v_ref[...])
        g_ref[...] = g
        d_ref[...] = delta
        nm_ref[...] = nm
        nv_ref[...] = nv

    blk = pl.BlockSpec((tile, FLAT_W), lambda i: (i, 0))
    return pl.pallas_call(
        body, name=name, grid=(R // tile,),
        in_specs=[blk, blk, blk, pl.BlockSpec((n, tile, FLAT_W), lambda i: (0, i, 0))],
        out_specs=[blk] * 4,
        out_shape=[jax.ShapeDtypeStruct((R, FLAT_W), F32)] * 4,
        compiler_params=_cparams(("parallel",)),
    )(w, m, v, parts)


def _add2(a, b, name):
    def body(a_ref, b_ref, o_ref):
        o_ref[...] = a_ref[...] + b_ref[...]

    blk = pl.BlockSpec((ADAM_TILE, FLAT_W), lambda i: (i, 0))
    return pl.pallas_call(
        body, name=name, grid=(a.shape[0] // ADAM_TILE,), in_specs=[blk, blk], out_specs=blk,
        out_shape=jax.ShapeDtypeStruct(a.shape, F32), compiler_params=_cparams(("parallel",)),
    )(a, b)


def _adamw_block(w, m, v, g, name):
    shape = w.shape
    R, C = math.prod(shape[:-1]), shape[-1]
    tile = _tile(R, ADAM_TILE, SUBLANE)

    def body(w_ref, m_ref, v_ref, g_ref, d_ref, nm_ref, nv_ref):
        delta, nm, nv = _adamw_math(w_ref[...], g_ref[...], m_ref[...], v_ref[...])
        d_ref[...] = delta
        nm_ref[...] = nm
        nv_ref[...] = nv

    blk = pl.BlockSpec((tile, C), lambda i: (i, 0))
    outs = pl.pallas_call(
        body, name=name, grid=(R // tile,), in_specs=[blk] * 4, out_specs=[blk] * 3,
        out_shape=[jax.ShapeDtypeStruct((R, C), F32)] * 3, compiler_params=_cparams(("parallel",)),
    )(*(a.reshape(R, C) for a in (w, m, v, g)))
    return tuple(a.reshape(shape) for a in outs)


_NAMES = ("norm_mix", "norm_ffn", "ssm_w_in", "ssm_conv_w", "ssm_conv_b", "ssm_dt_bias", "ssm_a_log", "ssm_d",
          "ssm_norm_w", "ssm_w_out", "sb_w_qkv", "sb_q_gain", "sb_k_gain", "sb_w_o", "ffn_w_in", "ffn_w_out")


def _step(x, loss_target, w, m, v):
    cx, cy, cc = lax.axis_index("x"), lax.axis_index("y"), lax.axis_index("c")
    chip = 2 * cx + cy

    conv8 = jnp.pad(w["ssm_conv_w"][0], ((0, SUBLANE - CONV_W), (0, 0)))
    gb, gs = _gather_weights(_pack_shard(w, BF16, SEG_FIRST), conv8)
    W = _prep_vectors(w)
    W["ssm_w_in_t"] = jnp.pad(_whole_of_shards(gb, SEG_FIRST)["ssm_w_in"][0], ((0, D_IN_PAD - D_IN_PROJ), (0, 0)))
    W["conv_w8"] = jnp.concatenate([gs[p] for p in range(N_CHIPS)], axis=1)

    sq, grad_x, grads, (slots_rest, recv_rest) = _local_step(
        x[0], loss_target[0], W, wb_rest=_pack_shard(w, BF16, SEG_REST))
    loss = lax.psum(0.5 * jnp.sum(sq) / D_MODEL, ("x", "y", "c"))

    slots_first = jnp.stack([_shard_of_whole(grads, s, SEG_FIRST) for s in range(N_CHIPS)])
    recv_first = _scatter_grads(slots_first.astype(BF16))
    own = lambda slots: lax.dynamic_index_in_dim(slots, chip, axis=0, keepdims=False)
    part_first = _partial_sum(own(slots_first), recv_first, "grad_partial_sum_first")
    part_rest = _partial_sum(own(slots_rest), recv_rest, "grad_partial_sum_rest")
    sib_first, sib_rest, small_all = _exchange_partials(part_first, part_rest, _pack_small(grads))
    g_blocks = {**_unpack_shard(_add2(part_first, sib_first, "grad_final_sum_first"), SEG_FIRST),
                **_unpack_shard(_add2(part_rest, sib_rest, "grad_final_sum_rest"), SEG_REST)}
    g_s, d_s, m_s, v_s = _adamw(_pack_small(w), _pack_small(m), _pack_small(v), small_all, SUBLANE, "adamw_replicated")

    g, d, nm, nv = (_unpack_small(a) for a in (g_s, d_s, m_s, v_s))
    for n, _, _ in SHARDED:
        g[n] = g_blocks[n]
        d[n], nm[n], nv[n] = _adamw_block(w[n], m[n], v[n], g[n], "adamw_" + n)
    return loss, grad_x[None], [[t[n] for n in _NAMES] for t in (g, d, nm, nv)]


def kernel(x, norm_mix, norm_ffn, ssm_w_in, ssm_conv_w, ssm_conv_b, ssm_dt_bias, ssm_a_log, ssm_d, ssm_norm_w, ssm_w_out, sb_w_qkv, sb_q_gain, sb_k_gain, sb_w_o, ffn_w_in, ffn_w_out, loss_target, m_norm_mix, m_norm_ffn, m_ssm_w_in, m_ssm_conv_w, m_ssm_conv_b, m_ssm_dt_bias, m_ssm_a_log, m_ssm_d, m_ssm_norm_w, m_ssm_w_out, m_sb_w_qkv, m_sb_q_gain, m_sb_k_gain, m_sb_w_o, m_ffn_w_in, m_ffn_w_out, v_norm_mix, v_norm_ffn, v_ssm_w_in, v_ssm_conv_w, v_ssm_conv_b, v_ssm_dt_bias, v_ssm_a_log, v_ssm_d, v_ssm_norm_w, v_ssm_w_out, v_sb_w_qkv, v_sb_q_gain, v_sb_k_gain, v_sb_w_o, v_ffn_w_in, v_ffn_w_out):
    w = dict(zip(_NAMES, (norm_mix, norm_ffn, ssm_w_in, ssm_conv_w, ssm_conv_b, ssm_dt_bias, ssm_a_log, ssm_d,
                          ssm_norm_w, ssm_w_out, sb_w_qkv, sb_q_gain, sb_k_gain, sb_w_o, ffn_w_in, ffn_w_out)))
    m = dict(zip(_NAMES, (m_norm_mix, m_norm_ffn, m_ssm_w_in, m_ssm_conv_w, m_ssm_conv_b, m_ssm_dt_bias, m_ssm_a_log,
                          m_ssm_d, m_ssm_norm_w, m_ssm_w_out, m_sb_w_qkv, m_sb_q_gain, m_sb_k_gain, m_sb_w_o,
                          m_ffn_w_in, m_ffn_w_out)))
    v = dict(zip(_NAMES, (v_norm_mix, v_norm_ffn, v_ssm_w_in, v_ssm_conv_w, v_ssm_conv_b, v_ssm_dt_bias, v_ssm_a_log,
                          v_ssm_d, v_ssm_norm_w, v_ssm_w_out, v_sb_w_qkv, v_sb_q_gain, v_sb_k_gain, v_sb_w_o,
                          v_ffn_w_in, v_ffn_w_out)))
    loss, grad_x, (g, d, nm, nv) = _step(x, loss_target, w, m, v)
    return (loss, grad_x, *g, *d, *nm, *nv)
```

```python
import functools
import math

import jax
import jax.numpy as jnp
from jax import lax
from jax.experimental import pallas as pl
from jax.experimental.pallas import tpu as pltpu

F32 = jnp.float32
BF16 = jnp.bfloat16

VMEM_LIMIT_BYTES = 48 * 1024 * 1024
LANE = 128
SUBLANE = 8

D_MODEL = 1024
CHUNK = 64
D_INNER = 2048
SSM_HEADS = 32
SSM_GROUPS = 8
GROUP_W = D_INNER // SSM_GROUPS
HEAD_P = 64
D_STATE = 128
CONV_W = 4
CONV_DIM = D_INNER + 2 * SSM_GROUPS * D_STATE
D_IN_PROJ = D_INNER + CONV_DIM + SSM_HEADS
DT_PAD = LANE
D_IN_PAD = D_INNER + CONV_DIM + DT_PAD
SB_HEADS = 16
SB_D = 64
D_FF = 2816
NORM_EPS = 1e-6
GATED_NORM_EPS = 1e-5

ADAM_LR = 0.001
ADAM_B1 = 0.9
ADAM_B2 = 0.999
ADAM_EPS = 1e-08
ADAM_WD = 0.01
ADAM_STEP = 10

MESH_T = pl.DeviceIdType.MESH


def _cparams(sem=None):
    return pltpu.CompilerParams(dimension_semantics=sem, vmem_limit_bytes=VMEM_LIMIT_BYTES)


def _tile(n, target, align):
    best = None
    for t in range(align, min(n, target) + 1, align):
        if n % t == 0:
            best = t
    return best or n


def _mm(a, b, mode, out_dtype, name):
    halves = a.ndim == 3
    a2 = (a.shape[1], 2 * a.shape[2]) if halves else a.shape
    if mode == "nn":
        (M, K), N = a2, b.shape[1]
    elif mode == "nt":
        (M, K), N = a2, b.shape[0]
    else:
        (K, M), N = a2, b.shape[1]
    tm = _tile(M // 2 if halves and mode == "tn" else M, 1408, LANE)
    tn = _tile(N, 1536, LANE)
    tk = _tile(K // 2 if halves and mode == "nn" else K, 1536, LANE)
    nk = K // tk
    if mode == "nn":
        a_spec = pl.BlockSpec((tm, tk), lambda i, j, k: (i, k))
        if halves:
            nkh = nk // 2
            a_spec = pl.BlockSpec((None, tm, tk), lambda i, j, k: (k // nkh, i, k % nkh))
        b_spec = pl.BlockSpec((tk, tn), lambda i, j, k: (k, j))
        dims = (((1,), (0,)), ((), ()))
    elif mode == "nt":
        a_spec = pl.BlockSpec((tm, tk), lambda i, j, k: (i, k))
        b_spec = pl.BlockSpec((tn, tk), lambda i, j, k: (j, k))
        dims = (((1,), (1,)), ((), ()))
    else:
        a_spec = pl.BlockSpec((tk, tm), lambda i, j, k: (k, i))
        if halves:
            nmh = M // tm // 2
            a_spec = pl.BlockSpec((None, tk, tm), lambda i, j, k: (i // nmh, k, i % nmh))
        b_spec = pl.BlockSpec((tk, tn), lambda i, j, k: (k, j))
        dims = (((0,), (0,)), ((), ()))

    def body(a_ref, b_ref, o_ref, acc_ref):
        k = pl.program_id(2)

        @pl.when(k == 0)
        def _():
            acc_ref[...] = jnp.zeros_like(acc_ref)

        acc_ref[...] += lax.dot_general(a_ref[...], b_ref[...], dims, preferred_element_type=F32)

        @pl.when(k == nk - 1)
        def _():
            o_ref[...] = acc_ref[...].astype(o_ref.dtype)

    return pl.pallas_call(
        body,
        name=name,
        grid=(M // tm, N // tn, nk),
        in_specs=[a_spec, b_spec],
        out_specs=pl.BlockSpec((tm, tn), lambda i, j, k: (i, j)),
        out_shape=jax.ShapeDtypeStruct((M, N), out_dtype),
        scratch_shapes=[pltpu.VMEM((tm, tn), F32)],
        compiler_params=_cparams(("parallel", "parallel", "arbitrary")),
    )(a, b)


def _rows(fn, name, tile, row_ins, const_ins, row_outs, acc_outs=(), ncol=1):
    L = row_ins[0][0].shape[0]
    tile = min(tile, L)
    nrow = L // tile
    n_ri, n_ci, n_ro, n_ao = len(row_ins), len(const_ins), len(row_outs), len(acc_outs)

    def body(*refs):
        i = pl.program_id(1)
        j = pl.program_id(0)
        ins = [r[...] for r in refs[: n_ri + n_ci]]
        outs = fn(*ins, j=j)
        o_refs = refs[n_ri + n_ci:]
        for r, v in zip(o_refs[:n_ro], outs[:n_ro]):
            r[...] = v.astype(r.dtype)

        @pl.when(i == 0)
        def _():
            for r in o_refs[n_ro:]:
                r[...] = jnp.zeros_like(r)

        for r, v in zip(o_refs[n_ro:], outs[n_ro:]):
            r[...] += v

    def rspec(bc, cf):
        return pl.BlockSpec((tile, bc), lambda j, i: (i, cf(j)))

    def cspec(r, bc, cf):
        return pl.BlockSpec((r, bc), lambda j, i: (0, cf(j)))

    in_specs = [rspec(bc, cf) for (_, bc, cf) in row_ins]
    in_specs += [cspec(a.shape[0], bc, cf) for (a, bc, cf) in const_ins]
    out_specs = [rspec(bc, cf) for (_, _, bc, cf) in row_outs]
    out_specs += [cspec(1, bc, cf) for (_, bc, cf) in acc_outs]
    out_shape = [jax.ShapeDtypeStruct((L, c), dt) for (c, dt, _, _) in row_outs]
    out_shape += [jax.ShapeDtypeStruct((1, c), F32) for (c, _, _) in acc_outs]
    res = pl.pallas_call(
        body,
        name=name,
        grid=(ncol, nrow),
        in_specs=in_specs,
        out_specs=out_specs,
        out_shape=out_shape,
        compiler_params=_cparams(("arbitrary", "arbitrary")),
    )(*[a for (a, _, _) in row_ins], *[a for (a, _, _) in const_ins])
    return res


def _zero(j):
    return 0


def _whole(a):
    return (a, a.shape[1], _zero)


def _rms(x, w, eps):
    return x * lax.rsqrt(jnp.mean(x * x, axis=-1, keepdims=True) + eps) * w


ROW_TILE = 256
COL_ROW_TILE = 1024
QK_ROW_TILE = 8192


def _norm_fwd(x, r, w, name):
    C = x.shape[1]
    if r is None:
        def fn(x_, w_, j):
            return (_rms(x_, w_, NORM_EPS),)
        (h,) = _rows(fn, name, ROW_TILE, [_whole(x)], [_whole(w)], [(C, BF16, C, _zero)])
        return x, h

    def fn(x_, r_, w_, j):
        x1 = x_ + r_
        return x1, _rms(x1, w_, NORM_EPS)

    x1, h = _rows(fn, name, ROW_TILE, [_whole(x), _whole(r)], [_whole(w)],
                  [(C, F32, C, _zero), (C, BF16, C, _zero)])
    return x1, h


def _norm_bwd(x, w, dh, dres, name):
    C = x.shape[1]

    def fn(x_, dh_, dres_, w_, j):
        _, vjp = jax.vjp(lambda a, b: _rms(a, b, NORM_EPS), x_, w_)
        dx, dw = vjp(dh_.astype(F32))
        return dres_ + dx, dw

    return _rows(fn, name, ROW_TILE, [_whole(x), _whole(dh), _whole(dres)], [_whole(w)],
                 [(C, F32, C, _zero)], [(C, C, _zero)])


def _silu(x):
    return x * jax.nn.sigmoid(x)


FF_TM = 512
FF_TN = D_FF // 2
_NT = (((1,), (1,)), ((), ()))


def _ffn_in_fused(h, w_in_t, name):
    L, K = h.shape
    tm = min(FF_TM, L)
    nb = D_FF // FF_TN

    def body(a_ref, bg_ref, bu_ref, gu_ref, act_ref):
        g = lax.dot_general(a_ref[...], bg_ref[...], _NT, preferred_element_type=F32)
        u = lax.dot_general(a_ref[...], bu_ref[...], _NT, preferred_element_type=F32)
        gu_ref[0] = g
        gu_ref[1] = u
        act_ref[...] = (_silu(g) * u).astype(act_ref.dtype)

    return pl.pallas_call(
        body,
        name=name,
        grid=(L // tm, nb),
        in_specs=[pl.BlockSpec((tm, K), lambda i, j: (i, 0)),
                  pl.BlockSpec((FF_TN, K), lambda i, j: (j, 0)),
                  pl.BlockSpec((FF_TN, K), lambda i, j: (j + nb, 0))],
        out_specs=[pl.BlockSpec((2, tm, FF_TN), lambda i, j: (0, i, j)),
                   pl.BlockSpec((tm, FF_TN), lambda i, j: (i, j))],
        out_shape=[jax.ShapeDtypeStruct((2, L, D_FF), F32), jax.ShapeDtypeStruct((L, D_FF), BF16)],
        compiler_params=_cparams(("parallel", "parallel")),
    )(h, w_in_t, w_in_t)


def _ffn_out_dx_fused(d16, w_out, gu, name):
    L, K = d16.shape
    tm = min(FF_TM, L)

    def body(a_ref, b_ref, gu_ref, dgu_ref):
        dact = lax.dot_general(a_ref[...], b_ref[...], _NT, preferred_element_type=F32)
        _, vjp = jax.vjp(lambda g, u: _silu(g) * u, gu_ref[0], gu_ref[1])
        dg, du = vjp(dact.astype(BF16).astype(F32))
        dgu_ref[0] = dg.astype(dgu_ref.dtype)
        dgu_ref[1] = du.astype(dgu_ref.dtype)

    return pl.pallas_call(
        body,
        name=name,
        grid=(L // tm, D_FF // FF_TN),
        in_specs=[pl.BlockSpec((tm, K), lambda i, j: (i, 0)),
                  pl.BlockSpec((FF_TN, K), lambda i, j: (j, 0)),
                  pl.BlockSpec((2, tm, FF_TN), lambda i, j: (0, i, j))],
        out_specs=pl.BlockSpec((2, tm, FF_TN), lambda i, j: (0, i, j)),
        out_shape=jax.ShapeDtypeStruct((2, L, D_FF), BF16),
        compiler_params=_cparams(("parallel", "parallel")),
    )(d16, w_out, gu)


def _gated_norm_fwd(y, zx, w, name):
    def fn(y_, z_, w_, j):
        return (_rms(y_ * _silu(z_), w_, GATED_NORM_EPS),)
    (yn,) = _rows(fn, name, COL_ROW_TILE,
                  [(y, GROUP_W, lambda j: j), (zx, GROUP_W, lambda j: j)], [(w, GROUP_W, lambda j: j)],
                  [(D_INNER, BF16, GROUP_W, lambda j: j)], ncol=SSM_GROUPS)
    return yn


def _gated_norm_bwd(y, zx, w, dyn, name):
    def fn(y_, z_, dyn_, w_, j):
        _, vjp = jax.vjp(lambda a, b, c: _rms(a * _silu(b), c, GATED_NORM_EPS), y_, z_, w_)
        return vjp(dyn_.astype(F32))
    cj = lambda j: j
    return _rows(fn, name, COL_ROW_TILE,
                 [(y, GROUP_W, cj), (zx, GROUP_W, cj), (dyn, GROUP_W, cj)], [(w, GROUP_W, cj)],
                 [(D_INNER, F32, GROUP_W, cj), (D_INNER, BF16, GROUP_W, cj)], [(D_INNER, GROUP_W, cj)],
                 ncol=SSM_GROUPS)


def _loss_bwd(x3, f, target, name):
    C = x3.shape[1]

    def fn(x_, f_, t_, j):
        err = (x_ + f_) - t_
        return err * (1.0 / C), jnp.sum(err * err, axis=0, keepdims=True)

    return _rows(fn, name, ROW_TILE, [_whole(x3), _whole(f), _whole(target)], [],
                 [(C, F32, C, _zero)], [(C, C, _zero)])


def _qk_norm_fwd(q, gain, scale, name):
    def fn(q_, g_, j):
        return (_rms(q_, g_, NORM_EPS) * scale,)
    (qn,) = _rows(fn, name, QK_ROW_TILE, [_whole(q)], [_whole(gain)], [(SB_D, BF16, SB_D, _zero)])
    return qn


def _qk_norm_bwd(q, gain, dqn, scale, cot_scale, name):
    def fn(q_, dqn_, g_, j):
        _, vjp = jax.vjp(lambda a, b: _rms(a, b, NORM_EPS) * scale, q_, g_)
        return vjp(dqn_ * cot_scale)
    return _rows(fn, name, QK_ROW_TILE, [_whole(q), _whole(dqn)], [_whole(gain)],
                 [(SB_D, F32, SB_D, _zero)], [(SB_D, SB_D, _zero)])


CONV_TILE = 512
CONV_BLK = 512
XBC_COL0 = D_INNER // CONV_BLK


def _shift_down(cur, prev8, k):
    if k == 0:
        return cur
    rolled = pltpu.roll(cur, k, 0)
    head_prev = pltpu.roll(prev8, k, 0)
    rid = lax.broadcasted_iota(jnp.int32, (SUBLANE, cur.shape[1]), 0)
    head = jnp.where(rid < k, head_prev, rolled[:SUBLANE])
    if cur.shape[0] == SUBLANE:
        return head
    return jnp.concatenate([head, rolled[SUBLANE:]], axis=0)


def _shift_up(cur, next8, k):
    if k == 0:
        return cur
    T = cur.shape[0]
    rolled = pltpu.roll(cur, T - k, 0)
    tail_next = pltpu.roll(next8, SUBLANE - k, 0)
    rid = lax.broadcasted_iota(jnp.int32, (SUBLANE, cur.shape[1]), 0)
    tail = jnp.where(rid >= SUBLANE - k, tail_next, rolled[T - SUBLANE:])
    return jnp.concatenate([rolled[: T - SUBLANE], tail], axis=0)


def _conv_pre(cur, prev8, w, b):
    pre = b
    for i in range(CONV_W):
        pre = pre + w[i:i + 1, :] * _shift_down(cur, prev8, CONV_W - 1 - i)
    return pre


def _conv_fwd(zx, w8, b, name):
    L = zx.shape[0]
    nrow = L // CONV_TILE
    r8 = CONV_TILE // SUBLANE

    def body(cur_ref, prev_ref, w_ref, b_ref, o_ref):
        i = pl.program_id(1)
        prev8 = jnp.where(i > 0, prev_ref[...], 0.0)
        pre = _conv_pre(cur_ref[...], prev8, w_ref[...], b_ref[...])
        o_ref[...] = _silu(pre)

    return pl.pallas_call(
        body,
        name=name,
        grid=(CONV_DIM // CONV_BLK, nrow),
        in_specs=[
            pl.BlockSpec((CONV_TILE, CONV_BLK), lambda j, i: (i, j + XBC_COL0)),
            pl.BlockSpec((SUBLANE, CONV_BLK), lambda j, i: (jnp.maximum(i * r8 - 1, 0), j + XBC_COL0)),
            pl.BlockSpec((SUBLANE, CONV_BLK), lambda j, i: (0, j)),
            pl.BlockSpec((1, CONV_BLK), lambda j, i: (0, j)),
        ],
        out_specs=pl.BlockSpec((CONV_TILE, CONV_BLK), lambda j, i: (i, j)),
        out_shape=jax.ShapeDtypeStruct((L, CONV_DIM), F32),
        compiler_params=_cparams(("arbitrary", "arbitrary")),
    )(zx, zx, w8, b)


def _conv_bwd(zx, w8, b, dact, name):
    L = zx.shape[0]
    nrow = L // CONV_TILE
    r8 = CONV_TILE // SUBLANE
    last8 = L // SUBLANE - 1

    def body(cur_ref, prev_ref, next_ref, da_ref, dan_ref, w_ref, b_ref, du_ref, dw_ref, db_ref):
        i = pl.program_id(1)
        w = w_ref[...]
        b_ = b_ref[...]
        cur = cur_ref[...]
        prev8 = jnp.where(i > 0, prev_ref[...], 0.0)
        pre = _conv_pre(cur, prev8, w, b_)
        _, vjp = jax.vjp(_silu, pre)
        (dpre,) = vjp(da_ref[...])
        nxt = next_ref[...]
        pre_n = _conv_pre(nxt, cur[CONV_TILE - SUBLANE:], w, b_)
        _, vjp_n = jax.vjp(_silu, pre_n)
        (dpre_n,) = vjp_n(dan_ref[...])
        dpre_n = jnp.where(i < nrow - 1, dpre_n, 0.0)
        du = jnp.zeros_like(cur)
        dws = []
        for k in range(CONV_W):
            wk = w[CONV_W - 1 - k:CONV_W - k, :]
            du = du + wk * _shift_up(dpre, dpre_n, k)
            dws.append(jnp.sum(dpre * _shift_down(cur, prev8, k), axis=0, keepdims=True))
        du_ref[...] = du.astype(du_ref.dtype)
        dw_tile = jnp.concatenate([dws[3], dws[2], dws[1], dws[0]] + [jnp.zeros_like(dws[0])] * 4, axis=0)

        @pl.when(i == 0)
        def _():
            dw_ref[...] = jnp.zeros_like(dw_ref)
            db_ref[...] = jnp.zeros_like(db_ref)

        dw_ref[...] += dw_tile
        db_ref[...] += jnp.sum(dpre, axis=0, keepdims=True)

    return pl.pallas_call(
        body,
        name=name,
        grid=(CONV_DIM // CONV_BLK, nrow),
        in_specs=[
            pl.BlockSpec((CONV_TILE, CONV_BLK), lambda j, i: (i, j + XBC_COL0)),
            pl.BlockSpec((SUBLANE, CONV_BLK), lambda j, i: (jnp.maximum(i * r8 - 1, 0), j + XBC_COL0)),
            pl.BlockSpec((SUBLANE, CONV_BLK), lambda j, i: (jnp.minimum((i + 1) * r8, last8), j + XBC_COL0)),
            pl.BlockSpec((CONV_TILE, CONV_BLK), lambda j, i: (i, j)),
            pl.BlockSpec((SUBLANE, CONV_BLK), lambda j, i: (jnp.minimum((i + 1) * r8, last8), j)),
            pl.BlockSpec((SUBLANE, CONV_BLK), lambda j, i: (0, j)),
            pl.BlockSpec((1, CONV_BLK), lambda j, i: (0, j)),
        ],
        out_specs=[
            pl.BlockSpec((CONV_TILE, CONV_BLK), lambda j, i: (i, j)),
            pl.BlockSpec((SUBLANE, CONV_BLK), lambda j, i: (0, j)),
            pl.BlockSpec((1, CONV_BLK), lambda j, i: (0, j)),
        ],
        out_shape=[
            jax.ShapeDtypeStruct((L, CONV_DIM), BF16),
            jax.ShapeDtypeStruct((SUBLANE, CONV_DIM), F32),
            jax.ShapeDtypeStruct((1, CONV_DIM), F32),
        ],
        compiler_params=_cparams(("arbitrary", "arbitrary")),
    )(zx, zx, zx, dact, dact, w8, b)


HI = lax.Precision.HIGHEST
XS_COL0 = 0
B_COL0 = D_INNER // D_STATE
C_COL0 = B_COL0 + SSM_GROUPS
DT_COL = (D_INNER + CONV_DIM) // DT_PAD


def _dot(a, b, dims=(((1,), (0,)), ((), ())), precision=None):
    return lax.dot_general(a, b, dims, precision=precision, preferred_element_type=F32)


_DOT_DIMS = {
    "nn": (((1,), (0,)), ((), ())),
    "nt": (((1,), (1,)), ((), ())),
    "tn": (((0,), (0,)), ((), ())),
}


@functools.partial(jax.custom_vjp, nondiff_argnums=(2,))
def _bdot(a, b, mode):
    return _dot(a.astype(BF16), b.astype(BF16), _DOT_DIMS[mode])


def _bdot_fwd(a, b, mode):
    return _bdot(a, b, mode), (a, b)


def _bdot_bwd(mode, res, g):
    a, b = res
    if mode == "nn":
        return _bdot(g, b, "nt"), _bdot(a, g, "tn")
    if mode == "nt":
        return _bdot(g, b, "nn"), _bdot(g, a, "tn")
    return _bdot(b, g, "nt"), _bdot(a, g, "nn")


_bdot.defvjp(_bdot_fwd, _bdot_bwd)


def _softplus(x):
    return jnp.maximum(x, 0.0) + jnp.log(1.0 + jnp.exp(-jnp.abs(x)))


def _split2(x):
    hi = x.astype(BF16)
    return hi, (x - hi.astype(F32)).astype(BF16)


@jax.custom_vjp
def _sel_left(m, mT, x):
    hi, lo = _split2(x)
    out = _dot(m, jnp.concatenate([hi, lo], axis=1))
    n = x.shape[1]
    return out[:, :n] + out[:, n:]


_sel_left.defvjp(lambda m, mT, x: (_sel_left(m, mT, x), (m, mT)),
                 lambda res, g: (jnp.zeros_like(res[0]), jnp.zeros_like(res[1]), _sel_left(res[1], res[0], g)))


@jax.custom_vjp
def _sel_right(x, m, mT):
    hi, lo = _split2(x)
    out = _dot(jnp.concatenate([hi, lo], axis=0), m)
    r = x.shape[0]
    return out[:r] + out[r:]


_sel_right.defvjp(lambda x, m, mT: (_sel_right(x, m, mT), (m, mT)),
                  lambda res, g: (_sel_right(g, res[1], res[0]), jnp.zeros_like(res[0]), jnp.zeros_like(res[1])))


def _ssd_consts():
    l_ = lax.broadcasted_iota(jnp.int32, (CHUNK, GROUP_W), 0)
    c_ = lax.broadcasted_iota(jnp.int32, (CHUNK, GROUP_W), 1)
    s_ = c_ % CHUNK
    causal = s_ <= l_
    eye_t = (s_ == l_).astype(F32)
    r0 = lax.broadcasted_iota(jnp.int32, (CHUNK, CHUNK), 0)
    c0 = lax.broadcasted_iota(jnp.int32, (CHUNK, CHUNK), 1)
    tril = (c0 <= r0).astype(BF16)
    triu = (r0 <= c0).astype(BF16)
    rb = lax.broadcasted_iota(jnp.int32, (GROUP_W, GROUP_W), 0) // HEAD_P
    cb = lax.broadcasted_iota(jnp.int32, (GROUP_W, GROUP_W), 1) // HEAD_P
    blockdiag = rb == cb
    return causal, eye_t, tril, triu, blockdiag


def _ssd_expand(g):
    h = lax.broadcasted_iota(jnp.int32, (DT_PAD, GROUP_W), 0)
    c = lax.broadcasted_iota(jnp.int32, (DT_PAD, GROUP_W), 1)
    cT = lax.broadcasted_iota(jnp.int32, (GROUP_W, DT_PAD), 0)
    hT = lax.broadcasted_iota(jnp.int32, (GROUP_W, DT_PAD), 1)
    hpg = GROUP_W // HEAD_P
    return (h == g * hpg + c // HEAD_P).astype(BF16), (hT == g * hpg + cT // HEAD_P).astype(BF16)


def _ssd_chunk(S, xs, dt_raw, dt_bias, alog_e, d_e, Bm, Cm, E):
    causal, eye_t, tril, triu, blockdiag = _ssd_consts()
    ones = jnp.ones((CHUNK, CHUNK), BF16)
    dt = _softplus(dt_raw + dt_bias)
    dtx = _sel_right(dt, E[0], E[1])
    a = dtx * (-jnp.exp(alog_e))
    acs = _sel_left(tril, triu, a)
    rowv = _sel_left(ones, ones, acs * eye_t)
    seg = acs - rowv
    Lc = jnp.where(causal, jnp.exp(jnp.where(causal, seg, 0.0)), 0.0)
    xdt = xs * dtx
    Bt = jnp.concatenate([Bm] * 4, axis=0)
    CBc = _bdot(Cm, Bt, "nt")
    Xbd = jnp.where(blockdiag, jnp.concatenate([xdt] * 4, axis=0), 0.0)
    y_intra = _bdot(CBc * Lc, Xbd, "nn")
    y_inter = _bdot(Cm, S, "nn") * jnp.exp(acs)
    y = y_intra + y_inter + d_e * xs
    last = jnp.sum(a, axis=0, keepdims=True)
    dec_end = jnp.exp(last - acs)
    S_new = S * jnp.exp(last) + _bdot(Bm, xdt * dec_end, "tn")
    return S_new, y


SSD_GPS = 8
SSD_W = SSD_GPS * GROUP_W
SSD_N = SSD_GPS * D_STATE
SSD_STEPS = SSM_GROUPS // SSD_GPS


def _ssd_in_specs(cmap):
    return [
        pl.BlockSpec((CHUNK, SSD_W), lambda c, g: (cmap(c), g)),
        pl.BlockSpec((CHUNK, DT_PAD), lambda c, g: (cmap(c), DT_COL)),
        pl.BlockSpec((1, DT_PAD), lambda c, g: (0, 0)),
        pl.BlockSpec((1, SSD_W), lambda c, g: (0, g)),
        pl.BlockSpec((1, SSD_W), lambda c, g: (0, g)),
        pl.BlockSpec((CHUNK, SSD_N), lambda c, g: (cmap(c), B_COL0 // SSD_GPS + g)),
        pl.BlockSpec((CHUNK, SSD_N), lambda c, g: (cmap(c), C_COL0 // SSD_GPS + g)),
    ]


def _gw(u):
    return slice(u * GROUP_W, (u + 1) * GROUP_W)


def _gn(u):
    return slice(u * D_STATE, (u + 1) * D_STATE)


def _ssd_fwd(xbc, zx, dt_bias, alog_e, d_e, name, travel=None):
    L = xbc.shape[0]
    nc = L // CHUNK
    n_in = 7 + (travel is not None)

    def body(*refs):
        xs_ref, dtr_ref, bias_ref, alog_ref, d_ref, b_ref, c_ref = refs[:7]
        y_ref, st_ref = refs[n_in:n_in + 2]
        S_ref = refs[n_in + 2 + (travel is not None)]
        c = pl.program_id(0)
        step = pl.program_id(1)
        g0 = step * SSD_GPS if SSD_STEPS > 1 else 0
        if travel is not None:
            moves = lambda: _gather_moves(refs[7], refs[n_in + 2], *refs[n_in + 4:])
            pl.when(jnp.logical_and(c == 0, step == 0))(lambda: moves().start())

        @pl.when(c == 0)
        def _():
            for u in range(SSD_GPS):
                S_ref[g0 + u] = jnp.zeros((D_STATE, GROUP_W), F32)

        dtr, bias = dtr_ref[...], bias_ref[...]
        for u in range(SSD_GPS):
            S = S_ref[g0 + u]
            st_ref[u] = S
            S_new, y = _ssd_chunk(S, xs_ref[:, _gw(u)], dtr, bias, alog_ref[:, _gw(u)], d_ref[:, _gw(u)],
                                  b_ref[:, _gn(u)], c_ref[:, _gn(u)], _ssd_expand(g0 + u))
            y_ref[:, _gw(u)] = y
            S_ref[g0 + u] = S_new

        if travel is not None:
            pl.when(jnp.logical_and(c == nc - 1, step == SSD_STEPS - 1))(lambda: moves().wait())

    extra = travel is not None
    return pl.pallas_call(
        body,
        name=name,
        grid=(nc, SSD_STEPS),
        in_specs=_ssd_in_specs(lambda c: c) + [ANY] * extra,
        out_specs=[
            pl.BlockSpec((CHUNK, SSD_W), lambda c, g: (c, g)),
            pl.BlockSpec((None, SSD_GPS, D_STATE, GROUP_W), lambda c, g: (c, g, 0, 0)),
        ] + [ANY] * extra,
        out_shape=[
            jax.ShapeDtypeStruct((L, D_INNER), F32),
            jax.ShapeDtypeStruct((nc, SSM_GROUPS, D_STATE, GROUP_W), F32),
        ] + ([jax.ShapeDtypeStruct((N_CHIPS,) + travel.shape, travel.dtype)] if extra else []),
        scratch_shapes=[pltpu.VMEM((SSM_GROUPS, D_STATE, GROUP_W), F32)]
        + ((_MOVE_SEMS + [pltpu.SemaphoreType.DMA]) if extra else []),
        compiler_params=_cparams(("arbitrary", "arbitrary")),
    )(xbc, zx, dt_bias, alog_e, d_e, xbc, xbc, *([travel] if extra else []))


def _ssd_bwd(xbc, zx, dt_bias, alog_e, d_e, states, dy, name, travel=None):
    assert SSD_STEPS == 1, "one grid step writes the whole d (xs | B | C) row block"
    L = xbc.shape[0]
    nc = L // CHUNK
    rev = lambda c: nc - 1 - c
    n_in = 9 + (travel is not None)

    def body(*refs):
        xs_ref, dtr_ref, bias_ref, alog_ref, d_ref, b_ref, c_ref, st_ref, dy_ref = refs[:9]
        dact_ref, ddt_ref, dbias_ref, dalog_ref, dd_ref = refs[n_in:n_in + 5]
        dS_ref = refs[n_in + 5 + (travel is not None)]
        c = pl.program_id(0)
        step = pl.program_id(1)
        g0 = 0
        first = jnp.logical_and(c == 0, step == 0)
        if travel is not None:
            moves = lambda: _scatter_moves(refs[9], refs[n_in + 5], *refs[n_in + 7:])
            pl.when(first)(lambda: moves().start())

        @pl.when(c == 0)
        def _():
            for u in range(SSD_GPS):
                dS_ref[g0 + u] = jnp.zeros((D_STATE, GROUP_W), F32)
                dalog_ref[g0 + u] = jnp.zeros((1, GROUP_W), F32)
                dd_ref[g0 + u] = jnp.zeros((1, GROUP_W), F32)

        @pl.when(first)
        def _():
            dbias_ref[...] = jnp.zeros_like(dbias_ref)

        dtr, bias = dtr_ref[...], bias_ref[...]
        ddt_sum = jnp.zeros((CHUNK, DT_PAD), F32)
        dbias_sum = jnp.zeros((1, DT_PAD), F32)
        for u in range(SSD_GPS):
            E = _ssd_expand(g0 + u)
            _, vjp = jax.vjp(
                lambda S, xs, dtr_, bias_, alog, dsk, Bm, Cm: _ssd_chunk(S, xs, dtr_, bias_, alog, dsk, Bm, Cm, E),
                st_ref[u], xs_ref[:, _gw(u)], dtr, bias, alog_ref[:, _gw(u)], d_ref[:, _gw(u)],
                b_ref[:, _gn(u)], c_ref[:, _gn(u)])
            dS, dxs, ddtr, dbias, dalog, dd, dB, dC = vjp((dS_ref[g0 + u], dy_ref[:, _gw(u)]))
            dS_ref[g0 + u] = dS
            dact_ref[:, _gw(u)] = dxs
            dact_ref[:, slice(D_INNER + u * D_STATE, D_INNER + (u + 1) * D_STATE)] = dB
            dact_ref[:, slice(D_INNER + SSD_N + u * D_STATE, D_INNER + SSD_N + (u + 1) * D_STATE)] = dC
            dalog_ref[g0 + u] += dalog
            dd_ref[g0 + u] += dd
            ddt_sum = ddt_sum + ddtr
            dbias_sum = dbias_sum + dbias

        ddt_ref[...] = ddt_sum
        dbias_ref[...] += dbias_sum
        if travel is not None:
            pl.when(jnp.logical_and(c == nc - 1, step == SSD_STEPS - 1))(lambda: moves().wait())

    extra = travel is not None
    whole3 = pl.BlockSpec((SSM_GROUPS, 1, GROUP_W), lambda c, g: (0, 0, 0))
    return pl.pallas_call(
        body,
        name=name,
        grid=(nc, SSD_STEPS),
        in_specs=_ssd_in_specs(rev) + [
            pl.BlockSpec((None, SSD_GPS, D_STATE, GROUP_W), lambda c, g: (rev(c), g, 0, 0)),
            pl.BlockSpec((CHUNK, SSD_W), lambda c, g: (rev(c), g)),
        ] + [ANY] * extra,
        out_specs=[
            pl.BlockSpec((CHUNK, CONV_DIM), lambda c, g: (rev(c), 0)),
            pl.BlockSpec((CHUNK, DT_PAD), lambda c, g: (rev(c), 0)),
            pl.BlockSpec((1, DT_PAD), lambda c, g: (0, 0)),
            whole3,
            whole3,
        ] + [ANY] * extra,
        out_shape=[
            jax.ShapeDtypeStruct((L, CONV_DIM), F32),
            jax.ShapeDtypeStruct((L, DT_PAD), F32),
            jax.ShapeDtypeStruct((1, DT_PAD), F32),
            jax.ShapeDtypeStruct((SSM_GROUPS, 1, GROUP_W), F32),
            jax.ShapeDtypeStruct((SSM_GROUPS, 1, GROUP_W), F32),
        ] + ([jax.ShapeDtypeStruct((N_CHIPS - 1,) + travel.shape[1:], travel.dtype)] if extra else []),
        scratch_shapes=[pltpu.VMEM((SSM_GROUPS, D_STATE, GROUP_W), F32)] + (_MOVE_SEMS if extra else []),
        compiler_params=_cparams(("arbitrary", "arbitrary")),
    )(xbc, zx, dt_bias, alog_e, d_e, xbc, xbc, states, dy, *([travel] if extra else []))


LOG2E = math.log2(math.e)
LN2 = math.log(2.0)
SB_Q_SCALE = SB_D ** -0.5 * LOG2E


def _sb_scores(q, kj):
    z2 = _dot(q, kj, dims=(((1,), (1,)), ((), ())))
    lb = jnp.minimum(z2, 0.0) - jnp.log2(1.0 + jnp.exp2(-jnp.abs(z2)))
    return lb, lb - z2


SB_TQ_FWD = 512
SB_TQ_BWD = 256
SB_TK = 512
SB_TC = SB_TK // 2


def _sb_after():
    j = lax.broadcasted_iota(jnp.int32, (2 * SB_TC, SB_TC), 0) % SB_TC
    s = lax.broadcasted_iota(jnp.int32, (2 * SB_TC, SB_TC), 1)
    return (j > s).astype(BF16)


def _sb_suffix(x, after):
    T = x.shape[0]
    hi = x.astype(BF16)
    lo = (x - hi.astype(F32)).astype(BF16)
    stacked = jnp.concatenate([jnp.concatenate([hi[:, :SB_TC], lo[:, :SB_TC]], axis=1),
                               jnp.concatenate([hi[:, SB_TC:], lo[:, SB_TC:]], axis=1)], axis=0)
    out = _dot(stacked, after)
    rs_right = jnp.sum(x[:, SB_TC:], axis=1, keepdims=True)
    return (jnp.concatenate([out[:T] + rs_right, out[T:]], axis=1),
            rs_right + jnp.sum(x[:, :SB_TC], axis=1, keepdims=True))


SB_ROWS_FWD = 256
SB_ROWS_BWD = 128


def _sb_seen(rows, row0, start):
    t = lax.broadcasted_iota(jnp.int32, (rows, SB_TK), 0) + row0
    s = lax.broadcasted_iota(jnp.int32, (rows, SB_TK), 1) + start
    return s < t


def _sb_walk(gd, groups, carry):
    carry = lax.cond(gd % 2 == 1, lambda c: groups([gd, gd - 1], c, True), lambda c: groups([gd], c, True), carry)
    base = gd - 1 - gd % 2
    return lax.fori_loop(0, gd // 2, lambda n, c: groups([base - 2 * n, base - 2 * n - 1], c, False), carry)


def _sb_fwd(qs, kn, v, name):
    H, L, d = qs.shape
    SB_TQ, SB_ROWS = SB_TQ_FWD, SB_ROWS_FWD
    nq = L // SB_TQ
    after = _sb_after()

    def body(q_ref, k_ref, v_ref, aft_ref, o_ref):
        i = pl.program_id(1)
        gd = (i * SB_TQ) // SB_TK
        subs = range(SB_TQ // SB_ROWS)
        qr = [q_ref[r * SB_ROWS:(r + 1) * SB_ROWS, :] for r in subs]

        def prep(G, diag, r):
            start = pl.multiple_of(G * SB_TK, SB_TK)
            lb, lk = _sb_scores(qr[r], k_ref[pl.ds(start, SB_TK), :])
            seen = _sb_seen(SB_ROWS, i * SB_TQ + r * SB_ROWS, start) if diag else None
            if diag:
                lk = jnp.where(seen, lk, 0.0)
            between, rs = _sb_suffix(lk, aft_ref[...])
            return lb + between, rs, seen, v_ref[pl.ds(start, SB_TK), :]

        def groups(Gs, carry, diag):
            carry = list(carry)
            pre = [[prep(G, diag and n == 0, r) for n, G in enumerate(Gs)] for r in subs]
            for r in subs:
                R, acc = carry[r]
                for s, rs, seen, vg in pre[r]:
                    w = jnp.exp2(s + R)
                    if seen is not None:
                        w = jnp.where(seen, w, 0.0)
                    acc = acc + _dot(w.astype(BF16), vg)
                    R = R + rs
                carry[r] = (R, acc)
            return tuple(carry)

        zero = (jnp.zeros((SB_ROWS, 1), F32), jnp.zeros((SB_ROWS, d), F32))
        out = _sb_walk(gd, groups, tuple(zero for _ in subs))
        for r in subs:
            o_ref[r * SB_ROWS:(r + 1) * SB_ROWS, :] = out[r][1]

    return pl.pallas_call(
        body,
        name=name,
        grid=(H, nq),
        in_specs=[
            pl.BlockSpec((None, SB_TQ, d), lambda h, i: (h, i, 0)),
            pl.BlockSpec((None, L, d), lambda h, i: (h, 0, 0)),
            pl.BlockSpec((None, L, d), lambda h, i: (h, 0, 0)),
            pl.BlockSpec((2 * SB_TC, SB_TC), lambda h, i: (0, 0)),
        ],
        out_specs=pl.BlockSpec((None, SB_TQ, d), lambda h, i: (h, i, 0)),
        out_shape=jax.ShapeDtypeStruct((H, L, d), F32),
        compiler_params=_cparams(("parallel", "arbitrary")),
    )(qs, kn, v, after)


def _sb_bwd(qs, kn, v, o, do, name):
    H, L, d = qs.shape
    SB_TQ, SB_ROWS = SB_TQ_BWD, SB_ROWS_BWD
    nq = L // SB_TQ
    after = _sb_after()
    qsT = qs.transpose(0, 2, 1)
    do16 = do.astype(BF16)
    doT = do16.transpose(0, 2, 1)

    def body(q_ref, qT_ref, k_ref, v_ref, o_ref, do_ref, doT_ref, aft_ref, dq_ref, dkT_ref, dvT_ref):
        i = pl.program_id(1)

        @pl.when(i == 0)
        def _():
            dkT_ref[...] = jnp.zeros_like(dkT_ref)
            dvT_ref[...] = jnp.zeros_like(dvT_ref)

        gd = (i * SB_TQ) // SB_TK
        subs = range(SB_TQ // SB_ROWS)
        rows = [slice(r * SB_ROWS, (r + 1) * SB_ROWS) for r in subs]
        qr = [q_ref[rows[r], :] for r in subs]
        qT = [qT_ref[:, rows[r]] for r in subs]
        do_ = [do_ref[rows[r], :] for r in subs]
        doT_ = [doT_ref[:, rows[r]] for r in subs]
        D = [jnp.sum(o_ref[rows[r], :] * do_[r].astype(F32), axis=1, keepdims=True) for r in subs]

        def prep(G, diag, r):
            start = pl.multiple_of(G * SB_TK, SB_TK)
            kg = k_ref[pl.ds(start, SB_TK), :]
            lb, lk = _sb_scores(qr[r], kg)
            seen = _sb_seen(SB_ROWS, i * SB_TQ + r * SB_ROWS, start) if diag else None
            if diag:
                lk = jnp.where(seen, lk, 0.0)
            between, rs = _sb_suffix(lk, aft_ref[...])
            dw = _dot(do_[r], v_ref[pl.ds(start, SB_TK), :], dims=(((1,), (1,)), ((), ())))
            return start, kg, lb + between, rs, dw, jnp.exp2(lb), seen

        def groups(Gs_, carry, diag):
            carry = list(carry)
            pre = [[prep(G, diag and n == 0, r) for n, G in enumerate(Gs_)] for r in subs]
            for r in subs:
                R, Gs, dq = carry[r]
                for start, kg, s, rs, dw, sig, seen in pre[r]:
                    w = jnp.exp2(s + R)
                    if seen is not None:
                        w = jnp.where(seen, w, 0.0)
                    w16 = w.astype(BF16)
                    g = w16.astype(F32) * dw
                    dvT_ref[:, pl.ds(start, SB_TK)] += _dot(doT_[r], w16)
                    g_right, gsum = _sb_suffix(g, aft_ref[...])
                    dz = g - sig * (D[r] - Gs - g_right)
                    if seen is not None:
                        dz = jnp.where(seen, dz, 0.0)
                    dz16 = dz.astype(BF16)
                    dq = dq + _dot(dz16, kg)
                    dkT_ref[:, pl.ds(start, SB_TK)] += _dot(qT[r], dz16)
                    R = R + rs
                    Gs = Gs + gsum
                carry[r] = (R, Gs, dq)
            return tuple(carry)

        zero1 = jnp.zeros((SB_ROWS, 1), F32)
        out = _sb_walk(gd, groups, tuple((zero1, zero1, jnp.zeros((SB_ROWS, d), F32)) for _ in subs))
        for r in subs:
            dq_ref[rows[r], :] = out[r][2]

    blk = pl.BlockSpec((None, SB_TQ, d), lambda h, i: (h, i, 0))
    blkT = pl.BlockSpec((None, d, SB_TQ), lambda h, i: (h, 0, i))
    full = pl.BlockSpec((None, L, d), lambda h, i: (h, 0, 0))
    fullT = pl.BlockSpec((None, d, L), lambda h, i: (h, 0, 0))
    return pl.pallas_call(
        body,
        name=name,
        grid=(H, nq),
        in_specs=[blk, blkT, full, full, blk, blk, blkT, pl.BlockSpec((2 * SB_TC, SB_TC), lambda h, i: (0, 0))],
        out_specs=[blk, fullT, fullT],
        out_shape=[jax.ShapeDtypeStruct((H, L, d), F32), jax.ShapeDtypeStruct((H, d, L), F32),
                   jax.ShapeDtypeStruct((H, d, L), F32)],
        compiler_params=_cparams(("parallel", "arbitrary")),
    )(qs, qsT, kn, v, o, do16, doT, after)


def _fold_heads(rows, name):
    def body(x_ref, o_ref):
        c = lax.broadcasted_iota(jnp.int32, (D_INNER, DT_PAD), 0)
        h = lax.broadcasted_iota(jnp.int32, (D_INNER, DT_PAD), 1)
        o_ref[...] = _dot(x_ref[...], (c // HEAD_P == h).astype(F32), precision=HI)

    return pl.pallas_call(
        body, name=name, out_shape=jax.ShapeDtypeStruct((SUBLANE, DT_PAD), F32),
        compiler_params=_cparams(),
    )(rows)


def _ffn_fwd(x, r, norm_w, w_in_t, w_out, tag):
    x1, h = _norm_fwd(x, r, norm_w, f"norm_ffn_fwd{tag}")
    gu, act = _ffn_in_fused(h, w_in_t, f"ffn_in_fwd{tag}")
    f = _mm(act, w_out, "nn", F32, f"ffn_out_fwd{tag}")
    return x1, h, gu, act, f


def _ffn_bwd(x1, h, gu, act, norm_w, w_in_t, w_out, dres, tag):
    d16 = dres.astype(BF16)
    dgu = _ffn_out_dx_fused(d16, w_out, gu, f"ffn_out_dx{tag}")
    g_w_out = _mm(act, d16, "tn", F32, f"ffn_out_dw{tag}")
    dh = _mm(dgu, w_in_t, "nn", BF16, f"ffn_in_dx{tag}")
    g_w_in_t = _mm(dgu, h, "tn", F32, f"ffn_in_dw{tag}")
    dres, g_norm = _norm_bwd(x1, norm_w, dh, dres, f"norm_ffn_bwd{tag}")
    return dres, g_norm, g_w_in_t, g_w_out


def _rest_weights(gb):
    whole = _whole_of_shards(gb, SEG_REST)
    return {"ssm_w_out": whole["ssm_w_out"][0], "sb_w_qkv_t": whole["sb_w_qkv"][0], "sb_w_o": whole["sb_w_o"][0],
            "ffn_w_in_t": whole["ffn_w_in"], "ffn_w_out": whole["ffn_w_out"]}


def _local_step(x, target, W, wb_rest=None):
    L = x.shape[0]
    H = SB_HEADS
    nm0, nm1 = W["norm_mix"][0:1], W["norm_mix"][1:2]
    nf0, nf1 = W["norm_ffn"][0:1], W["norm_ffn"][1:2]

    _, h0 = _norm_fwd(x, None, nm0, "norm_mix_fwd0")
    zx = _mm(h0, W["ssm_w_in_t"], "nt", F32, "ssm_in_fwd")
    xbc = _conv_fwd(zx, W["conv_w8"], W["conv_b"], "conv_fwd")
    if wb_rest is None:
        y, states = _ssd_fwd(xbc, zx, W["dt_bias"], W["alog_e"], W["d_e"], "ssd_fwd")
    else:
        y, states, gb_rest = _ssd_fwd(xbc, zx, W["dt_bias"], W["alog_e"], W["d_e"], "ssd_fwd", travel=wb_rest)
        W = {**W, **_rest_weights(gb_rest)}
    yn = _gated_norm_fwd(y, zx, W["ssm_norm_w"], "gated_norm_fwd")
    mix0 = _mm(yn, W["ssm_w_out"], "nn", F32, "ssm_out_fwd")
    x1, h1, gu0, act0, f0 = _ffn_fwd(x, mix0, nf0, W["ffn_w_in_t"][0], W["ffn_w_out"][0], "0")

    x2, h2 = _norm_fwd(x1, f0, nm1, "norm_mix_fwd1")
    qkv = _mm(h2, W["sb_w_qkv_t"], "nt", F32, "sb_qkv_fwd")
    qkv_t = qkv.reshape(L, 3, H, SB_D).transpose(1, 2, 0, 3)
    q_r = qkv_t[0].reshape(H * L, SB_D)
    k_r = qkv_t[1].reshape(H * L, SB_D)
    qs = _qk_norm_fwd(q_r, W["sb_q_gain"], SB_Q_SCALE, "q_norm_fwd").reshape(H, L, SB_D)
    kn = _qk_norm_fwd(k_r, W["sb_k_gain"], 1.0, "k_norm_fwd").reshape(H, L, SB_D)
    vb = qkv_t[2].astype(BF16)
    o = _sb_fwd(qs, kn, vb, "sb_fwd")
    o_flat = o.transpose(1, 0, 2).reshape(L, D_MODEL).astype(BF16)
    mix1 = _mm(o_flat, W["sb_w_o"], "nn", F32, "sb_o_fwd")
    x3, h3, gu1, act1, f1 = _ffn_fwd(x2, mix1, nf1, W["ffn_w_in_t"][1], W["ffn_w_out"][1], "1")

    dres, sq = _loss_bwd(x3, f1, target, "loss")

    dres, g_nf1, g_fin1, g_fout1 = _ffn_bwd(x3, h3, gu1, act1, nf1, W["ffn_w_in_t"][1], W["ffn_w_out"][1], dres, "1")
    d16 = dres.astype(BF16)
    do_flat = _mm(d16, W["sb_w_o"], "nt", F32, "sb_o_dx")
    g_w_o = _mm(o_flat, d16, "tn", F32, "sb_o_dw")
    do = do_flat.reshape(L, H, SB_D).transpose(1, 0, 2)
    dqs, dknT, dvT = _sb_bwd(qs, kn, vb, o, do, "sb_bwd")
    dkn = dknT.transpose(0, 2, 1)
    dq, g_qg = _qk_norm_bwd(q_r, W["sb_q_gain"], dqs.reshape(H * L, SB_D), SB_Q_SCALE, LN2, "q_norm_bwd")
    dk, g_kg = _qk_norm_bwd(k_r, W["sb_k_gain"], dkn.reshape(H * L, SB_D), 1.0, LN2, "k_norm_bwd")
    dqkv = jnp.stack([dq.reshape(H, L, SB_D), dk.reshape(H, L, SB_D), dvT.transpose(0, 2, 1)])
    dqkv = dqkv.transpose(2, 0, 1, 3).reshape(L, 3 * D_MODEL).astype(BF16)
    dh2 = _mm(dqkv, W["sb_w_qkv_t"], "nn", BF16, "sb_qkv_dx")
    g_w_qkv_t = _mm(dqkv, h2, "tn", F32, "sb_qkv_dw")
    dres, g_nm1 = _norm_bwd(x2, nm1, dh2, dres, "norm_mix_bwd1")

    dres, g_nf0, g_fin0, g_fout0 = _ffn_bwd(x1, h1, gu0, act0, nf0, W["ffn_w_in_t"][0], W["ffn_w_out"][0], dres, "0")
    d16 = dres.astype(BF16)
    dyn = _mm(d16, W["ssm_w_out"], "nt", BF16, "ssm_out_dx")
    g_w_out = _mm(yn, d16, "tn", F32, "ssm_out_dw")
    dy, dz, g_snw = _gated_norm_bwd(y, zx, W["ssm_norm_w"], dyn, "gated_norm_bwd")
    rest = {"ssm_w_out": [g_w_out], "sb_w_qkv": [g_w_qkv_t], "sb_w_o": [g_w_o],
            "ffn_w_in": [g_fin0, g_fin1], "ffn_w_out": [g_fout0, g_fout1]}
    if wb_rest is None:
        dact, ddt, g_dtb, g_alog_e, g_d_e = _ssd_bwd(
            xbc, zx, W["dt_bias"], W["alog_e"], W["d_e"], states, dy, "ssd_bwd")
        exchanged = None
    else:
        slots = jnp.stack([_shard_of_whole(rest, s, SEG_REST) for s in range(N_CHIPS)])
        dact, ddt, g_dtb, g_alog_e, g_d_e, received = _ssd_bwd(
            xbc, zx, W["dt_bias"], W["alog_e"], W["d_e"], states, dy, "ssd_bwd", travel=slots.astype(BF16))
        exchanged = (slots, received)
    dxbc, g_cw8, g_cb = _conv_bwd(zx, W["conv_w8"], W["conv_b"], dact, "conv_bwd")
    dzx = jnp.concatenate([dz, dxbc, ddt.astype(BF16)], axis=1)
    dh0 = _mm(dzx, W["ssm_w_in_t"], "nn", BF16, "ssm_in_dx")
    g_w_in_t = _mm(dzx, h0, "tn", F32, "ssm_in_dw")
    grad_x, g_nm0 = _norm_bwd(x, nm0, dh0, dres, "norm_mix_bwd0")

    per_head = jnp.concatenate(
        [g_alog_e.reshape(1, D_INNER), g_d_e.reshape(1, D_INNER), jnp.zeros((SUBLANE - 2, D_INNER), F32)], axis=0)
    folded = _fold_heads(per_head, "fold_heads")
    grads = {
        "norm_mix": jnp.concatenate([g_nm0, g_nm1], axis=0),
        "norm_ffn": jnp.concatenate([g_nf0, g_nf1], axis=0),
        "ssm_w_in": [g_w_in_t],
        "ssm_conv_w": [g_cw8],
        "ssm_conv_b": g_cb,
        "ssm_dt_bias": g_dtb[:, :SSM_HEADS],
        "ssm_a_log": folded[0:1, :SSM_HEADS],
        "ssm_d": folded[1:2, :SSM_HEADS],
        "ssm_norm_w": g_snw,
        "sb_q_gain": g_qg,
        "sb_k_gain": g_kg,
        **rest,
    }
    return (sq, grad_x, grads) if wb_rest is None else (sq, grad_x, grads, exchanged)


def _prep_weights(full):
    W = _prep_vectors(full)
    w_in_t = full["ssm_w_in"].T
    W["ssm_w_in_t"] = jnp.pad(w_in_t, ((0, D_IN_PAD - D_IN_PROJ), (0, 0))).astype(BF16)
    W["conv_w8"] = jnp.pad(full["ssm_conv_w"], ((0, SUBLANE - CONV_W), (0, 0)))
    W["ssm_w_out"] = full["ssm_w_out"].astype(BF16)
    W["sb_w_qkv_t"] = full["sb_w_qkv"].T.astype(BF16)
    W["sb_w_o"] = full["sb_w_o"].astype(BF16)
    W["ffn_w_in_t"] = jnp.swapaxes(full["ffn_w_in"], 1, 2).astype(BF16)
    W["ffn_w_out"] = full["ffn_w_out"].astype(BF16)
    return W


def _prep_vectors(full):
    W = {}
    W["norm_mix"] = full["norm_mix"]
    W["norm_ffn"] = full["norm_ffn"]
    W["conv_b"] = full["ssm_conv_b"]
    W["dt_bias"] = jnp.pad(full["ssm_dt_bias"], ((0, 0), (0, DT_PAD - SSM_HEADS)))
    W["alog_e"] = jnp.repeat(full["ssm_a_log"], HEAD_P, axis=1)
    W["d_e"] = jnp.repeat(full["ssm_d"], HEAD_P, axis=1)
    W["ssm_norm_w"] = full["ssm_norm_w"]
    W["sb_q_gain"] = full["sb_q_gain"]
    W["sb_k_gain"] = full["sb_k_gain"]
    return W


N_CHIPS = 4
N_DEV = 8
FLAT_W = D_MODEL

SEG_FIRST = (
    ("ssm_w_in", (1, D_MODEL, D_IN_PROJ // N_CHIPS), True),
    ("ssm_conv_w", (1, CONV_W, CONV_DIM // N_CHIPS), False),
)
SEG_REST = (
    ("ssm_w_out", (1, D_INNER // N_CHIPS, D_MODEL), False),
    ("sb_w_qkv", (1, D_MODEL, 3 * D_MODEL // N_CHIPS), True),
    ("sb_w_o", (1, D_MODEL // N_CHIPS, D_MODEL), False),
    ("ffn_w_in", (2, D_MODEL, 2 * D_FF // N_CHIPS), True),
    ("ffn_w_out", (2, D_FF // N_CHIPS, D_MODEL), False),
)
SHARDED = SEG_FIRST + SEG_REST
REPLICATED = (
    ("norm_mix", (2, D_MODEL)), ("norm_ffn", (2, D_MODEL)), ("ssm_conv_b", (1, CONV_DIM)),
    ("ssm_norm_w", (1, D_INNER)), ("ssm_dt_bias", (1, SSM_HEADS)), ("ssm_a_log", (1, SSM_HEADS)),
    ("ssm_d", (1, SSM_HEADS)), ("sb_q_gain", (1, SB_D)), ("sb_k_gain", (1, SB_D)),
)
ADAM_TILE = 256
_USED_SMALL = sum(math.prod(s) for _, s in REPLICATED)
SMALL_ROWS = -(-_USED_SMALL // (SUBLANE * FLAT_W)) * SUBLANE


def _flat_rows(shape):
    return math.prod(shape) // FLAT_W


def _seg_rows(seg):
    used = sum(_flat_rows(s) for _, s, _ in seg)
    return used, -(-used // ADAM_TILE) * ADAM_TILE


def _pad_rows(parts, seg, dtype):
    used, rows = _seg_rows(seg)
    return parts + [jnp.zeros((rows - used, FLAT_W), dtype)] if rows > used else parts


def _pack_shard(d, dtype, seg):
    parts = []
    for n, _, transposed in seg:
        a = jnp.swapaxes(d[n], 1, 2) if transposed else d[n]
        parts.append(a.reshape(-1, FLAT_W).astype(dtype))
    return jnp.concatenate(_pad_rows(parts, seg, dtype), axis=0)


def _unpack_shard(flat, seg):
    out, r = {}, 0
    for n, s, transposed in seg:
        k = _flat_rows(s)
        a = flat[r:r + k]
        out[n] = jnp.swapaxes(a.reshape(s[0], s[2], s[1]), 1, 2) if transposed else a.reshape(s)
        r += k
    return out


def _shard_of_whole(whole, chip, seg):
    parts = []
    for n, s, _ in seg:
        k = _flat_rows(s) // s[0]
        if n == "ssm_conv_w":
            parts.append(whole[n][0][:CONV_W, chip * FLAT_W:(chip + 1) * FLAT_W])
        else:
            parts.extend(a[chip * k:(chip + 1) * k] for a in whole[n])
    return jnp.concatenate(_pad_rows(parts, seg, parts[0].dtype), axis=0)


def _whole_of_shards(gb, seg):
    out, r = {}, 0
    for n, s, _ in seg:
        k = _flat_rows(s) // s[0]
        if n != "ssm_conv_w":
            out[n] = [jnp.concatenate([gb[p, r + l * k:r + (l + 1) * k] for p in range(N_CHIPS)], axis=0)
                      for l in range(s[0])]
        r += k * s[0]
    return out


def _pack_small(d):
    parts = [d[n].reshape(-1) for n, _ in REPLICATED]
    parts.append(jnp.zeros((SMALL_ROWS * FLAT_W - _USED_SMALL,), parts[0].dtype))
    return jnp.concatenate(parts).reshape(SMALL_ROWS, FLAT_W)


def _unpack_small(flat):
    flat = flat.reshape(-1)
    out, r = {}, 0
    for n, s in REPLICATED:
        k = math.prod(s)
        out[n] = flat[r:r + k].reshape(s)
        r += k
    return out


ANY = pl.BlockSpec(memory_space=pl.ANY)


def _other_chips():
    x, y, c = lax.axis_index("x"), lax.axis_index("y"), lax.axis_index("c")
    return x, y, c, [(1 - x, y), (x, 1 - y), (1 - x, 1 - y)]


def _remote(src, dst, send_sem, recv_sem, dev):
    return pltpu.make_async_remote_copy(src_ref=src, dst_ref=dst, send_sem=send_sem, recv_sem=recv_sem,
                                        device_id=dev, device_id_type=MESH_T)


def _gather_weights(wb, ws):
    def body(wb_ref, ws_ref, gb_ref, gs_ref, send_sems, recv_sems, loc_sems):
        x, y, c, others = _other_chips()
        me = 2 * x + y
        pairs = ((wb_ref, gb_ref), (ws_ref, gs_ref))
        local = [pltpu.make_async_copy(src, dst.at[me], loc_sems.at[t]) for t, (src, dst) in enumerate(pairs)]
        for cp in local:
            cp.start()
        sends = []
        for k, (px, py) in enumerate(others):
            for t, (src, dst) in enumerate(pairs):
                cp = _remote(src, dst.at[me], send_sems.at[2 * k + t], recv_sems.at[2 * k + t], (px, py, c))
                cp.start()
                sends.append(cp)
        for k, (px, py) in enumerate(others):
            for t, (src, dst) in enumerate(pairs):
                _remote(src, dst.at[2 * px + py], send_sems.at[2 * k + t], recv_sems.at[2 * k + t],
                        (px, py, c)).wait_recv()
        for cp in sends:
            cp.wait_send()
        for cp in local:
            cp.wait()

    return pl.pallas_call(
        body, name="gather_weights",
        in_specs=[ANY, ANY], out_specs=[ANY, ANY],
        out_shape=[jax.ShapeDtypeStruct((N_CHIPS,) + wb.shape, wb.dtype),
                   jax.ShapeDtypeStruct((N_CHIPS,) + ws.shape, ws.dtype)],
        scratch_shapes=[pltpu.SemaphoreType.DMA((6,)), pltpu.SemaphoreType.DMA((6,)), pltpu.SemaphoreType.DMA((2,))],
    )(wb, ws)


def _scatter_grads(gpack):
    def body(g_ref, r_ref, send_sems, recv_sems):
        x, y, c, others = _other_chips()
        sends = []
        for k, (px, py) in enumerate(others):
            cp = _remote(g_ref.at[2 * px + py], r_ref.at[k], send_sems.at[k], recv_sems.at[k], (px, py, c))
            cp.start()
            sends.append(cp)
        for cp in sends:
            cp.wait_recv()
        for cp in sends:
            cp.wait_send()

    return pl.pallas_call(
        body, name="scatter_grads",
        in_specs=[ANY], out_specs=ANY,
        out_shape=jax.ShapeDtypeStruct((N_CHIPS - 1,) + gpack.shape[1:], gpack.dtype),
        scratch_shapes=[pltpu.SemaphoreType.DMA((3,)), pltpu.SemaphoreType.DMA((3,))],
    )(gpack)


class _Moves:
    def __init__(self, sends, recvs, local=None):
        self.sends, self.recvs, self.local = sends, recvs, local

    def start(self):
        if self.local is not None:
            self.local.start()
        for cp in self.sends:
            cp.start()

    def wait(self):
        for cp in self.recvs:
            cp.wait_recv()
        for cp in self.sends:
            cp.wait_send()
        if self.local is not None:
            self.local.wait()


def _gather_moves(wb_ref, gb_ref, send_sems, recv_sems, loc_sem):
    x, y, c, others = _other_chips()
    me = 2 * x + y
    sends = [_remote(wb_ref, gb_ref.at[me], send_sems.at[k], recv_sems.at[k], (px, py, c))
             for k, (px, py) in enumerate(others)]
    recvs = [_remote(wb_ref, gb_ref.at[2 * px + py], send_sems.at[k], recv_sems.at[k], (px, py, c))
             for k, (px, py) in enumerate(others)]
    return _Moves(sends, recvs, pltpu.make_async_copy(wb_ref, gb_ref.at[me], loc_sem))


def _scatter_moves(g_ref, r_ref, send_sems, recv_sems):
    x, y, c, others = _other_chips()
    sends = [_remote(g_ref.at[2 * px + py], r_ref.at[k], send_sems.at[k], recv_sems.at[k], (px, py, c))
             for k, (px, py) in enumerate(others)]
    return _Moves(sends, sends)


_MOVE_SEMS = [pltpu.SemaphoreType.DMA((N_CHIPS - 1,)), pltpu.SemaphoreType.DMA((N_CHIPS - 1,))]


def _exchange_partials(part_a, part_b, small):
    def body(pa_ref, pb_ref, s_ref, psa_ref, psb_ref, sa_ref, send_sems, recv_sems, loc_sem):
        x, y, c = lax.axis_index("x"), lax.axis_index("y"), lax.axis_index("c")
        me = 4 * x + 2 * y + c
        local = pltpu.make_async_copy(s_ref, sa_ref.at[me], loc_sem)
        local.start()
        sib = _remote(pa_ref, psa_ref, send_sems.at[0], recv_sems.at[0], (x, y, 1 - c))
        sib_b = _remote(pb_ref, psb_ref, send_sems.at[N_DEV], recv_sems.at[N_DEV], (x, y, 1 - c))
        sib.start()
        sib_b.start()
        sends = [sib, sib_b]
        for k in range(1, N_DEV):
            fx, fy, fc = (k >> 2) & 1, (k >> 1) & 1, k & 1
            px, py, pc = x ^ fx, y ^ fy, c ^ fc
            cp = _remote(s_ref, sa_ref.at[me], send_sems.at[k], recv_sems.at[k], (px, py, pc))
            cp.start()
            sends.append(cp)
        sib.wait_recv()
        sib_b.wait_recv()
        for k in range(1, N_DEV):
            fx, fy, fc = (k >> 2) & 1, (k >> 1) & 1, k & 1
            px, py, pc = x ^ fx, y ^ fy, c ^ fc
            _remote(s_ref, sa_ref.at[4 * px + 2 * py + pc], send_sems.at[k], recv_sems.at[k], (px, py, pc)).wait_recv()
        for cp in sends:
            cp.wait_send()
        local.wait()

    return pl.pallas_call(
        body, name="exchange_partials",
        in_specs=[ANY, ANY, ANY], out_specs=[ANY, ANY, ANY],
        out_shape=[jax.ShapeDtypeStruct(part_a.shape, part_a.dtype),
                   jax.ShapeDtypeStruct(part_b.shape, part_b.dtype),
                   jax.ShapeDtypeStruct((N_DEV,) + small.shape, small.dtype)],
        scratch_shapes=[pltpu.SemaphoreType.DMA((N_DEV + 1,)), pltpu.SemaphoreType.DMA((N_DEV + 1,)),
                        pltpu.SemaphoreType.DMA],
    )(part_a, part_b, small)


def _partial_sum(own, recv, name):
    R = own.shape[0]

    def body(o_ref, r_ref, p_ref):
        acc = o_ref[...]
        for k in range(N_CHIPS - 1):
            acc = acc + r_ref[k].astype(F32)
        p_ref[...] = acc

    return pl.pallas_call(
        body, name=name, grid=(R // ADAM_TILE,),
        in_specs=[pl.BlockSpec((ADAM_TILE, FLAT_W), lambda i: (i, 0)),
                  pl.BlockSpec((N_CHIPS - 1, ADAM_TILE, FLAT_W), lambda i: (0, i, 0))],
        out_specs=pl.BlockSpec((ADAM_TILE, FLAT_W), lambda i: (i, 0)),
        out_shape=jax.ShapeDtypeStruct(own.shape, F32),
        compiler_params=_cparams(("parallel",)),
    )(own, recv)


def _adamw_math(w, g, m, v):
    m = ADAM_B1 * m + (1.0 - ADAM_B1) * g
    v = ADAM_B2 * v + (1.0 - ADAM_B2) * jnp.square(g)
    m_hat = m / (1.0 - ADAM_B1 ** ADAM_STEP)
    v_hat = v / (1.0 - ADAM_B2 ** ADAM_STEP)
    delta = -ADAM_LR * (m_hat / (jnp.sqrt(v_hat) + ADAM_EPS) + ADAM_WD * w)
    return delta, m, v


def _adamw(w, m, v, parts, tile, name):
    n, R, _ = parts.shape

    def body(w_ref, m_ref, v_ref, p_ref, g_ref, d_ref, nm_ref, nv_ref):
        g = p_ref[0]
        for k in range(1, n):
            g = g + p_ref[k]
        delta, nm, nv = _adamw_math(w_ref[...], g, m_ref[...], v_ref[...])
        g_ref[...] = g
        d_ref[...] = delta
        nm_ref[...] = nm
        nv_ref[...] = nv

    blk = pl.BlockSpec((tile, FLAT_W), lambda i: (i, 0))
    return pl.pallas_call(
        body, name=name, grid=(R // tile,),
        in_specs=[blk, blk, blk, pl.BlockSpec((n, tile, FLAT_W), lambda i: (0, i, 0))],
        out_specs=[blk] * 4,
        out_shape=[jax.ShapeDtypeStruct((R, FLAT_W), F32)] * 4,
        compiler_params=_cparams(("parallel",)),
    )(w, m, v, parts)


def _add2(a, b, name):
    def body(a_ref, b_ref, o_ref):
        o_ref[...] = a_ref[...] + b_ref[...]

    blk = pl.BlockSpec((ADAM_TILE, FLAT_W), lambda i: (i, 0))
    return pl.pallas_call(
        body, name=name, grid=(a.shape[0] // ADAM_TILE,), in_specs=[blk, blk], out_specs=blk,
        out_shape=jax.ShapeDtypeStruct(a.shape, F32), compiler_params=_cparams(("parallel",)),
    )(a, b)


def _adamw_block(w, m, v, g, name):
    shape = w.shape
    R, C = math.prod(shape[:-1]), shape[-1]
    tile = _tile(R, ADAM_TILE, SUBLANE)

    def body(w_ref, m_ref, v_ref, g_ref, d_ref, nm_ref, nv_ref):
        delta, nm, nv = _adamw_math(w_ref[...], g_ref[...], m_ref[...], v_ref[...])
        d_ref[...] = delta
        nm_ref[...] = nm
        nv_ref[...] = nv

    blk = pl.BlockSpec((tile, C), lambda i: (i, 0))
    outs = pl.pallas_call(
        body, name=name, grid=(R // tile,), in_specs=[blk] * 4, out_specs=[blk] * 3,
        out_shape=[jax.ShapeDtypeStruct((R, C), F32)] * 3, compiler_params=_cparams(("parallel",)),
    )(*(a.reshape(R, C) for a in (w, m, v, g)))
    return tuple(a.reshape(shape) for a in outs)


_NAMES = ("norm_mix", "norm_ffn", "ssm_w_in", "ssm_conv_w", "ssm_conv_b", "ssm_dt_bias", "ssm_a_log", "ssm_d",
          "ssm_norm_w", "ssm_w_out", "sb_w_qkv", "sb_q_gain", "sb_k_gain", "sb_w_o", "ffn_w_in", "ffn_w_out")


def _step(x, loss_target, w, m, v):
    cx, cy, cc = lax.axis_index("x"), lax.axis_index("y"), lax.axis_index("c")
    chip = 2 * cx + cy

    conv8 = jnp.pad(w["ssm_conv_w"][0], ((0, SUBLANE - CONV_W), (0, 0)))
    gb, gs = _gather_weights(_pack_shard(w, BF16, SEG_FIRST), conv8)
    W = _prep_vectors(w)
    W["ssm_w_in_t"] = jnp.pad(_whole_of_shards(gb, SEG_FIRST)["ssm_w_in"][0], ((0, D_IN_PAD - D_IN_PROJ), (0, 0)))
    W["conv_w8"] = jnp.concatenate([gs[p] for p in range(N_CHIPS)], axis=1)

    sq, grad_x, grads, (slots_rest, recv_rest) = _local_step(
        x[0], loss_target[0], W, wb_rest=_pack_shard(w, BF16, SEG_REST))
    loss = lax.psum(0.5 * jnp.sum(sq) / D_MODEL, ("x", "y", "c"))

    slots_first = jnp.stack([_shard_of_whole(grads, s, SEG_FIRST) for s in range(N_CHIPS)])
    recv_first = _scatter_grads(slots_first.astype(BF16))
    own = lambda slots: lax.dynamic_index_in_dim(slots, chip, axis=0, keepdims=False)
    part_first = _partial_sum(own(slots_first), recv_first, "grad_partial_sum_first")
    part_rest = _partial_sum(own(slots_rest), recv_rest, "grad_partial_sum_rest")
    sib_first, sib_rest, small_all = _exchange_partials(part_first, part_rest, _pack_small(grads))
    g_blocks = {**_unpack_shard(_add2(part_first, sib_first, "grad_final_sum_first"), SEG_FIRST),
                **_unpack_shard(_add2(part_rest, sib_rest, "grad_final_sum_rest"), SEG_REST)}
    g_s, d_s, m_s, v_s = _adamw(_pack_small(w), _pack_small(m), _pack_small(v), small_all, SUBLANE, "adamw_replicated")

    g, d, nm, nv = (_unpack_small(a) for a in (g_s, d_s, m_s, v_s))
    for n, _, _ in SHARDED:
        g[n] = g_blocks[n]
        d[n], nm[n], nv[n] = _adamw_block(w[n], m[n], v[n], g[n], "adamw_" + n)
    return loss, grad_x[None], [[t[n] for n in _NAMES] for t in (g, d, nm, nv)]


def kernel(x, norm_mix, norm_ffn, ssm_w_in, ssm_conv_w, ssm_conv_b, ssm_dt_bias, ssm_a_log, ssm_d, ssm_norm_w, ssm_w_out, sb_w_qkv, sb_q_gain, sb_k_gain, sb_w_o, ffn_w_in, ffn_w_out, loss_target, m_norm_mix, m_norm_ffn, m_ssm_w_in, m_ssm_conv_w, m_ssm_conv_b, m_ssm_dt_bias, m_ssm_a_log, m_ssm_d, m_ssm_norm_w, m_ssm_w_out, m_sb_w_qkv, m_sb_q_gain, m_sb_k_gain, m_sb_w_o, m_ffn_w_in, m_ffn_w_out, v_norm_mix, v_norm_ffn, v_ssm_w_in, v_ssm_conv_w, v_ssm_conv_b, v_ssm_dt_bias, v_ssm_a_log, v_ssm_d, v_ssm_norm_w, v_ssm_w_out, v_sb_w_qkv, v_sb_q_gain, v_sb_k_gain, v_sb_w_o, v_ffn_w_in, v_ffn_w_out):
    w = dict(zip(_NAMES, (norm_mix, norm_ffn, ssm_w_in, ssm_conv_w, ssm_conv_b, ssm_dt_bias, ssm_a_log, ssm_d,
                          ssm_norm_w, ssm_w_out, sb_w_qkv, sb_q_gain, sb_k_gain, sb_w_o, ffn_w_in, ffn_w_out)))
    m = dict(zip(_NAMES, (m_norm_mix, m_norm_ffn, m_ssm_w_in, m_ssm_conv_w, m_ssm_conv_b, m_ssm_dt_bias, m_ssm_a_log,
                          m_ssm_d, m_ssm_norm_w, m_ssm_w_out, m_sb_w_qkv, m_sb_q_gain, m_sb_k_gain, m_sb_w_o,
                          m_ffn_w_in, m_ffn_w_out)))
    v = dict(zip(_NAMES, (v_norm_mix, v_norm_ffn, v_ssm_w_in, v_ssm_conv_w, v_ssm_conv_b, v_ssm_dt_bias, v_ssm_a_log,
                          v_ssm_d, v_ssm_norm_w, v_ssm_w_out, v_sb_w_qkv, v_sb_q_gain, v_sb_k_gain, v_sb_w_o,
                          v_ffn_w_in, v_ffn_w_out)))
    loss, grad_x, (g, d, nm, nv) = _step(x, loss_target, w, m, v)
    return (loss, grad_x, *g, *d, *nm, *nv)
```

```python
import functools
import math

import jax
import jax.numpy as jnp
from jax import lax
from jax.experimental import pallas as pl
from jax.experimental.pallas import tpu as pltpu

F32 = jnp.float32
BF16 = jnp.bfloat16

VMEM_LIMIT_BYTES = 48 * 1024 * 1024
LANE = 128
SUBLANE = 8

D_MODEL = 1024
CHUNK = 64
D_INNER = 2048
SSM_HEADS = 32
SSM_GROUPS = 8
GROUP_W = D_INNER // SSM_GROUPS
HEAD_P = 64
D_STATE = 128
CONV_W = 4
CONV_DIM = D_INNER + 2 * SSM_GROUPS * D_STATE
D_IN_PROJ = D_INNER + CONV_DIM + SSM_HEADS
DT_PAD = LANE
D_IN_PAD = D_INNER + CONV_DIM + DT_PAD
SB_HEADS = 16
SB_D = 64
D_FF = 2816
NORM_EPS = 1e-6
GATED_NORM_EPS = 1e-5

ADAM_LR = 0.001
ADAM_B1 = 0.9
ADAM_B2 = 0.999
ADAM_EPS = 1e-08
ADAM_WD = 0.01
ADAM_STEP = 10

MESH_T = pl.DeviceIdType.MESH


def _cparams(sem=None):
    return pltpu.CompilerParams(dimension_semantics=sem, vmem_limit_bytes=VMEM_LIMIT_BYTES)


def _tile(n, target, align):
    best = None
    for t in range(align, min(n, target) + 1, align):
        if n % t == 0:
            best = t
    return best or n


def _mm(a, b, mode, out_dtype, name):
    halves = a.ndim == 3
    a2 = (a.shape[1], 2 * a.shape[2]) if halves else a.shape
    if mode == "nn":
        (M, K), N = a2, b.shape[1]
    elif mode == "nt":
        (M, K), N = a2, b.shape[0]
    else:
        (K, M), N = a2, b.shape[1]
    tm = _tile(M // 2 if halves and mode == "tn" else M, 1408, LANE)
    tn = _tile(N, 1536, LANE)
    tk = _tile(K // 2 if halves and mode == "nn" else K, 1536, LANE)
    nk = K // tk
    if mode == "nn":
        a_spec = pl.BlockSpec((tm, tk), lambda i, j, k: (i, k))
        if halves:
            nkh = nk // 2
            a_spec = pl.BlockSpec((None, tm, tk), lambda i, j, k: (k // nkh, i, k % nkh))
        b_spec = pl.BlockSpec((tk, tn), lambda i, j, k: (k, j))
        dims = (((1,), (0,)), ((), ()))
    elif mode == "nt":
        a_spec = pl.BlockSpec((tm, tk), lambda i, j, k: (i, k))
        b_spec = pl.BlockSpec((tn, tk), lambda i, j, k: (j, k))
        dims = (((1,), (1,)), ((), ()))
    else:
        a_spec = pl.BlockSpec((tk, tm), lambda i, j, k: (k, i))
        if halves:
            nmh = M // tm // 2
            a_spec = pl.BlockSpec((None, tk, tm), lambda i, j, k: (i // nmh, k, i % nmh))
        b_spec = pl.BlockSpec((tk, tn), lambda i, j, k: (k, j))
        dims = (((0,), (0,)), ((), ()))

    def body(a_ref, b_ref, o_ref, acc_ref):
        k = pl.program_id(2)

        @pl.when(k == 0)
        def _():
            acc_ref[...] = jnp.zeros_like(acc_ref)

        acc_ref[...] += lax.dot_general(a_ref[...], b_ref[...], dims, preferred_element_type=F32)

        @pl.when(k == nk - 1)
        def _():
            o_ref[...] = acc_ref[...].astype(o_ref.dtype)

    return pl.pallas_call(
        body,
        name=name,
        grid=(M // tm, N // tn, nk),
        in_specs=[a_spec, b_spec],
        out_specs=pl.BlockSpec((tm, tn), lambda i, j, k: (i, j)),
        out_shape=jax.ShapeDtypeStruct((M, N), out_dtype),
        scratch_shapes=[pltpu.VMEM((tm, tn), F32)],
        compiler_params=_cparams(("parallel", "parallel", "arbitrary")),
    )(a, b)


def _rows(fn, name, tile, row_ins, const_ins, row_outs, acc_outs=(), ncol=1):
    L = row_ins[0][0].shape[0]
    tile = min(tile, L)
    nrow = L // tile
    n_ri, n_ci, n_ro, n_ao = len(row_ins), len(const_ins), len(row_outs), len(acc_outs)

    def body(*refs):
        i = pl.program_id(1)
        j = pl.program_id(0)
        ins = [r[...] for r in refs[: n_ri + n_ci]]
        outs = fn(*ins, j=j)
        o_refs = refs[n_ri + n_ci:]
        for r, v in zip(o_refs[:n_ro], outs[:n_ro]):
            r[...] = v.astype(r.dtype)

        @pl.when(i == 0)
        def _():
            for r in o_refs[n_ro:]:
                r[...] = jnp.zeros_like(r)

        for r, v in zip(o_refs[n_ro:], outs[n_ro:]):
            r[...] += v

    def rspec(bc, cf):
        return pl.BlockSpec((tile, bc), lambda j, i: (i, cf(j)))

    def cspec(r, bc, cf):
        return pl.BlockSpec((r, bc), lambda j, i: (0, cf(j)))

    in_specs = [rspec(bc, cf) for (_, bc, cf) in row_ins]
    in_specs += [cspec(a.shape[0], bc, cf) for (a, bc, cf) in const_ins]
    out_specs = [rspec(bc, cf) for (_, _, bc, cf) in row_outs]
    out_specs += [cspec(1, bc, cf) for (_, bc, cf) in acc_outs]
    out_shape = [jax.ShapeDtypeStruct((L, c), dt) for (c, dt, _, _) in row_outs]
    out_shape += [jax.ShapeDtypeStruct((1, c), F32) for (c, _, _) in acc_outs]
    res = pl.pallas_call(
        body,
        name=name,
        grid=(ncol, nrow),
        in_specs=in_specs,
        out_specs=out_specs,
        out_shape=out_shape,
        compiler_params=_cparams(("arbitrary", "arbitrary")),
    )(*[a for (a, _, _) in row_ins], *[a for (a, _, _) in const_ins])
    return res


def _zero(j):
    return 0


def _whole(a):
    return (a, a.shape[1], _zero)


def _rms(x, w, eps):
    return x * lax.rsqrt(jnp.mean(x * x, axis=-1, keepdims=True) + eps) * w


ROW_TILE = 256
COL_ROW_TILE = 1024
QK_ROW_TILE = 8192


def _norm_fwd(x, r, w, name):
    C = x.shape[1]
    if r is None:
        def fn(x_, w_, j):
            return (_rms(x_, w_, NORM_EPS),)
        (h,) = _rows(fn, name, ROW_TILE, [_whole(x)], [_whole(w)], [(C, BF16, C, _zero)])
        return x, h

    def fn(x_, r_, w_, j):
        x1 = x_ + r_
        return x1, _rms(x1, w_, NORM_EPS)

    x1, h = _rows(fn, name, ROW_TILE, [_whole(x), _whole(r)], [_whole(w)],
                  [(C, F32, C, _zero), (C, BF16, C, _zero)])
    return x1, h


def _norm_bwd(x, w, dh, dres, name):
    C = x.shape[1]

    def fn(x_, dh_, dres_, w_, j):
        _, vjp = jax.vjp(lambda a, b: _rms(a, b, NORM_EPS), x_, w_)
        dx, dw = vjp(dh_.astype(F32))
        return dres_ + dx, dw

    return _rows(fn, name, ROW_TILE, [_whole(x), _whole(dh), _whole(dres)], [_whole(w)],
                 [(C, F32, C, _zero)], [(C, C, _zero)])


def _silu(x):
    return x * jax.nn.sigmoid(x)


FF_TM = 512
FF_TN = D_FF // 2
_NT = (((1,), (1,)), ((), ()))


def _ffn_in_fused(h, w_in_t, name):
    L, K = h.shape
    tm = min(FF_TM, L)
    nb = D_FF // FF_TN

    def body(a_ref, bg_ref, bu_ref, gu_ref, act_ref):
        g = lax.dot_general(a_ref[...], bg_ref[...], _NT, preferred_element_type=F32)
        u = lax.dot_general(a_ref[...], bu_ref[...], _NT, preferred_element_type=F32)
        gu_ref[0] = g
        gu_ref[1] = u
        act_ref[...] = (_silu(g) * u).astype(act_ref.dtype)

    return pl.pallas_call(
        body,
        name=name,
        grid=(L // tm, nb),
        in_specs=[pl.BlockSpec((tm, K), lambda i, j: (i, 0)),
                  pl.BlockSpec((FF_TN, K), lambda i, j: (j, 0)),
                  pl.BlockSpec((FF_TN, K), lambda i, j: (j + nb, 0))],
        out_specs=[pl.BlockSpec((2, tm, FF_TN), lambda i, j: (0, i, j)),
                   pl.BlockSpec((tm, FF_TN), lambda i, j: (i, j))],
        out_shape=[jax.ShapeDtypeStruct((2, L, D_FF), F32), jax.ShapeDtypeStruct((L, D_FF), BF16)],
        compiler_params=_cparams(("parallel", "parallel")),
    )(h, w_in_t, w_in_t)


def _ffn_out_dx_fused(d16, w_out, gu, name):
    L, K = d16.shape
    tm = min(FF_TM, L)

    def body(a_ref, b_ref, gu_ref, dgu_ref):
        dact = lax.dot_general(a_ref[...], b_ref[...], _NT, preferred_element_type=F32)
        _, vjp = jax.vjp(lambda g, u: _silu(g) * u, gu_ref[0], gu_ref[1])
        dg, du = vjp(dact.astype(BF16).astype(F32))
        dgu_ref[0] = dg.astype(dgu_ref.dtype)
        dgu_ref[1] = du.astype(dgu_ref.dtype)

    return pl.pallas_call(
        body,
        name=name,
        grid=(L // tm, D_FF // FF_TN),
        in_specs=[pl.BlockSpec((tm, K), lambda i, j: (i, 0)),
                  pl.BlockSpec((FF_TN, K), lambda i, j: (j, 0)),
                  pl.BlockSpec((2, tm, FF_TN), lambda i, j: (0, i, j))],
        out_specs=pl.BlockSpec((2, tm, FF_TN), lambda i, j: (0, i, j)),
        out_shape=jax.ShapeDtypeStruct((2, L, D_FF), BF16),
        compiler_params=_cparams(("parallel", "parallel")),
    )(d16, w_out, gu)


def _gated_norm_fwd(y, zx, w, name):
    def fn(y_, z_, w_, j):
        return (_rms(y_ * _silu(z_), w_, GATED_NORM_EPS),)
    (yn,) = _rows(fn, name, COL_ROW_TILE,
                  [(y, GROUP_W, lambda j: j), (zx, GROUP_W, lambda j: j)], [(w, GROUP_W, lambda j: j)],
                  [(D_INNER, BF16, GROUP_W, lambda j: j)], ncol=SSM_GROUPS)
    return yn


def _gated_norm_bwd(y, zx, w, dyn, name):
    def fn(y_, z_, dyn_, w_, j):
        _, vjp = jax.vjp(lambda a, b, c: _rms(a * _silu(b), c, GATED_NORM_EPS), y_, z_, w_)
        return vjp(dyn_.astype(F32))
    cj = lambda j: j
    return _rows(fn, name, COL_ROW_TILE,
                 [(y, GROUP_W, cj), (zx, GROUP_W, cj), (dyn, GROUP_W, cj)], [(w, GROUP_W, cj)],
                 [(D_INNER, F32, GROUP_W, cj), (D_INNER, BF16, GROUP_W, cj)], [(D_INNER, GROUP_W, cj)],
                 ncol=SSM_GROUPS)


def _loss_bwd(x3, f, target, name):
    C = x3.shape[1]

    def fn(x_, f_, t_, j):
        err = (x_ + f_) - t_
        return err * (1.0 / C), jnp.sum(err * err, axis=0, keepdims=True)

    return _rows(fn, name, ROW_TILE, [_whole(x3), _whole(f), _whole(target)], [],
                 [(C, F32, C, _zero)], [(C, C, _zero)])


def _qk_norm_fwd(q, gain, scale, name):
    def fn(q_, g_, j):
        return (_rms(q_, g_, NORM_EPS) * scale,)
    (qn,) = _rows(fn, name, QK_ROW_TILE, [_whole(q)], [_whole(gain)], [(SB_D, BF16, SB_D, _zero)])
    return qn


def _qk_norm_bwd(q, gain, dqn, scale, cot_scale, name):
    def fn(q_, dqn_, g_, j):
        _, vjp = jax.vjp(lambda a, b: _rms(a, b, NORM_EPS) * scale, q_, g_)
        return vjp(dqn_ * cot_scale)
    return _rows(fn, name, QK_ROW_TILE, [_whole(q), _whole(dqn)], [_whole(gain)],
                 [(SB_D, F32, SB_D, _zero)], [(SB_D, SB_D, _zero)])


CONV_TILE = 512
CONV_BLK = 512
XBC_COL0 = D_INNER // CONV_BLK


def _shift_down(cur, prev8, k):
    if k == 0:
        return cur
    rolled = pltpu.roll(cur, k, 0)
    head_prev = pltpu.roll(prev8, k, 0)
    rid = lax.broadcasted_iota(jnp.int32, (SUBLANE, cur.shape[1]), 0)
    head = jnp.where(rid < k, head_prev, rolled[:SUBLANE])
    if cur.shape[0] == SUBLANE:
        return head
    return jnp.concatenate([head, rolled[SUBLANE:]], axis=0)


def _shift_up(cur, next8, k):
    if k == 0:
        return cur
    T = cur.shape[0]
    rolled = pltpu.roll(cur, T - k, 0)
    tail_next = pltpu.roll(next8, SUBLANE - k, 0)
    rid = lax.broadcasted_iota(jnp.int32, (SUBLANE, cur.shape[1]), 0)
    tail = jnp.where(rid >= SUBLANE - k, tail_next, rolled[T - SUBLANE:])
    return jnp.concatenate([rolled[: T - SUBLANE], tail], axis=0)


def _conv_pre(cur, prev8, w, b):
    pre = b
    for i in range(CONV_W):
        pre = pre + w[i:i + 1, :] * _shift_down(cur, prev8, CONV_W - 1 - i)
    return pre


def _conv_fwd(zx, w8, b, name):
    L = zx.shape[0]
    nrow = L // CONV_TILE
    r8 = CONV_TILE // SUBLANE

    def body(cur_ref, prev_ref, w_ref, b_ref, o_ref):
        i = pl.program_id(1)
        prev8 = jnp.where(i > 0, prev_ref[...], 0.0)
        pre = _conv_pre(cur_ref[...], prev8, w_ref[...], b_ref[...])
        o_ref[...] = _silu(pre)

    return pl.pallas_call(
        body,
        name=name,
        grid=(CONV_DIM // CONV_BLK, nrow),
        in_specs=[
            pl.BlockSpec((CONV_TILE, CONV_BLK), lambda j, i: (i, j + XBC_COL0)),
            pl.BlockSpec((SUBLANE, CONV_BLK), lambda j, i: (jnp.maximum(i * r8 - 1, 0), j + XBC_COL0)),
            pl.BlockSpec((SUBLANE, CONV_BLK), lambda j, i: (0, j)),
            pl.BlockSpec((1, CONV_BLK), lambda j, i: (0, j)),
        ],
        out_specs=pl.BlockSpec((CONV_TILE, CONV_BLK), lambda j, i: (i, j)),
        out_shape=jax.ShapeDtypeStruct((L, CONV_DIM), F32),
        compiler_params=_cparams(("arbitrary", "arbitrary")),
    )(zx, zx, w8, b)


def _conv_bwd(zx, w8, b, dact, name):
    L = zx.shape[0]
    nrow = L // CONV_TILE
    r8 = CONV_TILE // SUBLANE
    last8 = L // SUBLANE - 1

    def body(cur_ref, prev_ref, next_ref, da_ref, dan_ref, w_ref, b_ref, du_ref, dw_ref, db_ref):
        i = pl.program_id(1)
        w = w_ref[...]
        b_ = b_ref[...]
        cur = cur_ref[...]
        prev8 = jnp.where(i > 0, prev_ref[...], 0.0)
        pre = _conv_pre(cur, prev8, w, b_)
        _, vjp = jax.vjp(_silu, pre)
        (dpre,) = vjp(da_ref[...])
        nxt = next_ref[...]
        pre_n = _conv_pre(nxt, cur[CONV_TILE - SUBLANE:], w, b_)
        _, vjp_n = jax.vjp(_silu, pre_n)
        (dpre_n,) = vjp_n(dan_ref[...])
        dpre_n = jnp.where(i < nrow - 1, dpre_n, 0.0)
        du = jnp.zeros_like(cur)
        dws = []
        for k in range(CONV_W):
            wk = w[CONV_W - 1 - k:CONV_W - k, :]
            du = du + wk * _shift_up(dpre, dpre_n, k)
            dws.append(jnp.sum(dpre * _shift_down(cur, prev8, k), axis=0, keepdims=True))
        du_ref[...] = du.astype(du_ref.dtype)
        dw_tile = jnp.concatenate([dws[3], dws[2], dws[1], dws[0]] + [jnp.zeros_like(dws[0])] * 4, axis=0)

        @pl.when(i == 0)
        def _():
            dw_ref[...] = jnp.zeros_like(dw_ref)
            db_ref[...] = jnp.zeros_like(db_ref)

        dw_ref[...] += dw_tile
        db_ref[...] += jnp.sum(dpre, axis=0, keepdims=True)

    return pl.pallas_call(
        body,
        name=name,
        grid=(CONV_DIM // CONV_BLK, nrow),
        in_specs=[
            pl.BlockSpec((CONV_TILE, CONV_BLK), lambda j, i: (i, j + XBC_COL0)),
            pl.BlockSpec((SUBLANE, CONV_BLK), lambda j, i: (jnp.maximum(i * r8 - 1, 0), j + XBC_COL0)),
            pl.BlockSpec((SUBLANE, CONV_BLK), lambda j, i: (jnp.minimum((i + 1) * r8, last8), j + XBC_COL0)),
            pl.BlockSpec((CONV_TILE, CONV_BLK), lambda j, i: (i, j)),
            pl.BlockSpec((SUBLANE, CONV_BLK), lambda j, i: (jnp.minimum((i + 1) * r8, last8), j)),
            pl.BlockSpec((SUBLANE, CONV_BLK), lambda j, i: (0, j)),
            pl.BlockSpec((1, CONV_BLK), lambda j, i: (0, j)),
        ],
        out_specs=[
            pl.BlockSpec((CONV_TILE, CONV_BLK), lambda j, i: (i, j)),
            pl.BlockSpec((SUBLANE, CONV_BLK), lambda j, i: (0, j)),
            pl.BlockSpec((1, CONV_BLK), lambda j, i: (0, j)),
        ],
        out_shape=[
            jax.ShapeDtypeStruct((L, CONV_DIM), BF16),
            jax.ShapeDtypeStruct((SUBLANE, CONV_DIM), F32),
            jax.ShapeDtypeStruct((1, CONV_DIM), F32),
        ],
        compiler_params=_cparams(("arbitrary", "arbitrary")),
    )(zx, zx, zx, dact, dact, w8, b)


HI = lax.Precision.HIGHEST
XS_COL0 = 0
B_COL0 = D_INNER // D_STATE
C_COL0 = B_COL0 + SSM_GROUPS
DT_COL = (D_INNER + CONV_DIM) // DT_PAD


def _dot(a, b, dims=(((1,), (0,)), ((), ())), precision=None):
    return lax.dot_general(a, b, dims, precision=precision, preferred_element_type=F32)


_DOT_DIMS = {
    "nn": (((1,), (0,)), ((), ())),
    "nt": (((1,), (1,)), ((), ())),
    "tn": (((0,), (0,)), ((), ())),
}


@functools.partial(jax.custom_vjp, nondiff_argnums=(2,))
def _bdot(a, b, mode):
    return _dot(a.astype(BF16), b.astype(BF16), _DOT_DIMS[mode])


def _bdot_fwd(a, b, mode):
    return _bdot(a, b, mode), (a, b)


def _bdot_bwd(mode, res, g):
    a, b = res
    if mode == "nn":
        return _bdot(g, b, "nt"), _bdot(a, g, "tn")
    if mode == "nt":
        return _bdot(g, b, "nn"), _bdot(g, a, "tn")
    return _bdot(b, g, "nt"), _bdot(a, g, "nn")


_bdot.defvjp(_bdot_fwd, _bdot_bwd)


def _softplus(x):
    return jnp.maximum(x, 0.0) + jnp.log(1.0 + jnp.exp(-jnp.abs(x)))


def _split2(x):
    hi = x.astype(BF16)
    return hi, (x - hi.astype(F32)).astype(BF16)


@jax.custom_vjp
def _sel_left(m, mT, x):
    hi, lo = _split2(x)
    out = _dot(m, jnp.concatenate([hi, lo], axis=1))
    n = x.shape[1]
    return out[:, :n] + out[:, n:]


_sel_left.defvjp(lambda m, mT, x: (_sel_left(m, mT, x), (m, mT)),
                 lambda res, g: (jnp.zeros_like(res[0]), jnp.zeros_like(res[1]), _sel_left(res[1], res[0], g)))


@jax.custom_vjp
def _sel_right(x, m, mT):
    hi, lo = _split2(x)
    out = _dot(jnp.concatenate([hi, lo], axis=0), m)
    r = x.shape[0]
    return out[:r] + out[r:]


_sel_right.defvjp(lambda x, m, mT: (_sel_right(x, m, mT), (m, mT)),
                  lambda res, g: (_sel_right(g, res[1], res[0]), jnp.zeros_like(res[0]), jnp.zeros_like(res[1])))


def _ssd_consts():
    l_ = lax.broadcasted_iota(jnp.int32, (CHUNK, GROUP_W), 0)
    c_ = lax.broadcasted_iota(jnp.int32, (CHUNK, GROUP_W), 1)
    s_ = c_ % CHUNK
    causal = s_ <= l_
    eye_t = (s_ == l_).astype(F32)
    r0 = lax.broadcasted_iota(jnp.int32, (CHUNK, CHUNK), 0)
    c0 = lax.broadcasted_iota(jnp.int32, (CHUNK, CHUNK), 1)
    tril = (c0 <= r0).astype(BF16)
    triu = (r0 <= c0).astype(BF16)
    rb = lax.broadcasted_iota(jnp.int32, (GROUP_W, GROUP_W), 0) // HEAD_P
    cb = lax.broadcasted_iota(jnp.int32, (GROUP_W, GROUP_W), 1) // HEAD_P
    blockdiag = rb == cb
    return causal, eye_t, tril, triu, blockdiag


def _ssd_expand(g):
    h = lax.broadcasted_iota(jnp.int32, (DT_PAD, GROUP_W), 0)
    c = lax.broadcasted_iota(jnp.int32, (DT_PAD, GROUP_W), 1)
    cT = lax.broadcasted_iota(jnp.int32, (GROUP_W, DT_PAD), 0)
    hT = lax.broadcasted_iota(jnp.int32, (GROUP_W, DT_PAD), 1)
    hpg = GROUP_W // HEAD_P
    return (h == g * hpg + c // HEAD_P).astype(BF16), (hT == g * hpg + cT // HEAD_P).astype(BF16)


def _ssd_chunk(S, xs, dt_raw, dt_bias, alog_e, d_e, Bm, Cm, E):
    causal, eye_t, tril, triu, blockdiag = _ssd_consts()
    ones = jnp.ones((CHUNK, CHUNK), BF16)
    dt = _softplus(dt_raw + dt_bias)
    dtx = _sel_right(dt, E[0], E[1])
    a = dtx * (-jnp.exp(alog_e))
    acs = _sel_left(tril, triu, a)
    rowv = _sel_left(ones, ones, acs * eye_t)
    seg = acs - rowv
    Lc = jnp.where(causal, jnp.exp(jnp.where(causal, seg, 0.0)), 0.0)
    xdt = xs * dtx
    Bt = jnp.concatenate([Bm] * 4, axis=0)
    CBc = _bdot(Cm, Bt, "nt")
    Xbd = jnp.where(blockdiag, jnp.concatenate([xdt] * 4, axis=0), 0.0)
    y_intra = _bdot(CBc * Lc, Xbd, "nn")
    y_inter = _bdot(Cm, S, "nn") * jnp.exp(acs)
    y = y_intra + y_inter + d_e * xs
    last = jnp.sum(a, axis=0, keepdims=True)
    dec_end = jnp.exp(last - acs)
    S_new = S * jnp.exp(last) + _bdot(Bm, xdt * dec_end, "tn")
    return S_new, y


SSD_GPS = 8
SSD_W = SSD_GPS * GROUP_W
SSD_N = SSD_GPS * D_STATE
SSD_STEPS = SSM_GROUPS // SSD_GPS


def _ssd_in_specs(cmap):
    return [
        pl.BlockSpec((CHUNK, SSD_W), lambda c, g: (cmap(c), g)),
        pl.BlockSpec((CHUNK, DT_PAD), lambda c, g: (cmap(c), DT_COL)),
        pl.BlockSpec((1, DT_PAD), lambda c, g: (0, 0)),
        pl.BlockSpec((1, SSD_W), lambda c, g: (0, g)),
        pl.BlockSpec((1, SSD_W), lambda c, g: (0, g)),
        pl.BlockSpec((CHUNK, SSD_N), lambda c, g: (cmap(c), B_COL0 // SSD_GPS + g)),
        pl.BlockSpec((CHUNK, SSD_N), lambda c, g: (cmap(c), C_COL0 // SSD_GPS + g)),
    ]


def _gw(u):
    return slice(u * GROUP_W, (u + 1) * GROUP_W)


def _gn(u):
    return slice(u * D_STATE, (u + 1) * D_STATE)


def _ssd_fwd(xbc, zx, dt_bias, alog_e, d_e, name, travel=None):
    L = xbc.shape[0]
    nc = L // CHUNK
    n_in = 7 + (travel is not None)

    def body(*refs):
        xs_ref, dtr_ref, bias_ref, alog_ref, d_ref, b_ref, c_ref = refs[:7]
        y_ref, st_ref = refs[n_in:n_in + 2]
        S_ref = refs[n_in + 2 + (travel is not None)]
        c = pl.program_id(0)
        step = pl.program_id(1)
        g0 = step * SSD_GPS if SSD_STEPS > 1 else 0
        if travel is not None:
            moves = lambda: _gather_moves(refs[7], refs[n_in + 2], *refs[n_in + 4:])
            pl.when(jnp.logical_and(c == 0, step == 0))(lambda: moves().start())

        @pl.when(c == 0)
        def _():
            for u in range(SSD_GPS):
                S_ref[g0 + u] = jnp.zeros((D_STATE, GROUP_W), F32)

        dtr, bias = dtr_ref[...], bias_ref[...]
        for u in range(SSD_GPS):
            S = S_ref[g0 + u]
            st_ref[u] = S
            S_new, y = _ssd_chunk(S, xs_ref[:, _gw(u)], dtr, bias, alog_ref[:, _gw(u)], d_ref[:, _gw(u)],
                                  b_ref[:, _gn(u)], c_ref[:, _gn(u)], _ssd_expand(g0 + u))
            y_ref[:, _gw(u)] = y
            S_ref[g0 + u] = S_new

        if travel is not None:
            pl.when(jnp.logical_and(c == nc - 1, step == SSD_STEPS - 1))(lambda: moves().wait())

    extra = travel is not None
    return pl.pallas_call(
        body,
        name=name,
        grid=(nc, SSD_STEPS),
        in_specs=_ssd_in_specs(lambda c: c) + [ANY] * extra,
        out_specs=[
            pl.BlockSpec((CHUNK, SSD_W), lambda c, g: (c, g)),
            pl.BlockSpec((None, SSD_GPS, D_STATE, GROUP_W), lambda c, g: (c, g, 0, 0)),
        ] + [ANY] * extra,
        out_shape=[
            jax.ShapeDtypeStruct((L, D_INNER), F32),
            jax.ShapeDtypeStruct((nc, SSM_GROUPS, D_STATE, GROUP_W), F32),
        ] + ([jax.ShapeDtypeStruct((N_CHIPS,) + travel.shape, travel.dtype)] if extra else []),
        scratch_shapes=[pltpu.VMEM((SSM_GROUPS, D_STATE, GROUP_W), F32)]
        + ((_MOVE_SEMS + [pltpu.SemaphoreType.DMA]) if extra else []),
        compiler_params=_cparams(("arbitrary", "arbitrary")),
    )(xbc, zx, dt_bias, alog_e, d_e, xbc, xbc, *([travel] if extra else []))


def _ssd_bwd(xbc, zx, dt_bias, alog_e, d_e, states, dy, name, travel=None):
    assert SSD_STEPS == 1, "one grid step writes the whole d (xs | B | C) row block"
    L = xbc.shape[0]
    nc = L // CHUNK
    rev = lambda c: nc - 1 - c
    n_in = 9 + (travel is not None)

    def body(*refs):
        xs_ref, dtr_ref, bias_ref, alog_ref, d_ref, b_ref, c_ref, st_ref, dy_ref = refs[:9]
        dact_ref, ddt_ref, dbias_ref, dalog_ref, dd_ref = refs[n_in:n_in + 5]
        dS_ref = refs[n_in + 5 + (travel is not None)]
        c = pl.program_id(0)
        step = pl.program_id(1)
        g0 = 0
        first = jnp.logical_and(c == 0, step == 0)
        if travel is not None:
            moves = lambda: _scatter_moves(refs[9], refs[n_in + 5], *refs[n_in + 7:])
            pl.when(first)(lambda: moves().start())

        @pl.when(c == 0)
        def _():
            for u in range(SSD_GPS):
                dS_ref[g0 + u] = jnp.zeros((D_STATE, GROUP_W), F32)
                dalog_ref[g0 + u] = jnp.zeros((1, GROUP_W), F32)
                dd_ref[g0 + u] = jnp.zeros((1, GROUP_W), F32)

        @pl.when(first)
        def _():
            dbias_ref[...] = jnp.zeros_like(dbias_ref)

        dtr, bias = dtr_ref[...], bias_ref[...]
        ddt_sum = jnp.zeros((CHUNK, DT_PAD), F32)
        dbias_sum = jnp.zeros((1, DT_PAD), F32)
        for u in range(SSD_GPS):
            E = _ssd_expand(g0 + u)
            _, vjp = jax.vjp(
                lambda S, xs, dtr_, bias_, alog, dsk, Bm, Cm: _ssd_chunk(S, xs, dtr_, bias_, alog, dsk, Bm, Cm, E),
                st_ref[u], xs_ref[:, _gw(u)], dtr, bias, alog_ref[:, _gw(u)], d_ref[:, _gw(u)],
                b_ref[:, _gn(u)], c_ref[:, _gn(u)])
            dS, dxs, ddtr, dbias, dalog, dd, dB, dC = vjp((dS_ref[g0 + u], dy_ref[:, _gw(u)]))
            dS_ref[g0 + u] = dS
            dact_ref[:, _gw(u)] = dxs
            dact_ref[:, slice(D_INNER + u * D_STATE, D_INNER + (u + 1) * D_STATE)] = dB
            dact_ref[:, slice(D_INNER + SSD_N + u * D_STATE, D_INNER + SSD_N + (u + 1) * D_STATE)] = dC
            dalog_ref[g0 + u] += dalog
            dd_ref[g0 + u] += dd
            ddt_sum = ddt_sum + ddtr
            dbias_sum = dbias_sum + dbias

        ddt_ref[...] = ddt_sum
        dbias_ref[...] += dbias_sum
        if travel is not None:
            pl.when(jnp.logical_and(c == nc - 1, step == SSD_STEPS - 1))(lambda: moves().wait())

    extra = travel is not None
    whole3 = pl.BlockSpec((SSM_GROUPS, 1, GROUP_W), lambda c, g: (0, 0, 0))
    return pl.pallas_call(
        body,
        name=name,
        grid=(nc, SSD_STEPS),
        in_specs=_ssd_in_specs(rev) + [
            pl.BlockSpec((None, SSD_GPS, D_STATE, GROUP_W), lambda c, g: (rev(c), g, 0, 0)),
            pl.BlockSpec((CHUNK, SSD_W), lambda c, g: (rev(c), g)),
        ] + [ANY] * extra,
        out_specs=[
            pl.BlockSpec((CHUNK, CONV_DIM), lambda c, g: (rev(c), 0)),
            pl.BlockSpec((CHUNK, DT_PAD), lambda c, g: (rev(c), 0)),
            pl.BlockSpec((1, DT_PAD), lambda c, g: (0, 0)),
            whole3,
            whole3,
        ] + [ANY] * extra,
        out_shape=[
            jax.ShapeDtypeStruct((L, CONV_DIM), F32),
            jax.ShapeDtypeStruct((L, DT_PAD), F32),
            jax.ShapeDtypeStruct((1, DT_PAD), F32),
            jax.ShapeDtypeStruct((SSM_GROUPS, 1, GROUP_W), F32),
            jax.ShapeDtypeStruct((SSM_GROUPS, 1, GROUP_W), F32),
        ] + ([jax.ShapeDtypeStruct((N_CHIPS - 1,) + travel.shape[1:], travel.dtype)] if extra else []),
        scratch_shapes=[pltpu.VMEM((SSM_GROUPS, D_STATE, GROUP_W), F32)] + (_MOVE_SEMS if extra else []),
        compiler_params=_cparams(("arbitrary", "arbitrary")),
    )(xbc, zx, dt_bias, alog_e, d_e, xbc, xbc, states, dy, *([travel] if extra else []))


LOG2E = math.log2(math.e)
LN2 = math.log(2.0)
SB_Q_SCALE = SB_D ** -0.5 * LOG2E


def _sb_scores(q, kj):
    z2 = _dot(q, kj, dims=(((1,), (1,)), ((), ())))
    lb = jnp.minimum(z2, 0.0) - jnp.log2(1.0 + jnp.exp2(-jnp.abs(z2)))
    return lb, lb - z2


SB_TQ_FWD = 512
SB_TQ_BWD = 512
SB_TK = 512
SB_TC = SB_TK // 2


def _sb_after():
    j = lax.broadcasted_iota(jnp.int32, (2 * SB_TC, SB_TC), 0) % SB_TC
    s = lax.broadcasted_iota(jnp.int32, (2 * SB_TC, SB_TC), 1)
    return (j > s).astype(BF16)


def _sb_suffix(x, after):
    T = x.shape[0]
    hi = x.astype(BF16)
    lo = (x - hi.astype(F32)).astype(BF16)
    stacked = jnp.concatenate([jnp.concatenate([hi[:, :SB_TC], lo[:, :SB_TC]], axis=1),
                               jnp.concatenate([hi[:, SB_TC:], lo[:, SB_TC:]], axis=1)], axis=0)
    out = _dot(stacked, after)
    rs_right = jnp.sum(x[:, SB_TC:], axis=1, keepdims=True)
    return (jnp.concatenate([out[:T] + rs_right, out[T:]], axis=1),
            rs_right + jnp.sum(x[:, :SB_TC], axis=1, keepdims=True))


SB_ROWS_FWD = 256
SB_ROWS_BWD = 128


def _sb_seen(rows, row0, start):
    t = lax.broadcasted_iota(jnp.int32, (rows, SB_TK), 0) + row0
    s = lax.broadcasted_iota(jnp.int32, (rows, SB_TK), 1) + start
    return s < t


def _sb_walk(gd, groups, carry):
    carry = lax.cond(gd % 2 == 1, lambda c: groups([gd, gd - 1], c, True), lambda c: groups([gd], c, True), carry)
    base = gd - 1 - gd % 2
    return lax.fori_loop(0, gd // 2, lambda n, c: groups([base - 2 * n, base - 2 * n - 1], c, False), carry)


def _sb_fwd(qs, kn, v, name):
    H, L, d = qs.shape
    SB_TQ, SB_ROWS = SB_TQ_FWD, SB_ROWS_FWD
    nq = L // SB_TQ
    after = _sb_after()

    def body(q_ref, k_ref, v_ref, aft_ref, o_ref):
        i = pl.program_id(1)
        gd = (i * SB_TQ) // SB_TK
        subs = range(SB_TQ // SB_ROWS)
        qr = [q_ref[r * SB_ROWS:(r + 1) * SB_ROWS, :] for r in subs]

        def prep(G, diag, r):
            start = pl.multiple_of(G * SB_TK, SB_TK)
            lb, lk = _sb_scores(qr[r], k_ref[pl.ds(start, SB_TK), :])
            seen = _sb_seen(SB_ROWS, i * SB_TQ + r * SB_ROWS, start) if diag else None
            if diag:
                lk = jnp.where(seen, lk, 0.0)
            between, rs = _sb_suffix(lk, aft_ref[...])
            return lb + between, rs, seen, v_ref[pl.ds(start, SB_TK), :]

        def groups(Gs, carry, diag):
            carry = list(carry)
            pre = [[prep(G, diag and n == 0, r) for n, G in enumerate(Gs)] for r in subs]
            for r in subs:
                R, acc = carry[r]
                for s, rs, seen, vg in pre[r]:
                    w = jnp.exp2(s + R)
                    if seen is not None:
                        w = jnp.where(seen, w, 0.0)
                    acc = acc + _dot(w.astype(BF16), vg)
                    R = R + rs
                carry[r] = (R, acc)
            return tuple(carry)

        zero = (jnp.zeros((SB_ROWS, 1), F32), jnp.zeros((SB_ROWS, d), F32))
        out = _sb_walk(gd, groups, tuple(zero for _ in subs))
        for r in subs:
            o_ref[r * SB_ROWS:(r + 1) * SB_ROWS, :] = out[r][1]

    return pl.pallas_call(
        body,
        name=name,
        grid=(H, nq),
        in_specs=[
            pl.BlockSpec((None, SB_TQ, d), lambda h, i: (h, i, 0)),
            pl.BlockSpec((None, L, d), lambda h, i: (h, 0, 0)),
            pl.BlockSpec((None, L, d), lambda h, i: (h, 0, 0)),
            pl.BlockSpec((2 * SB_TC, SB_TC), lambda h, i: (0, 0)),
        ],
        out_specs=pl.BlockSpec((None, SB_TQ, d), lambda h, i: (h, i, 0)),
        out_shape=jax.ShapeDtypeStruct((H, L, d), F32),
        compiler_params=_cparams(("parallel", "arbitrary")),
    )(qs, kn, v, after)


def _sb_bwd(qs, kn, v, o, do, name):
    H, L, d = qs.shape
    SB_TQ, SB_ROWS = SB_TQ_BWD, SB_ROWS_BWD
    nq = L // SB_TQ
    after = _sb_after()
    qsT = qs.transpose(0, 2, 1)
    do16 = do.astype(BF16)
    doT = do16.transpose(0, 2, 1)

    def body(q_ref, qT_ref, k_ref, v_ref, o_ref, do_ref, doT_ref, aft_ref, dq_ref, dkT_ref, dvT_ref):
        i = pl.program_id(1)

        @pl.when(i == 0)
        def _():
            dkT_ref[...] = jnp.zeros_like(dkT_ref)
            dvT_ref[...] = jnp.zeros_like(dvT_ref)

        gd = (i * SB_TQ) // SB_TK
        subs = range(SB_TQ // SB_ROWS)
        rows = [slice(r * SB_ROWS, (r + 1) * SB_ROWS) for r in subs]
        qr = [q_ref[rows[r], :] for r in subs]
        qT = [qT_ref[:, rows[r]] for r in subs]
        do_ = [do_ref[rows[r], :] for r in subs]
        doT_ = [doT_ref[:, rows[r]] for r in subs]
        D = [jnp.sum(o_ref[rows[r], :] * do_[r].astype(F32), axis=1, keepdims=True) for r in subs]

        def prep(G, diag, r):
            start = pl.multiple_of(G * SB_TK, SB_TK)
            kg = k_ref[pl.ds(start, SB_TK), :]
            lb, lk = _sb_scores(qr[r], kg)
            seen = _sb_seen(SB_ROWS, i * SB_TQ + r * SB_ROWS, start) if diag else None
            if diag:
                lk = jnp.where(seen, lk, 0.0)
            between, rs = _sb_suffix(lk, aft_ref[...])
            dw = _dot(do_[r], v_ref[pl.ds(start, SB_TK), :], dims=(((1,), (1,)), ((), ())))
            return start, kg, lb + between, rs, dw, jnp.exp2(lb), seen

        def groups(Gs_, carry, diag):
            carry = list(carry)
            pre = [[prep(G, diag and n == 0, r) for n, G in enumerate(Gs_)] for r in subs]
            for r in subs:
                R, Gs, dq = carry[r]
                for start, kg, s, rs, dw, sig, seen in pre[r]:
                    w = jnp.exp2(s + R)
                    if seen is not None:
                        w = jnp.where(seen, w, 0.0)
                    w16 = w.astype(BF16)
                    g = w16.astype(F32) * dw
                    dvT_ref[:, pl.ds(start, SB_TK)] += _dot(doT_[r], w16)
                    g_right, gsum = _sb_suffix(g, aft_ref[...])
                    dz = g - sig * (D[r] - Gs - g_right)
                    if seen is not None:
                        dz = jnp.where(seen, dz, 0.0)
                    dz16 = dz.astype(BF16)
                    dq = dq + _dot(dz16, kg)
                    dkT_ref[:, pl.ds(start, SB_TK)] += _dot(qT[r], dz16)
                    R = R + rs
                    Gs = Gs + gsum
                carry[r] = (R, Gs, dq)
            return tuple(carry)

        zero1 = jnp.zeros((SB_ROWS, 1), F32)
        out = _sb_walk(gd, groups, tuple((zero1, zero1, jnp.zeros((SB_ROWS, d), F32)) for _ in subs))
        for r in subs:
            dq_ref[rows[r], :] = out[r][2]

    blk = pl.BlockSpec((None, SB_TQ, d), lambda h, i: (h, i, 0))
    blkT = pl.BlockSpec((None, d, SB_TQ), lambda h, i: (h, 0, i))
    full = pl.BlockSpec((None, L, d), lambda h, i: (h, 0, 0))
    fullT = pl.BlockSpec((None, d, L), lambda h, i: (h, 0, 0))
    return pl.pallas_call(
        body,
        name=name,
        grid=(H, nq),
        in_specs=[blk, blkT, full, full, blk, blk, blkT, pl.BlockSpec((2 * SB_TC, SB_TC), lambda h, i: (0, 0))],
        out_specs=[blk, fullT, fullT],
        out_shape=[jax.ShapeDtypeStruct((H, L, d), F32), jax.ShapeDtypeStruct((H, d, L), F32),
                   jax.ShapeDtypeStruct((H, d, L), F32)],
        compiler_params=_cparams(("parallel", "arbitrary")),
    )(qs, qsT, kn, v, o, do16, doT, after)


def _fold_heads(rows, name):
    def body(x_ref, o_ref):
        c = lax.broadcasted_iota(jnp.int32, (D_INNER, DT_PAD), 0)
        h = lax.broadcasted_iota(jnp.int32, (D_INNER, DT_PAD), 1)
        o_ref[...] = _dot(x_ref[...], (c // HEAD_P == h).astype(F32), precision=HI)

    return pl.pallas_call(
        body, name=name, out_shape=jax.ShapeDtypeStruct((SUBLANE, DT_PAD), F32),
        compiler_params=_cparams(),
    )(rows)


def _ffn_fwd(x, r, norm_w, w_in_t, w_out, tag):
    x1, h = _norm_fwd(x, r, norm_w, f"norm_ffn_fwd{tag}")
    gu, act = _ffn_in_fused(h, w_in_t, f"ffn_in_fwd{tag}")
    f = _mm(act, w_out, "nn", F32, f"ffn_out_fwd{tag}")
    return x1, h, gu, act, f


def _ffn_bwd(x1, h, gu, act, norm_w, w_in_t, w_out, dres, tag):
    d16 = dres.astype(BF16)
    dgu = _ffn_out_dx_fused(d16, w_out, gu, f"ffn_out_dx{tag}")
    g_w_out = _mm(act, d16, "tn", F32, f"ffn_out_dw{tag}")
    dh = _mm(dgu, w_in_t, "nn", BF16, f"ffn_in_dx{tag}")
    g_w_in_t = _mm(dgu, h, "tn", F32, f"ffn_in_dw{tag}")
    dres, g_norm = _norm_bwd(x1, norm_w, dh, dres, f"norm_ffn_bwd{tag}")
    return dres, g_norm, g_w_in_t, g_w_out


def _rest_weights(gb):
    whole = _whole_of_shards(gb, SEG_REST)
    return {"ssm_w_out": whole["ssm_w_out"][0], "sb_w_qkv_t": whole["sb_w_qkv"][0], "sb_w_o": whole["sb_w_o"][0],
            "ffn_w_in_t": whole["ffn_w_in"], "ffn_w_out": whole["ffn_w_out"]}


def _local_step(x, target, W, wb_rest=None):
    L = x.shape[0]
    H = SB_HEADS
    nm0, nm1 = W["norm_mix"][0:1], W["norm_mix"][1:2]
    nf0, nf1 = W["norm_ffn"][0:1], W["norm_ffn"][1:2]

    _, h0 = _norm_fwd(x, None, nm0, "norm_mix_fwd0")
    zx = _mm(h0, W["ssm_w_in_t"], "nt", F32, "ssm_in_fwd")
    xbc = _conv_fwd(zx, W["conv_w8"], W["conv_b"], "conv_fwd")
    if wb_rest is None:
        y, states = _ssd_fwd(xbc, zx, W["dt_bias"], W["alog_e"], W["d_e"], "ssd_fwd")
    else:
        y, states, gb_rest = _ssd_fwd(xbc, zx, W["dt_bias"], W["alog_e"], W["d_e"], "ssd_fwd", travel=wb_rest)
        W = {**W, **_rest_weights(gb_rest)}
    yn = _gated_norm_fwd(y, zx, W["ssm_norm_w"], "gated_norm_fwd")
    mix0 = _mm(yn, W["ssm_w_out"], "nn", F32, "ssm_out_fwd")
    x1, h1, gu0, act0, f0 = _ffn_fwd(x, mix0, nf0, W["ffn_w_in_t"][0], W["ffn_w_out"][0], "0")

    x2, h2 = _norm_fwd(x1, f0, nm1, "norm_mix_fwd1")
    qkv = _mm(h2, W["sb_w_qkv_t"], "nt", F32, "sb_qkv_fwd")
    qkv_t = qkv.reshape(L, 3, H, SB_D).transpose(1, 2, 0, 3)
    q_r = qkv_t[0].reshape(H * L, SB_D)
    k_r = qkv_t[1].reshape(H * L, SB_D)
    qs = _qk_norm_fwd(q_r, W["sb_q_gain"], SB_Q_SCALE, "q_norm_fwd").reshape(H, L, SB_D)
    kn = _qk_norm_fwd(k_r, W["sb_k_gain"], 1.0, "k_norm_fwd").reshape(H, L, SB_D)
    vb = qkv_t[2].astype(BF16)
    o = _sb_fwd(qs, kn, vb, "sb_fwd")
    o_flat = o.transpose(1, 0, 2).reshape(L, D_MODEL).astype(BF16)
    mix1 = _mm(o_flat, W["sb_w_o"], "nn", F32, "sb_o_fwd")
    x3, h3, gu1, act1, f1 = _ffn_fwd(x2, mix1, nf1, W["ffn_w_in_t"][1], W["ffn_w_out"][1], "1")

    dres, sq = _loss_bwd(x3, f1, target, "loss")

    dres, g_nf1, g_fin1, g_fout1 = _ffn_bwd(x3, h3, gu1, act1, nf1, W["ffn_w_in_t"][1], W["ffn_w_out"][1], dres, "1")
    d16 = dres.astype(BF16)
    do_flat = _mm(d16, W["sb_w_o"], "nt", F32, "sb_o_dx")
    g_w_o = _mm(o_flat, d16, "tn", F32, "sb_o_dw")
    do = do_flat.reshape(L, H, SB_D).transpose(1, 0, 2)
    dqs, dknT, dvT = _sb_bwd(qs, kn, vb, o, do, "sb_bwd")
    dkn = dknT.transpose(0, 2, 1)
    dq, g_qg = _qk_norm_bwd(q_r, W["sb_q_gain"], dqs.reshape(H * L, SB_D), SB_Q_SCALE, LN2, "q_norm_bwd")
    dk, g_kg = _qk_norm_bwd(k_r, W["sb_k_gain"], dkn.reshape(H * L, SB_D), 1.0, LN2, "k_norm_bwd")
    dqkv = jnp.stack([dq.reshape(H, L, SB_D), dk.reshape(H, L, SB_D), dvT.transpose(0, 2, 1)])
    dqkv = dqkv.transpose(2, 0, 1, 3).reshape(L, 3 * D_MODEL).astype(BF16)
    dh2 = _mm(dqkv, W["sb_w_qkv_t"], "nn", BF16, "sb_qkv_dx")
    g_w_qkv_t = _mm(dqkv, h2, "tn", F32, "sb_qkv_dw")
    dres, g_nm1 = _norm_bwd(x2, nm1, dh2, dres, "norm_mix_bwd1")

    dres, g_nf0, g_fin0, g_fout0 = _ffn_bwd(x1, h1, gu0, act0, nf0, W["ffn_w_in_t"][0], W["ffn_w_out"][0], dres, "0")
    d16 = dres.astype(BF16)
    dyn = _mm(d16, W["ssm_w_out"], "nt", BF16, "ssm_out_dx")
    g_w_out = _mm(yn, d16, "tn", F32, "ssm_out_dw")
    dy, dz, g_snw = _gated_norm_bwd(y, zx, W["ssm_norm_w"], dyn, "gated_norm_bwd")
    rest = {"ssm_w_out": [g_w_out], "sb_w_qkv": [g_w_qkv_t], "sb_w_o": [g_w_o],
            "ffn_w_in": [g_fin0, g_fin1], "ffn_w_out": [g_fout0, g_fout1]}
    if wb_rest is None:
        dact, ddt, g_dtb, g_alog_e, g_d_e = _ssd_bwd(
            xbc, zx, W["dt_bias"], W["alog_e"], W["d_e"], states, dy, "ssd_bwd")
        exchanged = None
    else:
        slots = jnp.stack([_shard_of_whole(rest, s, SEG_REST) for s in range(N_CHIPS)])
        dact, ddt, g_dtb, g_alog_e, g_d_e, received = _ssd_bwd(
            xbc, zx, W["dt_bias"], W["alog_e"], W["d_e"], states, dy, "ssd_bwd", travel=slots.astype(BF16))
        exchanged = (slots, received)
    dxbc, g_cw8, g_cb = _conv_bwd(zx, W["conv_w8"], W["conv_b"], dact, "conv_bwd")
    dzx = jnp.concatenate([dz, dxbc, ddt.astype(BF16)], axis=1)
    dh0 = _mm(dzx, W["ssm_w_in_t"], "nn", BF16, "ssm_in_dx")
    g_w_in_t = _mm(dzx, h0, "tn", F32, "ssm_in_dw")
    grad_x, g_nm0 = _norm_bwd(x, nm0, dh0, dres, "norm_mix_bwd0")

    per_head = jnp.concatenate(
        [g_alog_e.reshape(1, D_INNER), g_d_e.reshape(1, D_INNER), jnp.zeros((SUBLANE - 2, D_INNER), F32)], axis=0)
    folded = _fold_heads(per_head, "fold_heads")
    grads = {
        "norm_mix": jnp.concatenate([g_nm0, g_nm1], axis=0),
        "norm_ffn": jnp.concatenate([g_nf0, g_nf1], axis=0),
        "ssm_w_in": [g_w_in_t],
        "ssm_conv_w": [g_cw8],
        "ssm_conv_b": g_cb,
        "ssm_dt_bias": g_dtb[:, :SSM_HEADS],
        "ssm_a_log": folded[0:1, :SSM_HEADS],
        "ssm_d": folded[1:2, :SSM_HEADS],
        "ssm_norm_w": g_snw,
        "sb_q_gain": g_qg,
        "sb_k_gain": g_kg,
        **rest,
    }
    return (sq, grad_x, grads) if wb_rest is None else (sq, grad_x, grads, exchanged)


def _prep_weights(full):
    W = _prep_vectors(full)
    w_in_t = full["ssm_w_in"].T
    W["ssm_w_in_t"] = jnp.pad(w_in_t, ((0, D_IN_PAD - D_IN_PROJ), (0, 0))).astype(BF16)
    W["conv_w8"] = jnp.pad(full["ssm_conv_w"], ((0, SUBLANE - CONV_W), (0, 0)))
    W["ssm_w_out"] = full["ssm_w_out"].astype(BF16)
    W["sb_w_qkv_t"] = full["sb_w_qkv"].T.astype(BF16)
    W["sb_w_o"] = full["sb_w_o"].astype(BF16)
    W["ffn_w_in_t"] = jnp.swapaxes(full["ffn_w_in"], 1, 2).astype(BF16)
    W["ffn_w_out"] = full["ffn_w_out"].astype(BF16)
    return W


def _prep_vectors(full):
    W = {}
    W["norm_mix"] = full["norm_mix"]
    W["norm_ffn"] = full["norm_ffn"]
    W["conv_b"] = full["ssm_conv_b"]
    W["dt_bias"] = jnp.pad(full["ssm_dt_bias"], ((0, 0), (0, DT_PAD - SSM_HEADS)))
    W["alog_e"] = jnp.repeat(full["ssm_a_log"], HEAD_P, axis=1)
    W["d_e"] = jnp.repeat(full["ssm_d"], HEAD_P, axis=1)
    W["ssm_norm_w"] = full["ssm_norm_w"]
    W["sb_q_gain"] = full["sb_q_gain"]
    W["sb_k_gain"] = full["sb_k_gain"]
    return W


N_CHIPS = 4
N_DEV = 8
FLAT_W = D_MODEL

SEG_FIRST = (
    ("ssm_w_in", (1, D_MODEL, D_IN_PROJ // N_CHIPS), True),
    ("ssm_conv_w", (1, CONV_W, CONV_DIM // N_CHIPS), False),
)
SEG_REST = (
    ("ssm_w_out", (1, D_INNER // N_CHIPS, D_MODEL), False),
    ("sb_w_qkv", (1, D_MODEL, 3 * D_MODEL // N_CHIPS), True),
    ("sb_w_o", (1, D_MODEL // N_CHIPS, D_MODEL), False),
    ("ffn_w_in", (2, D_MODEL, 2 * D_FF // N_CHIPS), True),
    ("ffn_w_out", (2, D_FF // N_CHIPS, D_MODEL), False),
)
SHARDED = SEG_FIRST + SEG_REST
REPLICATED = (
    ("norm_mix", (2, D_MODEL)), ("norm_ffn", (2, D_MODEL)), ("ssm_conv_b", (1, CONV_DIM)),
    ("ssm_norm_w", (1, D_INNER)), ("ssm_dt_bias", (1, SSM_HEADS)), ("ssm_a_log", (1, SSM_HEADS)),
    ("ssm_d", (1, SSM_HEADS)), ("sb_q_gain", (1, SB_D)), ("sb_k_gain", (1, SB_D)),
)
ADAM_TILE = 256
_USED_SMALL = sum(math.prod(s) for _, s in REPLICATED)
SMALL_ROWS = -(-_USED_SMALL // (SUBLANE * FLAT_W)) * SUBLANE


def _flat_rows(shape):
    return math.prod(shape) // FLAT_W


def _seg_rows(seg):
    used = sum(_flat_rows(s) for _, s, _ in seg)
    return used, -(-used // ADAM_TILE) * ADAM_TILE


def _pad_rows(parts, seg, dtype):
    used, rows = _seg_rows(seg)
    return parts + [jnp.zeros((rows - used, FLAT_W), dtype)] if rows > used else parts


def _pack_shard(d, dtype, seg):
    parts = []
    for n, _, transposed in seg:
        a = jnp.swapaxes(d[n], 1, 2) if transposed else d[n]
        parts.append(a.reshape(-1, FLAT_W).astype(dtype))
    return jnp.concatenate(_pad_rows(parts, seg, dtype), axis=0)


def _unpack_shard(flat, seg):
    out, r = {}, 0
    for n, s, transposed in seg:
        k = _flat_rows(s)
        a = flat[r:r + k]
        out[n] = jnp.swapaxes(a.reshape(s[0], s[2], s[1]), 1, 2) if transposed else a.reshape(s)
        r += k
    return out


def _shard_of_whole(whole, chip, seg):
    parts = []
    for n, s, _ in seg:
        k = _flat_rows(s) // s[0]
        if n == "ssm_conv_w":
            parts.append(whole[n][0][:CONV_W, chip * FLAT_W:(chip + 1) * FLAT_W])
        else:
            parts.extend(a[chip * k:(chip + 1) * k] for a in whole[n])
    return jnp.concatenate(_pad_rows(parts, seg, parts[0].dtype), axis=0)


def _whole_of_shards(gb, seg):
    out, r = {}, 0
    for n, s, _ in seg:
        k = _flat_rows(s) // s[0]
        if n != "ssm_conv_w":
            out[n] = [jnp.concatenate([gb[p, r + l * k:r + (l + 1) * k] for p in range(N_CHIPS)], axis=0)
                      for l in range(s[0])]
        r += k * s[0]
    return out


def _pack_small(d):
    parts = [d[n].reshape(-1) for n, _ in REPLICATED]
    parts.append(jnp.zeros((SMALL_ROWS * FLAT_W - _USED_SMALL,), parts[0].dtype))
    return jnp.concatenate(parts).reshape(SMALL_ROWS, FLAT_W)


def _unpack_small(flat):
    flat = flat.reshape(-1)
    out, r = {}, 0
    for n, s in REPLICATED:
        k = math.prod(s)
        out[n] = flat[r:r + k].reshape(s)
        r += k
    return out


ANY = pl.BlockSpec(memory_space=pl.ANY)


def _other_chips():
    x, y, c = lax.axis_index("x"), lax.axis_index("y"), lax.axis_index("c")
    return x, y, c, [(1 - x, y), (x, 1 - y), (1 - x, 1 - y)]


def _remote(src, dst, send_sem, recv_sem, dev):
    return pltpu.make_async_remote_copy(src_ref=src, dst_ref=dst, send_sem=send_sem, recv_sem=recv_sem,
                                        device_id=dev, device_id_type=MESH_T)


def _gather_weights(wb, ws):
    def body(wb_ref, ws_ref, gb_ref, gs_ref, send_sems, recv_sems, loc_sems):
        x, y, c, others = _other_chips()
        me = 2 * x + y
        pairs = ((wb_ref, gb_ref), (ws_ref, gs_ref))
        local = [pltpu.make_async_copy(src, dst.at[me], loc_sems.at[t]) for t, (src, dst) in enumerate(pairs)]
        for cp in local:
            cp.start()
        sends = []
        for k, (px, py) in enumerate(others):
            for t, (src, dst) in enumerate(pairs):
                cp = _remote(src, dst.at[me], send_sems.at[2 * k + t], recv_sems.at[2 * k + t], (px, py, c))
                cp.start()
                sends.append(cp)
        for k, (px, py) in enumerate(others):
            for t, (src, dst) in enumerate(pairs):
                _remote(src, dst.at[2 * px + py], send_sems.at[2 * k + t], recv_sems.at[2 * k + t],
                        (px, py, c)).wait_recv()
        for cp in sends:
            cp.wait_send()
        for cp in local:
            cp.wait()

    return pl.pallas_call(
        body, name="gather_weights",
        in_specs=[ANY, ANY], out_specs=[ANY, ANY],
        out_shape=[jax.ShapeDtypeStruct((N_CHIPS,) + wb.shape, wb.dtype),
                   jax.ShapeDtypeStruct((N_CHIPS,) + ws.shape, ws.dtype)],
        scratch_shapes=[pltpu.SemaphoreType.DMA((6,)), pltpu.SemaphoreType.DMA((6,)), pltpu.SemaphoreType.DMA((2,))],
    )(wb, ws)


def _scatter_grads(gpack):
    def body(g_ref, r_ref, send_sems, recv_sems):
        x, y, c, others = _other_chips()
        sends = []
        for k, (px, py) in enumerate(others):
            cp = _remote(g_ref.at[2 * px + py], r_ref.at[k], send_sems.at[k], recv_sems.at[k], (px, py, c))
            cp.start()
            sends.append(cp)
        for cp in sends:
            cp.wait_recv()
        for cp in sends:
            cp.wait_send()

    return pl.pallas_call(
        body, name="scatter_grads",
        in_specs=[ANY], out_specs=ANY,
        out_shape=jax.ShapeDtypeStruct((N_CHIPS - 1,) + gpack.shape[1:], gpack.dtype),
        scratch_shapes=[pltpu.SemaphoreType.DMA((3,)), pltpu.SemaphoreType.DMA((3,))],
    )(gpack)


class _Moves:
    def __init__(self, sends, recvs, local=None):
        self.sends, self.recvs, self.local = sends, recvs, local

    def start(self):
        if self.local is not None:
            self.local.start()
        for cp in self.sends:
            cp.start()

    def wait(self):
        for cp in self.recvs:
            cp.wait_recv()
        for cp in self.sends:
            cp.wait_send()
        if self.local is not None:
            self.local.wait()


def _gather_moves(wb_ref, gb_ref, send_sems, recv_sems, loc_sem):
    x, y, c, others = _other_chips()
    me = 2 * x + y
    sends = [_remote(wb_ref, gb_ref.at[me], send_sems.at[k], recv_sems.at[k], (px, py, c))
             for k, (px, py) in enumerate(others)]
    recvs = [_remote(wb_ref, gb_ref.at[2 * px + py], send_sems.at[k], recv_sems.at[k], (px, py, c))
             for k, (px, py) in enumerate(others)]
    return _Moves(sends, recvs, pltpu.make_async_copy(wb_ref, gb_ref.at[me], loc_sem))


def _scatter_moves(g_ref, r_ref, send_sems, recv_sems):
    x, y, c, others = _other_chips()
    sends = [_remote(g_ref.at[2 * px + py], r_ref.at[k], send_sems.at[k], recv_sems.at[k], (px, py, c))
             for k, (px, py) in enumerate(others)]
    return _Moves(sends, sends)


_MOVE_SEMS = [pltpu.SemaphoreType.DMA((N_CHIPS - 1,)), pltpu.SemaphoreType.DMA((N_CHIPS - 1,))]


def _exchange_partials(part_a, part_b, small):
    def body(pa_ref, pb_ref, s_ref, psa_ref, psb_ref, sa_ref, send_sems, recv_sems, loc_sem):
        x, y, c = lax.axis_index("x"), lax.axis_index("y"), lax.axis_index("c")
        me = 4 * x + 2 * y + c
        local = pltpu.make_async_copy(s_ref, sa_ref.at[me], loc_sem)
        local.start()
        sib = _remote(pa_ref, psa_ref, send_sems.at[0], recv_sems.at[0], (x, y, 1 - c))
        sib_b = _remote(pb_ref, psb_ref, send_sems.at[N_DEV], recv_sems.at[N_DEV], (x, y, 1 - c))
        sib.start()
        sib_b.start()
        sends = [sib, sib_b]
        for k in range(1, N_DEV):
            fx, fy, fc = (k >> 2) & 1, (k >> 1) & 1, k & 1
            px, py, pc = x ^ fx, y ^ fy, c ^ fc
            cp = _remote(s_ref, sa_ref.at[me], send_sems.at[k], recv_sems.at[k], (px, py, pc))
            cp.start()
            sends.append(cp)
        sib.wait_recv()
        sib_b.wait_recv()
        for k in range(1, N_DEV):
            fx, fy, fc = (k >> 2) & 1, (k >> 1) & 1, k & 1
            px, py, pc = x ^ fx, y ^ fy, c ^ fc
            _remote(s_ref, sa_ref.at[4 * px + 2 * py + pc], send_sems.at[k], recv_sems.at[k], (px, py, pc)).wait_recv()
        for cp in sends:
            cp.wait_send()
        local.wait()

    return pl.pallas_call(
        body, name="exchange_partials",
        in_specs=[ANY, ANY, ANY], out_specs=[ANY, ANY, ANY],
        out_shape=[jax.ShapeDtypeStruct(part_a.shape, part_a.dtype),
                   jax.ShapeDtypeStruct(part_b.shape, part_b.dtype),
                   jax.ShapeDtypeStruct((N_DEV,) + small.shape, small.dtype)],
        scratch_shapes=[pltpu.SemaphoreType.DMA((N_DEV + 1,)), pltpu.SemaphoreType.DMA((N_DEV + 1,)),
                        pltpu.SemaphoreType.DMA],
    )(part_a, part_b, small)


def _partial_sum(own, recv, name):
    R = own.shape[0]

    def body(o_ref, r_ref, p_ref):
        acc = o_ref[...]
        for k in range(N_CHIPS - 1):
            acc = acc + r_ref[k].astype(F32)
        p_ref[...] = acc

    return pl.pallas_call(
        body, name=name, grid=(R // ADAM_TILE,),
        in_specs=[pl.BlockSpec((ADAM_TILE, FLAT_W), lambda i: (i, 0)),
                  pl.BlockSpec((N_CHIPS - 1, ADAM_TILE, FLAT_W), lambda i: (0, i, 0))],
        out_specs=pl.BlockSpec((ADAM_TILE, FLAT_W), lambda i: (i, 0)),
        out_shape=jax.ShapeDtypeStruct(own.shape, F32),
        compiler_params=_cparams(("parallel",)),
    )(own, recv)


def _adamw_math(w, g, m, v):
    m = ADAM_B1 * m + (1.0 - ADAM_B1) * g
    v = ADAM_B2 * v + (1.0 - ADAM_B2) * jnp.square(g)
    m_hat = m / (1.0 - ADAM_B1 ** ADAM_STEP)
    v_hat = v / (1.0 - ADAM_B2 ** ADAM_STEP)
    delta = -ADAM_LR * (m_hat / (jnp.sqrt(v_hat) + ADAM_EPS) + ADAM_WD * w)
    return delta, m, v


def _adamw(w, m, v, parts, tile, name):
    n, R, _ = parts.shape

    def body(w_ref, m_ref, v_ref, p_ref, g_ref, d_ref, nm_ref, nv_ref):
        g = p_ref[0]
        for k in range(1, n):
            g = g + p_ref[k]
        delta, nm, nv = _adamw_math(w_ref[...], g, m_ref[...], v_ref[...])
        g_ref[...] = g
        d_ref[...] = delta
        nm_ref[...] = nm
        nv_ref[...] = nv

    blk = pl.BlockSpec((tile, FLAT_W), lambda i: (i, 0))
    return pl.pallas_call(
        body, name=name, grid=(R // tile,),
        in_specs=[blk, blk, blk, pl.BlockSpec((n, tile, FLAT_W), lambda i: (0, i, 0))],
        out_specs=[blk] * 4,
        out_shape=[jax.ShapeDtypeStruct((R, FLAT_W), F32)] * 4,
        compiler_params=_cparams(("parallel",)),
    )(w, m, v, parts)


def _add2(a, b, name):
    def body(a_ref, b_ref, o_ref):
        o_ref[...] = a_ref[...] + b_ref[...]

    blk = pl.BlockSpec((ADAM_TILE, FLAT_W), lambda i: (i, 0))
    return pl.pallas_call(
        body, name=name, grid=(a.shape[0] // ADAM_TILE,), in_specs=[blk, blk], out_specs=blk,
        out_shape=jax.ShapeDtypeStruct(a.shape, F32), compiler_params=_cparams(("parallel",)),
    )(a, b)


def _adamw_block(w, m, v, g, name):
    shape = w.shape
    R, C = math.prod(shape[:-1]), shape[-1]
    tile = _tile(R, ADAM_TILE, SUBLANE)

    def body(w_ref, m_ref, v_ref, g_ref, d_ref, nm_ref, nv_ref):
        delta, nm, nv = _adamw_math(w_ref[...], g_ref[...], m_ref[...], v_ref[...])
        d_ref[...] = delta
        nm_ref[...] = nm
        nv_ref[...] = nv

    blk = pl.BlockSpec((tile, C), lambda i: (i, 0))
    outs = pl.pallas_call(
        body, name=name, grid=(R // tile,), in_specs=[blk] * 4, out_specs=[blk] * 3,
        out_shape=[jax.ShapeDtypeStruct((R, C), F32)] * 3, compiler_params=_cparams(("parallel",)),
    )(*(a.reshape(R, C) for a in (w, m, v, g)))
    return tuple(a.reshape(shape) for a in outs)


_NAMES = ("norm_mix", "norm_ffn", "ssm_w_in", "ssm_conv_w", "ssm_conv_b", "ssm_dt_bias", "ssm_a_log", "ssm_d",
          "ssm_norm_w", "ssm_w_out", "sb_w_qkv", "sb_q_gain", "sb_k_gain", "sb_w_o", "ffn_w_in", "ffn_w_out")


def _step(x, loss_target, w, m, v):
    cx, cy, cc = lax.axis_index("x"), lax.axis_index("y"), lax.axis_index("c")
    chip = 2 * cx + cy

    conv8 = jnp.pad(w["ssm_conv_w"][0], ((0, SUBLANE - CONV_W), (0, 0)))
    gb, gs = _gather_weights(_pack_shard(w, BF16, SEG_FIRST), conv8)
    W = _prep_vectors(w)
    W["ssm_w_in_t"] = jnp.pad(_whole_of_shards(gb, SEG_FIRST)["ssm_w_in"][0], ((0, D_IN_PAD - D_IN_PROJ), (0, 0)))
    W["conv_w8"] = jnp.concatenate([gs[p] for p in range(N_CHIPS)], axis=1)

    sq, grad_x, grads, (slots_rest, recv_rest) = _local_step(
        x[0], loss_target[0], W, wb_rest=_pack_shard(w, BF16, SEG_REST))
    loss = lax.psum(0.5 * jnp.sum(sq) / D_MODEL, ("x", "y", "c"))

    slots_first = jnp.stack([_shard_of_whole(grads, s, SEG_FIRST) for s in range(N_CHIPS)])
    recv_first = _scatter_grads(slots_first.astype(BF16))
    own = lambda slots: lax.dynamic_index_in_dim(slots, chip, axis=0, keepdims=False)
    part_first = _partial_sum(own(slots_first), recv_first, "grad_partial_sum_first")
    part_rest = _partial_sum(own(slots_rest), recv_rest, "grad_partial_sum_rest")
    sib_first, sib_rest, small_all = _exchange_partials(part_first, part_rest, _pack_small(grads))
    g_blocks = {**_unpack_shard(_add2(part_first, sib_first, "grad_final_sum_first"), SEG_FIRST),
                **_unpack_shard(_add2(part_rest, sib_rest, "grad_final_sum_rest"), SEG_REST)}
    g_s, d_s, m_s, v_s = _adamw(_pack_small(w), _pack_small(m), _pack_small(v), small_all, SUBLANE, "adamw_replicated")

    g, d, nm, nv = (_unpack_small(a) for a in (g_s, d_s, m_s, v_s))
    for n, _, _ in SHARDED:
        g[n] = g_blocks[n]
        d[n], nm[n], nv[n] = _adamw_block(w[n], m[n], v[n], g[n], "adamw_" + n)
    return loss, grad_x[None], [[t[n] for n in _NAMES] for t in (g, d, nm, nv)]


def kernel(x, norm_mix, norm_ffn, ssm_w_in, ssm_conv_w, ssm_conv_b, ssm_dt_bias, ssm_a_log, ssm_d, ssm_norm_w, ssm_w_out, sb_w_qkv, sb_q_gain, sb_k_gain, sb_w_o, ffn_w_in, ffn_w_out, loss_target, m_norm_mix, m_norm_ffn, m_ssm_w_in, m_ssm_conv_w, m_ssm_conv_b, m_ssm_dt_bias, m_ssm_a_log, m_ssm_d, m_ssm_norm_w, m_ssm_w_out, m_sb_w_qkv, m_sb_q_gain, m_sb_k_gain, m_sb_w_o, m_ffn_w_in, m_ffn_w_out, v_norm_mix, v_norm_ffn, v_ssm_w_in, v_ssm_conv_w, v_ssm_conv_b, v_ssm_dt_bias, v_ssm_a_log, v_ssm_d, v_ssm_norm_w, v_ssm_w_out, v_sb_w_qkv, v_sb_q_gain, v_sb_k_gain, v_sb_w_o, v_ffn_w_in, v_ffn_w_out):
    w = dict(zip(_NAMES, (norm_mix, norm_ffn, ssm_w_in, ssm_conv_w, ssm_conv_b, ssm_dt_bias, ssm_a_log, ssm_d,
                          ssm_norm_w, ssm_w_out, sb_w_qkv, sb_q_gain, sb_k_gain, sb_w_o, ffn_w_in, ffn_w_out)))
    m = dict(zip(_NAMES, (m_norm_mix, m_norm_ffn, m_ssm_w_in, m_ssm_conv_w, m_ssm_conv_b, m_ssm_dt_bias, m_ssm_a_log,
                          m_ssm_d, m_ssm_norm_w, m_ssm_w_out, m_sb_w_qkv, m_sb_q_gain, m_sb_k_gain, m_sb_w_o,
                          m_ffn_w_in, m_ffn_w_out)))
    v = dict(zip(_NAMES, (v_norm_mix, v_norm_ffn, v_ssm_w_in, v_ssm_conv_w, v_ssm_conv_b, v_ssm_dt_bias, v_ssm_a_log,
                          v_ssm_d, v_ssm_norm_w, v_ssm_w_out, v_sb_w_qkv, v_sb_q_gain, v_sb_k_gain, v_sb_w_o,
                          v_ffn_w_in, v_ffn_w_out)))
    loss, grad_x, (g, d, nm, nv) = _step(x, loss_target, w, m, v)
    return (loss, grad_x, *g, *d, *nm, *nv)
```

```python
import functools
import math

import jax
import jax.numpy as jnp
from jax import lax
from jax.experimental import pallas as pl
from jax.experimental.pallas import tpu as pltpu

F32 = jnp.float32
BF16 = jnp.bfloat16

VMEM_LIMIT_BYTES = 48 * 1024 * 1024
LANE = 128
SUBLANE = 8

D_MODEL = 1024
CHUNK = 64
D_INNER = 2048
SSM_HEADS = 32
SSM_GROUPS = 8
GROUP_W = D_INNER // SSM_GROUPS
HEAD_P = 64
D_STATE = 128
CONV_W = 4
CONV_DIM = D_INNER + 2 * SSM_GROUPS * D_STATE
D_IN_PROJ = D_INNER + CONV_DIM + SSM_HEADS
DT_PAD = LANE
D_IN_PAD = D_INNER + CONV_DIM + DT_PAD
SB_HEADS = 16
SB_D = 64
D_FF = 2816
NORM_EPS = 1e-6
GATED_NORM_EPS = 1e-5

ADAM_LR = 0.001
ADAM_B1 = 0.9
ADAM_B2 = 0.999
ADAM_EPS = 1e-08
ADAM_WD = 0.01
ADAM_STEP = 10

MESH_T = pl.DeviceIdType.MESH


def _cparams(sem=None):
    return pltpu.CompilerParams(dimension_semantics=sem, vmem_limit_bytes=VMEM_LIMIT_BYTES)


def _tile(n, target, align):
    best = None
    for t in range(align, min(n, target) + 1, align):
        if n % t == 0:
            best = t
    return best or n


def _mm(a, b, mode, out_dtype, name):
    halves = a.ndim == 3
    a2 = (a.shape[1], 2 * a.shape[2]) if halves else a.shape
    if mode == "nn":
        (M, K), N = a2, b.shape[1]
    elif mode == "nt":
        (M, K), N = a2, b.shape[0]
    else:
        (K, M), N = a2, b.shape[1]
    tm = _tile(M // 2 if halves and mode == "tn" else M, 1408, LANE)
    tn = _tile(N, 1536, LANE)
    tk = _tile(K // 2 if halves and mode == "nn" else K, 1536, LANE)
    nk = K // tk
    if mode == "nn":
        a_spec = pl.BlockSpec((tm, tk), lambda i, j, k: (i, k))
        if halves:
            nkh = nk // 2
            a_spec = pl.BlockSpec((None, tm, tk), lambda i, j, k: (k // nkh, i, k % nkh))
        b_spec = pl.BlockSpec((tk, tn), lambda i, j, k: (k, j))
        dims = (((1,), (0,)), ((), ()))
    elif mode == "nt":
        a_spec = pl.BlockSpec((tm, tk), lambda i, j, k: (i, k))
        b_spec = pl.BlockSpec((tn, tk), lambda i, j, k: (j, k))
        dims = (((1,), (1,)), ((), ()))
    else:
        a_spec = pl.BlockSpec((tk, tm), lambda i, j, k: (k, i))
        if halves:
            nmh = M // tm // 2
            a_spec = pl.BlockSpec((None, tk, tm), lambda i, j, k: (i // nmh, k, i % nmh))
        b_spec = pl.BlockSpec((tk, tn), lambda i, j, k: (k, j))
        dims = (((0,), (0,)), ((), ()))

    def body(a_ref, b_ref, o_ref, acc_ref):
        k = pl.program_id(2)

        @pl.when(k == 0)
        def _():
            acc_ref[...] = jnp.zeros_like(acc_ref)

        acc_ref[...] += lax.dot_general(a_ref[...], b_ref[...], dims, preferred_element_type=F32)

        @pl.when(k == nk - 1)
        def _():
            o_ref[...] = acc_ref[...].astype(o_ref.dtype)

    return pl.pallas_call(
        body,
        name=name,
        grid=(M // tm, N // tn, nk),
        in_specs=[a_spec, b_spec],
        out_specs=pl.BlockSpec((tm, tn), lambda i, j, k: (i, j)),
        out_shape=jax.ShapeDtypeStruct((M, N), out_dtype),
        scratch_shapes=[pltpu.VMEM((tm, tn), F32)],
        compiler_params=_cparams(("parallel", "parallel", "arbitrary")),
    )(a, b)


def _rows(fn, name, tile, row_ins, const_ins, row_outs, acc_outs=(), ncol=1):
    L = row_ins[0][0].shape[0]
    tile = min(tile, L)
    nrow = L // tile
    n_ri, n_ci, n_ro, n_ao = len(row_ins), len(const_ins), len(row_outs), len(acc_outs)

    def body(*refs):
        i = pl.program_id(1)
        j = pl.program_id(0)
        ins = [r[...] for r in refs[: n_ri + n_ci]]
        outs = fn(*ins, j=j)
        o_refs = refs[n_ri + n_ci:]
        for r, v in zip(o_refs[:n_ro], outs[:n_ro]):
            r[...] = v.astype(r.dtype)

        @pl.when(i == 0)
        def _():
            for r in o_refs[n_ro:]:
                r[...] = jnp.zeros_like(r)

        for r, v in zip(o_refs[n_ro:], outs[n_ro:]):
            r[...] += v

    def rspec(bc, cf):
        return pl.BlockSpec((tile, bc), lambda j, i: (i, cf(j)))

    def cspec(r, bc, cf):
        return pl.BlockSpec((r, bc), lambda j, i: (0, cf(j)))

    in_specs = [rspec(bc, cf) for (_, bc, cf) in row_ins]
    in_specs += [cspec(a.shape[0], bc, cf) for (a, bc, cf) in const_ins]
    out_specs = [rspec(bc, cf) for (_, _, bc, cf) in row_outs]
    out_specs += [cspec(1, bc, cf) for (_, bc, cf) in acc_outs]
    out_shape = [jax.ShapeDtypeStruct((L, c), dt) for (c, dt, _, _) in row_outs]
    out_shape += [jax.ShapeDtypeStruct((1, c), F32) for (c, _, _) in acc_outs]
    res = pl.pallas_call(
        body,
        name=name,
        grid=(ncol, nrow),
        in_specs=in_specs,
        out_specs=out_specs,
        out_shape=out_shape,
        compiler_params=_cparams(("arbitrary", "arbitrary")),
    )(*[a for (a, _, _) in row_ins], *[a for (a, _, _) in const_ins])
    return res


def _zero(j):
    return 0


def _whole(a):
    return (a, a.shape[1], _zero)


def _rms(x, w, eps):
    return x * lax.rsqrt(jnp.mean(x * x, axis=-1, keepdims=True) + eps) * w


ROW_TILE = 1024
COL_ROW_TILE = 1024
QK_ROW_TILE = 8192


def _norm_fwd(x, r, w, name):
    C = x.shape[1]
    if r is None:
        def fn(x_, w_, j):
            return (_rms(x_, w_, NORM_EPS),)
        (h,) = _rows(fn, name, ROW_TILE, [_whole(x)], [_whole(w)], [(C, BF16, C, _zero)])
        return x, h

    def fn(x_, r_, w_, j):
        x1 = x_ + r_
        return x1, _rms(x1, w_, NORM_EPS)

    x1, h = _rows(fn, name, ROW_TILE, [_whole(x), _whole(r)], [_whole(w)],
                  [(C, F32, C, _zero), (C, BF16, C, _zero)])
    return x1, h


def _norm_bwd(x, w, dh, dres, name):
    C = x.shape[1]

    def fn(x_, dh_, dres_, w_, j):
        _, vjp = jax.vjp(lambda a, b: _rms(a, b, NORM_EPS), x_, w_)
        dx, dw = vjp(dh_.astype(F32))
        return dres_ + dx, dw

    return _rows(fn, name, ROW_TILE, [_whole(x), _whole(dh), _whole(dres)], [_whole(w)],
                 [(C, F32, C, _zero)], [(C, C, _zero)])


def _silu(x):
    return x * jax.nn.sigmoid(x)


FF_TM = 512
FF_TN = D_FF // 2
_NT = (((1,), (1,)), ((), ()))


def _ffn_in_fused(h, w_in_t, name):
    L, K = h.shape
    tm = min(FF_TM, L)
    nb = D_FF // FF_TN

    def body(a_ref, bg_ref, bu_ref, gu_ref, act_ref):
        g = lax.dot_general(a_ref[...], bg_ref[...], _NT, preferred_element_type=F32)
        u = lax.dot_general(a_ref[...], bu_ref[...], _NT, preferred_element_type=F32)
        gu_ref[0] = g
        gu_ref[1] = u
        act_ref[...] = (_silu(g) * u).astype(act_ref.dtype)

    return pl.pallas_call(
        body,
        name=name,
        grid=(L // tm, nb),
        in_specs=[pl.BlockSpec((tm, K), lambda i, j: (i, 0)),
                  pl.BlockSpec((FF_TN, K), lambda i, j: (j, 0)),
                  pl.BlockSpec((FF_TN, K), lambda i, j: (j + nb, 0))],
        out_specs=[pl.BlockSpec((2, tm, FF_TN), lambda i, j: (0, i, j)),
                   pl.BlockSpec((tm, FF_TN), lambda i, j: (i, j))],
        out_shape=[jax.ShapeDtypeStruct((2, L, D_FF), F32), jax.ShapeDtypeStruct((L, D_FF), BF16)],
        compiler_params=_cparams(("parallel", "parallel")),
    )(h, w_in_t, w_in_t)


def _ffn_out_dx_fused(d16, w_out, gu, name):
    L, K = d16.shape
    tm = min(FF_TM, L)

    def body(a_ref, b_ref, gu_ref, dgu_ref):
        dact = lax.dot_general(a_ref[...], b_ref[...], _NT, preferred_element_type=F32)
        _, vjp = jax.vjp(lambda g, u: _silu(g) * u, gu_ref[0], gu_ref[1])
        dg, du = vjp(dact.astype(BF16).astype(F32))
        dgu_ref[0] = dg.astype(dgu_ref.dtype)
        dgu_ref[1] = du.astype(dgu_ref.dtype)

    return pl.pallas_call(
        body,
        name=name,
        grid=(L // tm, D_FF // FF_TN),
        in_specs=[pl.BlockSpec((tm, K), lambda i, j: (i, 0)),
                  pl.BlockSpec((FF_TN, K), lambda i, j: (j, 0)),
                  pl.BlockSpec((2, tm, FF_TN), lambda i, j: (0, i, j))],
        out_specs=pl.BlockSpec((2, tm, FF_TN), lambda i, j: (0, i, j)),
        out_shape=jax.ShapeDtypeStruct((2, L, D_FF), BF16),
        compiler_params=_cparams(("parallel", "parallel")),
    )(d16, w_out, gu)


def _gated_norm_fwd(y, zx, w, name):
    def fn(y_, z_, w_, j):
        return (_rms(y_ * _silu(z_), w_, GATED_NORM_EPS),)
    (yn,) = _rows(fn, name, COL_ROW_TILE,
                  [(y, GROUP_W, lambda j: j), (zx, GROUP_W, lambda j: j)], [(w, GROUP_W, lambda j: j)],
                  [(D_INNER, BF16, GROUP_W, lambda j: j)], ncol=SSM_GROUPS)
    return yn


def _gated_norm_bwd(y, zx, w, dyn, name):
    def fn(y_, z_, dyn_, w_, j):
        _, vjp = jax.vjp(lambda a, b, c: _rms(a * _silu(b), c, GATED_NORM_EPS), y_, z_, w_)
        return vjp(dyn_.astype(F32))
    cj = lambda j: j
    return _rows(fn, name, COL_ROW_TILE,
                 [(y, GROUP_W, cj), (zx, GROUP_W, cj), (dyn, GROUP_W, cj)], [(w, GROUP_W, cj)],
                 [(D_INNER, F32, GROUP_W, cj), (D_INNER, BF16, GROUP_W, cj)], [(D_INNER, GROUP_W, cj)],
                 ncol=SSM_GROUPS)


def _loss_bwd(x3, f, target, name):
    C = x3.shape[1]

    def fn(x_, f_, t_, j):
        err = (x_ + f_) - t_
        return err * (1.0 / C), jnp.sum(err * err, axis=0, keepdims=True)

    return _rows(fn, name, ROW_TILE, [_whole(x3), _whole(f), _whole(target)], [],
                 [(C, F32, C, _zero)], [(C, C, _zero)])


def _qk_norm_fwd(q, gain, scale, name):
    def fn(q_, g_, j):
        return (_rms(q_, g_, NORM_EPS) * scale,)
    (qn,) = _rows(fn, name, QK_ROW_TILE, [_whole(q)], [_whole(gain)], [(SB_D, BF16, SB_D, _zero)])
    return qn


def _qk_norm_bwd(q, gain, dqn, scale, cot_scale, name):
    def fn(q_, dqn_, g_, j):
        _, vjp = jax.vjp(lambda a, b: _rms(a, b, NORM_EPS) * scale, q_, g_)
        return vjp(dqn_ * cot_scale)
    return _rows(fn, name, QK_ROW_TILE, [_whole(q), _whole(dqn)], [_whole(gain)],
                 [(SB_D, F32, SB_D, _zero)], [(SB_D, SB_D, _zero)])


CONV_TILE = 1024
CONV_BLK = 512
XBC_COL0 = D_INNER // CONV_BLK


def _shift_down(cur, prev8, k):
    if k == 0:
        return cur
    rolled = pltpu.roll(cur, k, 0)
    head_prev = pltpu.roll(prev8, k, 0)
    rid = lax.broadcasted_iota(jnp.int32, (SUBLANE, cur.shape[1]), 0)
    head = jnp.where(rid < k, head_prev, rolled[:SUBLANE])
    if cur.shape[0] == SUBLANE:
        return head
    return jnp.concatenate([head, rolled[SUBLANE:]], axis=0)


def _shift_up(cur, next8, k):
    if k == 0:
        return cur
    T = cur.shape[0]
    rolled = pltpu.roll(cur, T - k, 0)
    tail_next = pltpu.roll(next8, SUBLANE - k, 0)
    rid = lax.broadcasted_iota(jnp.int32, (SUBLANE, cur.shape[1]), 0)
    tail = jnp.where(rid >= SUBLANE - k, tail_next, rolled[T - SUBLANE:])
    return jnp.concatenate([rolled[: T - SUBLANE], tail], axis=0)


def _conv_pre(cur, prev8, w, b):
    pre = b
    for i in range(CONV_W):
        pre = pre + w[i:i + 1, :] * _shift_down(cur, prev8, CONV_W - 1 - i)
    return pre


def _conv_fwd(zx, w8, b, name):
    L = zx.shape[0]
    nrow = L // CONV_TILE
    r8 = CONV_TILE // SUBLANE

    def body(cur_ref, prev_ref, w_ref, b_ref, o_ref):
        i = pl.program_id(1)
        prev8 = jnp.where(i > 0, prev_ref[...], 0.0)
        pre = _conv_pre(cur_ref[...], prev8, w_ref[...], b_ref[...])
        o_ref[...] = _silu(pre)

    return pl.pallas_call(
        body,
        name=name,
        grid=(CONV_DIM // CONV_BLK, nrow),
        in_specs=[
            pl.BlockSpec((CONV_TILE, CONV_BLK), lambda j, i: (i, j + XBC_COL0)),
            pl.BlockSpec((SUBLANE, CONV_BLK), lambda j, i: (jnp.maximum(i * r8 - 1, 0), j + XBC_COL0)),
            pl.BlockSpec((SUBLANE, CONV_BLK), lambda j, i: (0, j)),
            pl.BlockSpec((1, CONV_BLK), lambda j, i: (0, j)),
        ],
        out_specs=pl.BlockSpec((CONV_TILE, CONV_BLK), lambda j, i: (i, j)),
        out_shape=jax.ShapeDtypeStruct((L, CONV_DIM), F32),
        compiler_params=_cparams(("arbitrary", "arbitrary")),
    )(zx, zx, w8, b)


def _conv_bwd(zx, w8, b, dact, name):
    L = zx.shape[0]
    nrow = L // CONV_TILE
    r8 = CONV_TILE // SUBLANE
    last8 = L // SUBLANE - 1

    def body(cur_ref, prev_ref, next_ref, da_ref, dan_ref, w_ref, b_ref, du_ref, dw_ref, db_ref):
        i = pl.program_id(1)
        w = w_ref[...]
        b_ = b_ref[...]
        cur = cur_ref[...]
        prev8 = jnp.where(i > 0, prev_ref[...], 0.0)
        pre = _conv_pre(cur, prev8, w, b_)
        _, vjp = jax.vjp(_silu, pre)
        (dpre,) = vjp(da_ref[...])
        nxt = next_ref[...]
        pre_n = _conv_pre(nxt, cur[CONV_TILE - SUBLANE:], w, b_)
        _, vjp_n = jax.vjp(_silu, pre_n)
        (dpre_n,) = vjp_n(dan_ref[...])
        dpre_n = jnp.where(i < nrow - 1, dpre_n, 0.0)
        du = jnp.zeros_like(cur)
        dws = []
        for k in range(CONV_W):
            wk = w[CONV_W - 1 - k:CONV_W - k, :]
            du = du + wk * _shift_up(dpre, dpre_n, k)
            dws.append(jnp.sum(dpre * _shift_down(cur, prev8, k), axis=0, keepdims=True))
        du_ref[...] = du.astype(du_ref.dtype)
        dw_tile = jnp.concatenate([dws[3], dws[2], dws[1], dws[0]] + [jnp.zeros_like(dws[0])] * 4, axis=0)

        @pl.when(i == 0)
        def _():
            dw_ref[...] = jnp.zeros_like(dw_ref)
            db_ref[...] = jnp.zeros_like(db_ref)

        dw_ref[...] += dw_tile
        db_ref[...] += jnp.sum(dpre, axis=0, keepdims=True)

    return pl.pallas_call(
        body,
        name=name,
        grid=(CONV_DIM // CONV_BLK, nrow),
        in_specs=[
            pl.BlockSpec((CONV_TILE, CONV_BLK), lambda j, i: (i, j + XBC_COL0)),
            pl.BlockSpec((SUBLANE, CONV_BLK), lambda j, i: (jnp.maximum(i * r8 - 1, 0), j + XBC_COL0)),
            pl.BlockSpec((SUBLANE, CONV_BLK), lambda j, i: (jnp.minimum((i + 1) * r8, last8), j + XBC_COL0)),
            pl.BlockSpec((CONV_TILE, CONV_BLK), lambda j, i: (i, j)),
            pl.BlockSpec((SUBLANE, CONV_BLK), lambda j, i: (jnp.minimum((i + 1) * r8, last8), j)),
            pl.BlockSpec((SUBLANE, CONV_BLK), lambda j, i: (0, j)),
            pl.BlockSpec((1, CONV_BLK), lambda j, i: (0, j)),
        ],
        out_specs=[
            pl.BlockSpec((CONV_TILE, CONV_BLK), lambda j, i: (i, j)),
            pl.BlockSpec((SUBLANE, CONV_BLK), lambda j, i: (0, j)),
            pl.BlockSpec((1, CONV_BLK), lambda j, i: (0, j)),
        ],
        out_shape=[
            jax.ShapeDtypeStruct((L, CONV_DIM), BF16),
            jax.ShapeDtypeStruct((SUBLANE, CONV_DIM), F32),
            jax.ShapeDtypeStruct((1, CONV_DIM), F32),
        ],
        compiler_params=_cparams(("arbitrary", "arbitrary")),
    )(zx, zx, zx, dact, dact, w8, b)


HI = lax.Precision.HIGHEST
XS_COL0 = 0
B_COL0 = D_INNER // D_STATE
C_COL0 = B_COL0 + SSM_GROUPS
DT_COL = (D_INNER + CONV_DIM) // DT_PAD


def _dot(a, b, dims=(((1,), (0,)), ((), ())), precision=None):
    return lax.dot_general(a, b, dims, precision=precision, preferred_element_type=F32)


_DOT_DIMS = {
    "nn": (((1,), (0,)), ((), ())),
    "nt": (((1,), (1,)), ((), ())),
    "tn": (((0,), (0,)), ((), ())),
}


@functools.partial(jax.custom_vjp, nondiff_argnums=(2,))
def _bdot(a, b, mode):
    return _dot(a.astype(BF16), b.astype(BF16), _DOT_DIMS[mode])


def _bdot_fwd(a, b, mode):
    return _bdot(a, b, mode), (a, b)


def _bdot_bwd(mode, res, g):
    a, b = res
    if mode == "nn":
        return _bdot(g, b, "nt"), _bdot(a, g, "tn")
    if mode == "nt":
        return _bdot(g, b, "nn"), _bdot(g, a, "tn")
    return _bdot(b, g, "nt"), _bdot(a, g, "nn")


_bdot.defvjp(_bdot_fwd, _bdot_bwd)


def _softplus(x):
    return jnp.maximum(x, 0.0) + jnp.log(1.0 + jnp.exp(-jnp.abs(x)))


def _split2(x):
    hi = x.astype(BF16)
    return hi, (x - hi.astype(F32)).astype(BF16)


@jax.custom_vjp
def _sel_left(m, mT, x):
    hi, lo = _split2(x)
    out = _dot(m, jnp.concatenate([hi, lo], axis=1))
    n = x.shape[1]
    return out[:, :n] + out[:, n:]


_sel_left.defvjp(lambda m, mT, x: (_sel_left(m, mT, x), (m, mT)),
                 lambda res, g: (jnp.zeros_like(res[0]), jnp.zeros_like(res[1]), _sel_left(res[1], res[0], g)))


@jax.custom_vjp
def _sel_right(x, m, mT):
    hi, lo = _split2(x)
    out = _dot(jnp.concatenate([hi, lo], axis=0), m)
    r = x.shape[0]
    return out[:r] + out[r:]


_sel_right.defvjp(lambda x, m, mT: (_sel_right(x, m, mT), (m, mT)),
                  lambda res, g: (_sel_right(g, res[1], res[0]), jnp.zeros_like(res[0]), jnp.zeros_like(res[1])))


def _ssd_consts():
    l_ = lax.broadcasted_iota(jnp.int32, (CHUNK, GROUP_W), 0)
    c_ = lax.broadcasted_iota(jnp.int32, (CHUNK, GROUP_W), 1)
    s_ = c_ % CHUNK
    causal = s_ <= l_
    eye_t = (s_ == l_).astype(F32)
    r0 = lax.broadcasted_iota(jnp.int32, (CHUNK, CHUNK), 0)
    c0 = lax.broadcasted_iota(jnp.int32, (CHUNK, CHUNK), 1)
    tril = (c0 <= r0).astype(BF16)
    triu = (r0 <= c0).astype(BF16)
    rb = lax.broadcasted_iota(jnp.int32, (GROUP_W, GROUP_W), 0) // HEAD_P
    cb = lax.broadcasted_iota(jnp.int32, (GROUP_W, GROUP_W), 1) // HEAD_P
    blockdiag = rb == cb
    return causal, eye_t, tril, triu, blockdiag


def _ssd_expand(g):
    h = lax.broadcasted_iota(jnp.int32, (DT_PAD, GROUP_W), 0)
    c = lax.broadcasted_iota(jnp.int32, (DT_PAD, GROUP_W), 1)
    cT = lax.broadcasted_iota(jnp.int32, (GROUP_W, DT_PAD), 0)
    hT = lax.broadcasted_iota(jnp.int32, (GROUP_W, DT_PAD), 1)
    hpg = GROUP_W // HEAD_P
    return (h == g * hpg + c // HEAD_P).astype(BF16), (hT == g * hpg + cT // HEAD_P).astype(BF16)


def _ssd_chunk(S, xs, dt_raw, dt_bias, alog_e, d_e, Bm, Cm, E):
    causal, eye_t, tril, triu, blockdiag = _ssd_consts()
    ones = jnp.ones((CHUNK, CHUNK), BF16)
    dt = _softplus(dt_raw + dt_bias)
    dtx = _sel_right(dt, E[0], E[1])
    a = dtx * (-jnp.exp(alog_e))
    acs = _sel_left(tril, triu, a)
    rowv = _sel_left(ones, ones, acs * eye_t)
    seg = acs - rowv
    Lc = jnp.where(causal, jnp.exp(jnp.where(causal, seg, 0.0)), 0.0)
    xdt = xs * dtx
    Bt = jnp.concatenate([Bm] * 4, axis=0)
    CBc = _bdot(Cm, Bt, "nt")
    Xbd = jnp.where(blockdiag, jnp.concatenate([xdt] * 4, axis=0), 0.0)
    y_intra = _bdot(CBc * Lc, Xbd, "nn")
    y_inter = _bdot(Cm, S, "nn") * jnp.exp(acs)
    y = y_intra + y_inter + d_e * xs
    last = jnp.sum(a, axis=0, keepdims=True)
    dec_end = jnp.exp(last - acs)
    S_new = S * jnp.exp(last) + _bdot(Bm, xdt * dec_end, "tn")
    return S_new, y


SSD_GPS = 8
SSD_W = SSD_GPS * GROUP_W
SSD_N = SSD_GPS * D_STATE
SSD_STEPS = SSM_GROUPS // SSD_GPS


def _ssd_in_specs(cmap):
    return [
        pl.BlockSpec((CHUNK, SSD_W), lambda c, g: (cmap(c), g)),
        pl.BlockSpec((CHUNK, DT_PAD), lambda c, g: (cmap(c), DT_COL)),
        pl.BlockSpec((1, DT_PAD), lambda c, g: (0, 0)),
        pl.BlockSpec((1, SSD_W), lambda c, g: (0, g)),
        pl.BlockSpec((1, SSD_W), lambda c, g: (0, g)),
        pl.BlockSpec((CHUNK, SSD_N), lambda c, g: (cmap(c), B_COL0 // SSD_GPS + g)),
        pl.BlockSpec((CHUNK, SSD_N), lambda c, g: (cmap(c), C_COL0 // SSD_GPS + g)),
    ]


def _gw(u):
    return slice(u * GROUP_W, (u + 1) * GROUP_W)


def _gn(u):
    return slice(u * D_STATE, (u + 1) * D_STATE)


def _ssd_fwd(xbc, zx, dt_bias, alog_e, d_e, name, travel=None):
    L = xbc.shape[0]
    nc = L // CHUNK
    n_in = 7 + (travel is not None)

    def body(*refs):
        xs_ref, dtr_ref, bias_ref, alog_ref, d_ref, b_ref, c_ref = refs[:7]
        y_ref, st_ref = refs[n_in:n_in + 2]
        S_ref = refs[n_in + 2 + (travel is not None)]
        c = pl.program_id(0)
        step = pl.program_id(1)
        g0 = step * SSD_GPS if SSD_STEPS > 1 else 0
        if travel is not None:
            moves = lambda: _gather_moves(refs[7], refs[n_in + 2], *refs[n_in + 4:])
            pl.when(jnp.logical_and(c == 0, step == 0))(lambda: moves().start())

        @pl.when(c == 0)
        def _():
            for u in range(SSD_GPS):
                S_ref[g0 + u] = jnp.zeros((D_STATE, GROUP_W), F32)

        dtr, bias = dtr_ref[...], bias_ref[...]
        for u in range(SSD_GPS):
            S = S_ref[g0 + u]
            st_ref[u] = S
            S_new, y = _ssd_chunk(S, xs_ref[:, _gw(u)], dtr, bias, alog_ref[:, _gw(u)], d_ref[:, _gw(u)],
                                  b_ref[:, _gn(u)], c_ref[:, _gn(u)], _ssd_expand(g0 + u))
            y_ref[:, _gw(u)] = y
            S_ref[g0 + u] = S_new

        if travel is not None:
            pl.when(jnp.logical_and(c == nc - 1, step == SSD_STEPS - 1))(lambda: moves().wait())

    extra = travel is not None
    return pl.pallas_call(
        body,
        name=name,
        grid=(nc, SSD_STEPS),
        in_specs=_ssd_in_specs(lambda c: c) + [ANY] * extra,
        out_specs=[
            pl.BlockSpec((CHUNK, SSD_W), lambda c, g: (c, g)),
            pl.BlockSpec((None, SSD_GPS, D_STATE, GROUP_W), lambda c, g: (c, g, 0, 0)),
        ] + [ANY] * extra,
        out_shape=[
            jax.ShapeDtypeStruct((L, D_INNER), F32),
            jax.ShapeDtypeStruct((nc, SSM_GROUPS, D_STATE, GROUP_W), F32),
        ] + ([jax.ShapeDtypeStruct((N_CHIPS,) + travel.shape, travel.dtype)] if extra else []),
        scratch_shapes=[pltpu.VMEM((SSM_GROUPS, D_STATE, GROUP_W), F32)]
        + ((_MOVE_SEMS + [pltpu.SemaphoreType.DMA]) if extra else []),
        compiler_params=_cparams(("arbitrary", "arbitrary")),
    )(xbc, zx, dt_bias, alog_e, d_e, xbc, xbc, *([travel] if extra else []))


def _ssd_bwd(xbc, zx, dt_bias, alog_e, d_e, states, dy, name, travel=None):
    assert SSD_STEPS == 1, "one grid step writes the whole d (xs | B | C) row block"
    L = xbc.shape[0]
    nc = L // CHUNK
    rev = lambda c: nc - 1 - c
    n_in = 9 + (travel is not None)

    def body(*refs):
        xs_ref, dtr_ref, bias_ref, alog_ref, d_ref, b_ref, c_ref, st_ref, dy_ref = refs[:9]
        dact_ref, ddt_ref, dbias_ref, dalog_ref, dd_ref = refs[n_in:n_in + 5]
        dS_ref = refs[n_in + 5 + (travel is not None)]
        c = pl.program_id(0)
        step = pl.program_id(1)
        g0 = 0
        first = jnp.logical_and(c == 0, step == 0)
        if travel is not None:
            moves = lambda: _scatter_moves(refs[9], refs[n_in + 5], *refs[n_in + 7:])
            pl.when(first)(lambda: moves().start())

        @pl.when(c == 0)
        def _():
            for u in range(SSD_GPS):
                dS_ref[g0 + u] = jnp.zeros((D_STATE, GROUP_W), F32)
                dalog_ref[g0 + u] = jnp.zeros((1, GROUP_W), F32)
                dd_ref[g0 + u] = jnp.zeros((1, GROUP_W), F32)

        @pl.when(first)
        def _():
            dbias_ref[...] = jnp.zeros_like(dbias_ref)

        dtr, bias = dtr_ref[...], bias_ref[...]
        ddt_sum = jnp.zeros((CHUNK, DT_PAD), F32)
        dbias_sum = jnp.zeros((1, DT_PAD), F32)
        for u in range(SSD_GPS):
            E = _ssd_expand(g0 + u)
            _, vjp = jax.vjp(
                lambda S, xs, dtr_, bias_, alog, dsk, Bm, Cm: _ssd_chunk(S, xs, dtr_, bias_, alog, dsk, Bm, Cm, E),
                st_ref[u], xs_ref[:, _gw(u)], dtr, bias, alog_ref[:, _gw(u)], d_ref[:, _gw(u)],
                b_ref[:, _gn(u)], c_ref[:, _gn(u)])
            dS, dxs, ddtr, dbias, dalog, dd, dB, dC = vjp((dS_ref[g0 + u], dy_ref[:, _gw(u)]))
            dS_ref[g0 + u] = dS
            dact_ref[:, _gw(u)] = dxs
            dact_ref[:, slice(D_INNER + u * D_STATE, D_INNER + (u + 1) * D_STATE)] = dB
            dact_ref[:, slice(D_INNER + SSD_N + u * D_STATE, D_INNER + SSD_N + (u + 1) * D_STATE)] = dC
            dalog_ref[g0 + u] += dalog
            dd_ref[g0 + u] += dd
            ddt_sum = ddt_sum + ddtr
            dbias_sum = dbias_sum + dbias

        ddt_ref[...] = ddt_sum
        dbias_ref[...] += dbias_sum
        if travel is not None:
            pl.when(jnp.logical_and(c == nc - 1, step == SSD_STEPS - 1))(lambda: moves().wait())

    extra = travel is not None
    whole3 = pl.BlockSpec((SSM_GROUPS, 1, GROUP_W), lambda c, g: (0, 0, 0))
    return pl.pallas_call(
        body,
        name=name,
        grid=(nc, SSD_STEPS),
        in_specs=_ssd_in_specs(rev) + [
            pl.BlockSpec((None, SSD_GPS, D_STATE, GROUP_W), lambda c, g: (rev(c), g, 0, 0)),
            pl.BlockSpec((CHUNK, SSD_W), lambda c, g: (rev(c), g)),
        ] + [ANY] * extra,
        out_specs=[
            pl.BlockSpec((CHUNK, CONV_DIM), lambda c, g: (rev(c), 0)),
            pl.BlockSpec((CHUNK, DT_PAD), lambda c, g: (rev(c), 0)),
            pl.BlockSpec((1, DT_PAD), lambda c, g: (0, 0)),
            whole3,
            whole3,
        ] + [ANY] * extra,
        out_shape=[
            jax.ShapeDtypeStruct((L, CONV_DIM), F32),
            jax.ShapeDtypeStruct((L, DT_PAD), F32),
            jax.ShapeDtypeStruct((1, DT_PAD), F32),
            jax.ShapeDtypeStruct((SSM_GROUPS, 1, GROUP_W), F32),
            jax.ShapeDtypeStruct((SSM_GROUPS, 1, GROUP_W), F32),
        ] + ([jax.ShapeDtypeStruct((N_CHIPS - 1,) + travel.shape[1:], travel.dtype)] if extra else []),
        scratch_shapes=[pltpu.VMEM((SSM_GROUPS, D_STATE, GROUP_W), F32)] + (_MOVE_SEMS if extra else []),
        compiler_params=_cparams(("arbitrary", "arbitrary")),
    )(xbc, zx, dt_bias, alog_e, d_e, xbc, xbc, states, dy, *([travel] if extra else []))


LOG2E = math.log2(math.e)
LN2 = math.log(2.0)
SB_Q_SCALE = SB_D ** -0.5 * LOG2E


def _sb_scores(q, kj):
    z2 = _dot(q, kj, dims=(((1,), (1,)), ((), ())))
    lb = jnp.minimum(z2, 0.0) - jnp.log2(1.0 + jnp.exp2(-jnp.abs(z2)))
    return lb, lb - z2


SB_TQ_FWD = 512
SB_TQ_BWD = 512
SB_TK = 512
SB_TC = SB_TK // 2


def _sb_after():
    j = lax.broadcasted_iota(jnp.int32, (2 * SB_TC, SB_TC), 0) % SB_TC
    s = lax.broadcasted_iota(jnp.int32, (2 * SB_TC, SB_TC), 1)
    return (j > s).astype(BF16)


def _sb_suffix(x, after):
    T = x.shape[0]
    hi = x.astype(BF16)
    lo = (x - hi.astype(F32)).astype(BF16)
    stacked = jnp.concatenate([jnp.concatenate([hi[:, :SB_TC], lo[:, :SB_TC]], axis=1),
                               jnp.concatenate([hi[:, SB_TC:], lo[:, SB_TC:]], axis=1)], axis=0)
    out = _dot(stacked, after)
    rs_right = jnp.sum(x[:, SB_TC:], axis=1, keepdims=True)
    return (jnp.concatenate([out[:T] + rs_right, out[T:]], axis=1),
            rs_right + jnp.sum(x[:, :SB_TC], axis=1, keepdims=True))


SB_ROWS_FWD = 256
SB_ROWS_BWD = 128


def _sb_seen(rows, row0, start):
    t = lax.broadcasted_iota(jnp.int32, (rows, SB_TK), 0) + row0
    s = lax.broadcasted_iota(jnp.int32, (rows, SB_TK), 1) + start
    return s < t


def _sb_walk(gd, groups, carry):
    carry = lax.cond(gd % 2 == 1, lambda c: groups([gd, gd - 1], c, True), lambda c: groups([gd], c, True), carry)
    base = gd - 1 - gd % 2
    return lax.fori_loop(0, gd // 2, lambda n, c: groups([base - 2 * n, base - 2 * n - 1], c, False), carry)


def _sb_fwd(qs, kn, v, name):
    H, L, d = qs.shape
    SB_TQ, SB_ROWS = SB_TQ_FWD, SB_ROWS_FWD
    nq = L // SB_TQ
    after = _sb_after()

    def body(q_ref, k_ref, v_ref, aft_ref, o_ref):
        i = pl.program_id(1)
        gd = (i * SB_TQ) // SB_TK
        subs = range(SB_TQ // SB_ROWS)
        qr = [q_ref[r * SB_ROWS:(r + 1) * SB_ROWS, :] for r in subs]

        def prep(G, diag, r):
            start = pl.multiple_of(G * SB_TK, SB_TK)
            lb, lk = _sb_scores(qr[r], k_ref[pl.ds(start, SB_TK), :])
            seen = _sb_seen(SB_ROWS, i * SB_TQ + r * SB_ROWS, start) if diag else None
            if diag:
                lk = jnp.where(seen, lk, 0.0)
            between, rs = _sb_suffix(lk, aft_ref[...])
            return lb + between, rs, seen, v_ref[pl.ds(start, SB_TK), :]

        def groups(Gs, carry, diag):
            carry = list(carry)
            pre = [[prep(G, diag and n == 0, r) for n, G in enumerate(Gs)] for r in subs]
            for r in subs:
                R, acc = carry[r]
                for s, rs, seen, vg in pre[r]:
                    w = jnp.exp2(s + R)
                    if seen is not None:
                        w = jnp.where(seen, w, 0.0)
                    acc = acc + _dot(w.astype(BF16), vg)
                    R = R + rs
                carry[r] = (R, acc)
            return tuple(carry)

        zero = (jnp.zeros((SB_ROWS, 1), F32), jnp.zeros((SB_ROWS, d), F32))
        out = _sb_walk(gd, groups, tuple(zero for _ in subs))
        for r in subs:
            o_ref[r * SB_ROWS:(r + 1) * SB_ROWS, :] = out[r][1]

    return pl.pallas_call(
        body,
        name=name,
        grid=(H, nq),
        in_specs=[
            pl.BlockSpec((None, SB_TQ, d), lambda h, i: (h, i, 0)),
            pl.BlockSpec((None, L, d), lambda h, i: (h, 0, 0)),
            pl.BlockSpec((None, L, d), lambda h, i: (h, 0, 0)),
            pl.BlockSpec((2 * SB_TC, SB_TC), lambda h, i: (0, 0)),
        ],
        out_specs=pl.BlockSpec((None, SB_TQ, d), lambda h, i: (h, i, 0)),
        out_shape=jax.ShapeDtypeStruct((H, L, d), F32),
        compiler_params=_cparams(("parallel", "arbitrary")),
    )(qs, kn, v, after)


def _sb_bwd(qs, kn, v, o, do, name):
    H, L, d = qs.shape
    SB_TQ, SB_ROWS = SB_TQ_BWD, SB_ROWS_BWD
    nq = L // SB_TQ
    after = _sb_after()
    qsT = qs.transpose(0, 2, 1)
    do16 = do.astype(BF16)
    doT = do16.transpose(0, 2, 1)

    def body(q_ref, qT_ref, k_ref, v_ref, o_ref, do_ref, doT_ref, aft_ref, dq_ref, dkT_ref, dvT_ref):
        i = pl.program_id(1)

        @pl.when(i == 0)
        def _():
            dkT_ref[...] = jnp.zeros_like(dkT_ref)
            dvT_ref[...] = jnp.zeros_like(dvT_ref)

        gd = (i * SB_TQ) // SB_TK
        subs = range(SB_TQ // SB_ROWS)
        rows = [slice(r * SB_ROWS, (r + 1) * SB_ROWS) for r in subs]
        qr = [q_ref[rows[r], :] for r in subs]
        qT = [qT_ref[:, rows[r]] for r in subs]
        do_ = [do_ref[rows[r], :] for r in subs]
        doT_ = [doT_ref[:, rows[r]] for r in subs]
        D = [jnp.sum(o_ref[rows[r], :] * do_[r].astype(F32), axis=1, keepdims=True) for r in subs]

        def prep(G, diag, r):
            start = pl.multiple_of(G * SB_TK, SB_TK)
            kg = k_ref[pl.ds(start, SB_TK), :]
            lb, lk = _sb_scores(qr[r], kg)
            seen = _sb_seen(SB_ROWS, i * SB_TQ + r * SB_ROWS, start) if diag else None
            if diag:
                lk = jnp.where(seen, lk, 0.0)
            between, rs = _sb_suffix(lk, aft_ref[...])
            dw = _dot(do_[r], v_ref[pl.ds(start, SB_TK), :], dims=(((1,), (1,)), ((), ())))
            return start, kg, lb + between, rs, dw, jnp.exp2(lb), seen

        def groups(Gs_, carry, diag):
            carry = list(carry)
            pre = [[prep(G, diag and n == 0, r) for n, G in enumerate(Gs_)] for r in subs]
            for r in subs:
                R, Gs, dq = carry[r]
                for start, kg, s, rs, dw, sig, seen in pre[r]:
                    w = jnp.exp2(s + R)
                    if seen is not None:
                        w = jnp.where(seen, w, 0.0)
                    w16 = w.astype(BF16)
                    g = w16.astype(F32) * dw
                    dvT_ref[:, pl.ds(start, SB_TK)] += _dot(doT_[r], w16)
                    g_right, gsum = _sb_suffix(g, aft_ref[...])
                    dz = g - sig * (D[r] - Gs - g_right)
                    if seen is not None:
                        dz = jnp.where(seen, dz, 0.0)
                    dz16 = dz.astype(BF16)
                    dq = dq + _dot(dz16, kg)
                    dkT_ref[:, pl.ds(start, SB_TK)] += _dot(qT[r], dz16)
                    R = R + rs
                    Gs = Gs + gsum
                carry[r] = (R, Gs, dq)
            return tuple(carry)

        zero1 = jnp.zeros((SB_ROWS, 1), F32)
        out = _sb_walk(gd, groups, tuple((zero1, zero1, jnp.zeros((SB_ROWS, d), F32)) for _ in subs))
        for r in subs:
            dq_ref[rows[r], :] = out[r][2]

    blk = pl.BlockSpec((None, SB_TQ, d), lambda h, i: (h, i, 0))
    blkT = pl.BlockSpec((None, d, SB_TQ), lambda h, i: (h, 0, i))
    full = pl.BlockSpec((None, L, d), lambda h, i: (h, 0, 0))
    fullT = pl.BlockSpec((None, d, L), lambda h, i: (h, 0, 0))
    return pl.pallas_call(
        body,
        name=name,
        grid=(H, nq),
        in_specs=[blk, blkT, full, full, blk, blk, blkT, pl.BlockSpec((2 * SB_TC, SB_TC), lambda h, i: (0, 0))],
        out_specs=[blk, fullT, fullT],
        out_shape=[jax.ShapeDtypeStruct((H, L, d), F32), jax.ShapeDtypeStruct((H, d, L), F32),
                   jax.ShapeDtypeStruct((H, d, L), F32)],
        compiler_params=_cparams(("parallel", "arbitrary")),
    )(qs, qsT, kn, v, o, do16, doT, after)


def _fold_heads(rows, name):
    def body(x_ref, o_ref):
        c = lax.broadcasted_iota(jnp.int32, (D_INNER, DT_PAD), 0)
        h = lax.broadcasted_iota(jnp.int32, (D_INNER, DT_PAD), 1)
        o_ref[...] = _dot(x_ref[...], (c // HEAD_P == h).astype(F32), precision=HI)

    return pl.pallas_call(
        body, name=name, out_shape=jax.ShapeDtypeStruct((SUBLANE, DT_PAD), F32),
        compiler_params=_cparams(),
    )(rows)


def _ffn_fwd(x, r, norm_w, w_in_t, w_out, tag):
    x1, h = _norm_fwd(x, r, norm_w, f"norm_ffn_fwd{tag}")
    gu, act = _ffn_in_fused(h, w_in_t, f"ffn_in_fwd{tag}")
    f = _mm(act, w_out, "nn", F32, f"ffn_out_fwd{tag}")
    return x1, h, gu, act, f


def _ffn_bwd(x1, h, gu, act, norm_w, w_in_t, w_out, dres, tag):
    d16 = dres.astype(BF16)
    dgu = _ffn_out_dx_fused(d16, w_out, gu, f"ffn_out_dx{tag}")
    g_w_out = _mm(act, d16, "tn", F32, f"ffn_out_dw{tag}")
    dh = _mm(dgu, w_in_t, "nn", BF16, f"ffn_in_dx{tag}")
    g_w_in_t = _mm(dgu, h, "tn", F32, f"ffn_in_dw{tag}")
    dres, g_norm = _norm_bwd(x1, norm_w, dh, dres, f"norm_ffn_bwd{tag}")
    return dres, g_norm, g_w_in_t, g_w_out


def _rest_weights(gb):
    whole = _whole_of_shards(gb, SEG_REST)
    return {"ssm_w_out": whole["ssm_w_out"][0], "sb_w_qkv_t": whole["sb_w_qkv"][0], "sb_w_o": whole["sb_w_o"][0],
            "ffn_w_in_t": whole["ffn_w_in"], "ffn_w_out": whole["ffn_w_out"]}


def _local_step(x, target, W, wb_rest=None):
    L = x.shape[0]
    H = SB_HEADS
    nm0, nm1 = W["norm_mix"][0:1], W["norm_mix"][1:2]
    nf0, nf1 = W["norm_ffn"][0:1], W["norm_ffn"][1:2]

    _, h0 = _norm_fwd(x, None, nm0, "norm_mix_fwd0")
    zx = _mm(h0, W["ssm_w_in_t"], "nt", F32, "ssm_in_fwd")
    xbc = _conv_fwd(zx, W["conv_w8"], W["conv_b"], "conv_fwd")
    if wb_rest is None:
        y, states = _ssd_fwd(xbc, zx, W["dt_bias"], W["alog_e"], W["d_e"], "ssd_fwd")
    else:
        y, states, gb_rest = _ssd_fwd(xbc, zx, W["dt_bias"], W["alog_e"], W["d_e"], "ssd_fwd", travel=wb_rest)
        W = {**W, **_rest_weights(gb_rest)}
    yn = _gated_norm_fwd(y, zx, W["ssm_norm_w"], "gated_norm_fwd")
    mix0 = _mm(yn, W["ssm_w_out"], "nn", F32, "ssm_out_fwd")
    x1, h1, gu0, act0, f0 = _ffn_fwd(x, mix0, nf0, W["ffn_w_in_t"][0], W["ffn_w_out"][0], "0")

    x2, h2 = _norm_fwd(x1, f0, nm1, "norm_mix_fwd1")
    qkv = _mm(h2, W["sb_w_qkv_t"], "nt", F32, "sb_qkv_fwd")
    qkv_t = qkv.reshape(L, 3, H, SB_D).transpose(1, 2, 0, 3)
    q_r = qkv_t[0].reshape(H * L, SB_D)
    k_r = qkv_t[1].reshape(H * L, SB_D)
    qs = _qk_norm_fwd(q_r, W["sb_q_gain"], SB_Q_SCALE, "q_norm_fwd").reshape(H, L, SB_D)
    kn = _qk_norm_fwd(k_r, W["sb_k_gain"], 1.0, "k_norm_fwd").reshape(H, L, SB_D)
    vb = qkv_t[2].astype(BF16)
    o = _sb_fwd(qs, kn, vb, "sb_fwd")
    o_flat = o.transpose(1, 0, 2).reshape(L, D_MODEL).astype(BF16)
    mix1 = _mm(o_flat, W["sb_w_o"], "nn", F32, "sb_o_fwd")
    x3, h3, gu1, act1, f1 = _ffn_fwd(x2, mix1, nf1, W["ffn_w_in_t"][1], W["ffn_w_out"][1], "1")

    dres, sq = _loss_bwd(x3, f1, target, "loss")

    dres, g_nf1, g_fin1, g_fout1 = _ffn_bwd(x3, h3, gu1, act1, nf1, W["ffn_w_in_t"][1], W["ffn_w_out"][1], dres, "1")
    d16 = dres.astype(BF16)
    do_flat = _mm(d16, W["sb_w_o"], "nt", F32, "sb_o_dx")
    g_w_o = _mm(o_flat, d16, "tn", F32, "sb_o_dw")
    do = do_flat.reshape(L, H, SB_D).transpose(1, 0, 2)
    dqs, dknT, dvT = _sb_bwd(qs, kn, vb, o, do, "sb_bwd")
    dkn = dknT.transpose(0, 2, 1)
    dq, g_qg = _qk_norm_bwd(q_r, W["sb_q_gain"], dqs.reshape(H * L, SB_D), SB_Q_SCALE, LN2, "q_norm_bwd")
    dk, g_kg = _qk_norm_bwd(k_r, W["sb_k_gain"], dkn.reshape(H * L, SB_D), 1.0, LN2, "k_norm_bwd")
    dqkv = jnp.stack([dq.reshape(H, L, SB_D), dk.reshape(H, L, SB_D), dvT.transpose(0, 2, 1)])
    dqkv = dqkv.transpose(2, 0, 1, 3).reshape(L, 3 * D_MODEL).astype(BF16)
    dh2 = _mm(dqkv, W["sb_w_qkv_t"], "nn", BF16, "sb_qkv_dx")
    g_w_qkv_t = _mm(dqkv, h2, "tn", F32, "sb_qkv_dw")
    dres, g_nm1 = _norm_bwd(x2, nm1, dh2, dres, "norm_mix_bwd1")

    dres, g_nf0, g_fin0, g_fout0 = _ffn_bwd(x1, h1, gu0, act0, nf0, W["ffn_w_in_t"][0], W["ffn_w_out"][0], dres, "0")
    d16 = dres.astype(BF16)
    dyn = _mm(d16, W["ssm_w_out"], "nt", BF16, "ssm_out_dx")
    g_w_out = _mm(yn, d16, "tn", F32, "ssm_out_dw")
    dy, dz, g_snw = _gated_norm_bwd(y, zx, W["ssm_norm_w"], dyn, "gated_norm_bwd")
    rest = {"ssm_w_out": [g_w_out], "sb_w_qkv": [g_w_qkv_t], "sb_w_o": [g_w_o],
            "ffn_w_in": [g_fin0, g_fin1], "ffn_w_out": [g_fout0, g_fout1]}
    if wb_rest is None:
        dact, ddt, g_dtb, g_alog_e, g_d_e = _ssd_bwd(
            xbc, zx, W["dt_bias"], W["alog_e"], W["d_e"], states, dy, "ssd_bwd")
        exchanged = None
    else:
        slots = jnp.stack([_shard_of_whole(rest, s, SEG_REST) for s in range(N_CHIPS)])
        dact, ddt, g_dtb, g_alog_e, g_d_e, received = _ssd_bwd(
            xbc, zx, W["dt_bias"], W["alog_e"], W["d_e"], states, dy, "ssd_bwd", travel=slots.astype(BF16))
        exchanged = (slots, received)
    dxbc, g_cw8, g_cb = _conv_bwd(zx, W["conv_w8"], W["conv_b"], dact, "conv_bwd")
    dzx = jnp.concatenate([dz, dxbc, ddt.astype(BF16)], axis=1)
    dh0 = _mm(dzx, W["ssm_w_in_t"], "nn", BF16, "ssm_in_dx")
    g_w_in_t = _mm(dzx, h0, "tn", F32, "ssm_in_dw")
    grad_x, g_nm0 = _norm_bwd(x, nm0, dh0, dres, "norm_mix_bwd0")

    per_head = jnp.concatenate(
        [g_alog_e.reshape(1, D_INNER), g_d_e.reshape(1, D_INNER), jnp.zeros((SUBLANE - 2, D_INNER), F32)], axis=0)
    folded = _fold_heads(per_head, "fold_heads")
    grads = {
        "norm_mix": jnp.concatenate([g_nm0, g_nm1], axis=0),
        "norm_ffn": jnp.concatenate([g_nf0, g_nf1], axis=0),
        "ssm_w_in": [g_w_in_t],
        "ssm_conv_w": [g_cw8],
        "ssm_conv_b": g_cb,
        "ssm_dt_bias": g_dtb[:, :SSM_HEADS],
        "ssm_a_log": folded[0:1, :SSM_HEADS],
        "ssm_d": folded[1:2, :SSM_HEADS],
        "ssm_norm_w": g_snw,
        "sb_q_gain": g_qg,
        "sb_k_gain": g_kg,
        **rest,
    }
    return (sq, grad_x, grads) if wb_rest is None else (sq, grad_x, grads, exchanged)


def _prep_weights(full):
    W = _prep_vectors(full)
    w_in_t = full["ssm_w_in"].T
    W["ssm_w_in_t"] = jnp.pad(w_in_t, ((0, D_IN_PAD - D_IN_PROJ), (0, 0))).astype(BF16)
    W["conv_w8"] = jnp.pad(full["ssm_conv_w"], ((0, SUBLANE - CONV_W), (0, 0)))
    W["ssm_w_out"] = full["ssm_w_out"].astype(BF16)
    W["sb_w_qkv_t"] = full["sb_w_qkv"].T.astype(BF16)
    W["sb_w_o"] = full["sb_w_o"].astype(BF16)
    W["ffn_w_in_t"] = jnp.swapaxes(full["ffn_w_in"], 1, 2).astype(BF16)
    W["ffn_w_out"] = full["ffn_w_out"].astype(BF16)
    return W


def _prep_vectors(full):
    W = {}
    W["norm_mix"] = full["norm_mix"]
    W["norm_ffn"] = full["norm_ffn"]
    W["conv_b"] = full["ssm_conv_b"]
    W["dt_bias"] = jnp.pad(full["ssm_dt_bias"], ((0, 0), (0, DT_PAD - SSM_HEADS)))
    W["alog_e"] = jnp.repeat(full["ssm_a_log"], HEAD_P, axis=1)
    W["d_e"] = jnp.repeat(full["ssm_d"], HEAD_P, axis=1)
    W["ssm_norm_w"] = full["ssm_norm_w"]
    W["sb_q_gain"] = full["sb_q_gain"]
    W["sb_k_gain"] = full["sb_k_gain"]
    return W


N_CHIPS = 4
N_DEV = 8
FLAT_W = D_MODEL

SEG_FIRST = (
    ("ssm_w_in", (1, D_MODEL, D_IN_PROJ // N_CHIPS), True),
    ("ssm_conv_w", (1, CONV_W, CONV_DIM // N_CHIPS), False),
)
SEG_REST = (
    ("ssm_w_out", (1, D_INNER // N_CHIPS, D_MODEL), False),
    ("sb_w_qkv", (1, D_MODEL, 3 * D_MODEL // N_CHIPS), True),
    ("sb_w_o", (1, D_MODEL // N_CHIPS, D_MODEL), False),
    ("ffn_w_in", (2, D_MODEL, 2 * D_FF // N_CHIPS), True),
    ("ffn_w_out", (2, D_FF // N_CHIPS, D_MODEL), False),
)
SHARDED = SEG_FIRST + SEG_REST
REPLICATED = (
    ("norm_mix", (2, D_MODEL)), ("norm_ffn", (2, D_MODEL)), ("ssm_conv_b", (1, CONV_DIM)),
    ("ssm_norm_w", (1, D_INNER)), ("ssm_dt_bias", (1, SSM_HEADS)), ("ssm_a_log", (1, SSM_HEADS)),
    ("ssm_d", (1, SSM_HEADS)), ("sb_q_gain", (1, SB_D)), ("sb_k_gain", (1, SB_D)),
)
ADAM_TILE = 256
_USED_SMALL = sum(math.prod(s) for _, s in REPLICATED)
SMALL_ROWS = -(-_USED_SMALL // (SUBLANE * FLAT_W)) * SUBLANE


def _flat_rows(shape):
    return math.prod(shape) // FLAT_W


def _seg_rows(seg):
    used = sum(_flat_rows(s) for _, s, _ in seg)
    return used, -(-used // ADAM_TILE) * ADAM_TILE


def _pad_rows(parts, seg, dtype):
    used, rows = _seg_rows(seg)
    return parts + [jnp.zeros((rows - used, FLAT_W), dtype)] if rows > used else parts


def _pack_shard(d, dtype, seg):
    parts = []
    for n, _, transposed in seg:
        a = jnp.swapaxes(d[n], 1, 2) if transposed else d[n]
        parts.append(a.reshape(-1, FLAT_W).astype(dtype))
    return jnp.concatenate(_pad_rows(parts, seg, dtype), axis=0)


def _unpack_shard(flat, seg):
    out, r = {}, 0
    for n, s, transposed in seg:
        k = _flat_rows(s)
        a = flat[r:r + k]
        out[n] = jnp.swapaxes(a.reshape(s[0], s[2], s[1]), 1, 2) if transposed else a.reshape(s)
        r += k
    return out


def _shard_of_whole(whole, chip, seg):
    parts = []
    for n, s, _ in seg:
        k = _flat_rows(s) // s[0]
        if n == "ssm_conv_w":
            parts.append(whole[n][0][:CONV_W, chip * FLAT_W:(chip + 1) * FLAT_W])
        else:
            parts.extend(a[chip * k:(chip + 1) * k] for a in whole[n])
    return jnp.concatenate(_pad_rows(parts, seg, parts[0].dtype), axis=0)


def _whole_of_shards(gb, seg):
    out, r = {}, 0
    for n, s, _ in seg:
        k = _flat_rows(s) // s[0]
        if n != "ssm_conv_w":
            out[n] = [jnp.concatenate([gb[p, r + l * k:r + (l + 1) * k] for p in range(N_CHIPS)], axis=0)
                      for l in range(s[0])]
        r += k * s[0]
    return out


def _pack_small(d):
    parts = [d[n].reshape(-1) for n, _ in REPLICATED]
    parts.append(jnp.zeros((SMALL_ROWS * FLAT_W - _USED_SMALL,), parts[0].dtype))
    return jnp.concatenate(parts).reshape(SMALL_ROWS, FLAT_W)


def _unpack_small(flat):
    flat = flat.reshape(-1)
    out, r = {}, 0
    for n, s in REPLICATED:
        k = math.prod(s)
        out[n] = flat[r:r + k].reshape(s)
        r += k
    return out


ANY = pl.BlockSpec(memory_space=pl.ANY)


def _other_chips():
    x, y, c = lax.axis_index("x"), lax.axis_index("y"), lax.axis_index("c")
    return x, y, c, [(1 - x, y), (x, 1 - y), (1 - x, 1 - y)]


def _remote(src, dst, send_sem, recv_sem, dev):
    return pltpu.make_async_remote_copy(src_ref=src, dst_ref=dst, send_sem=send_sem, recv_sem=recv_sem,
                                        device_id=dev, device_id_type=MESH_T)


def _gather_weights(wb, ws):
    def body(wb_ref, ws_ref, gb_ref, gs_ref, send_sems, recv_sems, loc_sems):
        x, y, c, others = _other_chips()
        me = 2 * x + y
        pairs = ((wb_ref, gb_ref), (ws_ref, gs_ref))
        local = [pltpu.make_async_copy(src, dst.at[me], loc_sems.at[t]) for t, (src, dst) in enumerate(pairs)]
        for cp in local:
            cp.start()
        sends = []
        for k, (px, py) in enumerate(others):
            for t, (src, dst) in enumerate(pairs):
                cp = _remote(src, dst.at[me], send_sems.at[2 * k + t], recv_sems.at[2 * k + t], (px, py, c))
                cp.start()
                sends.append(cp)
        for k, (px, py) in enumerate(others):
            for t, (src, dst) in enumerate(pairs):
                _remote(src, dst.at[2 * px + py], send_sems.at[2 * k + t], recv_sems.at[2 * k + t],
                        (px, py, c)).wait_recv()
        for cp in sends:
            cp.wait_send()
        for cp in local:
            cp.wait()

    return pl.pallas_call(
        body, name="gather_weights",
        in_specs=[ANY, ANY], out_specs=[ANY, ANY],
        out_shape=[jax.ShapeDtypeStruct((N_CHIPS,) + wb.shape, wb.dtype),
                   jax.ShapeDtypeStruct((N_CHIPS,) + ws.shape, ws.dtype)],
        scratch_shapes=[pltpu.SemaphoreType.DMA((6,)), pltpu.SemaphoreType.DMA((6,)), pltpu.SemaphoreType.DMA((2,))],
    )(wb, ws)


def _scatter_grads(gpack):
    def body(g_ref, r_ref, send_sems, recv_sems):
        x, y, c, others = _other_chips()
        sends = []
        for k, (px, py) in enumerate(others):
            cp = _remote(g_ref.at[2 * px + py], r_ref.at[k], send_sems.at[k], recv_sems.at[k], (px, py, c))
            cp.start()
            sends.append(cp)
        for cp in sends:
            cp.wait_recv()
        for cp in sends:
            cp.wait_send()

    return pl.pallas_call(
        body, name="scatter_grads",
        in_specs=[ANY], out_specs=ANY,
        out_shape=jax.ShapeDtypeStruct((N_CHIPS - 1,) + gpack.shape[1:], gpack.dtype),
        scratch_shapes=[pltpu.SemaphoreType.DMA((3,)), pltpu.SemaphoreType.DMA((3,))],
    )(gpack)


class _Moves:
    def __init__(self, sends, recvs, local=None):
        self.sends, self.recvs, self.local = sends, recvs, local

    def start(self):
        if self.local is not None:
            self.local.start()
        for cp in self.sends:
            cp.start()

    def wait(self):
        for cp in self.recvs:
            cp.wait_recv()
        for cp in self.sends:
            cp.wait_send()
        if self.local is not None:
            self.local.wait()


def _gather_moves(wb_ref, gb_ref, send_sems, recv_sems, loc_sem):
    x, y, c, others = _other_chips()
    me = 2 * x + y
    sends = [_remote(wb_ref, gb_ref.at[me], send_sems.at[k], recv_sems.at[k], (px, py, c))
             for k, (px, py) in enumerate(others)]
    recvs = [_remote(wb_ref, gb_ref.at[2 * px + py], send_sems.at[k], recv_sems.at[k], (px, py, c))
             for k, (px, py) in enumerate(others)]
    return _Moves(sends, recvs, pltpu.make_async_copy(wb_ref, gb_ref.at[me], loc_sem))


def _scatter_moves(g_ref, r_ref, send_sems, recv_sems):
    x, y, c, others = _other_chips()
    sends = [_remote(g_ref.at[2 * px + py], r_ref.at[k], send_sems.at[k], recv_sems.at[k], (px, py, c))
             for k, (px, py) in enumerate(others)]
    return _Moves(sends, sends)


_MOVE_SEMS = [pltpu.SemaphoreType.DMA((N_CHIPS - 1,)), pltpu.SemaphoreType.DMA((N_CHIPS - 1,))]


def _exchange_partials(part_a, part_b, small):
    def body(pa_ref, pb_ref, s_ref, psa_ref, psb_ref, sa_ref, send_sems, recv_sems, loc_sem):
        x, y, c = lax.axis_index("x"), lax.axis_index("y"), lax.axis_index("c")
        me = 4 * x + 2 * y + c
        local = pltpu.make_async_copy(s_ref, sa_ref.at[me], loc_sem)
        local.start()
        sib = _remote(pa_ref, psa_ref, send_sems.at[0], recv_sems.at[0], (x, y, 1 - c))
        sib_b = _remote(pb_ref, psb_ref, send_sems.at[N_DEV], recv_sems.at[N_DEV], (x, y, 1 - c))
        sib.start()
        sib_b.start()
        sends = [sib, sib_b]
        for k in range(1, N_DEV):
            fx, fy, fc = (k >> 2) & 1, (k >> 1) & 1, k & 1
            px, py, pc = x ^ fx, y ^ fy, c ^ fc
            cp = _remote(s_ref, sa_ref.at[me], send_sems.at[k], recv_sems.at[k], (px, py, pc))
            cp.start()
            sends.append(cp)
        sib.wait_recv()
        sib_b.wait_recv()
        for k in range(1, N_DEV):
            fx, fy, fc = (k >> 2) & 1, (k >> 1) & 1, k & 1
            px, py, pc = x ^ fx, y ^ fy, c ^ fc
            _remote(s_ref, sa_ref.at[4 * px + 2 * py + pc], send_sems.at[k], recv_sems.at[k], (px, py, pc)).wait_recv()
        for cp in sends:
            cp.wait_send()
        local.wait()

    return pl.pallas_call(
        body, name="exchange_partials",
        in_specs=[ANY, ANY, ANY], out_specs=[ANY, ANY, ANY],
        out_shape=[jax.ShapeDtypeStruct(part_a.shape, part_a.dtype),
                   jax.ShapeDtypeStruct(part_b.shape, part_b.dtype),
                   jax.ShapeDtypeStruct((N_DEV,) + small.shape, small.dtype)],
        scratch_shapes=[pltpu.SemaphoreType.DMA((N_DEV + 1,)), pltpu.SemaphoreType.DMA((N_DEV + 1,)),
                        pltpu.SemaphoreType.DMA],
    )(part_a, part_b, small)


def _partial_sum(own, recv, name):
    R = own.shape[0]

    def body(o_ref, r_ref, p_ref):
        acc = o_ref[...]
        for k in range(N_CHIPS - 1):
            acc = acc + r_ref[k].astype(F32)
        p_ref[...] = acc

    return pl.pallas_call(
        body, name=name, grid=(R // ADAM_TILE,),
        in_specs=[pl.BlockSpec((ADAM_TILE, FLAT_W), lambda i: (i, 0)),
                  pl.BlockSpec((N_CHIPS - 1, ADAM_TILE, FLAT_W), lambda i: (0, i, 0))],
        out_specs=pl.BlockSpec((ADAM_TILE, FLAT_W), lambda i: (i, 0)),
        out_shape=jax.ShapeDtypeStruct(own.shape, F32),
        compiler_params=_cparams(("parallel",)),
    )(own, recv)


def _adamw_math(w, g, m, v):
    m = ADAM_B1 * m + (1.0 - ADAM_B1) * g
    v = ADAM_B2 * v + (1.0 - ADAM_B2) * jnp.square(g)
    m_hat = m / (1.0 - ADAM_B1 ** ADAM_STEP)
    v_hat = v / (1.0 - ADAM_B2 ** ADAM_STEP)
    delta = -ADAM_LR * (m_hat / (jnp.sqrt(v_hat) + ADAM_EPS) + ADAM_WD * w)
    return delta, m, v


def _adamw(w, m, v, parts, tile, name):
    n, R, _ = parts.shape

    def body(w_ref, m_ref, v_ref, p_ref, g_ref, d_ref, nm_ref, nv_ref):
        g = p_ref[0]
        for k in range(1, n):
            g = g + p_ref[k]
        delta, nm, nv = _adamw_math(w_ref[...], g, m_ref[...], v_ref[...])
        g_ref[...] = g
        d_ref[...] = delta
        nm_ref[...] = nm
        nv_ref[...] = nv

    blk = pl.BlockSpec((tile, FLAT_W), lambda i: (i, 0))
    return pl.pallas_call(
        body, name=name, grid=(R // tile,),
        in_specs=[blk, blk, blk, pl.BlockSpec((n, tile, FLAT_W), lambda i: (0, i, 0))],
        out_specs=[blk] * 4,
        out_shape=[jax.ShapeDtypeStruct((R, FLAT_W), F32)] * 4,
        compiler_params=_cparams(("parallel",)),
    )(w, m, v, parts)


def _add2(a, b, name):
    def body(a_ref, b_ref, o_ref):
        o_ref[...] = a_ref[...] + b_ref[...]

    blk = pl.BlockSpec((ADAM_TILE, FLAT_W), lambda i: (i, 0))
    return pl.pallas_call(
        body, name=name, grid=(a.shape[0] // ADAM_TILE,), in_specs=[blk, blk], out_specs=blk,
        out_shape=jax.ShapeDtypeStruct(a.shape, F32), compiler_params=_cparams(("parallel",)),
    )(a, b)


def _adamw_block(w, m, v, g, name):
    shape = w.shape
    R, C = math.prod(shape[:-1]), shape[-1]
    tile = _tile(R, ADAM_TILE, SUBLANE)

    def body(w_ref, m_ref, v_ref, g_ref, d_ref, nm_ref, nv_ref):
        delta, nm, nv = _adamw_math(w_ref[...], g_ref[...], m_ref[...], v_ref[...])
        d_ref[...] = delta
        nm_ref[...] = nm
        nv_ref[...] = nv

    blk = pl.BlockSpec((tile, C), lambda i: (i, 0))
    outs = pl.pallas_call(
        body, name=name, grid=(R // tile,), in_specs=[blk] * 4, out_specs=[blk] * 3,
        out_shape=[jax.ShapeDtypeStruct((R, C), F32)] * 3, compiler_params=_cparams(("parallel",)),
    )(*(a.reshape(R, C) for a in (w, m, v, g)))
    return tuple(a.reshape(shape) for a in outs)


_NAMES = ("norm_mix", "norm_ffn", "ssm_w_in", "ssm_conv_w", "ssm_conv_b", "ssm_dt_bias", "ssm_a_log", "ssm_d",
          "ssm_norm_w", "ssm_w_out", "sb_w_qkv", "sb_q_gain", "sb_k_gain", "sb_w_o", "ffn_w_in", "ffn_w_out")


def _step(x, loss_target, w, m, v):
    cx, cy, cc = lax.axis_index("x"), lax.axis_index("y"), lax.axis_index("c")
    chip = 2 * cx + cy

    conv8 = jnp.pad(w["ssm_conv_w"][0], ((0, SUBLANE - CONV_W), (0, 0)))
    gb, gs = _gather_weights(_pack_shard(w, BF16, SEG_FIRST), conv8)
    W = _prep_vectors(w)
    W["ssm_w_in_t"] = jnp.pad(_whole_of_shards(gb, SEG_FIRST)["ssm_w_in"][0], ((0, D_IN_PAD - D_IN_PROJ), (0, 0)))
    W["conv_w8"] = jnp.concatenate([gs[p] for p in range(N_CHIPS)], axis=1)

    sq, grad_x, grads, (slots_rest, recv_rest) = _local_step(
        x[0], loss_target[0], W, wb_rest=_pack_shard(w, BF16, SEG_REST))
    loss = lax.psum(0.5 * jnp.sum(sq) / D_MODEL, ("x", "y", "c"))

    slots_first = jnp.stack([_shard_of_whole(grads, s, SEG_FIRST) for s in range(N_CHIPS)])
    recv_first = _scatter_grads(slots_first.astype(BF16))
    own = lambda slots: lax.dynamic_index_in_dim(slots, chip, axis=0, keepdims=False)
    part_first = _partial_sum(own(slots_first), recv_first, "grad_partial_sum_first")
    part_rest = _partial_sum(own(slots_rest), recv_rest, "grad_partial_sum_rest")
    sib_first, sib_rest, small_all = _exchange_partials(part_first, part_rest, _pack_small(grads))
    g_blocks = {**_unpack_shard(_add2(part_first, sib_first, "grad_final_sum_first"), SEG_FIRST),
                **_unpack_shard(_add2(part_rest, sib_rest, "grad_final_sum_rest"), SEG_REST)}
    g_s, d_s, m_s, v_s = _adamw(_pack_small(w), _pack_small(m), _pack_small(v), small_all, SUBLANE, "adamw_replicated")

    g, d, nm, nv = (_unpack_small(a) for a in (g_s, d_s, m_s, v_s))
    for n, _, _ in SHARDED:
        g[n] = g_blocks[n]
        d[n], nm[n], nv[n] = _adamw_block(w[n], m[n], v[n], g[n], "adamw_" + n)
    return loss, grad_x[None], [[t[n] for n in _NAMES] for t in (g, d, nm, nv)]


def kernel(x, norm_mix, norm_ffn, ssm_w_in, ssm_conv_w, ssm_conv_b, ssm_dt_bias, ssm_a_log, ssm_d, ssm_norm_w, ssm_w_out, sb_w_qkv, sb_q_gain, sb_k_gain, sb_w_o, ffn_w_in, ffn_w_out, loss_target, m_norm_mix, m_norm_ffn, m_ssm_w_in, m_ssm_conv_w, m_ssm_conv_b, m_ssm_dt_bias, m_ssm_a_log, m_ssm_d, m_ssm_norm_w, m_ssm_w_out, m_sb_w_qkv, m_sb_q_gain, m_sb_k_gain, m_sb_w_o, m_ffn_w_in, m_ffn_w_out, v_norm_mix, v_norm_ffn, v_ssm_w_in, v_ssm_conv_w, v_ssm_conv_b, v_ssm_dt_bias, v_ssm_a_log, v_ssm_d, v_ssm_norm_w, v_ssm_w_out, v_sb_w_qkv, v_sb_q_gain, v_sb_k_gain, v_sb_w_o, v_ffn_w_in, v_ffn_w_out):
    w = dict(zip(_NAMES, (norm_mix, norm_ffn, ssm_w_in, ssm_conv_w, ssm_conv_b, ssm_dt_bias, ssm_a_log, ssm_d,
                          ssm_norm_w, ssm_w_out, sb_w_qkv, sb_q_gain, sb_k_gain, sb_w_o, ffn_w_in, ffn_w_out)))
    m = dict(zip(_NAMES, (m_norm_mix, m_norm_ffn, m_ssm_w_in, m_ssm_conv_w, m_ssm_conv_b, m_ssm_dt_bias, m_ssm_a_log,
                          m_ssm_d, m_ssm_norm_w, m_ssm_w_out, m_sb_w_qkv, m_sb_q_gain, m_sb_k_gain, m_sb_w_o,
                          m_ffn_w_in, m_ffn_w_out)))
    v = dict(zip(_NAMES, (v_norm_mix, v_norm_ffn, v_ssm_w_in, v_ssm_conv_w, v_ssm_conv_b, v_ssm_dt_bias, v_ssm_a_log,
                          v_ssm_d, v_ssm_norm_w, v_ssm_w_out, v_sb_w_qkv, v_sb_q_gain, v_sb_k_gain, v_sb_w_o,
                          v_ffn_w_in, v_ffn_w_out)))
    loss, grad_x, (g, d, nm, nv) = _step(x, loss_target, w, m, v)
    return (loss, grad_x, *g, *d, *nm, *nv)
```

```python
import functools
import math

import jax
import jax.numpy as jnp
from jax import lax
from jax.experimental import pallas as pl
from jax.experimental.pallas import tpu as pltpu

F32 = jnp.float32
BF16 = jnp.bfloat16

VMEM_LIMIT_BYTES = 48 * 1024 * 1024
LANE = 128
SUBLANE = 8

D_MODEL = 1024
CHUNK = 64
D_INNER = 2048
SSM_HEADS = 32
SSM_GROUPS = 8
GROUP_W = D_INNER // SSM_GROUPS
HEAD_P = 64
D_STATE = 128
CONV_W = 4
CONV_DIM = D_INNER + 2 * SSM_GROUPS * D_STATE
D_IN_PROJ = D_INNER + CONV_DIM + SSM_HEADS
DT_PAD = LANE
D_IN_PAD = D_INNER + CONV_DIM + DT_PAD
SB_HEADS = 16
SB_D = 64
D_FF = 2816
NORM_EPS = 1e-6
GATED_NORM_EPS = 1e-5

ADAM_LR = 0.001
ADAM_B1 = 0.9
ADAM_B2 = 0.999
ADAM_EPS = 1e-08
ADAM_WD = 0.01
ADAM_STEP = 10

MESH_T = pl.DeviceIdType.MESH


def _cparams(sem=None):
    return pltpu.CompilerParams(dimension_semantics=sem, vmem_limit_bytes=VMEM_LIMIT_BYTES)


def _tile(n, target, align):
    best = None
    for t in range(align, min(n, target) + 1, align):
        if n % t == 0:
            best = t
    return best or n


def _mm(a, b, mode, out_dtype, name):
    halves = a.ndim == 3
    a2 = (a.shape[1], 2 * a.shape[2]) if halves else a.shape
    if mode == "nn":
        (M, K), N = a2, b.shape[1]
    elif mode == "nt":
        (M, K), N = a2, b.shape[0]
    else:
        (K, M), N = a2, b.shape[1]
    tm = _tile(M // 2 if halves and mode == "tn" else M, 1408, LANE)
    tn = _tile(N, 1536, LANE)
    tk = _tile(K // 2 if halves and mode == "nn" else K, 1536, LANE)
    nk = K // tk
    if mode == "nn":
        a_spec = pl.BlockSpec((tm, tk), lambda i, j, k: (i, k))
        if halves:
            nkh = nk // 2
            a_spec = pl.BlockSpec((None, tm, tk), lambda i, j, k: (k // nkh, i, k % nkh))
        b_spec = pl.BlockSpec((tk, tn), lambda i, j, k: (k, j))
        dims = (((1,), (0,)), ((), ()))
    elif mode == "nt":
        a_spec = pl.BlockSpec((tm, tk), lambda i, j, k: (i, k))
        b_spec = pl.BlockSpec((tn, tk), lambda i, j, k: (j, k))
        dims = (((1,), (1,)), ((), ()))
    else:
        a_spec = pl.BlockSpec((tk, tm), lambda i, j, k: (k, i))
        if halves:
            nmh = M // tm // 2
            a_spec = pl.BlockSpec((None, tk, tm), lambda i, j, k: (i // nmh, k, i % nmh))
        b_spec = pl.BlockSpec((tk, tn), lambda i, j, k: (k, j))
        dims = (((0,), (0,)), ((), ()))

    def body(a_ref, b_ref, o_ref, acc_ref):
        k = pl.program_id(2)

        @pl.when(k == 0)
        def _():
            acc_ref[...] = jnp.zeros_like(acc_ref)

        acc_ref[...] += lax.dot_general(a_ref[...], b_ref[...], dims, preferred_element_type=F32)

        @pl.when(k == nk - 1)
        def _():
            o_ref[...] = acc_ref[...].astype(o_ref.dtype)

    return pl.pallas_call(
        body,
        name=name,
        grid=(M // tm, N // tn, nk),
        in_specs=[a_spec, b_spec],
        out_specs=pl.BlockSpec((tm, tn), lambda i, j, k: (i, j)),
        out_shape=jax.ShapeDtypeStruct((M, N), out_dtype),
        scratch_shapes=[pltpu.VMEM((tm, tn), F32)],
        compiler_params=_cparams(("parallel", "parallel", "arbitrary")),
    )(a, b)


def _rows(fn, name, tile, row_ins, const_ins, row_outs, acc_outs=(), ncol=1):
    L = row_ins[0][0].shape[0]
    tile = min(tile, L)
    nrow = L // tile
    n_ri, n_ci, n_ro, n_ao = len(row_ins), len(const_ins), len(row_outs), len(acc_outs)

    def body(*refs):
        i = pl.program_id(1)
        j = pl.program_id(0)
        ins = [r[...] for r in refs[: n_ri + n_ci]]
        outs = fn(*ins, j=j)
        o_refs = refs[n_ri + n_ci:]
        for r, v in zip(o_refs[:n_ro], outs[:n_ro]):
            r[...] = v.astype(r.dtype)

        @pl.when(i == 0)
        def _():
            for r in o_refs[n_ro:]:
                r[...] = jnp.zeros_like(r)

        for r, v in zip(o_refs[n_ro:], outs[n_ro:]):
            r[...] += v

    def rspec(bc, cf):
        return pl.BlockSpec((tile, bc), lambda j, i: (i, cf(j)))

    def cspec(r, bc, cf):
        return pl.BlockSpec((r, bc), lambda j, i: (0, cf(j)))

    in_specs = [rspec(bc, cf) for (_, bc, cf) in row_ins]
    in_specs += [cspec(a.shape[0], bc, cf) for (a, bc, cf) in const_ins]
    out_specs = [rspec(bc, cf) for (_, _, bc, cf) in row_outs]
    out_specs += [cspec(1, bc, cf) for (_, bc, cf) in acc_outs]
    out_shape = [jax.ShapeDtypeStruct((L, c), dt) for (c, dt, _, _) in row_outs]
    out_shape += [jax.ShapeDtypeStruct((1, c), F32) for (c, _, _) in acc_outs]
    res = pl.pallas_call(
        body,
        name=name,
        grid=(ncol, nrow),
        in_specs=in_specs,
        out_specs=out_specs,
        out_shape=out_shape,
        compiler_params=_cparams(("arbitrary", "arbitrary")),
    )(*[a for (a, _, _) in row_ins], *[a for (a, _, _) in const_ins])
    return res


def _zero(j):
    return 0


def _whole(a):
    return (a, a.shape[1], _zero)


def _rms(x, w, eps):
    return x * lax.rsqrt(jnp.mean(x * x, axis=-1, keepdims=True) + eps) * w


ROW_TILE = 1024
COL_ROW_TILE = 1024
QK_ROW_TILE = 8192


def _norm_fwd(x, r, w, name):
    C = x.shape[1]
    if r is None:
        def fn(x_, w_, j):
            return (_rms(x_, w_, NORM_EPS),)
        (h,) = _rows(fn, name, ROW_TILE, [_whole(x)], [_whole(w)], [(C, BF16, C, _zero)])
        return x, h

    def fn(x_, r_, w_, j):
        x1 = x_ + r_
        return x1, _rms(x1, w_, NORM_EPS)

    x1, h = _rows(fn, name, ROW_TILE, [_whole(x), _whole(r)], [_whole(w)],
                  [(C, F32, C, _zero), (C, BF16, C, _zero)])
    return x1, h


def _norm_bwd(x, w, dh, dres, name):
    C = x.shape[1]

    def fn(x_, dh_, dres_, w_, j):
        _, vjp = jax.vjp(lambda a, b: _rms(a, b, NORM_EPS), x_, w_)
        dx, dw = vjp(dh_.astype(F32))
        return dres_ + dx, dw

    return _rows(fn, name, ROW_TILE, [_whole(x), _whole(dh), _whole(dres)], [_whole(w)],
                 [(C, F32, C, _zero)], [(C, C, _zero)])


def _silu(x):
    return x * jax.nn.sigmoid(x)


FF_TM = 512
FF_TN = D_FF // 2
_NT = (((1,), (1,)), ((), ()))


def _ffn_in_fused(h, w_in_t, name):
    L, K = h.shape
    tm = min(FF_TM, L)
    nb = D_FF // FF_TN

    def body(a_ref, bg_ref, bu_ref, gu_ref, act_ref):
        g = lax.dot_general(a_ref[...], bg_ref[...], _NT, preferred_element_type=F32)
        u = lax.dot_general(a_ref[...], bu_ref[...], _NT, preferred_element_type=F32)
        gu_ref[0] = g
        gu_ref[1] = u
        act_ref[...] = (_silu(g) * u).astype(act_ref.dtype)

    return pl.pallas_call(
        body,
        name=name,
        grid=(L // tm, nb),
        in_specs=[pl.BlockSpec((tm, K), lambda i, j: (i, 0)),
                  pl.BlockSpec((FF_TN, K), lambda i, j: (j, 0)),
                  pl.BlockSpec((FF_TN, K), lambda i, j: (j + nb, 0))],
        out_specs=[pl.BlockSpec((2, tm, FF_TN), lambda i, j: (0, i, j)),
                   pl.BlockSpec((tm, FF_TN), lambda i, j: (i, j))],
        out_shape=[jax.ShapeDtypeStruct((2, L, D_FF), F32), jax.ShapeDtypeStruct((L, D_FF), BF16)],
        compiler_params=_cparams(("parallel", "parallel")),
    )(h, w_in_t, w_in_t)


def _ffn_out_dx_fused(d16, w_out, gu, name):
    L, K = d16.shape
    tm = min(FF_TM, L)

    def body(a_ref, b_ref, gu_ref, dgu_ref):
        dact = lax.dot_general(a_ref[...], b_ref[...], _NT, preferred_element_type=F32)
        _, vjp = jax.vjp(lambda g, u: _silu(g) * u, gu_ref[0], gu_ref[1])
        dg, du = vjp(dact.astype(BF16).astype(F32))
        dgu_ref[0] = dg.astype(dgu_ref.dtype)
        dgu_ref[1] = du.astype(dgu_ref.dtype)

    return pl.pallas_call(
        body,
        name=name,
        grid=(L // tm, D_FF // FF_TN),
        in_specs=[pl.BlockSpec((tm, K), lambda i, j: (i, 0)),
                  pl.BlockSpec((FF_TN, K), lambda i, j: (j, 0)),
                  pl.BlockSpec((2, tm, FF_TN), lambda i, j: (0, i, j))],
        out_specs=pl.BlockSpec((2, tm, FF_TN), lambda i, j: (0, i, j)),
        out_shape=jax.ShapeDtypeStruct((2, L, D_FF), BF16),
        compiler_params=_cparams(("parallel", "parallel")),
    )(d16, w_out, gu)


def _gated_norm_fwd(y, zx, w, name):
    def fn(y_, z_, w_, j):
        return (_rms(y_ * _silu(z_), w_, GATED_NORM_EPS),)
    (yn,) = _rows(fn, name, COL_ROW_TILE,
                  [(y, GROUP_W, lambda j: j), (zx, GROUP_W, lambda j: j)], [(w, GROUP_W, lambda j: j)],
                  [(D_INNER, BF16, GROUP_W, lambda j: j)], ncol=SSM_GROUPS)
    return yn


def _gated_norm_bwd(y, zx, w, dyn, name):
    def fn(y_, z_, dyn_, w_, j):
        _, vjp = jax.vjp(lambda a, b, c: _rms(a * _silu(b), c, GATED_NORM_EPS), y_, z_, w_)
        return vjp(dyn_.astype(F32))
    cj = lambda j: j
    return _rows(fn, name, COL_ROW_TILE,
                 [(y, GROUP_W, cj), (zx, GROUP_W, cj), (dyn, GROUP_W, cj)], [(w, GROUP_W, cj)],
                 [(D_INNER, F32, GROUP_W, cj), (D_INNER, BF16, GROUP_W, cj)], [(D_INNER, GROUP_W, cj)],
                 ncol=SSM_GROUPS)


def _loss_bwd(x3, f, target, name):
    C = x3.shape[1]

    def fn(x_, f_, t_, j):
        err = (x_ + f_) - t_
        return err * (1.0 / C), jnp.sum(err * err, axis=0, keepdims=True)

    return _rows(fn, name, ROW_TILE, [_whole(x3), _whole(f), _whole(target)], [],
                 [(C, F32, C, _zero)], [(C, C, _zero)])


def _qk_norm_fwd(q, gain, scale, name):
    def fn(q_, g_, j):
        return (_rms(q_, g_, NORM_EPS) * scale,)
    (qn,) = _rows(fn, name, QK_ROW_TILE, [_whole(q)], [_whole(gain)], [(SB_D, BF16, SB_D, _zero)])
    return qn


def _qk_norm_bwd(q, gain, dqn, scale, cot_scale, name):
    def fn(q_, dqn_, g_, j):
        _, vjp = jax.vjp(lambda a, b: _rms(a, b, NORM_EPS) * scale, q_, g_)
        return vjp(dqn_ * cot_scale)
    return _rows(fn, name, QK_ROW_TILE, [_whole(q), _whole(dqn)], [_whole(gain)],
                 [(SB_D, F32, SB_D, _zero)], [(SB_D, SB_D, _zero)])


CONV_TILE = 1024
CONV_BLK = 512
XBC_COL0 = D_INNER // CONV_BLK


def _shift_down(cur, prev8, k):
    if k == 0:
        return cur
    rolled = pltpu.roll(cur, k, 0)
    head_prev = pltpu.roll(prev8, k, 0)
    rid = lax.broadcasted_iota(jnp.int32, (SUBLANE, cur.shape[1]), 0)
    head = jnp.where(rid < k, head_prev, rolled[:SUBLANE])
    if cur.shape[0] == SUBLANE:
        return head
    return jnp.concatenate([head, rolled[SUBLANE:]], axis=0)


def _shift_up(cur, next8, k):
    if k == 0:
        return cur
    T = cur.shape[0]
    rolled = pltpu.roll(cur, T - k, 0)
    tail_next = pltpu.roll(next8, SUBLANE - k, 0)
    rid = lax.broadcasted_iota(jnp.int32, (SUBLANE, cur.shape[1]), 0)
    tail = jnp.where(rid >= SUBLANE - k, tail_next, rolled[T - SUBLANE:])
    return jnp.concatenate([rolled[: T - SUBLANE], tail], axis=0)


def _conv_pre(cur, prev8, w, b):
    pre = b
    for i in range(CONV_W):
        pre = pre + w[i:i + 1, :] * _shift_down(cur, prev8, CONV_W - 1 - i)
    return pre


def _conv_fwd(zx, w8, b, name):
    L = zx.shape[0]
    nrow = L // CONV_TILE
    r8 = CONV_TILE // SUBLANE

    def body(cur_ref, prev_ref, w_ref, b_ref, o_ref):
        i = pl.program_id(1)
        prev8 = jnp.where(i > 0, prev_ref[...], 0.0)
        pre = _conv_pre(cur_ref[...], prev8, w_ref[...], b_ref[...])
        o_ref[...] = _silu(pre)

    return pl.pallas_call(
        body,
        name=name,
        grid=(CONV_DIM // CONV_BLK, nrow),
        in_specs=[
            pl.BlockSpec((CONV_TILE, CONV_BLK), lambda j, i: (i, j + XBC_COL0)),
            pl.BlockSpec((SUBLANE, CONV_BLK), lambda j, i: (jnp.maximum(i * r8 - 1, 0), j + XBC_COL0)),
            pl.BlockSpec((SUBLANE, CONV_BLK), lambda j, i: (0, j)),
            pl.BlockSpec((1, CONV_BLK), lambda j, i: (0, j)),
        ],
        out_specs=pl.BlockSpec((CONV_TILE, CONV_BLK), lambda j, i: (i, j)),
        out_shape=jax.ShapeDtypeStruct((L, CONV_DIM), F32),
        compiler_params=_cparams(("arbitrary", "arbitrary")),
    )(zx, zx, w8, b)


def _conv_bwd(zx, w8, b, dact, name):
    L = zx.shape[0]
    nrow = L // CONV_TILE
    r8 = CONV_TILE // SUBLANE
    last8 = L // SUBLANE - 1

    def body(cur_ref, prev_ref, next_ref, da_ref, dan_ref, w_ref, b_ref, du_ref, dw_ref, db_ref):
        i = pl.program_id(1)
        w = w_ref[...]
        b_ = b_ref[...]
        cur = cur_ref[...]
        prev8 = jnp.where(i > 0, prev_ref[...], 0.0)
        pre = _conv_pre(cur, prev8, w, b_)
        _, vjp = jax.vjp(_silu, pre)
        (dpre,) = vjp(da_ref[...])
        nxt = next_ref[...]
        pre_n = _conv_pre(nxt, cur[CONV_TILE - SUBLANE:], w, b_)
        _, vjp_n = jax.vjp(_silu, pre_n)
        (dpre_n,) = vjp_n(dan_ref[...])
        dpre_n = jnp.where(i < nrow - 1, dpre_n, 0.0)
        du = jnp.zeros_like(cur)
        dws = []
        for k in range(CONV_W):
            wk = w[CONV_W - 1 - k:CONV_W - k, :]
            du = du + wk * _shift_up(dpre, dpre_n, k)
            dws.append(jnp.sum(dpre * _shift_down(cur, prev8, k), axis=0, keepdims=True))
        du_ref[...] = du.astype(du_ref.dtype)
        dw_tile = jnp.concatenate([dws[3], dws[2], dws[1], dws[0]] + [jnp.zeros_like(dws[0])] * 4, axis=0)

        @pl.when(i == 0)
        def _():
            dw_ref[...] = jnp.zeros_like(dw_ref)
            db_ref[...] = jnp.zeros_like(db_ref)

        dw_ref[...] += dw_tile
        db_ref[...] += jnp.sum(dpre, axis=0, keepdims=True)

    return pl.pallas_call(
        body,
        name=name,
        grid=(CONV_DIM // CONV_BLK, nrow),
        in_specs=[
            pl.BlockSpec((CONV_TILE, CONV_BLK), lambda j, i: (i, j + XBC_COL0)),
            pl.BlockSpec((SUBLANE, CONV_BLK), lambda j, i: (jnp.maximum(i * r8 - 1, 0), j + XBC_COL0)),
            pl.BlockSpec((SUBLANE, CONV_BLK), lambda j, i: (jnp.minimum((i + 1) * r8, last8), j + XBC_COL0)),
            pl.BlockSpec((CONV_TILE, CONV_BLK), lambda j, i: (i, j)),
            pl.BlockSpec((SUBLANE, CONV_BLK), lambda j, i: (jnp.minimum((i + 1) * r8, last8), j)),
            pl.BlockSpec((SUBLANE, CONV_BLK), lambda j, i: (0, j)),
            pl.BlockSpec((1, CONV_BLK), lambda j, i: (0, j)),
        ],
        out_specs=[
            pl.BlockSpec((CONV_TILE, CONV_BLK), lambda j, i: (i, j)),
            pl.BlockSpec((SUBLANE, CONV_BLK), lambda j, i: (0, j)),
            pl.BlockSpec((1, CONV_BLK), lambda j, i: (0, j)),
        ],
        out_shape=[
            jax.ShapeDtypeStruct((L, CONV_DIM), BF16),
            jax.ShapeDtypeStruct((SUBLANE, CONV_DIM), F32),
            jax.ShapeDtypeStruct((1, CONV_DIM), F32),
        ],
        compiler_params=_cparams(("arbitrary", "arbitrary")),
    )(zx, zx, zx, dact, dact, w8, b)


HI = lax.Precision.HIGHEST
XS_COL0 = 0
B_COL0 = D_INNER // D_STATE
C_COL0 = B_COL0 + SSM_GROUPS
DT_COL = (D_INNER + CONV_DIM) // DT_PAD


def _dot(a, b, dims=(((1,), (0,)), ((), ())), precision=None):
    return lax.dot_general(a, b, dims, precision=precision, preferred_element_type=F32)


_DOT_DIMS = {
    "nn": (((1,), (0,)), ((), ())),
    "nt": (((1,), (1,)), ((), ())),
    "tn": (((0,), (0,)), ((), ())),
}


@functools.partial(jax.custom_vjp, nondiff_argnums=(2,))
def _bdot(a, b, mode):
    return _dot(a.astype(BF16), b.astype(BF16), _DOT_DIMS[mode])


def _bdot_fwd(a, b, mode):
    return _bdot(a, b, mode), (a, b)


def _bdot_bwd(mode, res, g):
    a, b = res
    if mode == "nn":
        return _bdot(g, b, "nt"), _bdot(a, g, "tn")
    if mode == "nt":
        return _bdot(g, b, "nn"), _bdot(g, a, "tn")
    return _bdot(b, g, "nt"), _bdot(a, g, "nn")


_bdot.defvjp(_bdot_fwd, _bdot_bwd)


def _softplus(x):
    return jnp.maximum(x, 0.0) + jnp.log(1.0 + jnp.exp(-jnp.abs(x)))


def _split2(x):
    hi = x.astype(BF16)
    return hi, (x - hi.astype(F32)).astype(BF16)


@jax.custom_vjp
def _sel_left(m, mT, x):
    hi, lo = _split2(x)
    out = _dot(m, jnp.concatenate([hi, lo], axis=1))
    n = x.shape[1]
    return out[:, :n] + out[:, n:]


_sel_left.defvjp(lambda m, mT, x: (_sel_left(m, mT, x), (m, mT)),
                 lambda res, g: (jnp.zeros_like(res[0]), jnp.zeros_like(res[1]), _sel_left(res[1], res[0], g)))


@jax.custom_vjp
def _sel_right(x, m, mT):
    hi, lo = _split2(x)
    out = _dot(jnp.concatenate([hi, lo], axis=0), m)
    r = x.shape[0]
    return out[:r] + out[r:]


_sel_right.defvjp(lambda x, m, mT: (_sel_right(x, m, mT), (m, mT)),
                  lambda res, g: (_sel_right(g, res[1], res[0]), jnp.zeros_like(res[0]), jnp.zeros_like(res[1])))


def _ssd_consts():
    l_ = lax.broadcasted_iota(jnp.int32, (CHUNK, GROUP_W), 0)
    c_ = lax.broadcasted_iota(jnp.int32, (CHUNK, GROUP_W), 1)
    s_ = c_ % CHUNK
    causal = s_ <= l_
    eye_t = (s_ == l_).astype(F32)
    r0 = lax.broadcasted_iota(jnp.int32, (CHUNK, CHUNK), 0)
    c0 = lax.broadcasted_iota(jnp.int32, (CHUNK, CHUNK), 1)
    tril = (c0 <= r0).astype(BF16)
    triu = (r0 <= c0).astype(BF16)
    rb = lax.broadcasted_iota(jnp.int32, (GROUP_W, GROUP_W), 0) // HEAD_P
    cb = lax.broadcasted_iota(jnp.int32, (GROUP_W, GROUP_W), 1) // HEAD_P
    blockdiag = rb == cb
    return causal, eye_t, tril, triu, blockdiag


def _ssd_expand(g):
    h = lax.broadcasted_iota(jnp.int32, (DT_PAD, GROUP_W), 0)
    c = lax.broadcasted_iota(jnp.int32, (DT_PAD, GROUP_W), 1)
    cT = lax.broadcasted_iota(jnp.int32, (GROUP_W, DT_PAD), 0)
    hT = lax.broadcasted_iota(jnp.int32, (GROUP_W, DT_PAD), 1)
    hpg = GROUP_W // HEAD_P
    return (h == g * hpg + c // HEAD_P).astype(BF16), (hT == g * hpg + cT // HEAD_P).astype(BF16)


def _ssd_chunk(S, xs, dt_raw, dt_bias, alog_e, d_e, Bm, Cm, E):
    causal, eye_t, tril, triu, blockdiag = _ssd_consts()
    ones = jnp.ones((CHUNK, CHUNK), BF16)
    dt = _softplus(dt_raw + dt_bias)
    dtx = _sel_right(dt, E[0], E[1])
    a = dtx * (-jnp.exp(alog_e))
    acs = _sel_left(tril, triu, a)
    rowv = _sel_left(ones, ones, acs * eye_t)
    seg = acs - rowv
    Lc = jnp.where(causal, jnp.exp(jnp.where(causal, seg, 0.0)), 0.0)
    xdt = xs * dtx
    Bt = jnp.concatenate([Bm] * 4, axis=0)
    CBc = _bdot(Cm, Bt, "nt")
    Xbd = jnp.where(blockdiag, jnp.concatenate([xdt] * 4, axis=0), 0.0)
    y_intra = _bdot(CBc * Lc, Xbd, "nn")
    y_inter = _bdot(Cm, S, "nn") * jnp.exp(acs)
    y = y_intra + y_inter + d_e * xs
    last = jnp.sum(a, axis=0, keepdims=True)
    dec_end = jnp.exp(last - acs)
    S_new = S * jnp.exp(last) + _bdot(Bm, xdt * dec_end, "tn")
    return S_new, y


SSD_GPS = 8
SSD_W = SSD_GPS * GROUP_W
SSD_N = SSD_GPS * D_STATE
SSD_STEPS = SSM_GROUPS // SSD_GPS


def _ssd_in_specs(cmap):
    return [
        pl.BlockSpec((CHUNK, SSD_W), lambda c, g: (cmap(c), g)),
        pl.BlockSpec((CHUNK, DT_PAD), lambda c, g: (cmap(c), DT_COL)),
        pl.BlockSpec((1, DT_PAD), lambda c, g: (0, 0)),
        pl.BlockSpec((1, SSD_W), lambda c, g: (0, g)),
        pl.BlockSpec((1, SSD_W), lambda c, g: (0, g)),
        pl.BlockSpec((CHUNK, SSD_N), lambda c, g: (cmap(c), B_COL0 // SSD_GPS + g)),
        pl.BlockSpec((CHUNK, SSD_N), lambda c, g: (cmap(c), C_COL0 // SSD_GPS + g)),
    ]


def _gw(u):
    return slice(u * GROUP_W, (u + 1) * GROUP_W)


def _gn(u):
    return slice(u * D_STATE, (u + 1) * D_STATE)


def _ssd_fwd(xbc, zx, dt_bias, alog_e, d_e, name, travel=None):
    L = xbc.shape[0]
    nc = L // CHUNK
    n_in = 7 + (travel is not None)

    def body(*refs):
        xs_ref, dtr_ref, bias_ref, alog_ref, d_ref, b_ref, c_ref = refs[:7]
        y_ref, st_ref = refs[n_in:n_in + 2]
        S_ref = refs[n_in + 2 + (travel is not None)]
        c = pl.program_id(0)
        step = pl.program_id(1)
        g0 = step * SSD_GPS if SSD_STEPS > 1 else 0
        if travel is not None:
            moves = lambda: _gather_moves(refs[7], refs[n_in + 2], *refs[n_in + 4:])
            pl.when(jnp.logical_and(c == 0, step == 0))(lambda: moves().start())

        @pl.when(c == 0)
        def _():
            for u in range(SSD_GPS):
                S_ref[g0 + u] = jnp.zeros((D_STATE, GROUP_W), F32)

        dtr, bias = dtr_ref[...], bias_ref[...]
        for u in range(SSD_GPS):
            S = S_ref[g0 + u]
            st_ref[u] = S
            S_new, y = _ssd_chunk(S, xs_ref[:, _gw(u)], dtr, bias, alog_ref[:, _gw(u)], d_ref[:, _gw(u)],
                                  b_ref[:, _gn(u)], c_ref[:, _gn(u)], _ssd_expand(g0 + u))
            y_ref[:, _gw(u)] = y
            S_ref[g0 + u] = S_new

        if travel is not None:
            pl.when(jnp.logical_and(c == nc - 1, step == SSD_STEPS - 1))(lambda: moves().wait())

    extra = travel is not None
    return pl.pallas_call(
        body,
        name=name,
        grid=(nc, SSD_STEPS),
        in_specs=_ssd_in_specs(lambda c: c) + [ANY] * extra,
        out_specs=[
            pl.BlockSpec((CHUNK, SSD_W), lambda c, g: (c, g)),
            pl.BlockSpec((None, SSD_GPS, D_STATE, GROUP_W), lambda c, g: (c, g, 0, 0)),
        ] + [ANY] * extra,
        out_shape=[
            jax.ShapeDtypeStruct((L, D_INNER), F32),
            jax.ShapeDtypeStruct((nc, SSM_GROUPS, D_STATE, GROUP_W), F32),
        ] + ([jax.ShapeDtypeStruct((N_CHIPS,) + travel.shape, travel.dtype)] if extra else []),
        scratch_shapes=[pltpu.VMEM((SSM_GROUPS, D_STATE, GROUP_W), F32)]
        + ((_MOVE_SEMS + [pltpu.SemaphoreType.DMA]) if extra else []),
        compiler_params=_cparams(("arbitrary", "arbitrary")),
    )(xbc, zx, dt_bias, alog_e, d_e, xbc, xbc, *([travel] if extra else []))


def _ssd_bwd(xbc, zx, dt_bias, alog_e, d_e, states, dy, name, travel=None):
    assert SSD_STEPS == 1, "one grid step writes the whole d (xs | B | C) row block"
    L = xbc.shape[0]
    nc = L // CHUNK
    rev = lambda c: nc - 1 - c
    n_in = 9 + (travel is not None)

    def body(*refs):
        xs_ref, dtr_ref, bias_ref, alog_ref, d_ref, b_ref, c_ref, st_ref, dy_ref = refs[:9]
        dact_ref, ddt_ref, dbias_ref, dalog_ref, dd_ref = refs[n_in:n_in + 5]
        dS_ref = refs[n_in + 5 + (travel is not None)]
        c = pl.program_id(0)
        step = pl.program_id(1)
        g0 = 0
        first = jnp.logical_and(c == 0, step == 0)
        if travel is not None:
            moves = lambda: _scatter_moves(refs[9], refs[n_in + 5], *refs[n_in + 7:])
            pl.when(first)(lambda: moves().start())

        @pl.when(c == 0)
        def _():
            for u in range(SSD_GPS):
                dS_ref[g0 + u] = jnp.zeros((D_STATE, GROUP_W), F32)
                dalog_ref[g0 + u] = jnp.zeros((1, GROUP_W), F32)
                dd_ref[g0 + u] = jnp.zeros((1, GROUP_W), F32)

        @pl.when(first)
        def _():
            dbias_ref[...] = jnp.zeros_like(dbias_ref)

        dtr, bias = dtr_ref[...], bias_ref[...]
        ddt_sum = jnp.zeros((CHUNK, DT_PAD), F32)
        dbias_sum = jnp.zeros((1, DT_PAD), F32)
        for u in range(SSD_GPS):
            E = _ssd_expand(g0 + u)
            _, vjp = jax.vjp(
                lambda S, xs, dtr_, bias_, alog, dsk, Bm, Cm: _ssd_chunk(S, xs, dtr_, bias_, alog, dsk, Bm, Cm, E),
                st_ref[u], xs_ref[:, _gw(u)], dtr, bias, alog_ref[:, _gw(u)], d_ref[:, _gw(u)],
                b_ref[:, _gn(u)], c_ref[:, _gn(u)])
            dS, dxs, ddtr, dbias, dalog, dd, dB, dC = vjp((dS_ref[g0 + u], dy_ref[:, _gw(u)]))
            dS_ref[g0 + u] = dS
            dact_ref[:, _gw(u)] = dxs
            dact_ref[:, slice(D_INNER + u * D_STATE, D_INNER + (u + 1) * D_STATE)] = dB
            dact_ref[:, slice(D_INNER + SSD_N + u * D_STATE, D_INNER + SSD_N + (u + 1) * D_STATE)] = dC
            dalog_ref[g0 + u] += dalog
            dd_ref[g0 + u] += dd
            ddt_sum = ddt_sum + ddtr
            dbias_sum = dbias_sum + dbias

        ddt_ref[...] = ddt_sum
        dbias_ref[...] += dbias_sum
        if travel is not None:
            pl.when(jnp.logical_and(c == nc - 1, step == SSD_STEPS - 1))(lambda: moves().wait())

    extra = travel is not None
    whole3 = pl.BlockSpec((SSM_GROUPS, 1, GROUP_W), lambda c, g: (0, 0, 0))
    return pl.pallas_call(
        body,
        name=name,
        grid=(nc, SSD_STEPS),
        in_specs=_ssd_in_specs(rev) + [
            pl.BlockSpec((None, SSD_GPS, D_STATE, GROUP_W), lambda c, g: (rev(c), g, 0, 0)),
            pl.BlockSpec((CHUNK, SSD_W), lambda c, g: (rev(c), g)),
        ] + [ANY] * extra,
        out_specs=[
            pl.BlockSpec((CHUNK, CONV_DIM), lambda c, g: (rev(c), 0)),
            pl.BlockSpec((CHUNK, DT_PAD), lambda c, g: (rev(c), 0)),
            pl.BlockSpec((1, DT_PAD), lambda c, g: (0, 0)),
            whole3,
            whole3,
        ] + [ANY] * extra,
        out_shape=[
            jax.ShapeDtypeStruct((L, CONV_DIM), F32),
            jax.ShapeDtypeStruct((L, DT_PAD), F32),
            jax.ShapeDtypeStruct((1, DT_PAD), F32),
            jax.ShapeDtypeStruct((SSM_GROUPS, 1, GROUP_W), F32),
            jax.ShapeDtypeStruct((SSM_GROUPS, 1, GROUP_W), F32),
        ] + ([jax.ShapeDtypeStruct((N_CHIPS - 1,) + travel.shape[1:], travel.dtype)] if extra else []),
        scratch_shapes=[pltpu.VMEM((SSM_GROUPS, D_STATE, GROUP_W), F32)] + (_MOVE_SEMS if extra else []),
        compiler_params=_cparams(("arbitrary", "arbitrary")),
    )(xbc, zx, dt_bias, alog_e, d_e, xbc, xbc, states, dy, *([travel] if extra else []))


LOG2E = math.log2(math.e)
LN2 = math.log(2.0)
SB_Q_SCALE = SB_D ** -0.5 * LOG2E


def _sb_scores(q, kj):
    z2 = _dot(q, kj, dims=(((1,), (1,)), ((), ())))
    lb = jnp.minimum(z2, 0.0) - jnp.log2(1.0 + jnp.exp2(-jnp.abs(z2)))
    return lb, lb - z2


SB_TQ_FWD = 512
SB_TQ_BWD = 512
SB_TK = 512
SB_TC = SB_TK // 2


def _sb_after():
    j = lax.broadcasted_iota(jnp.int32, (2 * SB_TC, SB_TC), 0) % SB_TC
    s = lax.broadcasted_iota(jnp.int32, (2 * SB_TC, SB_TC), 1)
    return (j > s).astype(BF16)


def _sb_suffix(x, after):
    T = x.shape[0]
    hi = x.astype(BF16)
    lo = (x - hi.astype(F32)).astype(BF16)
    stacked = jnp.concatenate([jnp.concatenate([hi[:, :SB_TC], lo[:, :SB_TC]], axis=1),
                               jnp.concatenate([hi[:, SB_TC:], lo[:, SB_TC:]], axis=1)], axis=0)
    out = _dot(stacked, after)
    rs_right = jnp.sum(x[:, SB_TC:], axis=1, keepdims=True)
    return (jnp.concatenate([out[:T] + rs_right, out[T:]], axis=1),
            rs_right + jnp.sum(x[:, :SB_TC], axis=1, keepdims=True))


SB_ROWS_FWD = 256
SB_ROWS_BWD = 128


def _sb_seen(rows, row0, start):
    t = lax.broadcasted_iota(jnp.int32, (rows, SB_TK), 0) + row0
    s = lax.broadcasted_iota(jnp.int32, (rows, SB_TK), 1) + start
    return s < t


def _sb_walk(gd, groups, carry):
    carry = lax.cond(gd % 2 == 1, lambda c: groups([gd, gd - 1], c, True), lambda c: groups([gd], c, True), carry)
    base = gd - 1 - gd % 2
    return lax.fori_loop(0, gd // 2, lambda n, c: groups([base - 2 * n, base - 2 * n - 1], c, False), carry)


def _sb_fwd(qs, kn, v, name):
    H, L, d = qs.shape
    SB_TQ, SB_ROWS = SB_TQ_FWD, SB_ROWS_FWD
    nq = L // SB_TQ
    after = _sb_after()

    def body(q_ref, k_ref, v_ref, aft_ref, o_ref):
        i = pl.program_id(1)
        gd = (i * SB_TQ) // SB_TK
        subs = range(SB_TQ // SB_ROWS)
        qr = [q_ref[r * SB_ROWS:(r + 1) * SB_ROWS, :] for r in subs]

        def prep(G, diag, r):
            start = pl.multiple_of(G * SB_TK, SB_TK)
            lb, lk = _sb_scores(qr[r], k_ref[pl.ds(start, SB_TK), :])
            seen = _sb_seen(SB_ROWS, i * SB_TQ + r * SB_ROWS, start) if diag else None
            if diag:
                lk = jnp.where(seen, lk, 0.0)
            between, rs = _sb_suffix(lk, aft_ref[...])
            return lb + between, rs, seen, v_ref[pl.ds(start, SB_TK), :]

        def groups(Gs, carry, diag):
            carry = list(carry)
            pre = [[prep(G, diag and n == 0, r) for n, G in enumerate(Gs)] for r in subs]
            for r in subs:
                R, acc = carry[r]
                for s, rs, seen, vg in pre[r]:
                    w = jnp.exp2(s + R)
                    if seen is not None:
                        w = jnp.where(seen, w, 0.0)
                    acc = acc + _dot(w.astype(BF16), vg)
                    R = R + rs
                carry[r] = (R, acc)
            return tuple(carry)

        zero = (jnp.zeros((SB_ROWS, 1), F32), jnp.zeros((SB_ROWS, d), F32))
        out = _sb_walk(gd, groups, tuple(zero for _ in subs))
        for r in subs:
            o_ref[r * SB_ROWS:(r + 1) * SB_ROWS, :] = out[r][1]

    return pl.pallas_call(
        body,
        name=name,
        grid=(H, nq),
        in_specs=[
            pl.BlockSpec((None, SB_TQ, d), lambda h, i: (h, i, 0)),
            pl.BlockSpec((None, L, d), lambda h, i: (h, 0, 0)),
            pl.BlockSpec((None, L, d), lambda h, i: (h, 0, 0)),
            pl.BlockSpec((2 * SB_TC, SB_TC), lambda h, i: (0, 0)),
        ],
        out_specs=pl.BlockSpec((None, SB_TQ, d), lambda h, i: (h, i, 0)),
        out_shape=jax.ShapeDtypeStruct((H, L, d), F32),
        compiler_params=_cparams(("parallel", "arbitrary")),
    )(qs, kn, v, after)


def _sb_bwd(qs, kn, v, o, do, name):
    H, L, d = qs.shape
    SB_TQ, SB_ROWS = SB_TQ_BWD, SB_ROWS_BWD
    nq = L // SB_TQ
    after = _sb_after()
    qsT = qs.transpose(0, 2, 1)
    do16 = do.astype(BF16)
    doT = do16.transpose(0, 2, 1)

    def body(q_ref, qT_ref, k_ref, v_ref, o_ref, do_ref, doT_ref, aft_ref, dq_ref, dkT_ref, dvT_ref):
        i = pl.program_id(1)

        @pl.when(i == 0)
        def _():
            dkT_ref[...] = jnp.zeros_like(dkT_ref)
            dvT_ref[...] = jnp.zeros_like(dvT_ref)

        gd = (i * SB_TQ) // SB_TK
        subs = range(SB_TQ // SB_ROWS)
        rows = [slice(r * SB_ROWS, (r + 1) * SB_ROWS) for r in subs]
        qr = [q_ref[rows[r], :] for r in subs]
        qT = [qT_ref[:, rows[r]] for r in subs]
        do_ = [do_ref[rows[r], :] for r in subs]
        doT_ = [doT_ref[:, rows[r]] for r in subs]
        D = [jnp.sum(o_ref[rows[r], :] * do_[r].astype(F32), axis=1, keepdims=True) for r in subs]

        def prep(G, diag, r):
            start = pl.multiple_of(G * SB_TK, SB_TK)
            kg = k_ref[pl.ds(start, SB_TK), :]
            lb, lk = _sb_scores(qr[r], kg)
            seen = _sb_seen(SB_ROWS, i * SB_TQ + r * SB_ROWS, start) if diag else None
            if diag:
                lk = jnp.where(seen, lk, 0.0)
            between, rs = _sb_suffix(lk, aft_ref[...])
            dw = _dot(do_[r], v_ref[pl.ds(start, SB_TK), :], dims=(((1,), (1,)), ((), ())))
            return start, kg, lb + between, rs, dw, jnp.exp2(lb), seen

        def groups(Gs_, carry, diag):
            carry = list(carry)
            pre = [[prep(G, diag and n == 0, r) for n, G in enumerate(Gs_)] for r in subs]
            mid = []
            for r in subs:
                R = carry[r][0]
                row = []
                for start, kg, s, rs, dw, sig, seen in pre[r]:
                    w = jnp.exp2(s + R)
                    if seen is not None:
                        w = jnp.where(seen, w, 0.0)
                    w16 = w.astype(BF16)
                    row.append((w16, w16.astype(F32) * dw))
                    R = R + rs
                mid.append((row, R))
            sums = [[_sb_suffix(g, aft_ref[...]) for _, g in mid[r][0]] for r in subs]
            for r in subs:
                _, Gs, dq = carry[r]
                for (start, kg, s, rs, dw, sig, seen), (w16, g), (g_right, gsum) in zip(pre[r], mid[r][0], sums[r]):
                    dvT_ref[:, pl.ds(start, SB_TK)] += _dot(doT_[r], w16)
                    dz = g - sig * (D[r] - Gs - g_right)
                    if seen is not None:
                        dz = jnp.where(seen, dz, 0.0)
                    dz16 = dz.astype(BF16)
                    dq = dq + _dot(dz16, kg)
                    dkT_ref[:, pl.ds(start, SB_TK)] += _dot(qT[r], dz16)
                    Gs = Gs + gsum
                carry[r] = (mid[r][1], Gs, dq)
            return tuple(carry)

        zero1 = jnp.zeros((SB_ROWS, 1), F32)
        out = _sb_walk(gd, groups, tuple((zero1, zero1, jnp.zeros((SB_ROWS, d), F32)) for _ in subs))
        for r in subs:
            dq_ref[rows[r], :] = out[r][2]

    blk = pl.BlockSpec((None, SB_TQ, d), lambda h, i: (h, i, 0))
    blkT = pl.BlockSpec((None, d, SB_TQ), lambda h, i: (h, 0, i))
    full = pl.BlockSpec((None, L, d), lambda h, i: (h, 0, 0))
    fullT = pl.BlockSpec((None, d, L), lambda h, i: (h, 0, 0))
    return pl.pallas_call(
        body,
        name=name,
        grid=(H, nq),
        in_specs=[blk, blkT, full, full, blk, blk, blkT, pl.BlockSpec((2 * SB_TC, SB_TC), lambda h, i: (0, 0))],
        out_specs=[blk, fullT, fullT],
        out_shape=[jax.ShapeDtypeStruct((H, L, d), F32), jax.ShapeDtypeStruct((H, d, L), F32),
                   jax.ShapeDtypeStruct((H, d, L), F32)],
        compiler_params=_cparams(("parallel", "arbitrary")),
    )(qs, qsT, kn, v, o, do16, doT, after)


def _fold_heads(rows, name):
    def body(x_ref, o_ref):
        c = lax.broadcasted_iota(jnp.int32, (D_INNER, DT_PAD), 0)
        h = lax.broadcasted_iota(jnp.int32, (D_INNER, DT_PAD), 1)
        o_ref[...] = _dot(x_ref[...], (c // HEAD_P == h).astype(F32), precision=HI)

    return pl.pallas_call(
        body, name=name, out_shape=jax.ShapeDtypeStruct((SUBLANE, DT_PAD), F32),
        compiler_params=_cparams(),
    )(rows)


def _ffn_fwd(x, r, norm_w, w_in_t, w_out, tag):
    x1, h = _norm_fwd(x, r, norm_w, f"norm_ffn_fwd{tag}")
    gu, act = _ffn_in_fused(h, w_in_t, f"ffn_in_fwd{tag}")
    f = _mm(act, w_out, "nn", F32, f"ffn_out_fwd{tag}")
    return x1, h, gu, act, f


def _ffn_bwd(x1, h, gu, act, norm_w, w_in_t, w_out, dres, tag):
    d16 = dres.astype(BF16)
    dgu = _ffn_out_dx_fused(d16, w_out, gu, f"ffn_out_dx{tag}")
    g_w_out = _mm(act, d16, "tn", F32, f"ffn_out_dw{tag}")
    dh = _mm(dgu, w_in_t, "nn", BF16, f"ffn_in_dx{tag}")
    g_w_in_t = _mm(dgu, h, "tn", F32, f"ffn_in_dw{tag}")
    dres, g_norm = _norm_bwd(x1, norm_w, dh, dres, f"norm_ffn_bwd{tag}")
    return dres, g_norm, g_w_in_t, g_w_out


def _rest_weights(gb):
    whole = _whole_of_shards(gb, SEG_REST)
    return {"ssm_w_out": whole["ssm_w_out"][0], "sb_w_qkv_t": whole["sb_w_qkv"][0], "sb_w_o": whole["sb_w_o"][0],
            "ffn_w_in_t": whole["ffn_w_in"], "ffn_w_out": whole["ffn_w_out"]}


def _local_step(x, target, W, wb_rest=None):
    L = x.shape[0]
    H = SB_HEADS
    nm0, nm1 = W["norm_mix"][0:1], W["norm_mix"][1:2]
    nf0, nf1 = W["norm_ffn"][0:1], W["norm_ffn"][1:2]

    _, h0 = _norm_fwd(x, None, nm0, "norm_mix_fwd0")
    zx = _mm(h0, W["ssm_w_in_t"], "nt", F32, "ssm_in_fwd")
    xbc = _conv_fwd(zx, W["conv_w8"], W["conv_b"], "conv_fwd")
    if wb_rest is None:
        y, states = _ssd_fwd(xbc, zx, W["dt_bias"], W["alog_e"], W["d_e"], "ssd_fwd")
    else:
        y, states, gb_rest = _ssd_fwd(xbc, zx, W["dt_bias"], W["alog_e"], W["d_e"], "ssd_fwd", travel=wb_rest)
        W = {**W, **_rest_weights(gb_rest)}
    yn = _gated_norm_fwd(y, zx, W["ssm_norm_w"], "gated_norm_fwd")
    mix0 = _mm(yn, W["ssm_w_out"], "nn", F32, "ssm_out_fwd")
    x1, h1, gu0, act0, f0 = _ffn_fwd(x, mix0, nf0, W["ffn_w_in_t"][0], W["ffn_w_out"][0], "0")

    x2, h2 = _norm_fwd(x1, f0, nm1, "norm_mix_fwd1")
    qkv = _mm(h2, W["sb_w_qkv_t"], "nt", F32, "sb_qkv_fwd")
    qkv_t = qkv.reshape(L, 3, H, SB_D).transpose(1, 2, 0, 3)
    q_r = qkv_t[0].reshape(H * L, SB_D)
    k_r = qkv_t[1].reshape(H * L, SB_D)
    qs = _qk_norm_fwd(q_r, W["sb_q_gain"], SB_Q_SCALE, "q_norm_fwd").reshape(H, L, SB_D)
    kn = _qk_norm_fwd(k_r, W["sb_k_gain"], 1.0, "k_norm_fwd").reshape(H, L, SB_D)
    vb = qkv_t[2].astype(BF16)
    o = _sb_fwd(qs, kn, vb, "sb_fwd")
    o_flat = o.transpose(1, 0, 2).reshape(L, D_MODEL).astype(BF16)
    mix1 = _mm(o_flat, W["sb_w_o"], "nn", F32, "sb_o_fwd")
    x3, h3, gu1, act1, f1 = _ffn_fwd(x2, mix1, nf1, W["ffn_w_in_t"][1], W["ffn_w_out"][1], "1")

    dres, sq = _loss_bwd(x3, f1, target, "loss")

    dres, g_nf1, g_fin1, g_fout1 = _ffn_bwd(x3, h3, gu1, act1, nf1, W["ffn_w_in_t"][1], W["ffn_w_out"][1], dres, "1")
    d16 = dres.astype(BF16)
    do_flat = _mm(d16, W["sb_w_o"], "nt", F32, "sb_o_dx")
    g_w_o = _mm(o_flat, d16, "tn", F32, "sb_o_dw")
    do = do_flat.reshape(L, H, SB_D).transpose(1, 0, 2)
    dqs, dknT, dvT = _sb_bwd(qs, kn, vb, o, do, "sb_bwd")
    dkn = dknT.transpose(0, 2, 1)
    dq, g_qg = _qk_norm_bwd(q_r, W["sb_q_gain"], dqs.reshape(H * L, SB_D), SB_Q_SCALE, LN2, "q_norm_bwd")
    dk, g_kg = _qk_norm_bwd(k_r, W["sb_k_gain"], dkn.reshape(H * L, SB_D), 1.0, LN2, "k_norm_bwd")
    dqkv = jnp.stack([dq.reshape(H, L, SB_D), dk.reshape(H, L, SB_D), dvT.transpose(0, 2, 1)])
    dqkv = dqkv.transpose(2, 0, 1, 3).reshape(L, 3 * D_MODEL).astype(BF16)
    dh2 = _mm(dqkv, W["sb_w_qkv_t"], "nn", BF16, "sb_qkv_dx")
    g_w_qkv_t = _mm(dqkv, h2, "tn", F32, "sb_qkv_dw")
    dres, g_nm1 = _norm_bwd(x2, nm1, dh2, dres, "norm_mix_bwd1")

    dres, g_nf0, g_fin0, g_fout0 = _ffn_bwd(x1, h1, gu0, act0, nf0, W["ffn_w_in_t"][0], W["ffn_w_out"][0], dres, "0")
    d16 = dres.astype(BF16)
    dyn = _mm(d16, W["ssm_w_out"], "nt", BF16, "ssm_out_dx")
    g_w_out = _mm(yn, d16, "tn", F32, "ssm_out_dw")
    dy, dz, g_snw = _gated_norm_bwd(y, zx, W["ssm_norm_w"], dyn, "gated_norm_bwd")
    rest = {"ssm_w_out": [g_w_out], "sb_w_qkv": [g_w_qkv_t], "sb_w_o": [g_w_o],
            "ffn_w_in": [g_fin0, g_fin1], "ffn_w_out": [g_fout0, g_fout1]}
    if wb_rest is None:
        dact, ddt, g_dtb, g_alog_e, g_d_e = _ssd_bwd(
            xbc, zx, W["dt_bias"], W["alog_e"], W["d_e"], states, dy, "ssd_bwd")
        exchanged = None
    else:
        slots = jnp.stack([_shard_of_whole(rest, s, SEG_REST) for s in range(N_CHIPS)])
        dact, ddt, g_dtb, g_alog_e, g_d_e, received = _ssd_bwd(
            xbc, zx, W["dt_bias"], W["alog_e"], W["d_e"], states, dy, "ssd_bwd", travel=slots.astype(BF16))
        exchanged = (slots, received)
    dxbc, g_cw8, g_cb = _conv_bwd(zx, W["conv_w8"], W["conv_b"], dact, "conv_bwd")
    dzx = jnp.concatenate([dz, dxbc, ddt.astype(BF16)], axis=1)
    dh0 = _mm(dzx, W["ssm_w_in_t"], "nn", BF16, "ssm_in_dx")
    g_w_in_t = _mm(dzx, h0, "tn", F32, "ssm_in_dw")
    grad_x, g_nm0 = _norm_bwd(x, nm0, dh0, dres, "norm_mix_bwd0")

    per_head = jnp.concatenate(
        [g_alog_e.reshape(1, D_INNER), g_d_e.reshape(1, D_INNER), jnp.zeros((SUBLANE - 2, D_INNER), F32)], axis=0)
    folded = _fold_heads(per_head, "fold_heads")
    grads = {
        "norm_mix": jnp.concatenate([g_nm0, g_nm1], axis=0),
        "norm_ffn": jnp.concatenate([g_nf0, g_nf1], axis=0),
        "ssm_w_in": [g_w_in_t],
        "ssm_conv_w": [g_cw8],
        "ssm_conv_b": g_cb,
        "ssm_dt_bias": g_dtb[:, :SSM_HEADS],
        "ssm_a_log": folded[0:1, :SSM_HEADS],
        "ssm_d": folded[1:2, :SSM_HEADS],
        "ssm_norm_w": g_snw,
        "sb_q_gain": g_qg,
        "sb_k_gain": g_kg,
        **rest,
    }
    return (sq, grad_x, grads) if wb_rest is None else (sq, grad_x, grads, exchanged)


def _prep_weights(full):
    W = _prep_vectors(full)
    w_in_t = full["ssm_w_in"].T
    W["ssm_w_in_t"] = jnp.pad(w_in_t, ((0, D_IN_PAD - D_IN_PROJ), (0, 0))).astype(BF16)
    W["conv_w8"] = jnp.pad(full["ssm_conv_w"], ((0, SUBLANE - CONV_W), (0, 0)))
    W["ssm_w_out"] = full["ssm_w_out"].astype(BF16)
    W["sb_w_qkv_t"] = full["sb_w_qkv"].T.astype(BF16)
    W["sb_w_o"] = full["sb_w_o"].astype(BF16)
    W["ffn_w_in_t"] = jnp.swapaxes(full["ffn_w_in"], 1, 2).astype(BF16)
    W["ffn_w_out"] = full["ffn_w_out"].astype(BF16)
    return W


def _prep_vectors(full):
    W = {}
    W["norm_mix"] = full["norm_mix"]
    W["norm_ffn"] = full["norm_ffn"]
    W["conv_b"] = full["ssm_conv_b"]
    W["dt_bias"] = jnp.pad(full["ssm_dt_bias"], ((0, 0), (0, DT_PAD - SSM_HEADS)))
    W["alog_e"] = jnp.repeat(full["ssm_a_log"], HEAD_P, axis=1)
    W["d_e"] = jnp.repeat(full["ssm_d"], HEAD_P, axis=1)
    W["ssm_norm_w"] = full["ssm_norm_w"]
    W["sb_q_gain"] = full["sb_q_gain"]
    W["sb_k_gain"] = full["sb_k_gain"]
    return W


N_CHIPS = 4
N_DEV = 8
FLAT_W = D_MODEL

SEG_FIRST = (
    ("ssm_w_in", (1, D_MODEL, D_IN_PROJ // N_CHIPS), True),
    ("ssm_conv_w", (1, CONV_W, CONV_DIM // N_CHIPS), False),
)
SEG_REST = (
    ("ssm_w_out", (1, D_INNER // N_CHIPS, D_MODEL), False),
    ("sb_w_qkv", (1, D_MODEL, 3 * D_MODEL // N_CHIPS), True),
    ("sb_w_o", (1, D_MODEL // N_CHIPS, D_MODEL), False),
    ("ffn_w_in", (2, D_MODEL, 2 * D_FF // N_CHIPS), True),
    ("ffn_w_out", (2, D_FF // N_CHIPS, D_MODEL), False),
)
SHARDED = SEG_FIRST + SEG_REST
REPLICATED = (
    ("norm_mix", (2, D_MODEL)), ("norm_ffn", (2, D_MODEL)), ("ssm_conv_b", (1, CONV_DIM)),
    ("ssm_norm_w", (1, D_INNER)), ("ssm_dt_bias", (1, SSM_HEADS)), ("ssm_a_log", (1, SSM_HEADS)),
    ("ssm_d", (1, SSM_HEADS)), ("sb_q_gain", (1, SB_D)), ("sb_k_gain", (1, SB_D)),
)
ADAM_TILE = 256
_USED_SMALL = sum(math.prod(s) for _, s in REPLICATED)
SMALL_ROWS = -(-_USED_SMALL // (SUBLANE * FLAT_W)) * SUBLANE


def _flat_rows(shape):
    return math.prod(shape) // FLAT_W


def _seg_rows(seg):
    used = sum(_flat_rows(s) for _, s, _ in seg)
    return used, -(-used // ADAM_TILE) * ADAM_TILE


def _pad_rows(parts, seg, dtype):
    used, rows = _seg_rows(seg)
    return parts + [jnp.zeros((rows - used, FLAT_W), dtype)] if rows > used else parts


def _pack_shard(d, dtype, seg):
    parts = []
    for n, _, transposed in seg:
        a = jnp.swapaxes(d[n], 1, 2) if transposed else d[n]
        parts.append(a.reshape(-1, FLAT_W).astype(dtype))
    return jnp.concatenate(_pad_rows(parts, seg, dtype), axis=0)


def _unpack_shard(flat, seg):
    out, r = {}, 0
    for n, s, transposed in seg:
        k = _flat_rows(s)
        a = flat[r:r + k]
        out[n] = jnp.swapaxes(a.reshape(s[0], s[2], s[1]), 1, 2) if transposed else a.reshape(s)
        r += k
    return out


def _shard_of_whole(whole, chip, seg):
    parts = []
    for n, s, _ in seg:
        k = _flat_rows(s) // s[0]
        if n == "ssm_conv_w":
            parts.append(whole[n][0][:CONV_W, chip * FLAT_W:(chip + 1) * FLAT_W])
        else:
            parts.extend(a[chip * k:(chip + 1) * k] for a in whole[n])
    return jnp.concatenate(_pad_rows(parts, seg, parts[0].dtype), axis=0)


def _whole_of_shards(gb, seg):
    out, r = {}, 0
    for n, s, _ in seg:
        k = _flat_rows(s) // s[0]
        if n != "ssm_conv_w":
            out[n] = [jnp.concatenate([gb[p, r + l * k:r + (l + 1) * k] for p in range(N_CHIPS)], axis=0)
                      for l in range(s[0])]
        r += k * s[0]
    return out


def _pack_small(d):
    parts = [d[n].reshape(-1) for n, _ in REPLICATED]
    parts.append(jnp.zeros((SMALL_ROWS * FLAT_W - _USED_SMALL,), parts[0].dtype))
    return jnp.concatenate(parts).reshape(SMALL_ROWS, FLAT_W)


def _unpack_small(flat):
    flat = flat.reshape(-1)
    out, r = {}, 0
    for n, s in REPLICATED:
        k = math.prod(s)
        out[n] = flat[r:r + k].reshape(s)
        r += k
    return out


ANY = pl.BlockSpec(memory_space=pl.ANY)


def _other_chips():
    x, y, c = lax.axis_index("x"), lax.axis_index("y"), lax.axis_index("c")
    return x, y, c, [(1 - x, y), (x, 1 - y), (1 - x, 1 - y)]


def _remote(src, dst, send_sem, recv_sem, dev):
    return pltpu.make_async_remote_copy(src_ref=src, dst_ref=dst, send_sem=send_sem, recv_sem=recv_sem,
                                        device_id=dev, device_id_type=MESH_T)


def _gather_weights(wb, ws):
    def body(wb_ref, ws_ref, gb_ref, gs_ref, send_sems, recv_sems, loc_sems):
        x, y, c, others = _other_chips()
        me = 2 * x + y
        pairs = ((wb_ref, gb_ref), (ws_ref, gs_ref))
        local = [pltpu.make_async_copy(src, dst.at[me], loc_sems.at[t]) for t, (src, dst) in enumerate(pairs)]
        for cp in local:
            cp.start()
        sends = []
        for k, (px, py) in enumerate(others):
            for t, (src, dst) in enumerate(pairs):
                cp = _remote(src, dst.at[me], send_sems.at[2 * k + t], recv_sems.at[2 * k + t], (px, py, c))
                cp.start()
                sends.append(cp)
        for k, (px, py) in enumerate(others):
            for t, (src, dst) in enumerate(pairs):
                _remote(src, dst.at[2 * px + py], send_sems.at[2 * k + t], recv_sems.at[2 * k + t],
                        (px, py, c)).wait_recv()
        for cp in sends:
            cp.wait_send()
        for cp in local:
            cp.wait()

    return pl.pallas_call(
        body, name="gather_weights",
        in_specs=[ANY, ANY], out_specs=[ANY, ANY],
        out_shape=[jax.ShapeDtypeStruct((N_CHIPS,) + wb.shape, wb.dtype),
                   jax.ShapeDtypeStruct((N_CHIPS,) + ws.shape, ws.dtype)],
        scratch_shapes=[pltpu.SemaphoreType.DMA((6,)), pltpu.SemaphoreType.DMA((6,)), pltpu.SemaphoreType.DMA((2,))],
    )(wb, ws)


def _scatter_grads(gpack):
    def body(g_ref, r_ref, send_sems, recv_sems):
        x, y, c, others = _other_chips()
        sends = []
        for k, (px, py) in enumerate(others):
            cp = _remote(g_ref.at[2 * px + py], r_ref.at[k], send_sems.at[k], recv_sems.at[k], (px, py, c))
            cp.start()
            sends.append(cp)
        for cp in sends:
            cp.wait_recv()
        for cp in sends:
            cp.wait_send()

    return pl.pallas_call(
        body, name="scatter_grads",
        in_specs=[ANY], out_specs=ANY,
        out_shape=jax.ShapeDtypeStruct((N_CHIPS - 1,) + gpack.shape[1:], gpack.dtype),
        scratch_shapes=[pltpu.SemaphoreType.DMA((3,)), pltpu.SemaphoreType.DMA((3,))],
    )(gpack)


class _Moves:
    def __init__(self, sends, recvs, local=None):
        self.sends, self.recvs, self.local = sends, recvs, local

    def start(self):
        if self.local is not None:
            self.local.start()
        for cp in self.sends:
            cp.start()

    def wait(self):
        for cp in self.recvs:
            cp.wait_recv()
        for cp in self.sends:
            cp.wait_send()
        if self.local is not None:
            self.local.wait()


def _gather_moves(wb_ref, gb_ref, send_sems, recv_sems, loc_sem):
    x, y, c, others = _other_chips()
    me = 2 * x + y
    sends = [_remote(wb_ref, gb_ref.at[me], send_sems.at[k], recv_sems.at[k], (px, py, c))
             for k, (px, py) in enumerate(others)]
    recvs = [_remote(wb_ref, gb_ref.at[2 * px + py], send_sems.at[k], recv_sems.at[k], (px, py, c))
             for k, (px, py) in enumerate(others)]
    return _Moves(sends, recvs, pltpu.make_async_copy(wb_ref, gb_ref.at[me], loc_sem))


def _scatter_moves(g_ref, r_ref, send_sems, recv_sems):
    x, y, c, others = _other_chips()
    sends = [_remote(g_ref.at[2 * px + py], r_ref.at[k], send_sems.at[k], recv_sems.at[k], (px, py, c))
             for k, (px, py) in enumerate(others)]
    return _Moves(sends, sends)


_MOVE_SEMS = [pltpu.SemaphoreType.DMA((N_CHIPS - 1,)), pltpu.SemaphoreType.DMA((N_CHIPS - 1,))]


def _exchange_partials(part_a, part_b, small):
    def body(pa_ref, pb_ref, s_ref, psa_ref, psb_ref, sa_ref, send_sems, recv_sems, loc_sem):
        x, y, c = lax.axis_index("x"), lax.axis_index("y"), lax.axis_index("c")
        me = 4 * x + 2 * y + c
        local = pltpu.make_async_copy(s_ref, sa_ref.at[me], loc_sem)
        local.start()
        sib = _remote(pa_ref, psa_ref, send_sems.at[0], recv_sems.at[0], (x, y, 1 - c))
        sib_b = _remote(pb_ref, psb_ref, send_sems.at[N_DEV], recv_sems.at[N_DEV], (x, y, 1 - c))
        sib.start()
        sib_b.start()
        sends = [sib, sib_b]
        for k in range(1, N_DEV):
            fx, fy, fc = (k >> 2) & 1, (k >> 1) & 1, k & 1
            px, py, pc = x ^ fx, y ^ fy, c ^ fc
            cp = _remote(s_ref, sa_ref.at[me], send_sems.at[k], recv_sems.at[k], (px, py, pc))
            cp.start()
            sends.append(cp)
        sib.wait_recv()
        sib_b.wait_recv()
        for k in range(1, N_DEV):
            fx, fy, fc = (k >> 2) & 1, (k >> 1) & 1, k & 1
            px, py, pc = x ^ fx, y ^ fy, c ^ fc
            _remote(s_ref, sa_ref.at[4 * px + 2 * py + pc], send_sems.at[k], recv_sems.at[k], (px, py, pc)).wait_recv()
        for cp in sends:
            cp.wait_send()
        local.wait()

    return pl.pallas_call(
        body, name="exchange_partials",
        in_specs=[ANY, ANY, ANY], out_specs=[ANY, ANY, ANY],
        out_shape=[jax.ShapeDtypeStruct(part_a.shape, part_a.dtype),
                   jax.ShapeDtypeStruct(part_b.shape, part_b.dtype),
                   jax.ShapeDtypeStruct((N_DEV,) + small.shape, small.dtype)],
        scratch_shapes=[pltpu.SemaphoreType.DMA((N_DEV + 1,)), pltpu.SemaphoreType.DMA((N_DEV + 1,)),
                        pltpu.SemaphoreType.DMA],
    )(part_a, part_b, small)


def _partial_sum(own, recv, name):
    R = own.shape[0]

    def body(o_ref, r_ref, p_ref):
        acc = o_ref[...]
        for k in range(N_CHIPS - 1):
            acc = acc + r_ref[k].astype(F32)
        p_ref[...] = acc

    return pl.pallas_call(
        body, name=name, grid=(R // ADAM_TILE,),
        in_specs=[pl.BlockSpec((ADAM_TILE, FLAT_W), lambda i: (i, 0)),
                  pl.BlockSpec((N_CHIPS - 1, ADAM_TILE, FLAT_W), lambda i: (0, i, 0))],
        out_specs=pl.BlockSpec((ADAM_TILE, FLAT_W), lambda i: (i, 0)),
        out_shape=jax.ShapeDtypeStruct(own.shape, F32),
        compiler_params=_cparams(("parallel",)),
    )(own, recv)


def _adamw_math(w, g, m, v):
    m = ADAM_B1 * m + (1.0 - ADAM_B1) * g
    v = ADAM_B2 * v + (1.0 - ADAM_B2) * jnp.square(g)
    m_hat = m / (1.0 - ADAM_B1 ** ADAM_STEP)
    v_hat = v / (1.0 - ADAM_B2 ** ADAM_STEP)
    delta = -ADAM_LR * (m_hat / (jnp.sqrt(v_hat) + ADAM_EPS) + ADAM_WD * w)
    return delta, m, v


def _adamw(w, m, v, parts, tile, name):
    n, R, _ = parts.shape

    def body(w_ref, m_ref, v_ref, p_ref, g_ref, d_ref, nm_ref, nv_ref):
        g = p_ref[0]
        for k in range(1, n):
            g = g + p_ref[k]
        delta, nm, nv = _adamw_math(w_ref[...], g, m_ref[...], v_ref[...])
        g_ref[...] = g
        d_ref[...] = delta
        nm_ref[...] = nm
        nv_ref[...] = nv

    blk = pl.BlockSpec((tile, FLAT_W), lambda i: (i, 0))
    return pl.pallas_call(
        body, name=name, grid=(R // tile,),
        in_specs=[blk, blk, blk, pl.BlockSpec((n, tile, FLAT_W), lambda i: (0, i, 0))],
        out_specs=[blk] * 4,
        out_shape=[jax.ShapeDtypeStruct((R, FLAT_W), F32)] * 4,
        compiler_params=_cparams(("parallel",)),
    )(w, m, v, parts)


def _add2(a, b, name):
    def body(a_ref, b_ref, o_ref):
        o_ref[...] = a_ref[...] + b_ref[...]

    blk = pl.BlockSpec((ADAM_TILE, FLAT_W), lambda i: (i, 0))
    return pl.pallas_call(
        body, name=name, grid=(a.shape[0] // ADAM_TILE,), in_specs=[blk, blk], out_specs=blk,
        out_shape=jax.ShapeDtypeStruct(a.shape, F32), compiler_params=_cparams(("parallel",)),
    )(a, b)


def _adamw_block(w, m, v, g, name):
    shape = w.shape
    R, C = math.prod(shape[:-1]), shape[-1]
    tile = _tile(R, ADAM_TILE, SUBLANE)

    def body(w_ref, m_ref, v_ref, g_ref, d_ref, nm_ref, nv_ref):
        delta, nm, nv = _adamw_math(w_ref[...], g_ref[...], m_ref[...], v_ref[...])
        d_ref[...] = delta
        nm_ref[...] = nm
        nv_ref[...] = nv

    blk = pl.BlockSpec((tile, C), lambda i: (i, 0))
    outs = pl.pallas_call(
        body, name=name, grid=(R // tile,), in_specs=[blk] * 4, out_specs=[blk] * 3,
        out_shape=[jax.ShapeDtypeStruct((R, C), F32)] * 3, compiler_params=_cparams(("parallel",)),
    )(*(a.reshape(R, C) for a in (w, m, v, g)))
    return tuple(a.reshape(shape) for a in outs)


_NAMES = ("norm_mix", "norm_ffn", "ssm_w_in", "ssm_conv_w", "ssm_conv_b", "ssm_dt_bias", "ssm_a_log", "ssm_d",
          "ssm_norm_w", "ssm_w_out", "sb_w_qkv", "sb_q_gain", "sb_k_gain", "sb_w_o", "ffn_w_in", "ffn_w_out")


def _step(x, loss_target, w, m, v):
    cx, cy, cc = lax.axis_index("x"), lax.axis_index("y"), lax.axis_index("c")
    chip = 2 * cx + cy

    conv8 = jnp.pad(w["ssm_conv_w"][0], ((0, SUBLANE - CONV_W), (0, 0)))
    gb, gs = _gather_weights(_pack_shard(w, BF16, SEG_FIRST), conv8)
    W = _prep_vectors(w)
    W["ssm_w_in_t"] = jnp.pad(_whole_of_shards(gb, SEG_FIRST)["ssm_w_in"][0], ((0, D_IN_PAD - D_IN_PROJ), (0, 0)))
    W["conv_w8"] = jnp.concatenate([gs[p] for p in range(N_CHIPS)], axis=1)

    sq, grad_x, grads, (slots_rest, recv_rest) = _local_step(
        x[0], loss_target[0], W, wb_rest=_pack_shard(w, BF16, SEG_REST))
    loss = lax.psum(0.5 * jnp.sum(sq) / D_MODEL, ("x", "y", "c"))

    slots_first = jnp.stack([_shard_of_whole(grads, s, SEG_FIRST) for s in range(N_CHIPS)])
    recv_first = _scatter_grads(slots_first.astype(BF16))
    own = lambda slots: lax.dynamic_index_in_dim(slots, chip, axis=0, keepdims=False)
    part_first = _partial_sum(own(slots_first), recv_first, "grad_partial_sum_first")
    part_rest = _partial_sum(own(slots_rest), recv_rest, "grad_partial_sum_rest")
    sib_first, sib_rest, small_all = _exchange_partials(part_first, part_rest, _pack_small(grads))
    g_blocks = {**_unpack_shard(_add2(part_first, sib_first, "grad_final_sum_first"), SEG_FIRST),
                **_unpack_shard(_add2(part_rest, sib_rest, "grad_final_sum_rest"), SEG_REST)}
    g_s, d_s, m_s, v_s = _adamw(_pack_small(w), _pack_small(m), _pack_small(v), small_all, SUBLANE, "adamw_replicated")

    g, d, nm, nv = (_unpack_small(a) for a in (g_s, d_s, m_s, v_s))
    for n, _, _ in SHARDED:
        g[n] = g_blocks[n]
        d[n], nm[n], nv[n] = _adamw_block(w[n], m[n], v[n], g[n], "adamw_" + n)
    return loss, grad_x[None], [[t[n] for n in _NAMES] for t in (g, d, nm, nv)]


def kernel(x, norm_mix, norm_ffn, ssm_w_in, ssm_conv_w, ssm_conv_b, ssm_dt_bias, ssm_a_log, ssm_d, ssm_norm_w, ssm_w_out, sb_w_qkv, sb_q_gain, sb_k_gain, sb_w_o, ffn_w_in, ffn_w_out, loss_target, m_norm_mix, m_norm_ffn, m_ssm_w_in, m_ssm_conv_w, m_ssm_conv_b, m_ssm_dt_bias, m_ssm_a_log, m_ssm_d, m_ssm_norm_w, m_ssm_w_out, m_sb_w_qkv, m_sb_q_gain, m_sb_k_gain, m_sb_w_o, m_ffn_w_in, m_ffn_w_out, v_norm_mix, v_norm_ffn, v_ssm_w_in, v_ssm_conv_w, v_ssm_conv_b, v_ssm_dt_bias, v_ssm_a_log, v_ssm_d, v_ssm_norm_w, v_ssm_w_out, v_sb_w_qkv, v_sb_q_gain, v_sb_k_gain, v_sb_w_o, v_ffn_w_in, v_ffn_w_out):
    w = dict(zip(_NAMES, (norm_mix, norm_ffn, ssm_w_in, ssm_conv_w, ssm_conv_b, ssm_dt_bias, ssm_a_log, ssm_d,
                          ssm_norm_w, ssm_w_out, sb_w_qkv, sb_q_gain, sb_k_gain, sb_w_o, ffn_w_in, ffn_w_out)))
    m = dict(zip(_NAMES, (m_norm_mix, m_norm_ffn, m_ssm_w_in, m_ssm_conv_w, m_ssm_conv_b, m_ssm_dt_bias, m_ssm_a_log,
                          m_ssm_d, m_ssm_norm_w, m_ssm_w_out, m_sb_w_qkv, m_sb_q_gain, m_sb_k_gain, m_sb_w_o,
                          m_ffn_w_in, m_ffn_w_out)))
    v = dict(zip(_NAMES, (v_norm_mix, v_norm_ffn, v_ssm_w_in, v_ssm_conv_w, v_ssm_conv_b, v_ssm_dt_bias, v_ssm_a_log,
                          v_ssm_d, v_ssm_norm_w, v_ssm_w_out, v_sb_w_qkv, v_sb_q_gain, v_sb_k_gain, v_sb_w_o,
                          v_ffn_w_in, v_ffn_w_out)))
    loss, grad_x, (g, d, nm, nv) = _step(x, loss_target, w, m, v)
    return (loss, grad_x, *g, *d, *nm, *nv)
```
